```python
import jax, jax.numpy as jnp
from jax import lax
import numpy as np

D_MODEL = 2048
BATCH = 8
SEQ = 8192
DEPTH = 2

MEM_LEN = 256
N_BRANCH = 4
BRANCH_W = 1024
POOL_GROUPS = 4
POOL_WINDOWS = (2, 4, 8, 16)
POOL_GW = BRANCH_W // POOL_GROUPS
MLA_HEADS = 8
Q_LORA = 512
KV_LORA = 512
QK_NOPE = 128
QK_ROPE = 64
QK_HEAD = QK_NOPE + QK_ROPE
V_HEAD = 128
ROPE_THETA = 10000.0
CONV_W = 3
XATTN_HEADS = 4
XATTN_HEAD_DIM = BRANCH_W // XATTN_HEADS
Q_BLOCK = 128
EPS = 1e-6
IN_SPLITS = (BRANCH_W, BRANCH_W, Q_LORA, KV_LORA, QK_ROPE, BRANCH_W, BRANCH_W, BRANCH_W, BRANCH_W, BRANCH_W, BRANCH_W, BRANCH_W, N_BRANCH * D_MODEL)
N_IN = 9 * BRANCH_W + Q_LORA + KV_LORA + QK_ROPE + N_BRANCH * D_MODEL

kernel_name = 'hybrid_gated_pool_mla_conv_memxattn'


def rms_norm(x, g):
    xf = x.astype(jnp.float32)
    y = xf * lax.rsqrt(jnp.mean(xf * xf, axis=-1, keepdims=True) + EPS)
    return (y * g.astype(jnp.float32)).astype(x.dtype)


def rope_tables(positions):
    inv = ROPE_THETA ** (-jnp.arange(0, QK_ROPE, 2, dtype=jnp.float32) / QK_ROPE)
    ang = positions.astype(jnp.float32)[..., None] * inv
    return jnp.cos(ang)[:, :, None, :], jnp.sin(ang)[:, :, None, :]


def rotate_tail(xh, cos, sin):
    nope = xh[..., :QK_NOPE]
    r = xh[..., QK_NOPE:].astype(jnp.float32)
    r1, r2 = r[..., :QK_ROPE // 2], r[..., QK_ROPE // 2:]
    rot = jnp.concatenate([r1 * cos - r2 * sin, r2 * cos + r1 * sin], axis=-1).astype(xh.dtype)
    return jnp.concatenate([nope, rot], axis=-1)


def pool_mixer(v, pool_w, pool_scale):
    B, S, _ = v.shape
    vg = v.reshape(B, S, POOL_GROUPS, POOL_GW).astype(jnp.float32)
    cs = jnp.cumsum(vg, axis=1)
    win = jnp.array(POOL_WINDOWS, dtype=jnp.int32)
    t = jnp.arange(S, dtype=jnp.int32)
    prev = t[:, None] - win[None, :]
    cs_prev = cs[:, jnp.maximum(prev, 0), jnp.arange(POOL_GROUPS)[None, :], :]
    cs_prev = jnp.where((prev >= 0)[None, :, :, None], cs_prev, 0.0)
    cnt = jnp.minimum(t[:, None] + 1, win[None, :]).astype(jnp.float32)
    mixed = ((cs - cs_prev) / cnt[None, :, :, None] - vg).astype(v.dtype)
    out = jnp.einsum('bsgc,gcd->bsgd', mixed, pool_w)
    return out.reshape(B, S, BRANCH_W) * pool_scale


def causal_block_attention(q, k, v):
    B, S, H, Dh = q.shape
    nb = S // Q_BLOCK
    scale = Dh ** -0.5
    qb = q.reshape(B, nb, Q_BLOCK, H, Dh).transpose(1, 0, 2, 3, 4)
    starts = jnp.arange(nb, dtype=jnp.int32) * Q_BLOCK
    kpos = jnp.arange(S, dtype=jnp.int32)

    def one_block(args):
        qi, s0 = args
        s = jnp.einsum('bqhd,bkhd->bhqk', qi, k).astype(jnp.float32) * scale
        mask = kpos[None, :] <= (s0 + jnp.arange(Q_BLOCK, dtype=jnp.int32))[:, None]
        s = jnp.where(mask[None, None], s, -jnp.inf)
        p = jax.nn.softmax(s, axis=-1).astype(v.dtype)
        return jnp.einsum('bhqk,bkhd->bqhd', p, v)

    o = lax.map(one_block, (qb, starts))
    return o.transpose(1, 0, 2, 3, 4).reshape(B, S, H, v.shape[-1])


def mla_mixer(cq, ckv, krope, cos, sin, q_a_g, kv_a_g, w_uq, w_ukv, q_g, k_g):
    B, S, _ = cq.shape
    q = (rms_norm(cq, q_a_g) @ w_uq).reshape(B, S, MLA_HEADS, QK_HEAD)
    kv = (rms_norm(ckv, kv_a_g) @ w_ukv).reshape(B, S, MLA_HEADS, QK_NOPE + V_HEAD)
    k_nope, v = kv[..., :QK_NOPE], kv[..., QK_NOPE:]
    k = jnp.concatenate([k_nope, jnp.broadcast_to(krope[:, :, None, :], (B, S, MLA_HEADS, QK_ROPE))], axis=-1)
    q = rotate_tail(rms_norm(q, q_g), cos, sin)
    k = rotate_tail(rms_norm(k, k_g), cos, sin)
    o = causal_block_attention(q, k, v)
    return o.reshape(B, S, MLA_HEADS * V_HEAD)


def conv_mixer(b, c, xc, conv_w):
    u = c * xc
    y = lax.conv_general_dilated(u, conv_w[:, None, :].astype(u.dtype), window_strides=(1,), padding=[(CONV_W - 1, 0)], dimension_numbers=('NWC', 'WIO', 'NWC'), feature_group_count=BRANCH_W)
    return b * y


def memory_xattn(q, mem_kv, q_g, k_g):
    B, S, _ = q.shape
    M = mem_kv.shape[1]
    qh = rms_norm(q.reshape(B, S, XATTN_HEADS, XATTN_HEAD_DIM), q_g)
    k = rms_norm(mem_kv[..., :BRANCH_W].reshape(B, M, XATTN_HEADS, XATTN_HEAD_DIM), k_g)
    v = mem_kv[..., BRANCH_W:].reshape(B, M, XATTN_HEADS, XATTN_HEAD_DIM)
    s = jnp.einsum('bshd,bmhd->bhsm', qh, k).astype(jnp.float32) * (XATTN_HEAD_DIM ** -0.5)
    p = jax.nn.softmax(s, axis=-1).astype(v.dtype)
    return jnp.einsum('bhsm,bmhd->bshd', p, v).reshape(B, S, BRANCH_W)


def _fwd_setup_inputs(seed: int = 0) -> dict:
    key = jax.random.key(seed)
    ks = jax.random.split(key, 24)
    f32 = jnp.float32

    def nrm(k, shape, scale):
        return jax.random.normal(k, shape, f32) * scale

    def gain(k, shape):
        return 1.0 + 0.02 * jax.random.normal(k, shape, f32)

    offs = jax.random.randint(ks[2], (BATCH, 1), 0, 4096, dtype=jnp.int32)
    positions = offs + jnp.arange(SEQ, dtype=jnp.int32)[None, :]
    return {
        'x': nrm(ks[0], (BATCH, SEQ, D_MODEL), 1.0),
        'mem': nrm(ks[1], (BATCH, MEM_LEN, D_MODEL), 1.0),
        'positions': positions,
        'norm_g': gain(ks[3], (DEPTH, D_MODEL)),
        'w_in': nrm(ks[4], (DEPTH, D_MODEL, N_IN), D_MODEL ** -0.5),
        'gate_b': nrm(ks[5], (DEPTH, N_BRANCH * D_MODEL), 0.02),
        'pool_w': nrm(ks[6], (DEPTH, POOL_GROUPS, POOL_GW, POOL_GW), POOL_GW ** -0.5),
        'pool_scale': gain(ks[7], (DEPTH, BRANCH_W)),
        'q_a_norm_g': gain(ks[8], (DEPTH, Q_LORA)),
        'kv_a_norm_g': gain(ks[9], (DEPTH, KV_LORA)),
        'w_uq': nrm(ks[10], (DEPTH, Q_LORA, MLA_HEADS * QK_HEAD), Q_LORA ** -0.5),
        'w_ukv': nrm(ks[11], (DEPTH, KV_LORA, MLA_HEADS * (QK_NOPE + V_HEAD)), KV_LORA ** -0.5),
        'mla_q_norm_g': gain(ks[12], (DEPTH, QK_HEAD)),
        'mla_k_norm_g': gain(ks[13], (DEPTH, QK_HEAD)),
        'conv_w': nrm(ks[14], (DEPTH, CONV_W, BRANCH_W), CONV_W ** -0.5),
        'mem_norm_g': gain(ks[15], (DEPTH, D_MODEL)),
        'w_mem_kv': nrm(ks[16], (DEPTH, D_MODEL, 2 * BRANCH_W), D_MODEL ** -0.5),
        'xattn_q_norm_g': gain(ks[17], (DEPTH, XATTN_HEAD_DIM)),
        'xattn_k_norm_g': gain(ks[18], (DEPTH, XATTN_HEAD_DIM)),
        'w_branch': nrm(ks[19], (DEPTH, N_BRANCH, BRANCH_W, D_MODEL), BRANCH_W ** -0.5),
        'w_out': nrm(ks[20], (DEPTH, D_MODEL, D_MODEL), D_MODEL ** -0.5),
    }


def _fwd_reference(x, mem, positions, norm_g, w_in, gate_b, pool_w, pool_scale, q_a_norm_g, kv_a_norm_g, w_uq, w_ukv, mla_q_norm_g, mla_k_norm_g, conv_w, mem_norm_g, w_mem_kv, xattn_q_norm_g, xattn_k_norm_g, w_branch, w_out):
    B, S, _ = x.shape
    cos, sin = rope_tables(positions)
    split_points = np.cumsum(IN_SPLITS)[:-1].tolist()
    for l in range(DEPTH):
        h = rms_norm(x, norm_g[l])
        proj = h @ w_in[l]
        (pv, pz, cq, ckv, kr, mz, cb, cc, cx, cz, xq, xz, gpre) = jnp.split(proj, split_points, axis=-1)
        y_pool = pool_mixer(pv, pool_w[l], pool_scale[l]) * jax.nn.silu(pz)
        y_mla = mla_mixer(cq, ckv, kr, cos, sin, q_a_norm_g[l], kv_a_norm_g[l], w_uq[l], w_ukv[l], mla_q_norm_g[l], mla_k_norm_g[l]) * jax.nn.silu(mz)
        y_conv = conv_mixer(cb, cc, cx, conv_w[l]) * jax.nn.silu(cz)
        mem_kv = rms_norm(mem, mem_norm_g[l]) @ w_mem_kv[l]
        y_mem = memory_xattn(xq, mem_kv, xattn_q_norm_g[l], xattn_k_norm_g[l]) * jax.nn.silu(xz)
        gates = jax.nn.sigmoid((gpre + gate_b[l]).astype(jnp.float32)).astype(x.dtype).reshape(B, S, N_BRANCH, D_MODEL)
        merged = gates[:, :, 0] * (y_pool @ w_branch[l, 0])
        merged = merged + gates[:, :, 1] * (y_mla @ w_branch[l, 1])
        merged = merged + gates[:, :, 2] * (y_conv @ w_branch[l, 2])
        merged = merged + gates[:, :, 3] * (y_mem @ w_branch[l, 3])
        x = x + merged @ w_out[l]
    return x


import jax as _jax
import jax.numpy as _jnp

TWIN_FORMAT = 'train_step'
FWD_PARAMS = ['x', 'mem', 'positions', 'norm_g', 'w_in', 'gate_b', 'pool_w', 'pool_scale', 'q_a_norm_g', 'kv_a_norm_g', 'w_uq', 'w_ukv', 'mla_q_norm_g', 'mla_k_norm_g', 'conv_w', 'mem_norm_g', 'w_mem_kv', 'xattn_q_norm_g', 'xattn_k_norm_g', 'w_branch', 'w_out']
TWIN_WEIGHTS = ['norm_g', 'w_in', 'gate_b', 'pool_w', 'pool_scale', 'q_a_norm_g', 'kv_a_norm_g', 'w_uq', 'w_ukv', 'mla_q_norm_g', 'mla_k_norm_g', 'conv_w', 'mem_norm_g', 'w_mem_kv', 'xattn_q_norm_g', 'xattn_k_norm_g', 'w_branch', 'w_out']
TWIN_DIFF_INPUT = 'x'
TWIN_INPUTS = ['x', 'mem', 'positions', 'norm_g', 'w_in', 'gate_b', 'pool_w', 'pool_scale', 'q_a_norm_g', 'kv_a_norm_g', 'w_uq', 'w_ukv', 'mla_q_norm_g', 'mla_k_norm_g', 'conv_w', 'mem_norm_g', 'w_mem_kv', 'xattn_q_norm_g', 'xattn_k_norm_g', 'w_branch', 'w_out', 'loss_target', 'm_norm_g', 'm_w_in', 'm_gate_b', 'm_pool_w', 'm_pool_scale', 'm_q_a_norm_g', 'm_kv_a_norm_g', 'm_w_uq', 'm_w_ukv', 'm_mla_q_norm_g', 'm_mla_k_norm_g', 'm_conv_w', 'm_mem_norm_g', 'm_w_mem_kv', 'm_xattn_q_norm_g', 'm_xattn_k_norm_g', 'm_w_branch', 'm_w_out', 'v_norm_g', 'v_w_in', 'v_gate_b', 'v_pool_w', 'v_pool_scale', 'v_q_a_norm_g', 'v_kv_a_norm_g', 'v_w_uq', 'v_w_ukv', 'v_mla_q_norm_g', 'v_mla_k_norm_g', 'v_conv_w', 'v_mem_norm_g', 'v_w_mem_kv', 'v_xattn_q_norm_g', 'v_xattn_k_norm_g', 'v_w_branch', 'v_w_out']
TWIN_OUTPUTS = ['loss', 'grad_x', 'grad_norm_g', 'grad_w_in', 'grad_gate_b', 'grad_pool_w', 'grad_pool_scale', 'grad_q_a_norm_g', 'grad_kv_a_norm_g', 'grad_w_uq', 'grad_w_ukv', 'grad_mla_q_norm_g', 'grad_mla_k_norm_g', 'grad_conv_w', 'grad_mem_norm_g', 'grad_w_mem_kv', 'grad_xattn_q_norm_g', 'grad_xattn_k_norm_g', 'grad_w_branch', 'grad_w_out', 'delta_norm_g', 'delta_w_in', 'delta_gate_b', 'delta_pool_w', 'delta_pool_scale', 'delta_q_a_norm_g', 'delta_kv_a_norm_g', 'delta_w_uq', 'delta_w_ukv', 'delta_mla_q_norm_g', 'delta_mla_k_norm_g', 'delta_conv_w', 'delta_mem_norm_g', 'delta_w_mem_kv', 'delta_xattn_q_norm_g', 'delta_xattn_k_norm_g', 'delta_w_branch', 'delta_w_out', 'new_m_norm_g', 'new_m_w_in', 'new_m_gate_b', 'new_m_pool_w', 'new_m_pool_scale', 'new_m_q_a_norm_g', 'new_m_kv_a_norm_g', 'new_m_w_uq', 'new_m_w_ukv', 'new_m_mla_q_norm_g', 'new_m_mla_k_norm_g', 'new_m_conv_w', 'new_m_mem_norm_g', 'new_m_w_mem_kv', 'new_m_xattn_q_norm_g', 'new_m_xattn_k_norm_g', 'new_m_w_branch', 'new_m_w_out', 'new_v_norm_g', 'new_v_w_in', 'new_v_gate_b', 'new_v_pool_w', 'new_v_pool_scale', 'new_v_q_a_norm_g', 'new_v_kv_a_norm_g', 'new_v_w_uq', 'new_v_w_ukv', 'new_v_mla_q_norm_g', 'new_v_mla_k_norm_g', 'new_v_conv_w', 'new_v_mem_norm_g', 'new_v_w_mem_kv', 'new_v_xattn_q_norm_g', 'new_v_xattn_k_norm_g', 'new_v_w_branch', 'new_v_w_out']
TWIN_LEAF_KINDS = {'loss': 'loss', 'grad_x': 'grad_x', 'grad_norm_g': 'grad_w', 'grad_w_in': 'grad_w', 'grad_gate_b': 'grad_w', 'grad_pool_w': 'grad_w', 'grad_pool_scale': 'grad_w', 'grad_q_a_norm_g': 'grad_w', 'grad_kv_a_norm_g': 'grad_w', 'grad_w_uq': 'grad_w', 'grad_w_ukv': 'grad_w', 'grad_mla_q_norm_g': 'grad_w', 'grad_mla_k_norm_g': 'grad_w', 'grad_conv_w': 'grad_w', 'grad_mem_norm_g': 'grad_w', 'grad_w_mem_kv': 'grad_w', 'grad_xattn_q_norm_g': 'grad_w', 'grad_xattn_k_norm_g': 'grad_w', 'grad_w_branch': 'grad_w', 'grad_w_out': 'grad_w', 'delta_norm_g': 'delta_w', 'delta_w_in': 'delta_w', 'delta_gate_b': 'delta_w', 'delta_pool_w': 'delta_w', 'delta_pool_scale': 'delta_w', 'delta_q_a_norm_g': 'delta_w', 'delta_kv_a_norm_g': 'delta_w', 'delta_w_uq': 'delta_w', 'delta_w_ukv': 'delta_w', 'delta_mla_q_norm_g': 'delta_w', 'delta_mla_k_norm_g': 'delta_w', 'delta_conv_w': 'delta_w', 'delta_mem_norm_g': 'delta_w', 'delta_w_mem_kv': 'delta_w', 'delta_xattn_q_norm_g': 'delta_w', 'delta_xattn_k_norm_g': 'delta_w', 'delta_w_branch': 'delta_w', 'delta_w_out': 'delta_w', 'new_m_norm_g': 'new_m', 'new_m_w_in': 'new_m', 'new_m_gate_b': 'new_m', 'new_m_pool_w': 'new_m', 'new_m_pool_scale': 'new_m', 'new_m_q_a_norm_g': 'new_m', 'new_m_kv_a_norm_g': 'new_m', 'new_m_w_uq': 'new_m', 'new_m_w_ukv': 'new_m', 'new_m_mla_q_norm_g': 'new_m', 'new_m_mla_k_norm_g': 'new_m', 'new_m_conv_w': 'new_m', 'new_m_mem_norm_g': 'new_m', 'new_m_w_mem_kv': 'new_m', 'new_m_xattn_q_norm_g': 'new_m', 'new_m_xattn_k_norm_g': 'new_m', 'new_m_w_branch': 'new_m', 'new_m_w_out': 'new_m', 'new_v_norm_g': 'new_v', 'new_v_w_in': 'new_v', 'new_v_gate_b': 'new_v', 'new_v_pool_w': 'new_v', 'new_v_pool_scale': 'new_v', 'new_v_q_a_norm_g': 'new_v', 'new_v_kv_a_norm_g': 'new_v', 'new_v_w_uq': 'new_v', 'new_v_w_ukv': 'new_v', 'new_v_mla_q_norm_g': 'new_v', 'new_v_mla_k_norm_g': 'new_v', 'new_v_conv_w': 'new_v', 'new_v_mem_norm_g': 'new_v', 'new_v_w_mem_kv': 'new_v', 'new_v_xattn_q_norm_g': 'new_v', 'new_v_xattn_k_norm_g': 'new_v', 'new_v_w_branch': 'new_v', 'new_v_w_out': 'new_v'}


def _forward(args):
    return _fwd_reference(*[args[k] for k in FWD_PARAMS])


def _output_shape():
    def fwd():
        inp = _fwd_setup_inputs(0)
        return _fwd_reference(*[inp[k] for k in FWD_PARAMS])
    out = _jax.eval_shape(fwd)
    return out.shape, out.dtype

N_MICROBATCH = 1
ADAM_LR = 0.001
ADAM_B1 = 0.9
ADAM_B2 = 0.999
ADAM_EPS = 1e-08
ADAM_WD = 0.01
ADAM_STEP = 10
PER_EXAMPLE_BATCH_AXIS = {'x': 0, 'mem': 0, 'positions': 0, 'loss_target': 0}
SHARED_INPUTS = []
_WEIGHT_DTYPES = {'norm_g': _jnp.float32, 'w_in': _jnp.float32, 'gate_b': _jnp.float32, 'pool_w': _jnp.float32, 'pool_scale': _jnp.float32, 'q_a_norm_g': _jnp.float32, 'kv_a_norm_g': _jnp.float32, 'w_uq': _jnp.float32, 'w_ukv': _jnp.float32, 'mla_q_norm_g': _jnp.float32, 'mla_k_norm_g': _jnp.float32, 'conv_w': _jnp.float32, 'mem_norm_g': _jnp.float32, 'w_mem_kv': _jnp.float32, 'xattn_q_norm_g': _jnp.float32, 'xattn_k_norm_g': _jnp.float32, 'w_branch': _jnp.float32, 'w_out': _jnp.float32}
MOMENT_SCALE = {'norm_g': 1.969443e+01, 'w_in': 1.488601e-01, 'gate_b': 7.201974e-01, 'pool_w': 2.483820e-01, 'pool_scale': 5.189003e+00, 'q_a_norm_g': 3.833701e-02, 'kv_a_norm_g': 1.240543e-01, 'w_uq': 2.210001e-02, 'w_ukv': 2.781009e-02, 'mla_q_norm_g': 1.793540e-01, 'mla_k_norm_g': 1.800738e-01, 'conv_w': 3.563691e+00, 'mem_norm_g': 2.682343e-02, 'w_mem_kv': 1.546074e-02, 'xattn_q_norm_g': 2.417151e-01, 'xattn_k_norm_g': 2.401612e-01, 'w_branch': 8.649414e-02, 'w_out': 1.644157e-01}


def _to_microbatches(a, axis):
    t = _jnp.moveaxis(a, axis, 0)
    t = t.reshape((N_MICROBATCH, t.shape[0] // N_MICROBATCH) + t.shape[1:])
    return _jnp.moveaxis(t, 1, axis + 1)


def setup_inputs(seed: int = 0) -> dict:
    inp = _fwd_setup_inputs(seed)
    key = _jax.random.fold_in(_jax.random.key(seed), 7919)
    shape, _ = _output_shape()
    out = dict(inp)
    out["loss_target"] = _jax.random.normal(_jax.random.fold_in(key, 0), shape, _jnp.float32)
    for i, name in enumerate(TWIN_WEIGHTS):
        w = inp[name].astype(_jnp.float32)
        if MOMENT_SCALE is None:
            s = _jnp.sqrt(_jnp.mean(_jnp.square(w)) + 1e-30)
        else:
            s = MOMENT_SCALE[name]
        km, kv = _jax.random.split(_jax.random.fold_in(key, i + 1))
        out[name] = w
        out["m_" + name] = s * _jax.random.normal(km, w.shape, _jnp.float32)
        out["v_" + name] = (s * s) * _jax.random.uniform(kv, w.shape, _jnp.float32, 0.5, 1.5)
    if N_MICROBATCH > 1:
        for name, axis in PER_EXAMPLE_BATCH_AXIS.items():
            out[name] = _to_microbatches(out[name], axis)
    return {'x': out['x'], 'mem': out['mem'], 'positions': out['positions'], 'norm_g': out['norm_g'], 'w_in': out['w_in'], 'gate_b': out['gate_b'], 'pool_w': out['pool_w'], 'pool_scale': out['pool_scale'], 'q_a_norm_g': out['q_a_norm_g'], 'kv_a_norm_g': out['kv_a_norm_g'], 'w_uq': out['w_uq'], 'w_ukv': out['w_ukv'], 'mla_q_norm_g': out['mla_q_norm_g'], 'mla_k_norm_g': out['mla_k_norm_g'], 'conv_w': out['conv_w'], 'mem_norm_g': out['mem_norm_g'], 'w_mem_kv': out['w_mem_kv'], 'xattn_q_norm_g': out['xattn_q_norm_g'], 'xattn_k_norm_g': out['xattn_k_norm_g'], 'w_branch': out['w_branch'], 'w_out': out['w_out'], 'loss_target': out['loss_target'], 'm_norm_g': out['m_norm_g'], 'm_w_in': out['m_w_in'], 'm_gate_b': out['m_gate_b'], 'm_pool_w': out['m_pool_w'], 'm_pool_scale': out['m_pool_scale'], 'm_q_a_norm_g': out['m_q_a_norm_g'], 'm_kv_a_norm_g': out['m_kv_a_norm_g'], 'm_w_uq': out['m_w_uq'], 'm_w_ukv': out['m_w_ukv'], 'm_mla_q_norm_g': out['m_mla_q_norm_g'], 'm_mla_k_norm_g': out['m_mla_k_norm_g'], 'm_conv_w': out['m_conv_w'], 'm_mem_norm_g': out['m_mem_norm_g'], 'm_w_mem_kv': out['m_w_mem_kv'], 'm_xattn_q_norm_g': out['m_xattn_q_norm_g'], 'm_xattn_k_norm_g': out['m_xattn_k_norm_g'], 'm_w_branch': out['m_w_branch'], 'm_w_out': out['m_w_out'], 'v_norm_g': out['v_norm_g'], 'v_w_in': out['v_w_in'], 'v_gate_b': out['v_gate_b'], 'v_pool_w': out['v_pool_w'], 'v_pool_scale': out['v_pool_scale'], 'v_q_a_norm_g': out['v_q_a_norm_g'], 'v_kv_a_norm_g': out['v_kv_a_norm_g'], 'v_w_uq': out['v_w_uq'], 'v_w_ukv': out['v_w_ukv'], 'v_mla_q_norm_g': out['v_mla_q_norm_g'], 'v_mla_k_norm_g': out['v_mla_k_norm_g'], 'v_conv_w': out['v_conv_w'], 'v_mem_norm_g': out['v_mem_norm_g'], 'v_w_mem_kv': out['v_w_mem_kv'], 'v_xattn_q_norm_g': out['v_xattn_q_norm_g'], 'v_xattn_k_norm_g': out['v_xattn_k_norm_g'], 'v_w_branch': out['v_w_branch'], 'v_w_out': out['v_w_out']}


def _loss(weights, diff, rest, loss_target):
    with _jax.named_scope("forward"):
        args = {**rest, TWIN_DIFF_INPUT: diff, **{k: w.astype(_WEIGHT_DTYPES[k]) for k, w in weights.items()}}
        y = _forward(args)
    with _jax.named_scope("loss_head"):
        err = _jnp.square(y.astype(_jnp.float32) - loss_target)
        return 0.5 * _jnp.sum(_jnp.mean(err, axis=-1)) if err.ndim else 0.5 * err


def _adamw(w, g, m, v):
    m = ADAM_B1 * m + (1.0 - ADAM_B1) * g
    v = ADAM_B2 * v + (1.0 - ADAM_B2) * _jnp.square(g)
    m_hat = m / (1.0 - ADAM_B1 ** ADAM_STEP)
    v_hat = v / (1.0 - ADAM_B2 ** ADAM_STEP)
    delta = -ADAM_LR * (m_hat / (_jnp.sqrt(v_hat) + ADAM_EPS) + ADAM_WD * w)
    return delta, m, v


def reference(x, mem, positions, norm_g, w_in, gate_b, pool_w, pool_scale, q_a_norm_g, kv_a_norm_g, w_uq, w_ukv, mla_q_norm_g, mla_k_norm_g, conv_w, mem_norm_g, w_mem_kv, xattn_q_norm_g, xattn_k_norm_g, w_branch, w_out, loss_target, m_norm_g, m_w_in, m_gate_b, m_pool_w, m_pool_scale, m_q_a_norm_g, m_kv_a_norm_g, m_w_uq, m_w_ukv, m_mla_q_norm_g, m_mla_k_norm_g, m_conv_w, m_mem_norm_g, m_w_mem_kv, m_xattn_q_norm_g, m_xattn_k_norm_g, m_w_branch, m_w_out, v_norm_g, v_w_in, v_gate_b, v_pool_w, v_pool_scale, v_q_a_norm_g, v_kv_a_norm_g, v_w_uq, v_w_ukv, v_mla_q_norm_g, v_mla_k_norm_g, v_conv_w, v_mem_norm_g, v_w_mem_kv, v_xattn_q_norm_g, v_xattn_k_norm_g, v_w_branch, v_w_out):
    given = dict(x=x, mem=mem, positions=positions, norm_g=norm_g, w_in=w_in, gate_b=gate_b, pool_w=pool_w, pool_scale=pool_scale, q_a_norm_g=q_a_norm_g, kv_a_norm_g=kv_a_norm_g, w_uq=w_uq, w_ukv=w_ukv, mla_q_norm_g=mla_q_norm_g, mla_k_norm_g=mla_k_norm_g, conv_w=conv_w, mem_norm_g=mem_norm_g, w_mem_kv=w_mem_kv, xattn_q_norm_g=xattn_q_norm_g, xattn_k_norm_g=xattn_k_norm_g, w_branch=w_branch, w_out=w_out, loss_target=loss_target, m_norm_g=m_norm_g, m_w_in=m_w_in, m_gate_b=m_gate_b, m_pool_w=m_pool_w, m_pool_scale=m_pool_scale, m_q_a_norm_g=m_q_a_norm_g, m_kv_a_norm_g=m_kv_a_norm_g, m_w_uq=m_w_uq, m_w_ukv=m_w_ukv, m_mla_q_norm_g=m_mla_q_norm_g, m_mla_k_norm_g=m_mla_k_norm_g, m_conv_w=m_conv_w, m_mem_norm_g=m_mem_norm_g, m_w_mem_kv=m_w_mem_kv, m_xattn_q_norm_g=m_xattn_q_norm_g, m_xattn_k_norm_g=m_xattn_k_norm_g, m_w_branch=m_w_branch, m_w_out=m_w_out, v_norm_g=v_norm_g, v_w_in=v_w_in, v_gate_b=v_gate_b, v_pool_w=v_pool_w, v_pool_scale=v_pool_scale, v_q_a_norm_g=v_q_a_norm_g, v_kv_a_norm_g=v_kv_a_norm_g, v_w_uq=v_w_uq, v_w_ukv=v_w_ukv, v_mla_q_norm_g=v_mla_q_norm_g, v_mla_k_norm_g=v_mla_k_norm_g, v_conv_w=v_conv_w, v_mem_norm_g=v_mem_norm_g, v_w_mem_kv=v_w_mem_kv, v_xattn_q_norm_g=v_xattn_q_norm_g, v_xattn_k_norm_g=v_xattn_k_norm_g, v_w_branch=v_w_branch, v_w_out=v_w_out)
    weights = {n: given[n] for n in TWIN_WEIGHTS}
    shared = {n: given[n] for n in SHARED_INPUTS}
    per_example = {n: given[n] for n in ['x', 'mem', 'positions']}
    grad_fn = _jax.value_and_grad(_loss, argnums=(0, 1))

    def one_microbatch(ex, loss_target):
        ex = dict(ex)
        diff = ex.pop(TWIN_DIFF_INPUT)
        return grad_fn(weights, diff, {**shared, **ex}, loss_target)

    if N_MICROBATCH == 1:
        loss, (grad_w, grad_x) = one_microbatch(per_example, given["loss_target"])
    else:
        def body(carry, xs):
            loss_sum, grad_sum = carry
            l_k, (gw_k, gx_k) = one_microbatch(xs[0], xs[1])
            with _jax.named_scope("update"):
                return (loss_sum + l_k, _jax.tree.map(_jnp.add, grad_sum, gw_k)), gx_k

        init = (_jnp.zeros((), _jnp.float32), _jax.tree.map(_jnp.zeros_like, weights))
        (loss, grad_w), grad_x = _jax.lax.scan(body, init, (per_example, given["loss_target"]))
    with _jax.named_scope("update"):
        delta_w, new_m, new_v = {}, {}, {}
        for n in TWIN_WEIGHTS:
            delta_w[n], new_m[n], new_v[n] = _adamw(weights[n], grad_w[n], given["m_" + n], given["v_" + n])
    return (loss, grad_x, *[grad_w[n] for n in TWIN_WEIGHTS], *[delta_w[n] for n in TWIN_WEIGHTS],
            *[new_m[n] for n in TWIN_WEIGHTS], *[new_v[n] for n in TWIN_WEIGHTS])
```

```python
import functools

import jax
import jax.numpy as jnp
from jax import lax
from jax.experimental import pallas as pl
from jax.experimental.pallas import tpu as pltpu

F32 = jnp.float32
BF16 = jnp.bfloat16
MESH = pl.DeviceIdType.MESH

EPS = 1e-6
N_BRANCH = 4
BRANCH_W = 1024
POOL_GROUPS = 4
POOL_GW = BRANCH_W // POOL_GROUPS
POOL_HALO = 16
CONV_HALO = 8
MLA_HEADS = 8
QK_NOPE = 128
QK_ROPE = 64
QK_HEAD = QK_NOPE + QK_ROPE
HEAD_PAD = 256
V_HEAD = 128
ROPE_THETA = 10000.0
XATTN_HEADS = 4
XATTN_HEAD_DIM = BRANCH_W // XATTN_HEADS
ADAM_LR, ADAM_B1, ADAM_B2, ADAM_EPS, ADAM_WD, ADAM_STEP = 0.001, 0.9, 0.999, 1e-08, 0.01, 10
NEG = -1e30
VMEM_LIMIT = 48 * 1024 * 1024

SHARDED = ("w_in", "pool_w", "w_uq", "w_ukv", "conv_w", "w_mem_kv", "w_branch", "w_out")
REPLICATED = ("norm_g", "gate_b", "pool_scale", "q_a_norm_g", "kv_a_norm_g", "mla_q_norm_g", "mla_k_norm_g",
              "mem_norm_g", "xattn_q_norm_g", "xattn_k_norm_g")
WEIGHTS = ("norm_g", "w_in", "gate_b", "pool_w", "pool_scale", "q_a_norm_g", "kv_a_norm_g", "w_uq", "w_ukv",
           "mla_q_norm_g", "mla_k_norm_g", "conv_w", "mem_norm_g", "w_mem_kv", "xattn_q_norm_g", "xattn_k_norm_g",
           "w_branch", "w_out")
SHARD_AXIS = {"w_in": 1, "pool_w": 1, "w_uq": 1, "w_ukv": 1, "conv_w": 1, "w_mem_kv": 0, "w_branch": 2, "w_out": 0}


def _cparams(sem=None):
    return pltpu.CompilerParams(dimension_semantics=sem, vmem_limit_bytes=VMEM_LIMIT)


def _tile(n, pref, unit=128):
    if n <= pref:
        return n
    t = (pref // unit) * unit
    while t >= unit:
        if n % t == 0:
            return t
        t -= unit
    return n


def _silu(z):
    return z * jax.nn.sigmoid(z)


def _dsilu(z):
    s = jax.nn.sigmoid(z)
    return s * (1.0 + z * (1.0 - s))


def _dot(a, b):
    return jnp.dot(a, b, preferred_element_type=F32)


def _dot_nt(a, b):
    return lax.dot_general(a, b, (((1,), (1,)), ((), ())), preferred_element_type=F32)


def _dot_tn(a, b):
    return lax.dot_general(a, b, (((0,), (0,)), ((), ())), preferred_element_type=F32)


def _rms(x, g, n):
    r = lax.rsqrt(jnp.sum(x * x, axis=-1, keepdims=True) * (1.0 / n) + EPS)
    return x * r * g


def _rms_bwd(x, g, n, dout):
    r = lax.rsqrt(jnp.sum(x * x, axis=-1, keepdims=True) * (1.0 / n) + EPS)
    y = x * r
    dy = dout * g
    dx = r * (dy - y * (jnp.sum(dy * y, axis=-1, keepdims=True) * (1.0 / n)))
    return dx, dout * y


def _rope(x, ctab, stab):
    lane = lax.broadcasted_iota(jnp.int32, x.shape, 1)
    partner = jnp.where(lane < QK_NOPE + QK_ROPE // 2, pltpu.roll(x, HEAD_PAD - QK_ROPE // 2, 1),
                        pltpu.roll(x, QK_ROPE // 2, 1))
    return x * ctab + partner * stab


def _rope_bwd(d, ctab, stab):
    lane = lax.broadcasted_iota(jnp.int32, d.shape, 1)
    ds = d * stab
    partner = jnp.where(lane < QK_NOPE + QK_ROPE // 2, pltpu.roll(ds, HEAD_PAD - QK_ROPE // 2, 1),
                        pltpu.roll(ds, QK_ROPE // 2, 1))
    return d * ctab + jnp.where((lane >= QK_NOPE) & (lane < QK_HEAD), partner, 0.0)


def _mm(a, b, *, name, trans_b=False, add=None, out_dtype=F32, tm=512, tn=1024, tk=2048):
    batched = a.ndim == 3
    if batched:
        nb, m, k = a.shape
    else:
        m, k = a.shape
    n = b.shape[-2] if trans_b else b.shape[-1]
    tm, tn, tk = _tile(m, tm, 8), _tile(n, tn), _tile(k, tk)
    nk = k // tk

    def body(*refs):
        if add is None:
            a_ref, b_ref, o_ref = refs[:3]
            add_ref = None
            rest = refs[3:]
        else:
            a_ref, b_ref, add_ref, o_ref = refs[:4]
            rest = refs[4:]
        av = a_ref[...].astype(BF16)
        bv = b_ref[...].astype(BF16)
        part = _dot_nt(av, bv) if trans_b else _dot(av, bv)

        def finish(acc):
            if add_ref is not None:
                acc = acc + add_ref[...]
            o_ref[...] = acc.astype(o_ref.dtype)

        if nk == 1:
            finish(part)
        else:
            acc_ref = rest[0]
            kk = pl.program_id(3 if batched else 2)

            @pl.when(kk == 0)
            def _():
                acc_ref[...] = part

            @pl.when(kk > 0)
            def _():
                acc_ref[...] += part

            @pl.when(kk == nk - 1)
            def _():
                finish(acc_ref[...])

    if batched:
        a_spec = pl.BlockSpec((None, tm, tk), lambda bb, i, j, kk: (bb, i, kk))
        b_spec = (pl.BlockSpec((None, tn, tk), lambda bb, i, j, kk: (bb, j, kk)) if trans_b
                  else pl.BlockSpec((None, tk, tn), lambda bb, i, j, kk: (bb, kk, j)))
        o_spec = pl.BlockSpec((None, tm, tn), lambda bb, i, j, kk: (bb, i, j))
        grid = (nb, m // tm, n // tn, nk)
        out_shape = jax.ShapeDtypeStruct((nb, m, n), out_dtype)
        sem = ("parallel", "parallel", "parallel", "arbitrary")
    else:
        a_spec = pl.BlockSpec((tm, tk), lambda i, j, kk: (i, kk))
        b_spec = (pl.BlockSpec((tn, tk), lambda i, j, kk: (j, kk)) if trans_b
                  else pl.BlockSpec((tk, tn), lambda i, j, kk: (kk, j)))
        o_spec = pl.BlockSpec((tm, tn), lambda i, j, kk: (i, j))
        grid = (m // tm, n // tn, nk)
        out_shape = jax.ShapeDtypeStruct((m, n), out_dtype)
        sem = ("parallel", "parallel", "arbitrary")
    in_specs = [a_spec, b_spec] + ([o_spec] if add is not None else [])
    args = (a, b) + ((add,) if add is not None else ())
    scratch = [pltpu.VMEM((tm, tn), F32)] if nk > 1 else []
    return pl.pallas_call(body, out_shape=out_shape, grid=grid, in_specs=in_specs, out_specs=o_spec,
                          scratch_shapes=scratch, compiler_params=_cparams(sem), name=name)(*args)


def _norm_fwd(x, g, *, name, t=256):
    s, d = x.shape
    t = _tile(s, t, 128)

    def body(x_ref, g_ref, h_ref, ht_ref):
        h = _rms(x_ref[...], g_ref[...], d)
        h_ref[...] = h.astype(BF16)
        ht_ref[...] = h.T.astype(BF16)

    return pl.pallas_call(
        body, out_shape=(jax.ShapeDtypeStruct((s, d), BF16), jax.ShapeDtypeStruct((d, s), BF16)),
        grid=(s // t,),
        in_specs=[pl.BlockSpec((t, d), lambda i: (i, 0)), pl.BlockSpec((1, d), lambda i: (0, 0))],
        out_specs=(pl.BlockSpec((t, d), lambda i: (i, 0)), pl.BlockSpec((d, t), lambda i: (0, i))),
        compiler_params=_cparams(("parallel",)), name=name)(x, g)


def _norm_bwd(x, g, dh, dres, *, name, t=256):
    s, d = x.shape
    t = _tile(s, t, 8)

    def body(x_ref, g_ref, dh_ref, dres_ref, dx_ref, dg_ref):
        dx, dgt = _rms_bwd(x_ref[...], g_ref[...], d, dh_ref[...])
        dx_ref[...] = dx + dres_ref[...]
        part = jnp.sum(dgt, axis=0, keepdims=True)

        @pl.when(pl.program_id(0) == 0)
        def _():
            dg_ref[...] = part

        @pl.when(pl.program_id(0) > 0)
        def _():
            dg_ref[...] += part

    row = pl.BlockSpec((t, d), lambda i: (i, 0))
    vec = pl.BlockSpec((1, d), lambda i: (0, 0))
    return pl.pallas_call(
        body, out_shape=(jax.ShapeDtypeStruct((s, d), F32), jax.ShapeDtypeStruct((1, d), F32)),
        grid=(s // t,), in_specs=[row, vec, row, row], out_specs=(row, vec),
        compiler_params=_cparams(("arbitrary",)), name=name)(x, g, dh, dres)


def _loss_head(y, tgt, *, t=256):
    s, d = y.shape
    t = _tile(s, t, 8)

    def body(y_ref, t_ref, dy_ref, l_ref):
        e = y_ref[...] - t_ref[...]
        dy_ref[...] = e * (1.0 / d)
        part = jnp.zeros((1, 128), F32) + jnp.sum(e * e) * (0.5 / d)

        @pl.when(pl.program_id(0) == 0)
        def _():
            l_ref[...] = part

        @pl.when(pl.program_id(0) > 0)
        def _():
            l_ref[...] += part

    row = pl.BlockSpec((t, d), lambda i: (i, 0))
    return pl.pallas_call(
        body, out_shape=(jax.ShapeDtypeStruct((s, d), F32), jax.ShapeDtypeStruct((1, 128), F32)),
        grid=(s // t,), in_specs=[row, row], out_specs=(row, pl.BlockSpec((1, 128), lambda i: (0, 0))),
        compiler_params=_cparams(("arbitrary",)), name="loss_head")(y, tgt)


def _pool_mixed(scr, v, halo, first, row0, t):
    scr[0:POOL_HALO, :] = jnp.where(first, 0.0, halo)
    scr[POOL_HALO:POOL_HALO + t, :] = v
    row = row0 + lax.broadcasted_iota(jnp.int32, (t, 1), 0)
    mixed = []
    for g in range(POOL_GROUPS):
        w = 2 ** (g + 1)
        acc = scr[:, g * POOL_GW:(g + 1) * POOL_GW]
        sh = 1
        while sh < w:
            acc = acc + pltpu.roll(acc, sh, 0)
            sh *= 2
        cnt = jnp.minimum(row + 1, w).astype(F32)
        mixed.append(acc[POOL_HALO:POOL_HALO + t, :] / cnt - v[:, g * POOL_GW:(g + 1) * POOL_GW])
    return mixed


def _pool_fwd(p1, pool_w, pool_scale, *, t=256):
    s = p1.shape[0]
    t = _tile(s, t, 128)
    hb = t // POOL_HALO

    def body(pv_ref, halo_ref, pz_ref, pw_ref, sc_ref, y_ref, yt_ref, scr):
        i = pl.program_id(0)
        mixed = _pool_mixed(scr, pv_ref[...], halo_ref[...], i == 0, i * t, t)
        outs = [_dot(mixed[g].astype(BF16), pw_ref[g]) for g in range(POOL_GROUPS)]
        y = jnp.concatenate(outs, axis=1) * sc_ref[...] * _silu(pz_ref[...])
        y_ref[...] = y.astype(BF16)
        yt_ref[...] = y.T.astype(BF16)

    return pl.pallas_call(
        body, out_shape=(jax.ShapeDtypeStruct((s, BRANCH_W), BF16), jax.ShapeDtypeStruct((BRANCH_W, s), BF16)),
        grid=(s // t,),
        in_specs=[pl.BlockSpec((t, BRANCH_W), lambda i: (i, 0)),
                  pl.BlockSpec((POOL_HALO, BRANCH_W), lambda i: (jnp.maximum(i * hb - 1, 0), 0)),
                  pl.BlockSpec((t, BRANCH_W), lambda i: (i, 1)),
                  pl.BlockSpec((POOL_GROUPS, POOL_GW, POOL_GW), lambda i: (0, 0, 0)),
                  pl.BlockSpec((1, BRANCH_W), lambda i: (0, 0))],
        out_specs=(pl.BlockSpec((t, BRANCH_W), lambda i: (i, 0)), pl.BlockSpec((BRANCH_W, t), lambda i: (0, i))),
        scratch_shapes=[pltpu.VMEM((t + POOL_HALO, BRANCH_W), F32)],
        compiler_params=_cparams(("parallel",)), name="pool_fwd")(p1, p1, p1, pool_w, pool_scale)


def _pool_bwd(p1, dy, pool_w, pool_scale, *, t=256):
    s = p1.shape[0]
    t = _tile(s, t, 128)
    hb = t // POOL_HALO
    nt = s // t
    last_hb = s // POOL_HALO - 1

    def body(pv_ref, halo_ref, pz_ref, pzn_ref, dy_ref, dyn_ref, pw_ref, sc_ref, d_ref, gw_ref, gs_ref, scr, scr2, scr3):
        i = pl.program_id(0)
        mixed = _pool_mixed(scr, pv_ref[...], halo_ref[...], i == 0, i * t, t)
        scale = sc_ref[...]
        pz = pz_ref[...]
        dy = dy_ref[...]
        raw = jnp.concatenate([_dot(mixed[g].astype(BF16), pw_ref[g]) for g in range(POOL_GROUPS)], axis=1)
        d_pool = dy * _silu(pz)
        d_ref[:, BRANCH_W:2 * BRANCH_W] = (dy * raw * scale * _dsilu(pz)).astype(BF16)
        gs_part = jnp.sum(d_pool * raw, axis=0, keepdims=True)
        scr2[0:t, :] = d_pool * scale
        scr2[t:t + POOL_HALO, :] = jnp.where(i == nt - 1, 0.0, dyn_ref[...] * _silu(pzn_ref[...]) * scale)
        row = i * t + lax.broadcasted_iota(jnp.int32, (t + POOL_HALO, 1), 0)
        gw_parts = []
        for g in range(POOL_GROUPS):
            w = 2 ** (g + 1)
            sl = slice(g * POOL_GW, (g + 1) * POOL_GW)
            do_g = scr2[:, sl].astype(BF16)
            dm = _dot_nt(do_g, pw_ref[g])
            gw_parts.append(_dot_tn(mixed[g].astype(BF16), do_g[0:t, :]))
            cnt = jnp.minimum(row + 1, w).astype(F32)
            acc = dm / cnt
            sh = 1
            while sh < w:
                acc = acc + pltpu.roll(acc, t + POOL_HALO - sh, 0)
                sh *= 2
            scr3[:, sl] = acc - dm
        d_ref[:, 0:BRANCH_W] = scr3[0:t, :].astype(BF16)

        @pl.when(i == 0)
        def _():
            for g in range(POOL_GROUPS):
                gw_ref[g] = gw_parts[g]
            gs_ref[...] = gs_part

        @pl.when(i > 0)
        def _():
            for g in range(POOL_GROUPS):
                gw_ref[g] += gw_parts[g]
            gs_ref[...] += gs_part

    tile = lambda col: pl.BlockSpec((t, BRANCH_W), lambda i: (i, col))
    nxt = lambda col: pl.BlockSpec((POOL_HALO, BRANCH_W), lambda i: (jnp.minimum((i + 1) * hb, last_hb), col))
    return pl.pallas_call(
        body,
        out_shape=(jax.ShapeDtypeStruct((s, 2 * BRANCH_W), BF16),
                   jax.ShapeDtypeStruct((POOL_GROUPS, POOL_GW, POOL_GW), F32),
                   jax.ShapeDtypeStruct((1, BRANCH_W), F32)),
        grid=(nt,),
        in_specs=[tile(0), pl.BlockSpec((POOL_HALO, BRANCH_W), lambda i: (jnp.maximum(i * hb - 1, 0), 0)),
                  tile(1), nxt(1), tile(0), nxt(0),
                  pl.BlockSpec((POOL_GROUPS, POOL_GW, POOL_GW), lambda i: (0, 0, 0)),
                  pl.BlockSpec((1, BRANCH_W), lambda i: (0, 0))],
        out_specs=(pl.BlockSpec((t, 2 * BRANCH_W), lambda i: (i, 0)),
                   pl.BlockSpec((POOL_GROUPS, POOL_GW, POOL_GW), lambda i: (0, 0, 0)),
                   pl.BlockSpec((1, BRANCH_W), lambda i: (0, 0))),
        scratch_shapes=[pltpu.VMEM((t + POOL_HALO, BRANCH_W), F32)] * 3,
        compiler_params=_cparams(("arbitrary",)), name="pool_bwd")(p1, p1, p1, p1, dy, dy, pool_w, pool_scale)


CONV_COL = 4


def _conv_taps(scr, u, uh, first, t):
    scr[0:CONV_HALO, :] = jnp.where(first, 0.0, uh)
    scr[CONV_HALO:CONV_HALO + t, :] = u
    e = scr[...]
    u1 = pltpu.roll(e, 1, 0)[CONV_HALO:CONV_HALO + t, :]
    u2 = pltpu.roll(e, 2, 0)[CONV_HALO:CONV_HALO + t, :]
    return u2, u1, u


def _conv_fwd(p1, conv_w, *, t=256):
    s = p1.shape[0]
    t = _tile(s, t, 128)
    hb = t // CONV_HALO

    def body(cb_ref, cc_ref, cx_ref, cz_ref, cch_ref, cxh_ref, w_ref, y_ref, yt_ref, scr):
        i = pl.program_id(0)
        u0, u1, u2 = _conv_taps(scr, cc_ref[...] * cx_ref[...], cch_ref[...] * cxh_ref[...], i == 0, t)
        w = w_ref[...]
        y = (w[0:1, :] * u0 + w[1:2, :] * u1 + w[2:3, :] * u2) * cb_ref[...] * _silu(cz_ref[...])
        y_ref[...] = y.astype(BF16)
        yt_ref[...] = y.T.astype(BF16)

    tile = lambda col: pl.BlockSpec((t, BRANCH_W), lambda i: (i, CONV_COL + col))
    prev = lambda col: pl.BlockSpec((CONV_HALO, BRANCH_W), lambda i: (jnp.maximum(i * hb - 1, 0), CONV_COL + col))
    return pl.pallas_call(
        body, out_shape=(jax.ShapeDtypeStruct((s, BRANCH_W), BF16), jax.ShapeDtypeStruct((BRANCH_W, s), BF16)),
        grid=(s // t,),
        in_specs=[tile(0), tile(1), tile(2), tile(3), prev(1), prev(2), pl.BlockSpec((3, BRANCH_W), lambda i: (0, 0))],
        out_specs=(pl.BlockSpec((t, BRANCH_W), lambda i: (i, 0)), pl.BlockSpec((BRANCH_W, t), lambda i: (0, i))),
        scratch_shapes=[pltpu.VMEM((t + CONV_HALO, BRANCH_W), F32)],
        compiler_params=_cparams(("parallel",)), name="conv_fwd")(p1, p1, p1, p1, p1, p1, conv_w)


def _conv_bwd(p1, dy, conv_w, *, t=256):
    s = p1.shape[0]
    t = _tile(s, t, 128)
    hb = t // CONV_HALO
    nt = s // t
    last_hb = s // CONV_HALO - 1

    def body(cb_ref, cc_ref, cx_ref, cz_ref, cch_ref, cxh_ref, cbn_ref, czn_ref, dy_ref, dyn_ref, w_ref,
             d_ref, g0_ref, g1_ref, g2_ref, scr, scr2):
        i = pl.program_id(0)
        cb, cc, cx, cz = cb_ref[...], cc_ref[...], cx_ref[...], cz_ref[...]
        u0, u1, u2 = _conv_taps(scr, cc * cx, cch_ref[...] * cxh_ref[...], i == 0, t)
        w = w_ref[...]
        w0, w1, w2 = w[0:1, :], w[1:2, :], w[2:3, :]
        y = w0 * u0 + w1 * u1 + w2 * u2
        dy = dy_ref[...]
        sz = _silu(cz)
        d_ref[:, 0:BRANCH_W] = (dy * sz * y).astype(BF16)
        d_ref[:, 3 * BRANCH_W:4 * BRANCH_W] = (dy * cb * y * _dsilu(cz)).astype(BF16)
        d_y = dy * sz * cb
        parts = [jnp.sum(d_y * u, axis=0, keepdims=True) for u in (u0, u1, u2)]
        scr2[0:t, :] = d_y
        scr2[t:t + CONV_HALO, :] = jnp.where(i == nt - 1, 0.0, dyn_ref[...] * _silu(czn_ref[...]) * cbn_ref[...])
        e = scr2[...]
        n = t + CONV_HALO
        du = (w2 * e + w1 * pltpu.roll(e, n - 1, 0) + w0 * pltpu.roll(e, n - 2, 0))[0:t, :]
        d_ref[:, BRANCH_W:2 * BRANCH_W] = (du * cx).astype(BF16)
        d_ref[:, 2 * BRANCH_W:3 * BRANCH_W] = (du * cc).astype(BF16)

        @pl.when(i == 0)
        def _():
            g0_ref[...] = parts[0]
            g1_ref[...] = parts[1]
            g2_ref[...] = parts[2]

        @pl.when(i > 0)
        def _():
            g0_ref[...] += parts[0]
            g1_ref[...] += parts[1]
            g2_ref[...] += parts[2]

    tile = lambda col: pl.BlockSpec((t, BRANCH_W), lambda i: (i, CONV_COL + col))
    prev = lambda col: pl.BlockSpec((CONV_HALO, BRANCH_W), lambda i: (jnp.maximum(i * hb - 1, 0), CONV_COL + col))
    nxt = lambda col: pl.BlockSpec((CONV_HALO, BRANCH_W), lambda i: (jnp.minimum((i + 1) * hb, last_hb), CONV_COL + col))
    vec = pl.BlockSpec((1, BRANCH_W), lambda i: (0, 0))
    gshape = jax.ShapeDtypeStruct((1, BRANCH_W), F32)
    return pl.pallas_call(
        body, out_shape=(jax.ShapeDtypeStruct((s, 4 * BRANCH_W), BF16), gshape, gshape, gshape),
        grid=(nt,),
        in_specs=[tile(0), tile(1), tile(2), tile(3), prev(1), prev(2), nxt(0), nxt(3),
                  pl.BlockSpec((t, BRANCH_W), lambda i: (i, 0)),
                  pl.BlockSpec((CONV_HALO, BRANCH_W), lambda i: (jnp.minimum((i + 1) * hb, last_hb), 0)),
                  pl.BlockSpec((3, BRANCH_W), lambda i: (0, 0))],
        out_specs=(pl.BlockSpec((t, 4 * BRANCH_W), lambda i: (i, 0)), vec, vec, vec),
        scratch_shapes=[pltpu.VMEM((t + CONV_HALO, BRANCH_W), F32)] * 2,
        compiler_params=_cparams(("arbitrary",)), name="conv_bwd")(p1, p1, p1, p1, p1, p1, p1, p1, dy, dy, conv_w)


def _mla_prep_fwd(p1, kr, ctab, stab, qa_g, kva_g, w_uq, w_ukv, q_g, k_g, *, t=256):
    s = p1.shape[0]
    t = _tile(s, t, 128)
    ql, kvl = qa_g.shape[1], kva_g.shape[1]
    assert ql == kvl and 2048 % ql == 0
    cq_blk = 2048 // ql

    def body(cq_ref, ckv_ref, kr_ref, c_ref, s_ref, qag_ref, kvag_ref, wuq_ref, wukv_ref, qg_ref, kg_ref,
             qf_ref, kf_ref, v_ref):
        q_raw = _dot(_rms(cq_ref[...], qag_ref[...], ql).astype(BF16), wuq_ref[...])
        kv_raw = _dot(_rms(ckv_ref[...], kvag_ref[...], kvl).astype(BF16), wukv_ref[...])
        krp = kr_ref[...]
        ct, st = c_ref[...], s_ref[...]
        for h in range(MLA_HEADS):
            qh = q_raw[:, h * HEAD_PAD:(h + 1) * HEAD_PAD]
            qf_ref[h] = _rope(_rms(qh, qg_ref[...], QK_HEAD), ct, st).astype(BF16)
            kh = jnp.concatenate([kv_raw[:, h * HEAD_PAD:h * HEAD_PAD + QK_NOPE], krp], axis=1)
            kf_ref[h] = _rope(_rms(kh, kg_ref[...], QK_HEAD), ct, st).astype(BF16)
            v_ref[h] = kv_raw[:, h * HEAD_PAD + QK_NOPE:(h + 1) * HEAD_PAD].astype(BF16)

    full = lambda shp: pl.BlockSpec(shp, lambda i: tuple(0 for _ in shp))
    return pl.pallas_call(
        body,
        out_shape=(jax.ShapeDtypeStruct((MLA_HEADS, s, HEAD_PAD), BF16), jax.ShapeDtypeStruct((MLA_HEADS, s, HEAD_PAD), BF16),
                   jax.ShapeDtypeStruct((MLA_HEADS, s, V_HEAD), BF16)),
        grid=(s // t,),
        in_specs=[pl.BlockSpec((t, ql), lambda i: (i, cq_blk)), pl.BlockSpec((t, kvl), lambda i: (i, cq_blk + 1)),
                  pl.BlockSpec((t, 128), lambda i: (i, 0)),
                  pl.BlockSpec((t, HEAD_PAD), lambda i: (i, 0)), pl.BlockSpec((t, HEAD_PAD), lambda i: (i, 0)),
                  full((1, ql)), full((1, kvl)), full(w_uq.shape), full(w_ukv.shape), full((1, HEAD_PAD)), full((1, HEAD_PAD))],
        out_specs=(pl.BlockSpec((MLA_HEADS, t, HEAD_PAD), lambda i: (0, i, 0)),
                   pl.BlockSpec((MLA_HEADS, t, HEAD_PAD), lambda i: (0, i, 0)),
                   pl.BlockSpec((MLA_HEADS, t, V_HEAD), lambda i: (0, i, 0))),
        compiler_params=_cparams(("parallel",)), name="mla_prep_fwd")(
            p1, p1, kr, ctab, stab, qa_g, kva_g, w_uq, w_ukv, q_g, k_g)


def _mla_prep_bwd(p1, kr, ctab, stab, qa_g, kva_g, w_uq, w_ukv, q_g, k_g, dqf, dkf, dv, *, t=256):
    s = p1.shape[0]
    t = _tile(s, t, 128)
    ql, kvl = qa_g.shape[1], kva_g.shape[1]
    cq_blk = 2048 // ql
    nq = MLA_HEADS * HEAD_PAD

    def body(cq_ref, ckv_ref, kr_ref, c_ref, s_ref, qag_ref, kvag_ref, wuq_ref, wukv_ref, qg_ref, kg_ref,
             dqf_ref, dkf_ref, dv_ref,
             dc_ref, dkr_ref, dqraw_ref, dkvraw_ref, cqnt_ref, ckvnt_ref, gqa_ref, gkva_ref, gqg_ref, gkg_ref):
        i = pl.program_id(0)
        cq, ckv = cq_ref[...], ckv_ref[...]
        cqn = _rms(cq, qag_ref[...], ql)
        ckvn = _rms(ckv, kvag_ref[...], kvl)
        cqnt_ref[...] = cqn.T.astype(BF16)
        ckvnt_ref[...] = ckvn.T.astype(BF16)
        q_raw = _dot(cqn.astype(BF16), wuq_ref[...])
        kv_raw = _dot(ckvn.astype(BF16), wukv_ref[...])
        krp = kr_ref[...]
        ct, st = c_ref[...], s_ref[...]
        gqg = jnp.zeros((1, HEAD_PAD), F32)
        gkg = jnp.zeros((1, HEAD_PAD), F32)
        dkr = jnp.zeros((t, HEAD_PAD - QK_NOPE), F32)
        dq_parts, dkv_parts = [], []
        for h in range(MLA_HEADS):
            qh = q_raw[:, h * HEAD_PAD:(h + 1) * HEAD_PAD]
            dx, dg = _rms_bwd(qh, qg_ref[...], QK_HEAD, _rope_bwd(dqf_ref[h], ct, st))
            gqg = gqg + jnp.sum(dg, axis=0, keepdims=True)
            dq_parts.append(dx)
            kh = jnp.concatenate([kv_raw[:, h * HEAD_PAD:h * HEAD_PAD + QK_NOPE], krp], axis=1)
            dx, dg = _rms_bwd(kh, kg_ref[...], QK_HEAD, _rope_bwd(dkf_ref[h], ct, st))
            gkg = gkg + jnp.sum(dg, axis=0, keepdims=True)
            dkr = dkr + dx[:, QK_NOPE:HEAD_PAD]
            dkv_parts += [dx[:, 0:QK_NOPE], dv_ref[h]]
        dq_raw = jnp.concatenate(dq_parts, axis=1).astype(BF16)
        dkv_raw = jnp.concatenate(dkv_parts, axis=1).astype(BF16)
        dqraw_ref[...] = dq_raw
        dkvraw_ref[...] = dkv_raw
        dkr_ref[...] = dkr.astype(BF16)
        dcq, gqa = _rms_bwd(cq, qag_ref[...], ql, _dot_nt(dq_raw, wuq_ref[...]))
        dckv, gkva = _rms_bwd(ckv, kvag_ref[...], kvl, _dot_nt(dkv_raw, wukv_ref[...]))
        dc_ref[:, 0:ql] = dcq.astype(BF16)
        dc_ref[:, ql:ql + kvl] = dckv.astype(BF16)
        gqa = jnp.sum(gqa, axis=0, keepdims=True)
        gkva = jnp.sum(gkva, axis=0, keepdims=True)

        @pl.when(i == 0)
        def _():
            gqa_ref[...] = gqa
            gkva_ref[...] = gkva
            gqg_ref[...] = gqg
            gkg_ref[...] = gkg

        @pl.when(i > 0)
        def _():
            gqa_ref[...] += gqa
            gkva_ref[...] += gkva
            gqg_ref[...] += gqg
            gkg_ref[...] += gkg

    full = lambda shp: pl.BlockSpec(shp, lambda i: tuple(0 for _ in shp))
    hblk = lambda w: pl.BlockSpec((MLA_HEADS, t, w), lambda i: (0, i, 0))
    sds = jax.ShapeDtypeStruct
    return pl.pallas_call(
        body,
        out_shape=(sds((s, ql + kvl), BF16), sds((s, 128), BF16), sds((s, nq), BF16), sds((s, nq), BF16),
                   sds((ql, s), BF16), sds((kvl, s), BF16),
                   sds((1, ql), F32), sds((1, kvl), F32), sds((1, HEAD_PAD), F32), sds((1, HEAD_PAD), F32)),
        grid=(s // t,),
        in_specs=[pl.BlockSpec((t, ql), lambda i: (i, cq_blk)), pl.BlockSpec((t, kvl), lambda i: (i, cq_blk + 1)),
                  pl.BlockSpec((t, 128), lambda i: (i, 0)),
                  pl.BlockSpec((t, HEAD_PAD), lambda i: (i, 0)), pl.BlockSpec((t, HEAD_PAD), lambda i: (i, 0)),
                  full((1, ql)), full((1, kvl)), full(w_uq.shape), full(w_ukv.shape), full((1, HEAD_PAD)), full((1, HEAD_PAD)),
                  hblk(HEAD_PAD), hblk(HEAD_PAD), hblk(V_HEAD)],
        out_specs=(pl.BlockSpec((t, ql + kvl), lambda i: (i, 0)), pl.BlockSpec((t, 128), lambda i: (i, 0)),
                   pl.BlockSpec((t, nq), lambda i: (i, 0)), pl.BlockSpec((t, nq), lambda i: (i, 0)),
                   pl.BlockSpec((ql, t), lambda i: (0, i)), pl.BlockSpec((kvl, t), lambda i: (0, i)),
                   full((1, ql)), full((1, kvl)), full((1, HEAD_PAD)), full((1, HEAD_PAD))),
        compiler_params=_cparams(("arbitrary",)), name="mla_prep_bwd")(
            p1, p1, kr, ctab, stab, qa_g, kva_g, w_uq, w_ukv, q_g, k_g, dqf, dkf, dv)


MZ_BLK128 = 3072 // 128
ATT_SCALE = QK_HEAD ** -0.5


def _attn_fwd(qf, kf, vv, p1, *, tq):
    nh, s, _ = qf.shape
    nq = s // tq

    def body(q_ref, k_ref, v_ref, mz_ref, y_ref, yt_ref, o_ref, lse_ref, lset_ref, m_scr, l_scr, acc_scr):
        qi, ki = pl.program_id(1), pl.program_id(2)

        @pl.when(ki == 0)
        def _():
            m_scr[...] = jnp.full((tq, 1), NEG, F32)
            l_scr[...] = jnp.zeros((tq, 1), F32)
            acc_scr[...] = jnp.zeros((tq, V_HEAD), F32)

        @pl.when(ki <= qi)
        def _():
            sc = _dot_nt(q_ref[...], k_ref[...]) * ATT_SCALE
            r = lax.broadcasted_iota(jnp.int32, (tq, tq), 0)
            c = lax.broadcasted_iota(jnp.int32, (tq, tq), 1)
            sc = jnp.where((c <= r) | (ki < qi), sc, NEG)
            m_old = m_scr[...]
            m_new = jnp.maximum(m_old, jnp.max(sc, axis=-1, keepdims=True))
            alpha = jnp.exp(m_old - m_new)
            p = jnp.exp(sc - m_new)
            l_scr[...] = alpha * l_scr[...] + jnp.sum(p, axis=-1, keepdims=True)
            acc_scr[...] = alpha * acc_scr[...] + _dot(p.astype(BF16), v_ref[...])
            m_scr[...] = m_new

        @pl.when(ki == nq - 1)
        def _():
            l = l_scr[...]
            o = acc_scr[...] / l
            o_ref[...] = o
            y = o * _silu(mz_ref[...])
            y_ref[...] = y.astype(BF16)
            yt_ref[...] = y.T.astype(BF16)
            lse = m_scr[...] + jnp.log(l)
            lse_ref[...] = lse
            lset_ref[...] = jnp.broadcast_to(lse, (tq, 128)).T[0:1, :]

    sds = jax.ShapeDtypeStruct
    return pl.pallas_call(
        body,
        out_shape=(sds((s, BRANCH_W), BF16), sds((BRANCH_W, s), BF16), sds((s, BRANCH_W), F32),
                   sds((nh, s, 1), F32), sds((nh, 1, s), F32)),
        grid=(nh, nq, nq),
        in_specs=[pl.BlockSpec((None, tq, HEAD_PAD), lambda h, qi, ki: (h, qi, 0)),
                  pl.BlockSpec((None, tq, HEAD_PAD), lambda h, qi, ki: (h, jnp.minimum(ki, qi), 0)),
                  pl.BlockSpec((None, tq, V_HEAD), lambda h, qi, ki: (h, jnp.minimum(ki, qi), 0)),
                  pl.BlockSpec((tq, V_HEAD), lambda h, qi, ki: (qi, MZ_BLK128 + h))],
        out_specs=(pl.BlockSpec((tq, V_HEAD), lambda h, qi, ki: (qi, h)),
                   pl.BlockSpec((V_HEAD, tq), lambda h, qi, ki: (h, qi)),
                   pl.BlockSpec((tq, V_HEAD), lambda h, qi, ki: (qi, h)),
                   pl.BlockSpec((None, tq, 1), lambda h, qi, ki: (h, qi, 0)),
                   pl.BlockSpec((None, 1, tq), lambda h, qi, ki: (h, 0, qi))),
        scratch_shapes=[pltpu.VMEM((tq, 1), F32), pltpu.VMEM((tq, 1), F32), pltpu.VMEM((tq, V_HEAD), F32)],
        compiler_params=_cparams(("parallel", "parallel", "arbitrary")), name="attn_fwd")(qf, kf, vv, p1)


def _attn_bwd_pre(dy4, o, p1, *, t=256):
    s = o.shape[0]
    t = _tile(s, t, 128)

    def body(dy_ref, o_ref, mz_ref, do_ref, dmz_ref, dl_ref, dlt_ref):
        dy, o_, mz = dy_ref[...], o_ref[...], mz_ref[...]
        do = dy * _silu(mz)
        do_ref[...] = do.astype(BF16)
        dmz_ref[...] = (dy * o_ * _dsilu(mz)).astype(BF16)
        prod = do * o_
        for h in range(MLA_HEADS):
            dl = jnp.sum(prod[:, h * V_HEAD:(h + 1) * V_HEAD], axis=-1, keepdims=True)
            dl_ref[h] = dl
            dlt_ref[h] = jnp.broadcast_to(dl, (t, 128)).T[0:1, :]

    sds = jax.ShapeDtypeStruct
    return pl.pallas_call(
        body,
        out_shape=(sds((s, BRANCH_W), BF16), sds((s, BRANCH_W), BF16), sds((MLA_HEADS, s, 1), F32), sds((MLA_HEADS, 1, s), F32)),
        grid=(s // t,),
        in_specs=[pl.BlockSpec((None, t, BRANCH_W), lambda i: (1, i, 0)), pl.BlockSpec((t, BRANCH_W), lambda i: (i, 0)),
                  pl.BlockSpec((t, BRANCH_W), lambda i: (i, 3))],
        out_specs=(pl.BlockSpec((t, BRANCH_W), lambda i: (i, 0)), pl.BlockSpec((t, BRANCH_W), lambda i: (i, 0)),
                   pl.BlockSpec((MLA_HEADS, t, 1), lambda i: (0, i, 0)), pl.BlockSpec((MLA_HEADS, 1, t), lambda i: (0, 0, i))),
        compiler_params=_cparams(("parallel",)), name="attn_bwd_pre")(dy4, o, p1)


def _attn_bwd_dq(qf, kf, vv, do, lse, delta, *, tq):
    nh, s, _ = qf.shape
    nq = s // tq

    def body(q_ref, k_ref, v_ref, do_ref, lse_ref, dl_ref, dq_ref, acc_scr):
        qi, ki = pl.program_id(1), pl.program_id(2)

        @pl.when(ki == 0)
        def _():
            acc_scr[...] = jnp.zeros((tq, HEAD_PAD), F32)

        @pl.when(ki <= qi)
        def _():
            k = k_ref[...]
            sc = _dot_nt(q_ref[...], k) * ATT_SCALE
            r = lax.broadcasted_iota(jnp.int32, (tq, tq), 0)
            c = lax.broadcasted_iota(jnp.int32, (tq, tq), 1)
            p = jnp.where((c <= r) | (ki < qi), jnp.exp(sc - lse_ref[...]), 0.0)
            dp = _dot_nt(do_ref[...], v_ref[...])
            ds = p * (dp - dl_ref[...]) * ATT_SCALE
            acc_scr[...] += _dot(ds.astype(BF16), k)

        @pl.when(ki == nq - 1)
        def _():
            dq_ref[...] = acc_scr[...]

    kmap = lambda h, qi, ki: (h, jnp.minimum(ki, qi), 0)
    return pl.pallas_call(
        body, out_shape=jax.ShapeDtypeStruct((nh, s, HEAD_PAD), F32), grid=(nh, nq, nq),
        in_specs=[pl.BlockSpec((None, tq, HEAD_PAD), lambda h, qi, ki: (h, qi, 0)),
                  pl.BlockSpec((None, tq, HEAD_PAD), kmap), pl.BlockSpec((None, tq, V_HEAD), kmap),
                  pl.BlockSpec((tq, V_HEAD), lambda h, qi, ki: (qi, h)),
                  pl.BlockSpec((None, tq, 1), lambda h, qi, ki: (h, qi, 0)),
                  pl.BlockSpec((None, tq, 1), lambda h, qi, ki: (h, qi, 0))],
        out_specs=pl.BlockSpec((None, tq, HEAD_PAD), lambda h, qi, ki: (h, qi, 0)),
        scratch_shapes=[pltpu.VMEM((tq, HEAD_PAD), F32)],
        compiler_params=_cparams(("parallel", "parallel", "arbitrary")), name="attn_bwd_dq")(qf, kf, vv, do, lse, delta)


def _attn_bwd_dkv(qf, kf, vv, do, lset, deltat, *, tq):
    nh, s, _ = qf.shape
    nq = s // tq

    def body(k_ref, v_ref, q_ref, do_ref, lse_ref, dl_ref, dk_ref, dv_ref, dk_scr, dv_scr):
        ki, qi = pl.program_id(1), pl.program_id(2)

        @pl.when(qi == 0)
        def _():
            dk_scr[...] = jnp.zeros((tq, HEAD_PAD), F32)
            dv_scr[...] = jnp.zeros((tq, V_HEAD), F32)

        @pl.when(qi >= ki)
        def _():
            q = q_ref[...]
            do_ = do_ref[...]
            st = _dot_nt(k_ref[...], q) * ATT_SCALE
            r = lax.broadcasted_iota(jnp.int32, (tq, tq), 0)
            c = lax.broadcasted_iota(jnp.int32, (tq, tq), 1)
            pt = jnp.where((r <= c) | (qi > ki), jnp.exp(st - lse_ref[...]), 0.0)
            dv_scr[...] += _dot(pt.astype(BF16), do_)
            dpt = _dot_nt(v_ref[...], do_)
            dst = pt * (dpt - dl_ref[...]) * ATT_SCALE
            dk_scr[...] += _dot(dst.astype(BF16), q)

        @pl.when(qi == nq - 1)
        def _():
            dk_ref[...] = dk_scr[...]
            dv_ref[...] = dv_scr[...]

    qmap = lambda h, ki, qi: (h, jnp.maximum(qi, ki), 0)
    sds = jax.ShapeDtypeStruct
    return pl.pallas_call(
        body, out_shape=(sds((nh, s, HEAD_PAD), F32), sds((nh, s, V_HEAD), F32)), grid=(nh, nq, nq),
        in_specs=[pl.BlockSpec((None, tq, HEAD_PAD), lambda h, ki, qi: (h, ki, 0)),
                  pl.BlockSpec((None, tq, V_HEAD), lambda h, ki, qi: (h, ki, 0)),
                  pl.BlockSpec((None, tq, HEAD_PAD), qmap),
                  pl.BlockSpec((tq, V_HEAD), lambda h, ki, qi: (jnp.maximum(qi, ki), h)),
                  pl.BlockSpec((None, 1, tq), lambda h, ki, qi: (h, 0, jnp.maximum(qi, ki))),
                  pl.BlockSpec((None, 1, tq), lambda h, ki, qi: (h, 0, jnp.maximum(qi, ki)))],
        out_specs=(pl.BlockSpec((None, tq, HEAD_PAD), lambda h, ki, qi: (h, ki, 0)),
                   pl.BlockSpec((None, tq, V_HEAD), lambda h, ki, qi: (h, ki, 0))),
        scratch_shapes=[pltpu.VMEM((tq, HEAD_PAD), F32), pltpu.VMEM((tq, V_HEAD), F32)],
        compiler_params=_cparams(("parallel", "parallel", "arbitrary")), name="attn_bwd_dkv")(kf, vv, qf, do, lset, deltat)


XATT_SCALE = XATTN_HEAD_DIM ** -0.5
XQ_COL = 8


def _memkv_prep(mem_kv, k_g):
    m = mem_kv.shape[0]

    def body(kv_ref, g_ref, k_ref, v_ref):
        for h in range(XATTN_HEADS):
            sl = slice(h * XATTN_HEAD_DIM, (h + 1) * XATTN_HEAD_DIM)
            k_ref[:, sl] = _rms(kv_ref[:, sl], g_ref[...], XATTN_HEAD_DIM).astype(BF16)
        v_ref[...] = kv_ref[:, BRANCH_W:2 * BRANCH_W].astype(BF16)

    sds = jax.ShapeDtypeStruct
    return pl.pallas_call(body, out_shape=(sds((m, BRANCH_W), BF16), sds((m, BRANCH_W), BF16)),
                          compiler_params=_cparams(), name="memkv_prep")(mem_kv, k_g)


def _memkv_prep_bwd(mem_kv, k_g, dk, dv):
    m = mem_kv.shape[0]

    def body(kv_ref, g_ref, dk_ref, dv_ref, d_ref, gk_ref):
        gk = jnp.zeros((1, XATTN_HEAD_DIM), F32)
        for h in range(XATTN_HEADS):
            sl = slice(h * XATTN_HEAD_DIM, (h + 1) * XATTN_HEAD_DIM)
            dx, dg = _rms_bwd(kv_ref[:, sl], g_ref[...], XATTN_HEAD_DIM, dk_ref[:, sl])
            d_ref[:, sl] = dx.astype(BF16)
            gk = gk + jnp.sum(dg, axis=0, keepdims=True)
        d_ref[:, BRANCH_W:2 * BRANCH_W] = dv_ref[...].astype(BF16)
        gk_ref[...] = gk

    sds = jax.ShapeDtypeStruct
    return pl.pallas_call(body, out_shape=(sds((m, 2 * BRANCH_W), BF16), sds((1, XATTN_HEAD_DIM), F32)),
                          compiler_params=_cparams(), name="memkv_prep_bwd")(mem_kv, k_g, dk, dv)


def _xattn_probs(xq, k_ref, qg, h):
    sl = slice(h * XATTN_HEAD_DIM, (h + 1) * XATTN_HEAD_DIM)
    q = _rms(xq[:, sl], qg, XATTN_HEAD_DIM).astype(BF16)
    sc = _dot_nt(q, k_ref[:, sl]) * XATT_SCALE
    e = jnp.exp(sc - jnp.max(sc, axis=-1, keepdims=True))
    return q, e / jnp.sum(e, axis=-1, keepdims=True)


def _xattn_fwd(p1, kx, vx, q_g, *, t=256):
    s = p1.shape[0]
    m = kx.shape[0]
    t = _tile(s, t, 128)

    def body(xq_ref, xz_ref, k_ref, v_ref, g_ref, y_ref, yt_ref):
        xq = xq_ref[...]
        outs = []
        for h in range(XATTN_HEADS):
            _, p = _xattn_probs(xq, k_ref, g_ref[...], h)
            outs.append(_dot(p.astype(BF16), v_ref[:, h * XATTN_HEAD_DIM:(h + 1) * XATTN_HEAD_DIM]))
        y = jnp.concatenate(outs, axis=1) * _silu(xz_ref[...])
        y_ref[...] = y.astype(BF16)
        yt_ref[...] = y.T.astype(BF16)

    full = lambda shp: pl.BlockSpec(shp, lambda i: tuple(0 for _ in shp))
    sds = jax.ShapeDtypeStruct
    return pl.pallas_call(
        body, out_shape=(sds((s, BRANCH_W), BF16), sds((BRANCH_W, s), BF16)), grid=(s // t,),
        in_specs=[pl.BlockSpec((t, BRANCH_W), lambda i: (i, XQ_COL)), pl.BlockSpec((t, BRANCH_W), lambda i: (i, XQ_COL + 1)),
                  full((m, BRANCH_W)), full((m, BRANCH_W)), full((1, XATTN_HEAD_DIM))],
        out_specs=(pl.BlockSpec((t, BRANCH_W), lambda i: (i, 0)), pl.BlockSpec((BRANCH_W, t), lambda i: (0, i))),
        compiler_params=_cparams(("parallel",)), name="xattn_fwd")(p1, p1, kx, vx, q_g)


def _xattn_bwd(p1, dy4, kx, vx, q_g, *, t=256):
    s = p1.shape[0]
    m = kx.shape[0]
    t = _tile(s, t, 128)

    def body(xq_ref, xz_ref, dy_ref, k_ref, v_ref, g_ref, d_ref, dk_ref, dv_ref, gq_ref):
        i = pl.program_id(0)
        xq, xz, dy = xq_ref[...], xz_ref[...], dy_ref[...]
        do = dy * _silu(xz)
        gq = jnp.zeros((1, XATTN_HEAD_DIM), F32)
        outs, dks, dvs = [], [], []
        for h in range(XATTN_HEADS):
            sl = slice(h * XATTN_HEAD_DIM, (h + 1) * XATTN_HEAD_DIM)
            q, p = _xattn_probs(xq, k_ref, g_ref[...], h)
            pb = p.astype(BF16)
            outs.append(_dot(pb, v_ref[:, sl]))
            do_h = do[:, sl].astype(BF16)
            dvs.append(_dot_tn(pb, do_h))
            dp = _dot_nt(do_h, v_ref[:, sl])
            ds = (p * (dp - jnp.sum(p * dp, axis=-1, keepdims=True)) * XATT_SCALE).astype(BF16)
            dks.append(_dot_tn(ds, q))
            dx, dg = _rms_bwd(xq[:, sl], g_ref[...], XATTN_HEAD_DIM, _dot(ds, k_ref[:, sl]))
            d_ref[:, sl] = dx.astype(BF16)
            gq = gq + jnp.sum(dg, axis=0, keepdims=True)
        o = jnp.concatenate(outs, axis=1)
        d_ref[:, BRANCH_W:2 * BRANCH_W] = (dy * o * _dsilu(xz)).astype(BF16)
        dk = jnp.concatenate(dks, axis=1)
        dv = jnp.concatenate(dvs, axis=1)

        @pl.when(i == 0)
        def _():
            dk_ref[...] = dk
            dv_ref[...] = dv
            gq_ref[...] = gq

        @pl.when(i > 0)
        def _():
            dk_ref[...] += dk
            dv_ref[...] += dv
            gq_ref[...] += gq

    full = lambda shp: pl.BlockSpec(shp, lambda i: tuple(0 for _ in shp))
    sds = jax.ShapeDtypeStruct
    return pl.pallas_call(
        body, out_shape=(sds((s, 2 * BRANCH_W), BF16), sds((m, BRANCH_W), F32), sds((m, BRANCH_W), F32), sds((1, XATTN_HEAD_DIM), F32)),
        grid=(s // t,),
        in_specs=[pl.BlockSpec((t, BRANCH_W), lambda i: (i, XQ_COL)), pl.BlockSpec((t, BRANCH_W), lambda i: (i, XQ_COL + 1)),
                  pl.BlockSpec((None, t, BRANCH_W), lambda i: (3, i, 0)),
                  full((m, BRANCH_W)), full((m, BRANCH_W)), full((1, XATTN_HEAD_DIM))],
        out_specs=(pl.BlockSpec((t, 2 * BRANCH_W), lambda i: (i, 0)), full((m, BRANCH_W)), full((m, BRANCH_W)),
                   full((1, XATTN_HEAD_DIM))),
        compiler_params=_cparams(("arbitrary",)), name="xattn_bwd")(p1, p1, dy4, kx, vx, q_g)


def _gate_fwd(ystack, w_branch, gp, gate_b, *, tm=512, tn=512):
    _, s, _ = ystack.shape
    d = w_branch.shape[2]
    tm, tn = _tile(s, tm, 128), _tile(d, tn)
    nj = d // tn

    def body(y_ref, w_ref, gp_ref, gb_ref, o_ref, ot_ref, acc_scr):
        b = pl.program_id(2)
        part = jax.nn.sigmoid(gp_ref[...] + gb_ref[...]) * _dot(y_ref[...], w_ref[...])

        @pl.when(b == 0)
        def _():
            acc_scr[...] = part

        @pl.when(b > 0)
        def _():
            acc_scr[...] += part

        @pl.when(b == N_BRANCH - 1)
        def _():
            acc = acc_scr[...]
            o_ref[...] = acc.astype(BF16)
            ot_ref[...] = acc.T.astype(BF16)

    sds = jax.ShapeDtypeStruct
    return pl.pallas_call(
        body, out_shape=(sds((s, d), BF16), sds((d, s), BF16)), grid=(s // tm, nj, N_BRANCH),
        in_specs=[pl.BlockSpec((None, tm, BRANCH_W), lambda i, j, b: (b, i, 0)),
                  pl.BlockSpec((None, BRANCH_W, tn), lambda i, j, b: (b, 0, j)),
                  pl.BlockSpec((tm, tn), lambda i, j, b: (i, b * nj + j)),
                  pl.BlockSpec((1, tn), lambda i, j, b: (0, b * nj + j))],
        out_specs=(pl.BlockSpec((tm, tn), lambda i, j, b: (i, j)), pl.BlockSpec((tn, tm), lambda i, j, b: (j, i))),
        scratch_shapes=[pltpu.VMEM((tm, tn), F32)],
        compiler_params=_cparams(("parallel", "parallel", "arbitrary")), name="gate_fwd")(ystack, w_branch, gp, gate_b)


def _gate_bwd(ystack, w_branch, gp, gate_b, dm, *, tm=512, tn=512):
    _, s, _ = ystack.shape
    d = w_branch.shape[2]
    tm, tn = _tile(s, tm, 128), _tile(d, tn)
    nj = d // tn

    def body(y_ref, w_ref, gp_ref, gb_ref, dm_ref, dp_ref, dg_ref, gb_out_ref):
        i = pl.program_id(2)
        proj = _dot(y_ref[...], w_ref[...])
        gate = jax.nn.sigmoid(gp_ref[...] + gb_ref[...])
        dmv = dm_ref[...]
        dp_ref[...] = (dmv * gate).astype(BF16)
        dpre = dmv * proj * gate * (1.0 - gate)
        dg_ref[...] = dpre.astype(BF16)
        part = jnp.sum(dpre, axis=0, keepdims=True)

        @pl.when(i == 0)
        def _():
            gb_out_ref[...] = part

        @pl.when(i > 0)
        def _():
            gb_out_ref[...] += part

    sds = jax.ShapeDtypeStruct
    return pl.pallas_call(
        body, out_shape=(sds((N_BRANCH, s, d), BF16), sds((s, N_BRANCH * d), BF16), sds((1, N_BRANCH * d), F32)),
        grid=(N_BRANCH, nj, s // tm),
        in_specs=[pl.BlockSpec((None, tm, BRANCH_W), lambda b, j, i: (b, i, 0)),
                  pl.BlockSpec((None, BRANCH_W, tn), lambda b, j, i: (b, 0, j)),
                  pl.BlockSpec((tm, tn), lambda b, j, i: (i, b * nj + j)),
                  pl.BlockSpec((1, tn), lambda b, j, i: (0, b * nj + j)),
                  pl.BlockSpec((tm, tn), lambda b, j, i: (i, j))],
        out_specs=(pl.BlockSpec((None, tm, tn), lambda b, j, i: (b, i, j)),
                   pl.BlockSpec((tm, tn), lambda b, j, i: (i, b * nj + j)),
                   pl.BlockSpec((1, tn), lambda b, j, i: (0, b * nj + j))),
        compiler_params=_cparams(("parallel", "parallel", "arbitrary")), name="gate_bwd")(ystack, w_branch, gp, gate_b, dm)


def _adamw(w, g, m, v, *, name):
    shape = w.shape
    c = shape[-1]
    r = 1
    for n in shape[:-1]:
        r *= n
    w2, g2, m2, v2 = (a.reshape(r, c) for a in (w, g, m, v))
    tr = _tile(r, max(8, (1 << 19) // c // 8 * 8), 8)
    c1 = 1.0 / (1.0 - ADAM_B1 ** ADAM_STEP)
    c2 = 1.0 / (1.0 - ADAM_B2 ** ADAM_STEP)

    def body(w_ref, g_ref, m_ref, v_ref, d_ref, nm_ref, nv_ref):
        gv = g_ref[...]
        nm = ADAM_B1 * m_ref[...] + (1.0 - ADAM_B1) * gv
        nv = ADAM_B2 * v_ref[...] + (1.0 - ADAM_B2) * (gv * gv)
        nm_ref[...] = nm
        nv_ref[...] = nv
        d_ref[...] = -ADAM_LR * ((nm * c1) / (jnp.sqrt(nv * c2) + ADAM_EPS) + ADAM_WD * w_ref[...])

    blk = pl.BlockSpec((tr, c), lambda i: (i, 0))
    sd = jax.ShapeDtypeStruct((r, c), F32)
    d2, nm2, nv2 = pl.pallas_call(body, out_shape=(sd, sd, sd), grid=(r // tr,), in_specs=[blk] * 4, out_specs=(blk,) * 3,
                                  compiler_params=_cparams(("parallel",)), name=name)(w2, g2, m2, v2)
    return d2.reshape(shape), nm2.reshape(shape), nv2.reshape(shape)


def _place():
    x, y, c = lax.axis_index("x"), lax.axis_index("y"), lax.axis_index("c")
    chips = [(1 - x, y), (x, 1 - y), (1 - x, 1 - y)]
    return x, y, c, 2 * x + y, chips, [2 * cx + cy for cx, cy in chips]


ANY = pl.BlockSpec(memory_space=pl.ANY)


def _all_gather(shards):
    n = len(shards)

    def body(*refs):
        ins, outs = refs[:n], refs[n:2 * n]
        send, recv, lsem = refs[2 * n:]
        x, y, c, k, chips, ks = _place()
        sib = (x, y, 1 - c)
        locals_, sends = [], []
        for a in range(n):
            for l in range(2):
                cp = pltpu.make_async_copy(ins[a].at[l], outs[a].at[l, k], lsem.at[2 * a + l])
                cp.start()
                locals_.append(cp)
            for j in range(3):
                cp = pltpu.make_async_remote_copy(src_ref=ins[a].at[c], dst_ref=outs[a].at[c, k], send_sem=send.at[6 * a + j],
                                                  recv_sem=recv.at[6 * a + j], device_id=(*chips[j], c), device_id_type=MESH)
                cp.start()
                sends.append(cp)
        for a in range(n):
            for j in range(3):
                slab = outs[a].at[c, ks[j]]
                pltpu.make_async_remote_copy(src_ref=slab, dst_ref=slab, send_sem=send.at[6 * a + j], recv_sem=recv.at[6 * a + j],
                                             device_id=(*chips[j], c), device_id_type=MESH).wait_recv()
                cp = pltpu.make_async_remote_copy(src_ref=slab, dst_ref=slab, send_sem=send.at[6 * a + 3 + j],
                                                  recv_sem=recv.at[6 * a + 3 + j], device_id=sib, device_id_type=MESH)
                cp.start()
                sends.append(cp)
        for a in range(n):
            for j in range(3):
                slab = outs[a].at[1 - c, ks[j]]
                pltpu.make_async_remote_copy(src_ref=slab, dst_ref=slab, send_sem=send.at[6 * a + 3 + j],
                                             recv_sem=recv.at[6 * a + 3 + j], device_id=sib, device_id_type=MESH).wait_recv()
        for cp in sends:
            cp.wait_send()
        for cp in locals_:
            cp.wait()

    out_shape = tuple(jax.ShapeDtypeStruct((2, 4) + s.shape[1:], s.dtype) for s in shards)
    return pl.pallas_call(
        body, out_shape=out_shape, in_specs=[ANY] * n, out_specs=(ANY,) * n,
        scratch_shapes=[pltpu.SemaphoreType.DMA((6 * n,)), pltpu.SemaphoreType.DMA((6 * n,)), pltpu.SemaphoreType.DMA((2 * n,))],
        name="weights_all_gather")(*shards)


def _rs_exchange_cores(grads):
    n = len(grads)

    def body(*refs):
        ins, outs = refs[:n], refs[n:2 * n]
        send, recv = refs[2 * n:]
        x, y, c, _, _, _ = _place()
        sib = (x, y, 1 - c)
        cps = []
        for a in range(n):
            cp = pltpu.make_async_remote_copy(src_ref=ins[a].at[1 - c], dst_ref=outs[a], send_sem=send.at[a], recv_sem=recv.at[a],
                                              device_id=sib, device_id_type=MESH)
            cp.start()
            cps.append(cp)
        for cp in cps:
            cp.wait()

    out_shape = tuple(jax.ShapeDtypeStruct(g.shape[1:], g.dtype) for g in grads)
    return pl.pallas_call(body, out_shape=out_shape, in_specs=[ANY] * n, out_specs=(ANY,) * n,
                          scratch_shapes=[pltpu.SemaphoreType.DMA((n,)), pltpu.SemaphoreType.DMA((n,))],
                          name="grads_exchange_cores")(*grads)


def _rs_exchange_chips(parts):
    n = len(parts)

    def body(*refs):
        ins, outs = refs[:n], refs[n:2 * n]
        send, recv, lsem = refs[2 * n:]
        x, y, c, k, chips, ks = _place()
        locals_, sends = [], []
        for a in range(n):
            cp = pltpu.make_async_copy(ins[a].at[k], outs[a].at[k], lsem.at[a])
            cp.start()
            locals_.append(cp)
            for j in range(3):
                cp = pltpu.make_async_remote_copy(src_ref=ins[a].at[ks[j]], dst_ref=outs[a].at[k], send_sem=send.at[3 * a + j],
                                                  recv_sem=recv.at[3 * a + j], device_id=(*chips[j], c), device_id_type=MESH)
                cp.start()
                sends.append(cp)
        for a in range(n):
            for j in range(3):
                slab = outs[a].at[ks[j]]
                pltpu.make_async_remote_copy(src_ref=slab, dst_ref=slab, send_sem=send.at[3 * a + j], recv_sem=recv.at[3 * a + j],
                                             device_id=(*chips[j], c), device_id_type=MESH).wait_recv()
        for cp in sends:
            cp.wait_send()
        for cp in locals_:
            cp.wait()

    out_shape = tuple(jax.ShapeDtypeStruct(p.shape, p.dtype) for p in parts)
    return pl.pallas_call(
        body, out_shape=out_shape, in_specs=[ANY] * n, out_specs=(ANY,) * n,
        scratch_shapes=[pltpu.SemaphoreType.DMA((3 * n,)), pltpu.SemaphoreType.DMA((3 * n,)), pltpu.SemaphoreType.DMA((n,))],
        name="grads_exchange_chips")(*parts)


def _rs_share_cores(halves):
    n = len(halves)

    def body(*refs):
        ins, outs = refs[:n], refs[n:2 * n]
        send, recv, lsem = refs[2 * n:]
        x, y, c, _, _, _ = _place()
        sib = (x, y, 1 - c)
        locals_, cps = [], []
        for a in range(n):
            cp = pltpu.make_async_copy(ins[a], outs[a].at[c], lsem.at[a])
            cp.start()
            locals_.append(cp)
            cp = pltpu.make_async_remote_copy(src_ref=ins[a], dst_ref=outs[a].at[c], send_sem=send.at[a], recv_sem=recv.at[a],
                                              device_id=sib, device_id_type=MESH)
            cp.start()
            cps.append(cp)
        for a in range(n):
            slab = outs[a].at[1 - c]
            pltpu.make_async_remote_copy(src_ref=slab, dst_ref=slab, send_sem=send.at[a], recv_sem=recv.at[a],
                                         device_id=sib, device_id_type=MESH).wait_recv()
        for cp in cps:
            cp.wait_send()
        for cp in locals_:
            cp.wait()

    out_shape = tuple(jax.ShapeDtypeStruct((2,) + h.shape, h.dtype) for h in halves)
    return pl.pallas_call(
        body, out_shape=out_shape, in_specs=[ANY] * n, out_specs=(ANY,) * n,
        scratch_shapes=[pltpu.SemaphoreType.DMA((n,)), pltpu.SemaphoreType.DMA((n,)), pltpu.SemaphoreType.DMA((n,))],
        name="grads_share_cores")(*halves)


def _add_core_halves(g, ra, c_idx, *, name):
    _, _, r, c = g.shape
    tr = _tile(r, max(8, (1 << 19) // c // 8 * 8), 8)

    def body(c_ref, g_ref, ra_ref, o_ref):
        o_ref[...] = g_ref[...] + ra_ref[...]

    gs = pltpu.PrefetchScalarGridSpec(
        num_scalar_prefetch=1, grid=(4, r // tr),
        in_specs=[pl.BlockSpec((None, None, tr, c), lambda j, i, cr: (cr[0], j, i, 0)),
                  pl.BlockSpec((None, tr, c), lambda j, i, cr: (j, i, 0))],
        out_specs=pl.BlockSpec((None, tr, c), lambda j, i, cr: (j, i, 0)))
    return pl.pallas_call(body, out_shape=jax.ShapeDtypeStruct((4, r, c), F32), grid_spec=gs,
                          compiler_params=_cparams(("parallel", "parallel")), name=name)(c_idx, g, ra)


def _add_chips(q, *, name):
    _, r, c = q.shape
    tr = _tile(r, max(8, (1 << 18) // c // 8 * 8), 8)

    def body(q_ref, o_ref):
        o_ref[...] = ((q_ref[0] + q_ref[1]) + q_ref[2]) + q_ref[3]

    return pl.pallas_call(body, out_shape=jax.ShapeDtypeStruct((r, c), F32), grid=(r // tr,),
                          in_specs=[pl.BlockSpec((4, tr, c), lambda i: (0, i, 0))], out_specs=pl.BlockSpec((tr, c), lambda i: (i, 0)),
                          compiler_params=_cparams(("parallel",)), name=name)(q)


def _all_reduce_small(vec):
    r = vec.shape[0]

    def body(v_ref, gath_ref, sum_ref, send, recv):
        x, y, c = lax.axis_index("x"), lax.axis_index("y"), lax.axis_index("c")
        me = 4 * x + 2 * y + c
        gath_ref[me] = v_ref[...]
        cps = []
        for f in range(1, 8):
            fx, fy, fc = (f >> 2) & 1, (f >> 1) & 1, f & 1
            peer = (x ^ fx, y ^ fy, c ^ fc)
            cp = pltpu.make_async_remote_copy(src_ref=v_ref, dst_ref=gath_ref.at[me], send_sem=send.at[f - 1], recv_sem=recv.at[f - 1],
                                              device_id=peer, device_id_type=MESH)
            cp.start()
            cps.append(cp)
        for f in range(1, 8):
            fx, fy, fc = (f >> 2) & 1, (f >> 1) & 1, f & 1
            src = 4 * (x ^ fx) + 2 * (y ^ fy) + (c ^ fc)
            pltpu.make_async_remote_copy(src_ref=v_ref, dst_ref=gath_ref.at[src], send_sem=send.at[f - 1], recv_sem=recv.at[f - 1],
                                         device_id=(x ^ fx, y ^ fy, c ^ fc), device_id_type=MESH).wait_recv()
        for cp in cps:
            cp.wait_send()
        acc = gath_ref[0]
        for i in range(1, 8):
            acc = acc + gath_ref[i]
        sum_ref[...] = acc

    vm = pl.BlockSpec(memory_space=pltpu.VMEM)
    _, total = pl.pallas_call(
        body, out_shape=(jax.ShapeDtypeStruct((8, r, 128), F32), jax.ShapeDtypeStruct((r, 128), F32)),
        in_specs=[vm], out_specs=(vm, vm),
        scratch_shapes=[pltpu.SemaphoreType.DMA((7,)), pltpu.SemaphoreType.DMA((7,))],
        name="small_all_reduce")(vec)
    return total


def _full_weight(gathered, name, layer):
    return jnp.concatenate([gathered[layer, k] for k in range(4)], axis=SHARD_AXIS[name])


def _to_shards(full, name):
    return jnp.stack(jnp.split(full, 4, axis=SHARD_AXIS[name]), axis=0)


def _rope_tables(positions):
    inv = ROPE_THETA ** (-jnp.arange(0, QK_ROPE, 2, dtype=F32) / QK_ROPE)
    ang = positions.astype(F32)[:, None] * inv
    cos, sin = jnp.cos(ang), jnp.sin(ang)
    s = positions.shape[0]
    pad = jnp.zeros((s, HEAD_PAD - QK_HEAD), F32)
    ctab = jnp.concatenate([jnp.ones((s, QK_NOPE), F32), cos, cos, pad], axis=1)
    stab = jnp.concatenate([jnp.zeros((s, QK_NOPE), F32), -sin, sin, pad], axis=1)
    return ctab, stab


def _pad_gain(g):
    return jnp.concatenate([g, jnp.zeros((HEAD_PAD - QK_HEAD,), F32)])[None, :]


def _layer_weights(gw, rep, l, ql, kvl):
    d = rep["norm_g"].shape[1]
    w_in = _full_weight(gw["w_in"], "w_in", l)
    o_kr = 2 * BRANCH_W + ql + kvl
    o_g = o_kr + QK_ROPE + 7 * BRANCH_W
    w = {}
    w["w1"] = jnp.concatenate([w_in[:, :o_kr], w_in[:, o_kr + QK_ROPE:o_g]], axis=1)
    w["wg"] = w_in[:, o_g:]
    w["wkr"] = jnp.concatenate([w_in[:, o_kr:o_kr + QK_ROPE], jnp.zeros((d, 128 - QK_ROPE), BF16)], axis=1)
    wuq = _full_weight(gw["w_uq"], "w_uq", l).reshape(ql, MLA_HEADS, QK_HEAD)
    w["w_uq"] = jnp.pad(wuq, ((0, 0), (0, 0), (0, HEAD_PAD - QK_HEAD))).reshape(ql, MLA_HEADS * HEAD_PAD)
    w["w_ukv"] = _full_weight(gw["w_ukv"], "w_ukv", l)
    w["pool_w"] = _full_weight(gw["pool_w"], "pool_w", l)
    w["conv_w"] = _full_weight(gw["conv_w"], "conv_w", l)
    w["w_mem_kv"] = _full_weight(gw["w_mem_kv"], "w_mem_kv", l)
    w["w_branch"] = _full_weight(gw["w_branch"], "w_branch", l)
    w["w_out"] = _full_weight(gw["w_out"], "w_out", l)
    for nme in ("norm_g", "gate_b", "pool_scale", "q_a_norm_g", "kv_a_norm_g", "mem_norm_g", "xattn_q_norm_g", "xattn_k_norm_g"):
        w[nme] = rep[nme][l][None, :]
    w["mla_q_norm_g"] = _pad_gain(rep["mla_q_norm_g"][l])
    w["mla_k_norm_g"] = _pad_gain(rep["mla_k_norm_g"][l])
    return w


def _forward_layer(x, mem, ctab, stab, w, tq, l):
    sfx = f"_l{l}"
    h, ht = _norm_fwd(x, w["norm_g"], name="norm_fwd" + sfx)
    p1 = _mm(h, w["w1"], name="proj_main" + sfx)
    gp = _mm(h, w["wg"], name="proj_gates" + sfx)
    kr = _mm(h, w["wkr"], name="proj_krope" + sfx)
    y_pool, yt_pool = _pool_fwd(p1, w["pool_w"], w["pool_scale"])
    qf, kf, vv = _mla_prep_fwd(p1, kr, ctab, stab, w["q_a_norm_g"], w["kv_a_norm_g"], w["w_uq"], w["w_ukv"],
                               w["mla_q_norm_g"], w["mla_k_norm_g"])
    y_mla, yt_mla, o_att, lse, lset = _attn_fwd(qf, kf, vv, p1, tq=tq)
    y_conv, yt_conv = _conv_fwd(p1, w["conv_w"])
    memn, memnt = _norm_fwd(mem, w["mem_norm_g"], name="mem_norm" + sfx)
    mem_kv = _mm(memn, w["w_mem_kv"], name="mem_kv" + sfx)
    kx, vx = _memkv_prep(mem_kv, w["xattn_k_norm_g"])
    y_mem, yt_mem = _xattn_fwd(p1, kx, vx, w["xattn_q_norm_g"])
    ystack = jnp.stack([y_pool, y_mla, y_conv, y_mem])
    ytstack = jnp.stack([yt_pool, yt_mla, yt_conv, yt_mem])
    merged, mergedt = _gate_fwd(ystack, w["w_branch"], gp, w["gate_b"])
    x_out = _mm(merged, w["w_out"], add=x, name="out_proj" + sfx)
    saved = dict(x=x, ht=ht, p1=p1, gp=gp, kr=kr, qf=qf, kf=kf, vv=vv, o_att=o_att, lse=lse, lset=lset, memnt=memnt,
                 mem_kv=mem_kv, kx=kx, vx=vx, ystack=ystack, ytstack=ytstack, mergedt=mergedt)
    return x_out, saved


def _backward_layer(dx_out, sv, mem, ctab, stab, w, tq, l, ql, kvl):
    sfx = f"_l{l}"
    g = {}
    g["w_out"] = _mm(sv["mergedt"], dx_out, name="g_w_out" + sfx)
    dm = _mm(dx_out, w["w_out"], trans_b=True, name="d_merged" + sfx)
    dproj, dgp, g_gate_b = _gate_bwd(sv["ystack"], w["w_branch"], sv["gp"], w["gate_b"], dm)
    g["gate_b"] = g_gate_b[0]
    g["w_branch"] = _mm(sv["ytstack"], dproj, name="g_w_branch" + sfx)
    dy4 = _mm(dproj, w["w_branch"], trans_b=True, name="d_branches" + sfx)
    p1, kr = sv["p1"], sv["kr"]
    d_pool, g_pw, g_ps = _pool_bwd(p1, dy4[0], w["pool_w"], w["pool_scale"])
    g["pool_w"], g["pool_scale"] = g_pw, g_ps[0]
    do, d_mz, delta, deltat = _attn_bwd_pre(dy4, sv["o_att"], p1)
    dqf = _attn_bwd_dq(sv["qf"], sv["kf"], sv["vv"], do, sv["lse"], delta, tq=tq)
    dkf, dvv = _attn_bwd_dkv(sv["qf"], sv["kf"], sv["vv"], do, sv["lset"], deltat, tq=tq)
    (d_c, d_kr, dq_raw, dkv_raw, cqnt, ckvnt, g_qa, g_kva, g_qg, g_kg) = _mla_prep_bwd(
        p1, kr, ctab, stab, w["q_a_norm_g"], w["kv_a_norm_g"], w["w_uq"], w["w_ukv"], w["mla_q_norm_g"], w["mla_k_norm_g"],
        dqf, dkf, dvv)
    g["q_a_norm_g"], g["kv_a_norm_g"] = g_qa[0], g_kva[0]
    g["mla_q_norm_g"], g["mla_k_norm_g"] = g_qg[0, :QK_HEAD], g_kg[0, :QK_HEAD]
    g_wuq = _mm(cqnt, dq_raw, name="g_w_uq" + sfx)
    g["w_uq"] = g_wuq.reshape(ql, MLA_HEADS, HEAD_PAD)[:, :, :QK_HEAD].reshape(ql, MLA_HEADS * QK_HEAD)
    g["w_ukv"] = _mm(ckvnt, dkv_raw, name="g_w_ukv" + sfx)
    d_conv, gc0, gc1, gc2 = _conv_bwd(p1, dy4[2], w["conv_w"])
    g["conv_w"] = jnp.concatenate([gc0, gc1, gc2], axis=0)
    d_x, dkx, dvx, g_xq = _xattn_bwd(p1, dy4, sv["kx"], sv["vx"], w["xattn_q_norm_g"])
    g["xattn_q_norm_g"] = g_xq[0]
    d_memkv, g_xk = _memkv_prep_bwd(sv["mem_kv"], w["xattn_k_norm_g"], dkx, dvx)
    g["xattn_k_norm_g"] = g_xk[0]
    g["w_mem_kv"] = _mm(sv["memnt"], d_memkv, name="g_w_mem_kv" + sfx)
    d_memn = _mm(d_memkv, w["w_mem_kv"], trans_b=True, name="d_memn" + sfx)
    _, g_mn = _norm_bwd(mem, w["mem_norm_g"], d_memn, d_memn, name="mem_norm_bwd" + sfx)
    g["mem_norm_g"] = g_mn[0]
    dp1 = jnp.concatenate([d_pool, d_c, d_mz, d_conv, d_x], axis=1)
    ht = sv["ht"]
    g_w1 = _mm(ht, dp1, name="g_w1" + sfx)
    g_wg = _mm(ht, dgp, name="g_wg" + sfx)
    g_wkr = _mm(ht, d_kr, name="g_wkr" + sfx)
    o_kr = 2 * BRANCH_W + ql + kvl
    g["w_in"] = jnp.concatenate([g_w1[:, :o_kr], g_wkr[:, :QK_ROPE], g_w1[:, o_kr:], g_wg], axis=1)
    dh = _mm(dp1, w["w1"], trans_b=True, name="dh_main" + sfx)
    dh = _mm(dgp, w["wg"], trans_b=True, add=dh, name="dh_gates" + sfx)
    dh = _mm(d_kr, w["wkr"], trans_b=True, add=dh, name="dh_krope" + sfx)
    dx, g_ng = _norm_bwd(sv["x"], w["norm_g"], dh, dx_out, name="norm_bwd" + sfx)
    g["norm_g"] = g_ng[0]
    return dx, g


def _as4(a):
    rest = a.shape[2:]
    r = 1
    for n in rest[:-1]:
        r *= n
    return a.reshape(2, 4, r, rest[-1])


def kernel(x, mem, positions, norm_g, w_in, gate_b, pool_w, pool_scale, q_a_norm_g, kv_a_norm_g, w_uq, w_ukv, mla_q_norm_g, mla_k_norm_g, conv_w, mem_norm_g, w_mem_kv, xattn_q_norm_g, xattn_k_norm_g, w_branch, w_out, loss_target, m_norm_g, m_w_in, m_gate_b, m_pool_w, m_pool_scale, m_q_a_norm_g, m_kv_a_norm_g, m_w_uq, m_w_ukv, m_mla_q_norm_g, m_mla_k_norm_g, m_conv_w, m_mem_norm_g, m_w_mem_kv, m_xattn_q_norm_g, m_xattn_k_norm_g, m_w_branch, m_w_out, v_norm_g, v_w_in, v_gate_b, v_pool_w, v_pool_scale, v_q_a_norm_g, v_kv_a_norm_g, v_w_uq, v_w_ukv, v_mla_q_norm_g, v_mla_k_norm_g, v_conv_w, v_mem_norm_g, v_w_mem_kv, v_xattn_q_norm_g, v_xattn_k_norm_g, v_w_branch, v_w_out):
    wts = dict(norm_g=norm_g, w_in=w_in, gate_b=gate_b, pool_w=pool_w, pool_scale=pool_scale, q_a_norm_g=q_a_norm_g,
               kv_a_norm_g=kv_a_norm_g, w_uq=w_uq, w_ukv=w_ukv, mla_q_norm_g=mla_q_norm_g, mla_k_norm_g=mla_k_norm_g,
               conv_w=conv_w, mem_norm_g=mem_norm_g, w_mem_kv=w_mem_kv, xattn_q_norm_g=xattn_q_norm_g,
               xattn_k_norm_g=xattn_k_norm_g, w_branch=w_branch, w_out=w_out)
    mom = dict(norm_g=m_norm_g, w_in=m_w_in, gate_b=m_gate_b, pool_w=m_pool_w, pool_scale=m_pool_scale, q_a_norm_g=m_q_a_norm_g,
               kv_a_norm_g=m_kv_a_norm_g, w_uq=m_w_uq, w_ukv=m_w_ukv, mla_q_norm_g=m_mla_q_norm_g, mla_k_norm_g=m_mla_k_norm_g,
               conv_w=m_conv_w, mem_norm_g=m_mem_norm_g, w_mem_kv=m_w_mem_kv, xattn_q_norm_g=m_xattn_q_norm_g,
               xattn_k_norm_g=m_xattn_k_norm_g, w_branch=m_w_branch, w_out=m_w_out)
    vel = dict(norm_g=v_norm_g, w_in=v_w_in, gate_b=v_gate_b, pool_w=v_pool_w, pool_scale=v_pool_scale, q_a_norm_g=v_q_a_norm_g,
               kv_a_norm_g=v_kv_a_norm_g, w_uq=v_w_uq, w_ukv=v_w_ukv, mla_q_norm_g=v_mla_q_norm_g, mla_k_norm_g=v_mla_k_norm_g,
               conv_w=v_conv_w, mem_norm_g=v_mem_norm_g, w_mem_kv=v_w_mem_kv, xattn_q_norm_g=v_xattn_q_norm_g,
               xattn_k_norm_g=v_xattn_k_norm_g, w_branch=v_w_branch, w_out=v_w_out)
    depth = norm_g.shape[0]
    assert depth == 2 and x.shape[0] == 1
    xs, mems, tgt = x[0], mem[0], loss_target[0]
    s = xs.shape[0]
    ql, kvl = q_a_norm_g.shape[1], kv_a_norm_g.shape[1]
    tq = _tile(s, 512, 128)
    ctab, stab = _rope_tables(positions[0])

    send = [wts[n].astype(F32 if n == "conv_w" else BF16) for n in SHARDED]
    gathered = dict(zip(SHARDED, _all_gather(send)))
    rep = {n: wts[n] for n in REPLICATED}
    lw = [_layer_weights(gathered, rep, l, ql, kvl) for l in range(depth)]

    act, saved = xs, []
    for l in range(depth):
        act, sv = _forward_layer(act, mems, ctab, stab, lw[l], tq, l)
        saved.append(sv)
    dy, loss_part = _loss_head(act, tgt)

    grads = [None] * depth
    dxl = dy
    for l in reversed(range(depth)):
        dxl, grads[l] = _backward_layer(dxl, saved[l], mems, ctab, stab, lw[l], tq, l, ql, kvl)
    grad_x = dxl[None]

    c_idx = lax.axis_index("c").astype(jnp.int32).reshape(1)
    g_full = [_as4(jnp.stack([_to_shards(grads[l][n], n) for l in range(depth)], axis=0)) for n in SHARDED]
    from_sib = _rs_exchange_cores(g_full)
    parts = [_add_core_halves(g, r, c_idx, name=f"add_cores_{n}") for g, r, n in zip(g_full, from_sib, SHARDED)]
    by_chip = _rs_exchange_chips(parts)
    halves = [_add_chips(q, name=f"add_chips_{n}") for q, n in zip(by_chip, SHARDED)]
    reduced = _rs_share_cores(halves)
    gsum = {n: r.reshape(wts[n].shape) for n, r in zip(SHARDED, reduced)}

    flat = [jnp.stack([grads[l][n] for l in range(depth)], axis=0).reshape(-1) for n in REPLICATED]
    sizes = [f.shape[0] for f in flat]
    total = sum(sizes) + 1
    rows = -(-total // 1024) * 8
    vec = jnp.concatenate(flat + [loss_part[0, :1], jnp.zeros((rows * 128 - total,), F32)]).reshape(rows, 128)
    red = _all_reduce_small(vec).reshape(-1)
    off = 0
    for n, sz in zip(REPLICATED, sizes):
        gsum[n] = red[off:off + sz].reshape(wts[n].shape)
        off += sz
    loss = red[off]

    delta, new_m, new_v = {}, {}, {}
    for n in WEIGHTS:
        delta[n], new_m[n], new_v[n] = _adamw(wts[n], gsum[n], mom[n], vel[n], name=f"adamw_{n}")
    return (loss, grad_x, *[gsum[n] for n in WEIGHTS], *[delta[n] for n in WEIGHTS],
            *[new_m[n] for n in WEIGHTS], *[new_v[n] for n in WEIGHTS])
```

```python
import functools

import jax
import jax.numpy as jnp
from jax import lax
from jax.experimental import pallas as pl
from jax.experimental.pallas import tpu as pltpu

F32 = jnp.float32
BF16 = jnp.bfloat16
MESH = pl.DeviceIdType.MESH

EPS = 1e-6
N_BRANCH = 4
BRANCH_W = 1024
POOL_GROUPS = 4
POOL_GW = BRANCH_W // POOL_GROUPS
POOL_HALO = 16
CONV_HALO = 8
MLA_HEADS = 8
QK_NOPE = 128
QK_ROPE = 64
QK_HEAD = QK_NOPE + QK_ROPE
HEAD_PAD = 256
V_HEAD = 128
ROPE_THETA = 10000.0
XATTN_HEADS = 4
XATTN_HEAD_DIM = BRANCH_W // XATTN_HEADS
ADAM_LR, ADAM_B1, ADAM_B2, ADAM_EPS, ADAM_WD, ADAM_STEP = 0.001, 0.9, 0.999, 1e-08, 0.01, 10
NEG = -1e30
VMEM_LIMIT = 48 * 1024 * 1024

SHARDED = ("w_in", "pool_w", "w_uq", "w_ukv", "conv_w", "w_mem_kv", "w_branch", "w_out")
REPLICATED = ("norm_g", "gate_b", "pool_scale", "q_a_norm_g", "kv_a_norm_g", "mla_q_norm_g", "mla_k_norm_g",
              "mem_norm_g", "xattn_q_norm_g", "xattn_k_norm_g")
WEIGHTS = ("norm_g", "w_in", "gate_b", "pool_w", "pool_scale", "q_a_norm_g", "kv_a_norm_g", "w_uq", "w_ukv",
           "mla_q_norm_g", "mla_k_norm_g", "conv_w", "mem_norm_g", "w_mem_kv", "xattn_q_norm_g", "xattn_k_norm_g",
           "w_branch", "w_out")
SHARD_AXIS = {"w_in": 1, "pool_w": 1, "w_uq": 1, "w_ukv": 1, "conv_w": 1, "w_mem_kv": 0, "w_branch": 2, "w_out": 0}


def _cparams(sem=None):
    return pltpu.CompilerParams(dimension_semantics=sem, vmem_limit_bytes=VMEM_LIMIT)


def _tile(n, pref, unit=128):
    if n <= pref:
        return n
    t = (pref // unit) * unit
    while t >= unit:
        if n % t == 0:
            return t
        t -= unit
    return n


def _silu(z):
    return z * jax.nn.sigmoid(z)


def _dsilu(z):
    s = jax.nn.sigmoid(z)
    return s * (1.0 + z * (1.0 - s))


def _dot(a, b):
    return jnp.dot(a, b, preferred_element_type=F32)


def _dot_nt(a, b):
    return lax.dot_general(a, b, (((1,), (1,)), ((), ())), preferred_element_type=F32)


def _dot_tn(a, b):
    return lax.dot_general(a, b, (((0,), (0,)), ((), ())), preferred_element_type=F32)


def _rms(x, g, n):
    r = lax.rsqrt(jnp.sum(x * x, axis=-1, keepdims=True) * (1.0 / n) + EPS)
    return x * r * g


def _rms_bwd(x, g, n, dout):
    r = lax.rsqrt(jnp.sum(x * x, axis=-1, keepdims=True) * (1.0 / n) + EPS)
    y = x * r
    dy = dout * g
    dx = r * (dy - y * (jnp.sum(dy * y, axis=-1, keepdims=True) * (1.0 / n)))
    return dx, dout * y


def _rope(x, ctab, stab):
    lane = lax.broadcasted_iota(jnp.int32, x.shape, 1)
    partner = jnp.where(lane < QK_NOPE + QK_ROPE // 2, pltpu.roll(x, HEAD_PAD - QK_ROPE // 2, 1),
                        pltpu.roll(x, QK_ROPE // 2, 1))
    return x * ctab + partner * stab


def _rope_bwd(d, ctab, stab):
    lane = lax.broadcasted_iota(jnp.int32, d.shape, 1)
    ds = d * stab
    partner = jnp.where(lane < QK_NOPE + QK_ROPE // 2, pltpu.roll(ds, HEAD_PAD - QK_ROPE // 2, 1),
                        pltpu.roll(ds, QK_ROPE // 2, 1))
    return d * ctab + jnp.where((lane >= QK_NOPE) & (lane < QK_HEAD), partner, 0.0)


def _mm(a, b, *, name, trans_b=False, add=None, out_dtype=F32, tm=512, tn=1024, tk=2048):
    batched = a.ndim == 3
    if batched:
        nb, m, k = a.shape
    else:
        m, k = a.shape
    n = b.shape[-2] if trans_b else b.shape[-1]
    tm, tn, tk = _tile(m, tm, 8), _tile(n, tn), _tile(k, tk)
    nk = k // tk

    def body(*refs):
        if add is None:
            a_ref, b_ref, o_ref = refs[:3]
            add_ref = None
            rest = refs[3:]
        else:
            a_ref, b_ref, add_ref, o_ref = refs[:4]
            rest = refs[4:]
        av = a_ref[...].astype(BF16)
        bv = b_ref[...].astype(BF16)
        part = _dot_nt(av, bv) if trans_b else _dot(av, bv)

        def finish(acc):
            if add_ref is not None:
                acc = acc + add_ref[...]
            o_ref[...] = acc.astype(o_ref.dtype)

        if nk == 1:
            finish(part)
        else:
            acc_ref = rest[0]
            kk = pl.program_id(3 if batched else 2)

            @pl.when(kk == 0)
            def _():
                acc_ref[...] = part

            @pl.when(kk > 0)
            def _():
                acc_ref[...] += part

            @pl.when(kk == nk - 1)
            def _():
                finish(acc_ref[...])

    if batched:
        a_spec = pl.BlockSpec((None, tm, tk), lambda bb, i, j, kk: (bb, i, kk))
        b_spec = (pl.BlockSpec((None, tn, tk), lambda bb, i, j, kk: (bb, j, kk)) if trans_b
                  else pl.BlockSpec((None, tk, tn), lambda bb, i, j, kk: (bb, kk, j)))
        o_spec = pl.BlockSpec((None, tm, tn), lambda bb, i, j, kk: (bb, i, j))
        grid = (nb, m // tm, n // tn, nk)
        out_shape = jax.ShapeDtypeStruct((nb, m, n), out_dtype)
        sem = ("parallel", "parallel", "parallel", "arbitrary")
    else:
        a_spec = pl.BlockSpec((tm, tk), lambda i, j, kk: (i, kk))
        b_spec = (pl.BlockSpec((tn, tk), lambda i, j, kk: (j, kk)) if trans_b
                  else pl.BlockSpec((tk, tn), lambda i, j, kk: (kk, j)))
        o_spec = pl.BlockSpec((tm, tn), lambda i, j, kk: (i, j))
        grid = (m // tm, n // tn, nk)
        out_shape = jax.ShapeDtypeStruct((m, n), out_dtype)
        sem = ("parallel", "parallel", "arbitrary")
    in_specs = [a_spec, b_spec] + ([o_spec] if add is not None else [])
    args = (a, b) + ((add,) if add is not None else ())
    scratch = [pltpu.VMEM((tm, tn), F32)] if nk > 1 else []
    return pl.pallas_call(body, out_shape=out_shape, grid=grid, in_specs=in_specs, out_specs=o_spec,
                          scratch_shapes=scratch, compiler_params=_cparams(sem), name=name)(*args)


def _norm_fwd(x, g, *, name, t=256):
    s, d = x.shape
    t = _tile(s, t, 128)

    def body(x_ref, g_ref, h_ref, ht_ref):
        h = _rms(x_ref[...], g_ref[...], d)
        h_ref[...] = h.astype(BF16)
        ht_ref[...] = h.T.astype(BF16)

    return pl.pallas_call(
        body, out_shape=(jax.ShapeDtypeStruct((s, d), BF16), jax.ShapeDtypeStruct((d, s), BF16)),
        grid=(s // t,),
        in_specs=[pl.BlockSpec((t, d), lambda i: (i, 0)), pl.BlockSpec((1, d), lambda i: (0, 0))],
        out_specs=(pl.BlockSpec((t, d), lambda i: (i, 0)), pl.BlockSpec((d, t), lambda i: (0, i))),
        compiler_params=_cparams(("parallel",)), name=name)(x, g)


def _norm_bwd(x, g, dh, dres, *, name, t=256):
    s, d = x.shape
    t = _tile(s, t, 8)

    def body(x_ref, g_ref, dh_ref, dres_ref, dx_ref, dg_ref):
        dx, dgt = _rms_bwd(x_ref[...], g_ref[...], d, dh_ref[...])
        dx_ref[...] = dx + dres_ref[...]
        part = jnp.sum(dgt, axis=0, keepdims=True)

        @pl.when(pl.program_id(0) == 0)
        def _():
            dg_ref[...] = part

        @pl.when(pl.program_id(0) > 0)
        def _():
            dg_ref[...] += part

    row = pl.BlockSpec((t, d), lambda i: (i, 0))
    vec = pl.BlockSpec((1, d), lambda i: (0, 0))
    return pl.pallas_call(
        body, out_shape=(jax.ShapeDtypeStruct((s, d), F32), jax.ShapeDtypeStruct((1, d), F32)),
        grid=(s // t,), in_specs=[row, vec, row, row], out_specs=(row, vec),
        compiler_params=_cparams(("arbitrary",)), name=name)(x, g, dh, dres)


def _loss_head(y, tgt, *, t=256):
    s, d = y.shape
    t = _tile(s, t, 8)

    def body(y_ref, t_ref, dy_ref, l_ref):
        e = y_ref[...] - t_ref[...]
        dy_ref[...] = e * (1.0 / d)
        part = jnp.zeros((1, 128), F32) + jnp.sum(e * e) * (0.5 / d)

        @pl.when(pl.program_id(0) == 0)
        def _():
            l_ref[...] = part

        @pl.when(pl.program_id(0) > 0)
        def _():
            l_ref[...] += part

    row = pl.BlockSpec((t, d), lambda i: (i, 0))
    return pl.pallas_call(
        body, out_shape=(jax.ShapeDtypeStruct((s, d), F32), jax.ShapeDtypeStruct((1, 128), F32)),
        grid=(s // t,), in_specs=[row, row], out_specs=(row, pl.BlockSpec((1, 128), lambda i: (0, 0))),
        compiler_params=_cparams(("arbitrary",)), name="loss_head")(y, tgt)


def _pool_mixed(scr, v, halo, first, row0, t):
    scr[0:POOL_HALO, :] = jnp.where(first, 0.0, halo)
    scr[POOL_HALO:POOL_HALO + t, :] = v
    row = row0 + lax.broadcasted_iota(jnp.int32, (t, 1), 0)
    mixed = []
    for g in range(POOL_GROUPS):
        w = 2 ** (g + 1)
        acc = scr[:, g * POOL_GW:(g + 1) * POOL_GW]
        sh = 1
        while sh < w:
            acc = acc + pltpu.roll(acc, sh, 0)
            sh *= 2
        cnt = jnp.minimum(row + 1, w).astype(F32)
        mixed.append(acc[POOL_HALO:POOL_HALO + t, :] / cnt - v[:, g * POOL_GW:(g + 1) * POOL_GW])
    return mixed


def _pool_fwd(p1, pool_w, pool_scale, *, t=256):
    s = p1.shape[0]
    t = _tile(s, t, 128)
    hb = t // POOL_HALO

    def body(pv_ref, halo_ref, pz_ref, pw_ref, sc_ref, y_ref, yt_ref, scr):
        i = pl.program_id(0)
        mixed = _pool_mixed(scr, pv_ref[...], halo_ref[...], i == 0, i * t, t)
        outs = [_dot(mixed[g].astype(BF16), pw_ref[g]) for g in range(POOL_GROUPS)]
        y = jnp.concatenate(outs, axis=1) * sc_ref[...] * _silu(pz_ref[...])
        y_ref[...] = y.astype(BF16)
        yt_ref[...] = y.T.astype(BF16)

    return pl.pallas_call(
        body, out_shape=(jax.ShapeDtypeStruct((s, BRANCH_W), BF16), jax.ShapeDtypeStruct((BRANCH_W, s), BF16)),
        grid=(s // t,),
        in_specs=[pl.BlockSpec((t, BRANCH_W), lambda i: (i, 0)),
                  pl.BlockSpec((POOL_HALO, BRANCH_W), lambda i: (jnp.maximum(i * hb - 1, 0), 0)),
                  pl.BlockSpec((t, BRANCH_W), lambda i: (i, 1)),
                  pl.BlockSpec((POOL_GROUPS, POOL_GW, POOL_GW), lambda i: (0, 0, 0)),
                  pl.BlockSpec((1, BRANCH_W), lambda i: (0, 0))],
        out_specs=(pl.BlockSpec((t, BRANCH_W), lambda i: (i, 0)), pl.BlockSpec((BRANCH_W, t), lambda i: (0, i))),
        scratch_shapes=[pltpu.VMEM((t + POOL_HALO, BRANCH_W), F32)],
        compiler_params=_cparams(("parallel",)), name="pool_fwd")(p1, p1, p1, pool_w, pool_scale)


def _pool_bwd(p1, dy, pool_w, pool_scale, *, t=256):
    s = p1.shape[0]
    t = _tile(s, t, 128)
    hb = t // POOL_HALO
    nt = s // t
    last_hb = s // POOL_HALO - 1

    def body(pv_ref, halo_ref, pz_ref, pzn_ref, dy_ref, dyn_ref, pw_ref, sc_ref, d_ref, gw_ref, gs_ref, scr, scr2, scr3):
        i = pl.program_id(0)
        mixed = _pool_mixed(scr, pv_ref[...], halo_ref[...], i == 0, i * t, t)
        scale = sc_ref[...]
        pz = pz_ref[...]
        dy = dy_ref[...]
        raw = jnp.concatenate([_dot(mixed[g].astype(BF16), pw_ref[g]) for g in range(POOL_GROUPS)], axis=1)
        d_pool = dy * _silu(pz)
        d_ref[:, BRANCH_W:2 * BRANCH_W] = (dy * raw * scale * _dsilu(pz)).astype(BF16)
        gs_part = jnp.sum(d_pool * raw, axis=0, keepdims=True)
        scr2[0:t, :] = d_pool * scale
        scr2[t:t + POOL_HALO, :] = jnp.where(i == nt - 1, 0.0, dyn_ref[...] * _silu(pzn_ref[...]) * scale)
        row = i * t + lax.broadcasted_iota(jnp.int32, (t + POOL_HALO, 1), 0)
        gw_parts = []
        for g in range(POOL_GROUPS):
            w = 2 ** (g + 1)
            sl = slice(g * POOL_GW, (g + 1) * POOL_GW)
            do_g = scr2[:, sl].astype(BF16)
            dm = _dot_nt(do_g, pw_ref[g])
            gw_parts.append(_dot_tn(mixed[g].astype(BF16), do_g[0:t, :]))
            cnt = jnp.minimum(row + 1, w).astype(F32)
            acc = dm / cnt
            sh = 1
            while sh < w:
                acc = acc + pltpu.roll(acc, t + POOL_HALO - sh, 0)
                sh *= 2
            scr3[:, sl] = acc - dm
        d_ref[:, 0:BRANCH_W] = scr3[0:t, :].astype(BF16)

        @pl.when(i == 0)
        def _():
            for g in range(POOL_GROUPS):
                gw_ref[g] = gw_parts[g]
            gs_ref[...] = gs_part

        @pl.when(i > 0)
        def _():
            for g in range(POOL_GROUPS):
                gw_ref[g] += gw_parts[g]
            gs_ref[...] += gs_part

    tile = lambda col: pl.BlockSpec((t, BRANCH_W), lambda i: (i, col))
    nxt = lambda col: pl.BlockSpec((POOL_HALO, BRANCH_W), lambda i: (jnp.minimum((i + 1) * hb, last_hb), col))
    return pl.pallas_call(
        body,
        out_shape=(jax.ShapeDtypeStruct((s, 2 * BRANCH_W), BF16),
                   jax.ShapeDtypeStruct((POOL_GROUPS, POOL_GW, POOL_GW), F32),
                   jax.ShapeDtypeStruct((1, BRANCH_W), F32)),
        grid=(nt,),
        in_specs=[tile(0), pl.BlockSpec((POOL_HALO, BRANCH_W), lambda i: (jnp.maximum(i * hb - 1, 0), 0)),
                  tile(1), nxt(1), tile(0), nxt(0),
                  pl.BlockSpec((POOL_GROUPS, POOL_GW, POOL_GW), lambda i: (0, 0, 0)),
                  pl.BlockSpec((1, BRANCH_W), lambda i: (0, 0))],
        out_specs=(pl.BlockSpec((t, 2 * BRANCH_W), lambda i: (i, 0)),
                   pl.BlockSpec((POOL_GROUPS, POOL_GW, POOL_GW), lambda i: (0, 0, 0)),
                   pl.BlockSpec((1, BRANCH_W), lambda i: (0, 0))),
        scratch_shapes=[pltpu.VMEM((t + POOL_HALO, BRANCH_W), F32)] * 3,
        compiler_params=_cparams(("arbitrary",)), name="pool_bwd")(p1, p1, p1, p1, dy, dy, pool_w, pool_scale)


CONV_COL = 4


def _conv_taps(scr, u, uh, first, t):
    scr[0:CONV_HALO, :] = jnp.where(first, 0.0, uh)
    scr[CONV_HALO:CONV_HALO + t, :] = u
    e = scr[...]
    u1 = pltpu.roll(e, 1, 0)[CONV_HALO:CONV_HALO + t, :]
    u2 = pltpu.roll(e, 2, 0)[CONV_HALO:CONV_HALO + t, :]
    return u2, u1, u


def _conv_fwd(p1, conv_w, *, t=256):
    s = p1.shape[0]
    t = _tile(s, t, 128)
    hb = t // CONV_HALO

    def body(cb_ref, cc_ref, cx_ref, cz_ref, cch_ref, cxh_ref, w_ref, y_ref, yt_ref, scr):
        i = pl.program_id(0)
        u0, u1, u2 = _conv_taps(scr, cc_ref[...] * cx_ref[...], cch_ref[...] * cxh_ref[...], i == 0, t)
        w = w_ref[...]
        y = (w[0:1, :] * u0 + w[1:2, :] * u1 + w[2:3, :] * u2) * cb_ref[...] * _silu(cz_ref[...])
        y_ref[...] = y.astype(BF16)
        yt_ref[...] = y.T.astype(BF16)

    tile = lambda col: pl.BlockSpec((t, BRANCH_W), lambda i: (i, CONV_COL + col))
    prev = lambda col: pl.BlockSpec((CONV_HALO, BRANCH_W), lambda i: (jnp.maximum(i * hb - 1, 0), CONV_COL + col))
    return pl.pallas_call(
        body, out_shape=(jax.ShapeDtypeStruct((s, BRANCH_W), BF16), jax.ShapeDtypeStruct((BRANCH_W, s), BF16)),
        grid=(s // t,),
        in_specs=[tile(0), tile(1), tile(2), tile(3), prev(1), prev(2), pl.BlockSpec((3, BRANCH_W), lambda i: (0, 0))],
        out_specs=(pl.BlockSpec((t, BRANCH_W), lambda i: (i, 0)), pl.BlockSpec((BRANCH_W, t), lambda i: (0, i))),
        scratch_shapes=[pltpu.VMEM((t + CONV_HALO, BRANCH_W), F32)],
        compiler_params=_cparams(("parallel",)), name="conv_fwd")(p1, p1, p1, p1, p1, p1, conv_w)


def _conv_bwd(p1, dy, conv_w, *, t=256):
    s = p1.shape[0]
    t = _tile(s, t, 128)
    hb = t // CONV_HALO
    nt = s // t
    last_hb = s // CONV_HALO - 1

    def body(cb_ref, cc_ref, cx_ref, cz_ref, cch_ref, cxh_ref, cbn_ref, czn_ref, dy_ref, dyn_ref, w_ref,
             d_ref, g0_ref, g1_ref, g2_ref, scr, scr2):
        i = pl.program_id(0)
        cb, cc, cx, cz = cb_ref[...], cc_ref[...], cx_ref[...], cz_ref[...]
        u0, u1, u2 = _conv_taps(scr, cc * cx, cch_ref[...] * cxh_ref[...], i == 0, t)
        w = w_ref[...]
        w0, w1, w2 = w[0:1, :], w[1:2, :], w[2:3, :]
        y = w0 * u0 + w1 * u1 + w2 * u2
        dy = dy_ref[...]
        sz = _silu(cz)
        d_ref[:, 0:BRANCH_W] = (dy * sz * y).astype(BF16)
        d_ref[:, 3 * BRANCH_W:4 * BRANCH_W] = (dy * cb * y * _dsilu(cz)).astype(BF16)
        d_y = dy * sz * cb
        parts = [jnp.sum(d_y * u, axis=0, keepdims=True) for u in (u0, u1, u2)]
        scr2[0:t, :] = d_y
        scr2[t:t + CONV_HALO, :] = jnp.where(i == nt - 1, 0.0, dyn_ref[...] * _silu(czn_ref[...]) * cbn_ref[...])
        e = scr2[...]
        n = t + CONV_HALO
        du = (w2 * e + w1 * pltpu.roll(e, n - 1, 0) + w0 * pltpu.roll(e, n - 2, 0))[0:t, :]
        d_ref[:, BRANCH_W:2 * BRANCH_W] = (du * cx).astype(BF16)
        d_ref[:, 2 * BRANCH_W:3 * BRANCH_W] = (du * cc).astype(BF16)

        @pl.when(i == 0)
        def _():
            g0_ref[...] = parts[0]
            g1_ref[...] = parts[1]
            g2_ref[...] = parts[2]

        @pl.when(i > 0)
        def _():
            g0_ref[...] += parts[0]
            g1_ref[...] += parts[1]
            g2_ref[...] += parts[2]

    tile = lambda col: pl.BlockSpec((t, BRANCH_W), lambda i: (i, CONV_COL + col))
    prev = lambda col: pl.BlockSpec((CONV_HALO, BRANCH_W), lambda i: (jnp.maximum(i * hb - 1, 0), CONV_COL + col))
    nxt = lambda col: pl.BlockSpec((CONV_HALO, BRANCH_W), lambda i: (jnp.minimum((i + 1) * hb, last_hb), CONV_COL + col))
    vec = pl.BlockSpec((1, BRANCH_W), lambda i: (0, 0))
    gshape = jax.ShapeDtypeStruct((1, BRANCH_W), F32)
    return pl.pallas_call(
        body, out_shape=(jax.ShapeDtypeStruct((s, 4 * BRANCH_W), BF16), gshape, gshape, gshape),
        grid=(nt,),
        in_specs=[tile(0), tile(1), tile(2), tile(3), prev(1), prev(2), nxt(0), nxt(3),
                  pl.BlockSpec((t, BRANCH_W), lambda i: (i, 0)),
                  pl.BlockSpec((CONV_HALO, BRANCH_W), lambda i: (jnp.minimum((i + 1) * hb, last_hb), 0)),
                  pl.BlockSpec((3, BRANCH_W), lambda i: (0, 0))],
        out_specs=(pl.BlockSpec((t, 4 * BRANCH_W), lambda i: (i, 0)), vec, vec, vec),
        scratch_shapes=[pltpu.VMEM((t + CONV_HALO, BRANCH_W), F32)] * 2,
        compiler_params=_cparams(("arbitrary",)), name="conv_bwd")(p1, p1, p1, p1, p1, p1, p1, p1, dy, dy, conv_w)


def _mla_prep_fwd(p1, kr, ctab, stab, qa_g, kva_g, w_uq, w_ukv, q_g, k_g, *, t=256):
    s = p1.shape[0]
    t = _tile(s, t, 128)
    ql, kvl = qa_g.shape[1], kva_g.shape[1]
    assert ql == kvl and 2048 % ql == 0
    cq_blk = 2048 // ql

    def body(cq_ref, ckv_ref, kr_ref, c_ref, s_ref, qag_ref, kvag_ref, wuq_ref, wukv_ref, qg_ref, kg_ref,
             qf_ref, kf_ref, v_ref):
        q_raw = _dot(_rms(cq_ref[...], qag_ref[...], ql).astype(BF16), wuq_ref[...])
        kv_raw = _dot(_rms(ckv_ref[...], kvag_ref[...], kvl).astype(BF16), wukv_ref[...])
        krp = kr_ref[...]
        ct, st = c_ref[...], s_ref[...]
        for h in range(MLA_HEADS):
            qh = q_raw[:, h * HEAD_PAD:(h + 1) * HEAD_PAD]
            qf_ref[h] = _rope(_rms(qh, qg_ref[...], QK_HEAD), ct, st).astype(BF16)
            kh = jnp.concatenate([kv_raw[:, h * HEAD_PAD:h * HEAD_PAD + QK_NOPE], krp], axis=1)
            kf_ref[h] = _rope(_rms(kh, kg_ref[...], QK_HEAD), ct, st).astype(BF16)
            v_ref[h] = kv_raw[:, h * HEAD_PAD + QK_NOPE:(h + 1) * HEAD_PAD].astype(BF16)

    full = lambda shp: pl.BlockSpec(shp, lambda i: tuple(0 for _ in shp))
    return pl.pallas_call(
        body,
        out_shape=(jax.ShapeDtypeStruct((MLA_HEADS, s, HEAD_PAD), BF16), jax.ShapeDtypeStruct((MLA_HEADS, s, HEAD_PAD), BF16),
                   jax.ShapeDtypeStruct((MLA_HEADS, s, V_HEAD), BF16)),
        grid=(s // t,),
        in_specs=[pl.BlockSpec((t, ql), lambda i: (i, cq_blk)), pl.BlockSpec((t, kvl), lambda i: (i, cq_blk + 1)),
                  pl.BlockSpec((t, 128), lambda i: (i, 0)),
                  pl.BlockSpec((t, HEAD_PAD), lambda i: (i, 0)), pl.BlockSpec((t, HEAD_PAD), lambda i: (i, 0)),
                  full((1, ql)), full((1, kvl)), full(w_uq.shape), full(w_ukv.shape), full((1, HEAD_PAD)), full((1, HEAD_PAD))],
        out_specs=(pl.BlockSpec((MLA_HEADS, t, HEAD_PAD), lambda i: (0, i, 0)),
                   pl.BlockSpec((MLA_HEADS, t, HEAD_PAD), lambda i: (0, i, 0)),
                   pl.BlockSpec((MLA_HEADS, t, V_HEAD), lambda i: (0, i, 0))),
        compiler_params=_cparams(("parallel",)), name="mla_prep_fwd")(
            p1, p1, kr, ctab, stab, qa_g, kva_g, w_uq, w_ukv, q_g, k_g)


def _mla_prep_bwd(p1, kr, ctab, stab, qa_g, kva_g, w_uq, w_ukv, q_g, k_g, dqf, dkf, dv, *, t=256):
    s = p1.shape[0]
    t = _tile(s, t, 128)
    ql, kvl = qa_g.shape[1], kva_g.shape[1]
    cq_blk = 2048 // ql
    nq = MLA_HEADS * HEAD_PAD

    def body(cq_ref, ckv_ref, kr_ref, c_ref, s_ref, qag_ref, kvag_ref, wuq_ref, wukv_ref, qg_ref, kg_ref,
             dqf_ref, dkf_ref, dv_ref,
             dc_ref, dkr_ref, dqraw_ref, dkvraw_ref, cqnt_ref, ckvnt_ref, gqa_ref, gkva_ref, gqg_ref, gkg_ref):
        i = pl.program_id(0)
        cq, ckv = cq_ref[...], ckv_ref[...]
        cqn = _rms(cq, qag_ref[...], ql)
        ckvn = _rms(ckv, kvag_ref[...], kvl)
        cqnt_ref[...] = cqn.T.astype(BF16)
        ckvnt_ref[...] = ckvn.T.astype(BF16)
        q_raw = _dot(cqn.astype(BF16), wuq_ref[...])
        kv_raw = _dot(ckvn.astype(BF16), wukv_ref[...])
        krp = kr_ref[...]
        ct, st = c_ref[...], s_ref[...]
        gqg = jnp.zeros((1, HEAD_PAD), F32)
        gkg = jnp.zeros((1, HEAD_PAD), F32)
        dkr = jnp.zeros((t, HEAD_PAD - QK_NOPE), F32)
        dq_parts, dkv_parts = [], []
        for h in range(MLA_HEADS):
            qh = q_raw[:, h * HEAD_PAD:(h + 1) * HEAD_PAD]
            dx, dg = _rms_bwd(qh, qg_ref[...], QK_HEAD, _rope_bwd(dqf_ref[h], ct, st))
            gqg = gqg + jnp.sum(dg, axis=0, keepdims=True)
            dq_parts.append(dx)
            kh = jnp.concatenate([kv_raw[:, h * HEAD_PAD:h * HEAD_PAD + QK_NOPE], krp], axis=1)
            dx, dg = _rms_bwd(kh, kg_ref[...], QK_HEAD, _rope_bwd(dkf_ref[h], ct, st))
            gkg = gkg + jnp.sum(dg, axis=0, keepdims=True)
            dkr = dkr + dx[:, QK_NOPE:HEAD_PAD]
            dkv_parts += [dx[:, 0:QK_NOPE], dv_ref[h]]
        dq_raw = jnp.concatenate(dq_parts, axis=1).astype(BF16)
        dkv_raw = jnp.concatenate(dkv_parts, axis=1).astype(BF16)
        dqraw_ref[...] = dq_raw
        dkvraw_ref[...] = dkv_raw
        dkr_ref[...] = dkr.astype(BF16)
        dcq, gqa = _rms_bwd(cq, qag_ref[...], ql, _dot_nt(dq_raw, wuq_ref[...]))
        dckv, gkva = _rms_bwd(ckv, kvag_ref[...], kvl, _dot_nt(dkv_raw, wukv_ref[...]))
        dc_ref[:, 0:ql] = dcq.astype(BF16)
        dc_ref[:, ql:ql + kvl] = dckv.astype(BF16)
        gqa = jnp.sum(gqa, axis=0, keepdims=True)
        gkva = jnp.sum(gkva, axis=0, keepdims=True)

        @pl.when(i == 0)
        def _():
            gqa_ref[...] = gqa
            gkva_ref[...] = gkva
            gqg_ref[...] = gqg
            gkg_ref[...] = gkg

        @pl.when(i > 0)
        def _():
            gqa_ref[...] += gqa
            gkva_ref[...] += gkva
            gqg_ref[...] += gqg
            gkg_ref[...] += gkg

    full = lambda shp: pl.BlockSpec(shp, lambda i: tuple(0 for _ in shp))
    hblk = lambda w: pl.BlockSpec((MLA_HEADS, t, w), lambda i: (0, i, 0))
    sds = jax.ShapeDtypeStruct
    return pl.pallas_call(
        body,
        out_shape=(sds((s, ql + kvl), BF16), sds((s, 128), BF16), sds((s, nq), BF16), sds((s, nq), BF16),
                   sds((ql, s), BF16), sds((kvl, s), BF16),
                   sds((1, ql), F32), sds((1, kvl), F32), sds((1, HEAD_PAD), F32), sds((1, HEAD_PAD), F32)),
        grid=(s // t,),
        in_specs=[pl.BlockSpec((t, ql), lambda i: (i, cq_blk)), pl.BlockSpec((t, kvl), lambda i: (i, cq_blk + 1)),
                  pl.BlockSpec((t, 128), lambda i: (i, 0)),
                  pl.BlockSpec((t, HEAD_PAD), lambda i: (i, 0)), pl.BlockSpec((t, HEAD_PAD), lambda i: (i, 0)),
                  full((1, ql)), full((1, kvl)), full(w_uq.shape), full(w_ukv.shape), full((1, HEAD_PAD)), full((1, HEAD_PAD)),
                  hblk(HEAD_PAD), hblk(HEAD_PAD), hblk(V_HEAD)],
        out_specs=(pl.BlockSpec((t, ql + kvl), lambda i: (i, 0)), pl.BlockSpec((t, 128), lambda i: (i, 0)),
                   pl.BlockSpec((t, nq), lambda i: (i, 0)), pl.BlockSpec((t, nq), lambda i: (i, 0)),
                   pl.BlockSpec((ql, t), lambda i: (0, i)), pl.BlockSpec((kvl, t), lambda i: (0, i)),
                   full((1, ql)), full((1, kvl)), full((1, HEAD_PAD)), full((1, HEAD_PAD))),
        compiler_params=_cparams(("arbitrary",)), name="mla_prep_bwd")(
            p1, p1, kr, ctab, stab, qa_g, kva_g, w_uq, w_ukv, q_g, k_g, dqf, dkf, dv)


MZ_BLK128 = 3072 // 128
ATT_SCALE = QK_HEAD ** -0.5


HPS = 2


def _causal_pairs(nq, by_query):
    if by_query:
        prs = [(qi, ki) for qi in range(nq) for ki in range(qi + 1)]
    else:
        prs = [(qi, ki) for ki in range(nq) for qi in range(ki, nq)]
    return (jnp.asarray([p[0] for p in prs], jnp.int32), jnp.asarray([p[1] for p in prs], jnp.int32), len(prs))


def _attn_fwd(qf, kf, vv, p1, *, tq):
    nh, s, _ = qf.shape
    nq = s // tq
    qtab, ktab, npairs = _causal_pairs(nq, True)
    wv = HPS * V_HEAD

    def body(qt_ref, kt_ref, q_ref, k_ref, v_ref, mz_ref, y_ref, yt_ref, o_ref, lse_ref, lset_ref, m_scr, l_scr, acc_scr):
        pr = pl.program_id(1)
        qi, ki = qt_ref[pr], kt_ref[pr]

        @pl.when(ki == 0)
        def _():
            m_scr[...] = jnp.full((HPS, tq, 1), NEG, F32)
            l_scr[...] = jnp.zeros((HPS, tq, 1), F32)
            acc_scr[...] = jnp.zeros((HPS, tq, V_HEAD), F32)

        def step(diagonal):
            for u in range(HPS):
                sc = _dot_nt(q_ref[u], k_ref[u]) * ATT_SCALE
                if diagonal:
                    r = lax.broadcasted_iota(jnp.int32, (tq, tq), 0)
                    c = lax.broadcasted_iota(jnp.int32, (tq, tq), 1)
                    sc = jnp.where(c <= r, sc, NEG)
                m_old = m_scr[u]
                m_new = jnp.maximum(m_old, jnp.max(sc, axis=-1, keepdims=True))
                alpha = jnp.exp(m_old - m_new)
                p = jnp.exp(sc - m_new)
                l_scr[u] = alpha * l_scr[u] + jnp.sum(p, axis=-1, keepdims=True)
                acc_scr[u] = alpha * acc_scr[u] + _dot(p.astype(BF16), v_ref[u])
                m_scr[u] = m_new

        @pl.when(ki < qi)
        def _():
            step(False)

        @pl.when(ki == qi)
        def _():
            step(True)
            outs = []
            for u in range(HPS):
                l = l_scr[u]
                outs.append(acc_scr[u] / l)
                lse = m_scr[u] + jnp.log(l)
                lse_ref[u] = lse
                lset_ref[u] = jnp.broadcast_to(lse, (tq, 128)).T[0:1, :]
            o = jnp.concatenate(outs, axis=1)
            o_ref[...] = o
            y = o * _silu(mz_ref[...])
            y_ref[...] = y.astype(BF16)
            yt_ref[...] = y.T.astype(BF16)

    sds = jax.ShapeDtypeStruct
    gs = pltpu.PrefetchScalarGridSpec(
        num_scalar_prefetch=2, grid=(nh // HPS, npairs),
        in_specs=[pl.BlockSpec((HPS, tq, HEAD_PAD), lambda h, p, qt, kt: (h, qt[p], 0)),
                  pl.BlockSpec((HPS, tq, HEAD_PAD), lambda h, p, qt, kt: (h, kt[p], 0)),
                  pl.BlockSpec((HPS, tq, V_HEAD), lambda h, p, qt, kt: (h, kt[p], 0)),
                  pl.BlockSpec((tq, wv), lambda h, p, qt, kt: (qt[p], MZ_BLK128 // HPS + h))],
        out_specs=(pl.BlockSpec((tq, wv), lambda h, p, qt, kt: (qt[p], h)),
                   pl.BlockSpec((wv, tq), lambda h, p, qt, kt: (h, qt[p])),
                   pl.BlockSpec((tq, wv), lambda h, p, qt, kt: (qt[p], h)),
                   pl.BlockSpec((HPS, tq, 1), lambda h, p, qt, kt: (h, qt[p], 0)),
                   pl.BlockSpec((HPS, 1, tq), lambda h, p, qt, kt: (h, 0, qt[p]))),
        scratch_shapes=[pltpu.VMEM((HPS, tq, 1), F32), pltpu.VMEM((HPS, tq, 1), F32), pltpu.VMEM((HPS, tq, V_HEAD), F32)])
    return pl.pallas_call(
        body,
        out_shape=(sds((s, BRANCH_W), BF16), sds((BRANCH_W, s), BF16), sds((s, BRANCH_W), F32),
                   sds((nh, s, 1), F32), sds((nh, 1, s), F32)),
        grid_spec=gs, compiler_params=_cparams(("parallel", "arbitrary")), name="attn_fwd")(qtab, ktab, qf, kf, vv, p1)


def _attn_bwd_pre(dy4, o, p1, *, t=256):
    s = o.shape[0]
    t = _tile(s, t, 128)

    def body(dy_ref, o_ref, mz_ref, do_ref, dmz_ref, dl_ref, dlt_ref):
        dy, o_, mz = dy_ref[...], o_ref[...], mz_ref[...]
        do = dy * _silu(mz)
        do_ref[...] = do.astype(BF16)
        dmz_ref[...] = (dy * o_ * _dsilu(mz)).astype(BF16)
        prod = do * o_
        for h in range(MLA_HEADS):
            dl = jnp.sum(prod[:, h * V_HEAD:(h + 1) * V_HEAD], axis=-1, keepdims=True)
            dl_ref[h] = dl
            dlt_ref[h] = jnp.broadcast_to(dl, (t, 128)).T[0:1, :]

    sds = jax.ShapeDtypeStruct
    return pl.pallas_call(
        body,
        out_shape=(sds((s, BRANCH_W), BF16), sds((s, BRANCH_W), BF16), sds((MLA_HEADS, s, 1), F32), sds((MLA_HEADS, 1, s), F32)),
        grid=(s // t,),
        in_specs=[pl.BlockSpec((None, t, BRANCH_W), lambda i: (1, i, 0)), pl.BlockSpec((t, BRANCH_W), lambda i: (i, 0)),
                  pl.BlockSpec((t, BRANCH_W), lambda i: (i, 3))],
        out_specs=(pl.BlockSpec((t, BRANCH_W), lambda i: (i, 0)), pl.BlockSpec((t, BRANCH_W), lambda i: (i, 0)),
                   pl.BlockSpec((MLA_HEADS, t, 1), lambda i: (0, i, 0)), pl.BlockSpec((MLA_HEADS, 1, t), lambda i: (0, 0, i))),
        compiler_params=_cparams(("parallel",)), name="attn_bwd_pre")(dy4, o, p1)


def _attn_bwd_dq(qf, kf, vv, do, lse, delta, *, tq):
    nh, s, _ = qf.shape
    nq = s // tq
    qtab, ktab, npairs = _causal_pairs(nq, True)
    wv = HPS * V_HEAD

    def body(qt_ref, kt_ref, q_ref, k_ref, v_ref, do_ref, lse_ref, dl_ref, dq_ref, acc_scr):
        pr = pl.program_id(1)
        qi, ki = qt_ref[pr], kt_ref[pr]

        @pl.when(ki == 0)
        def _():
            acc_scr[...] = jnp.zeros((HPS, tq, HEAD_PAD), F32)

        def step(diagonal):
            for u in range(HPS):
                k = k_ref[u]
                sc = _dot_nt(q_ref[u], k) * ATT_SCALE
                p = jnp.exp(sc - lse_ref[u])
                if diagonal:
                    r = lax.broadcasted_iota(jnp.int32, (tq, tq), 0)
                    c = lax.broadcasted_iota(jnp.int32, (tq, tq), 1)
                    p = jnp.where(c <= r, p, 0.0)
                dp = _dot_nt(do_ref[:, u * V_HEAD:(u + 1) * V_HEAD], v_ref[u])
                ds = p * (dp - dl_ref[u]) * ATT_SCALE
                acc_scr[u] += _dot(ds.astype(BF16), k)

        @pl.when(ki < qi)
        def _():
            step(False)

        @pl.when(ki == qi)
        def _():
            step(True)
            dq_ref[...] = acc_scr[...]

    gs = pltpu.PrefetchScalarGridSpec(
        num_scalar_prefetch=2, grid=(nh // HPS, npairs),
        in_specs=[pl.BlockSpec((HPS, tq, HEAD_PAD), lambda h, p, qt, kt: (h, qt[p], 0)),
                  pl.BlockSpec((HPS, tq, HEAD_PAD), lambda h, p, qt, kt: (h, kt[p], 0)),
                  pl.BlockSpec((HPS, tq, V_HEAD), lambda h, p, qt, kt: (h, kt[p], 0)),
                  pl.BlockSpec((tq, wv), lambda h, p, qt, kt: (qt[p], h)),
                  pl.BlockSpec((HPS, tq, 1), lambda h, p, qt, kt: (h, qt[p], 0)),
                  pl.BlockSpec((HPS, tq, 1), lambda h, p, qt, kt: (h, qt[p], 0))],
        out_specs=pl.BlockSpec((HPS, tq, HEAD_PAD), lambda h, p, qt, kt: (h, qt[p], 0)),
        scratch_shapes=[pltpu.VMEM((HPS, tq, HEAD_PAD), F32)])
    return pl.pallas_call(
        body, out_shape=jax.ShapeDtypeStruct((nh, s, HEAD_PAD), F32), grid_spec=gs,
        compiler_params=_cparams(("parallel", "arbitrary")), name="attn_bwd_dq")(qtab, ktab, qf, kf, vv, do, lse, delta)


def _attn_bwd_dkv(qf, kf, vv, do, lset, deltat, *, tq):
    nh, s, _ = qf.shape
    nq = s // tq
    qtab, ktab, npairs = _causal_pairs(nq, False)
    wv = HPS * V_HEAD

    def body(qt_ref, kt_ref, k_ref, v_ref, q_ref, do_ref, lse_ref, dl_ref, dk_ref, dv_ref, dk_scr, dv_scr):
        pr = pl.program_id(1)
        qi, ki = qt_ref[pr], kt_ref[pr]

        def step(diagonal):
            for u in range(HPS):
                q = q_ref[u]
                do_ = do_ref[:, u * V_HEAD:(u + 1) * V_HEAD]
                st = _dot_nt(k_ref[u], q) * ATT_SCALE
                pt = jnp.exp(st - lse_ref[u])
                if diagonal:
                    r = lax.broadcasted_iota(jnp.int32, (tq, tq), 0)
                    c = lax.broadcasted_iota(jnp.int32, (tq, tq), 1)
                    pt = jnp.where(r <= c, pt, 0.0)
                dpt = _dot_nt(v_ref[u], do_)
                dst = pt * (dpt - dl_ref[u]) * ATT_SCALE
                if diagonal:
                    dv_scr[u] = _dot(pt.astype(BF16), do_)
                    dk_scr[u] = _dot(dst.astype(BF16), q)
                else:
                    dv_scr[u] += _dot(pt.astype(BF16), do_)
                    dk_scr[u] += _dot(dst.astype(BF16), q)

        @pl.when(qi == ki)
        def _():
            step(True)

        @pl.when(qi > ki)
        def _():
            step(False)

        @pl.when(qi == nq - 1)
        def _():
            dk_ref[...] = dk_scr[...]
            dv_ref[...] = dv_scr[...]

    sds = jax.ShapeDtypeStruct
    gs = pltpu.PrefetchScalarGridSpec(
        num_scalar_prefetch=2, grid=(nh // HPS, npairs),
        in_specs=[pl.BlockSpec((HPS, tq, HEAD_PAD), lambda h, p, qt, kt: (h, kt[p], 0)),
                  pl.BlockSpec((HPS, tq, V_HEAD), lambda h, p, qt, kt: (h, kt[p], 0)),
                  pl.BlockSpec((HPS, tq, HEAD_PAD), lambda h, p, qt, kt: (h, qt[p], 0)),
                  pl.BlockSpec((tq, wv), lambda h, p, qt, kt: (qt[p], h)),
                  pl.BlockSpec((HPS, 1, tq), lambda h, p, qt, kt: (h, 0, qt[p])),
                  pl.BlockSpec((HPS, 1, tq), lambda h, p, qt, kt: (h, 0, qt[p]))],
        out_specs=(pl.BlockSpec((HPS, tq, HEAD_PAD), lambda h, p, qt, kt: (h, kt[p], 0)),
                   pl.BlockSpec((HPS, tq, V_HEAD), lambda h, p, qt, kt: (h, kt[p], 0))),
        scratch_shapes=[pltpu.VMEM((HPS, tq, HEAD_PAD), F32), pltpu.VMEM((HPS, tq, V_HEAD), F32)])
    return pl.pallas_call(
        body, out_shape=(sds((nh, s, HEAD_PAD), F32), sds((nh, s, V_HEAD), F32)), grid_spec=gs,
        compiler_params=_cparams(("parallel", "arbitrary")), name="attn_bwd_dkv")(qtab, ktab, kf, vv, qf, do, lset, deltat)


XATT_SCALE = XATTN_HEAD_DIM ** -0.5
XQ_COL = 8


def _memkv_prep(mem_kv, k_g):
    m = mem_kv.shape[0]

    def body(kv_ref, g_ref, k_ref, v_ref):
        for h in range(XATTN_HEADS):
            sl = slice(h * XATTN_HEAD_DIM, (h + 1) * XATTN_HEAD_DIM)
            k_ref[:, sl] = _rms(kv_ref[:, sl], g_ref[...], XATTN_HEAD_DIM).astype(BF16)
        v_ref[...] = kv_ref[:, BRANCH_W:2 * BRANCH_W].astype(BF16)

    sds = jax.ShapeDtypeStruct
    return pl.pallas_call(body, out_shape=(sds((m, BRANCH_W), BF16), sds((m, BRANCH_W), BF16)),
                          compiler_params=_cparams(), name="memkv_prep")(mem_kv, k_g)


def _memkv_prep_bwd(mem_kv, k_g, dk, dv):
    m = mem_kv.shape[0]

    def body(kv_ref, g_ref, dk_ref, dv_ref, d_ref, gk_ref):
        gk = jnp.zeros((1, XATTN_HEAD_DIM), F32)
        for h in range(XATTN_HEADS):
            sl = slice(h * XATTN_HEAD_DIM, (h + 1) * XATTN_HEAD_DIM)
            dx, dg = _rms_bwd(kv_ref[:, sl], g_ref[...], XATTN_HEAD_DIM, dk_ref[:, sl])
            d_ref[:, sl] = dx.astype(BF16)
            gk = gk + jnp.sum(dg, axis=0, keepdims=True)
        d_ref[:, BRANCH_W:2 * BRANCH_W] = dv_ref[...].astype(BF16)
        gk_ref[...] = gk

    sds = jax.ShapeDtypeStruct
    return pl.pallas_call(body, out_shape=(sds((m, 2 * BRANCH_W), BF16), sds((1, XATTN_HEAD_DIM), F32)),
                          compiler_params=_cparams(), name="memkv_prep_bwd")(mem_kv, k_g, dk, dv)


def _xattn_probs(xq, k_ref, qg, h):
    sl = slice(h * XATTN_HEAD_DIM, (h + 1) * XATTN_HEAD_DIM)
    q = _rms(xq[:, sl], qg, XATTN_HEAD_DIM).astype(BF16)
    sc = _dot_nt(q, k_ref[:, sl]) * XATT_SCALE
    e = jnp.exp(sc - jnp.max(sc, axis=-1, keepdims=True))
    return q, e / jnp.sum(e, axis=-1, keepdims=True)


def _xattn_fwd(p1, kx, vx, q_g, *, t=256):
    s = p1.shape[0]
    m = kx.shape[0]
    t = _tile(s, t, 128)

    def body(xq_ref, xz_ref, k_ref, v_ref, g_ref, y_ref, yt_ref):
        xq = xq_ref[...]
        outs = []
        for h in range(XATTN_HEADS):
            _, p = _xattn_probs(xq, k_ref, g_ref[...], h)
            outs.append(_dot(p.astype(BF16), v_ref[:, h * XATTN_HEAD_DIM:(h + 1) * XATTN_HEAD_DIM]))
        y = jnp.concatenate(outs, axis=1) * _silu(xz_ref[...])
        y_ref[...] = y.astype(BF16)
        yt_ref[...] = y.T.astype(BF16)

    full = lambda shp: pl.BlockSpec(shp, lambda i: tuple(0 for _ in shp))
    sds = jax.ShapeDtypeStruct
    return pl.pallas_call(
        body, out_shape=(sds((s, BRANCH_W), BF16), sds((BRANCH_W, s), BF16)), grid=(s // t,),
        in_specs=[pl.BlockSpec((t, BRANCH_W), lambda i: (i, XQ_COL)), pl.BlockSpec((t, BRANCH_W), lambda i: (i, XQ_COL + 1)),
                  full((m, BRANCH_W)), full((m, BRANCH_W)), full((1, XATTN_HEAD_DIM))],
        out_specs=(pl.BlockSpec((t, BRANCH_W), lambda i: (i, 0)), pl.BlockSpec((BRANCH_W, t), lambda i: (0, i))),
        compiler_params=_cparams(("parallel",)), name="xattn_fwd")(p1, p1, kx, vx, q_g)


def _xattn_bwd(p1, dy4, kx, vx, q_g, *, t=256):
    s = p1.shape[0]
    m = kx.shape[0]
    t = _tile(s, t, 128)

    def body(xq_ref, xz_ref, dy_ref, k_ref, v_ref, g_ref, d_ref, dk_ref, dv_ref, gq_ref):
        i = pl.program_id(0)
        xq, xz, dy = xq_ref[...], xz_ref[...], dy_ref[...]
        do = dy * _silu(xz)
        gq = jnp.zeros((1, XATTN_HEAD_DIM), F32)
        outs, dks, dvs = [], [], []
        for h in range(XATTN_HEADS):
            sl = slice(h * XATTN_HEAD_DIM, (h + 1) * XATTN_HEAD_DIM)
            q, p = _xattn_probs(xq, k_ref, g_ref[...], h)
            pb = p.astype(BF16)
            outs.append(_dot(pb, v_ref[:, sl]))
            do_h = do[:, sl].astype(BF16)
            dvs.append(_dot_tn(pb, do_h))
            dp = _dot_nt(do_h, v_ref[:, sl])
            ds = (p * (dp - jnp.sum(p * dp, axis=-1, keepdims=True)) * XATT_SCALE).astype(BF16)
            dks.append(_dot_tn(ds, q))
            dx, dg = _rms_bwd(xq[:, sl], g_ref[...], XATTN_HEAD_DIM, _dot(ds, k_ref[:, sl]))
            d_ref[:, sl] = dx.astype(BF16)
            gq = gq + jnp.sum(dg, axis=0, keepdims=True)
        o = jnp.concatenate(outs, axis=1)
        d_ref[:, BRANCH_W:2 * BRANCH_W] = (dy * o * _dsilu(xz)).astype(BF16)
        dk = jnp.concatenate(dks, axis=1)
        dv = jnp.concatenate(dvs, axis=1)

        @pl.when(i == 0)
        def _():
            dk_ref[...] = dk
            dv_ref[...] = dv
            gq_ref[...] = gq

        @pl.when(i > 0)
        def _():
            dk_ref[...] += dk
            dv_ref[...] += dv
            gq_ref[...] += gq

    full = lambda shp: pl.BlockSpec(shp, lambda i: tuple(0 for _ in shp))
    sds = jax.ShapeDtypeStruct
    return pl.pallas_call(
        body, out_shape=(sds((s, 2 * BRANCH_W), BF16), sds((m, BRANCH_W), F32), sds((m, BRANCH_W), F32), sds((1, XATTN_HEAD_DIM), F32)),
        grid=(s // t,),
        in_specs=[pl.BlockSpec((t, BRANCH_W), lambda i: (i, XQ_COL)), pl.BlockSpec((t, BRANCH_W), lambda i: (i, XQ_COL + 1)),
                  pl.BlockSpec((None, t, BRANCH_W), lambda i: (3, i, 0)),
                  full((m, BRANCH_W)), full((m, BRANCH_W)), full((1, XATTN_HEAD_DIM))],
        out_specs=(pl.BlockSpec((t, 2 * BRANCH_W), lambda i: (i, 0)), full((m, BRANCH_W)), full((m, BRANCH_W)),
                   full((1, XATTN_HEAD_DIM))),
        compiler_params=_cparams(("arbitrary",)), name="xattn_bwd")(p1, p1, dy4, kx, vx, q_g)


def _gate_fwd(ystack, w_branch, gp, gate_b, *, tm=512, tn=512):
    _, s, _ = ystack.shape
    d = w_branch.shape[2]
    tm, tn = _tile(s, tm, 128), _tile(d, tn)
    nj = d // tn

    def body(y_ref, w_ref, gp_ref, gb_ref, o_ref, ot_ref, acc_scr):
        b = pl.program_id(2)
        part = jax.nn.sigmoid(gp_ref[...] + gb_ref[...]) * _dot(y_ref[...], w_ref[...])

        @pl.when(b == 0)
        def _():
            acc_scr[...] = part

        @pl.when(b > 0)
        def _():
            acc_scr[...] += part

        @pl.when(b == N_BRANCH - 1)
        def _():
            acc = acc_scr[...]
            o_ref[...] = acc.astype(BF16)
            ot_ref[...] = acc.T.astype(BF16)

    sds = jax.ShapeDtypeStruct
    return pl.pallas_call(
        body, out_shape=(sds((s, d), BF16), sds((d, s), BF16)), grid=(s // tm, nj, N_BRANCH),
        in_specs=[pl.BlockSpec((None, tm, BRANCH_W), lambda i, j, b: (b, i, 0)),
                  pl.BlockSpec((None, BRANCH_W, tn), lambda i, j, b: (b, 0, j)),
                  pl.BlockSpec((tm, tn), lambda i, j, b: (i, b * nj + j)),
                  pl.BlockSpec((1, tn), lambda i, j, b: (0, b * nj + j))],
        out_specs=(pl.BlockSpec((tm, tn), lambda i, j, b: (i, j)), pl.BlockSpec((tn, tm), lambda i, j, b: (j, i))),
        scratch_shapes=[pltpu.VMEM((tm, tn), F32)],
        compiler_params=_cparams(("parallel", "parallel", "arbitrary")), name="gate_fwd")(ystack, w_branch, gp, gate_b)


def _gate_bwd(ystack, w_branch, gp, gate_b, dm, *, tm=512, tn=512):
    _, s, _ = ystack.shape
    d = w_branch.shape[2]
    tm, tn = _tile(s, tm, 128), _tile(d, tn)
    nj = d // tn

    def body(y_ref, w_ref, gp_ref, gb_ref, dm_ref, dp_ref, dg_ref, gb_out_ref):
        i = pl.program_id(2)
        proj = _dot(y_ref[...], w_ref[...])
        gate = jax.nn.sigmoid(gp_ref[...] + gb_ref[...])
        dmv = dm_ref[...]
        dp_ref[...] = (dmv * gate).astype(BF16)
        dpre = dmv * proj * gate * (1.0 - gate)
        dg_ref[...] = dpre.astype(BF16)
        part = jnp.sum(dpre, axis=0, keepdims=True)

        @pl.when(i == 0)
        def _():
            gb_out_ref[...] = part

        @pl.when(i > 0)
        def _():
            gb_out_ref[...] += part

    sds = jax.ShapeDtypeStruct
    return pl.pallas_call(
        body, out_shape=(sds((N_BRANCH, s, d), BF16), sds((s, N_BRANCH * d), BF16), sds((1, N_BRANCH * d), F32)),
        grid=(N_BRANCH, nj, s // tm),
        in_specs=[pl.BlockSpec((None, tm, BRANCH_W), lambda b, j, i: (b, i, 0)),
                  pl.BlockSpec((None, BRANCH_W, tn), lambda b, j, i: (b, 0, j)),
                  pl.BlockSpec((tm, tn), lambda b, j, i: (i, b * nj + j)),
                  pl.BlockSpec((1, tn), lambda b, j, i: (0, b * nj + j)),
                  pl.BlockSpec((tm, tn), lambda b, j, i: (i, j))],
        out_specs=(pl.BlockSpec((None, tm, tn), lambda b, j, i: (b, i, j)),
                   pl.BlockSpec((tm, tn), lambda b, j, i: (i, b * nj + j)),
                   pl.BlockSpec((1, tn), lambda b, j, i: (0, b * nj + j))),
        compiler_params=_cparams(("parallel", "parallel", "arbitrary")), name="gate_bwd")(ystack, w_branch, gp, gate_b, dm)


def _adamw(w, g, m, v, *, name):
    shape = w.shape
    c = shape[-1]
    r = 1
    for n in shape[:-1]:
        r *= n
    w2, g2, m2, v2 = (a.reshape(r, c) for a in (w, g, m, v))
    tr = _tile(r, max(8, (1 << 19) // c // 8 * 8), 8)
    c1 = 1.0 / (1.0 - ADAM_B1 ** ADAM_STEP)
    c2 = 1.0 / (1.0 - ADAM_B2 ** ADAM_STEP)

    def body(w_ref, g_ref, m_ref, v_ref, d_ref, nm_ref, nv_ref):
        gv = g_ref[...]
        nm = ADAM_B1 * m_ref[...] + (1.0 - ADAM_B1) * gv
        nv = ADAM_B2 * v_ref[...] + (1.0 - ADAM_B2) * (gv * gv)
        nm_ref[...] = nm
        nv_ref[...] = nv
        d_ref[...] = -ADAM_LR * ((nm * c1) / (jnp.sqrt(nv * c2) + ADAM_EPS) + ADAM_WD * w_ref[...])

    blk = pl.BlockSpec((tr, c), lambda i: (i, 0))
    sd = jax.ShapeDtypeStruct((r, c), F32)
    d2, nm2, nv2 = pl.pallas_call(body, out_shape=(sd, sd, sd), grid=(r // tr,), in_specs=[blk] * 4, out_specs=(blk,) * 3,
                                  compiler_params=_cparams(("parallel",)), name=name)(w2, g2, m2, v2)
    return d2.reshape(shape), nm2.reshape(shape), nv2.reshape(shape)


def _place():
    x, y, c = lax.axis_index("x"), lax.axis_index("y"), lax.axis_index("c")
    chips = [(1 - x, y), (x, 1 - y), (1 - x, 1 - y)]
    return x, y, c, 2 * x + y, chips, [2 * cx + cy for cx, cy in chips]


ANY = pl.BlockSpec(memory_space=pl.ANY)


def _all_gather(shards):
    n = len(shards)

    def body(*refs):
        ins, outs = refs[:n], refs[n:2 * n]
        send, recv = refs[2 * n:]
        x, y, c, k, chips, ks = _place()
        sib = (x, y, 1 - c)
        sends = []
        for a in range(n):
            for j in range(3):
                cp = pltpu.make_async_remote_copy(src_ref=ins[a].at[c], dst_ref=outs[a].at[c, k], send_sem=send.at[6 * a + j],
                                                  recv_sem=recv.at[6 * a + j], device_id=(*chips[j], c), device_id_type=MESH)
                cp.start()
                sends.append(cp)
        for a in range(n):
            for j in range(3):
                slab = outs[a].at[c, ks[j]]
                pltpu.make_async_remote_copy(src_ref=slab, dst_ref=slab, send_sem=send.at[6 * a + j], recv_sem=recv.at[6 * a + j],
                                             device_id=(*chips[j], c), device_id_type=MESH).wait_recv()
                cp = pltpu.make_async_remote_copy(src_ref=slab, dst_ref=slab, send_sem=send.at[6 * a + 3 + j],
                                                  recv_sem=recv.at[6 * a + 3 + j], device_id=sib, device_id_type=MESH)
                cp.start()
                sends.append(cp)
        for a in range(n):
            for j in range(3):
                slab = outs[a].at[1 - c, ks[j]]
                pltpu.make_async_remote_copy(src_ref=slab, dst_ref=slab, send_sem=send.at[6 * a + 3 + j],
                                             recv_sem=recv.at[6 * a + 3 + j], device_id=sib, device_id_type=MESH).wait_recv()
        for cp in sends:
            cp.wait_send()

    out_shape = tuple(jax.ShapeDtypeStruct((2, 4) + s.shape[1:], s.dtype) for s in shards)
    return pl.pallas_call(
        body, out_shape=out_shape, in_specs=[ANY] * n, out_specs=(ANY,) * n,
        scratch_shapes=[pltpu.SemaphoreType.DMA((6 * n,)), pltpu.SemaphoreType.DMA((6 * n,))],
        name="weights_all_gather")(*shards)


def _rs_exchange_cores(grads):
    n = len(grads)

    def body(*refs):
        ins, outs = refs[:n], refs[n:2 * n]
        send, recv = refs[2 * n:]
        x, y, c, _, _, _ = _place()
        sib = (x, y, 1 - c)
        cps = []
        for a in range(n):
            cp = pltpu.make_async_remote_copy(src_ref=ins[a].at[1 - c], dst_ref=outs[a], send_sem=send.at[a], recv_sem=recv.at[a],
                                              device_id=sib, device_id_type=MESH)
            cp.start()
            cps.append(cp)
        for cp in cps:
            cp.wait()

    out_shape = tuple(jax.ShapeDtypeStruct(g.shape[1:], g.dtype) for g in grads)
    return pl.pallas_call(body, out_shape=out_shape, in_specs=[ANY] * n, out_specs=(ANY,) * n,
                          scratch_shapes=[pltpu.SemaphoreType.DMA((n,)), pltpu.SemaphoreType.DMA((n,))],
                          name="grads_exchange_cores")(*grads)


def _rs_exchange_chips(parts):
    n = len(parts)

    def body(*refs):
        ins, outs = refs[:n], refs[n:2 * n]
        send, recv = refs[2 * n:]
        x, y, c, k, chips, ks = _place()
        sends = []
        for a in range(n):
            for j in range(3):
                cp = pltpu.make_async_remote_copy(src_ref=ins[a].at[ks[j]], dst_ref=outs[a].at[j], send_sem=send.at[3 * a + j],
                                                  recv_sem=recv.at[3 * a + j], device_id=(*chips[j], c), device_id_type=MESH)
                cp.start()
                sends.append(cp)
        for cp in sends:
            cp.wait()

    out_shape = tuple(jax.ShapeDtypeStruct((3,) + p.shape[1:], p.dtype) for p in parts)
    return pl.pallas_call(
        body, out_shape=out_shape, in_specs=[ANY] * n, out_specs=(ANY,) * n,
        scratch_shapes=[pltpu.SemaphoreType.DMA((3 * n,)), pltpu.SemaphoreType.DMA((3 * n,))],
        name="grads_exchange_chips")(*parts)


def _rs_share_cores(bufs):
    n = len(bufs)

    def body(*refs):
        outs = refs[n:2 * n]
        send, recv = refs[2 * n:]
        x, y, c, _, _, _ = _place()
        sib = (x, y, 1 - c)
        cps = []
        for a in range(n):
            cp = pltpu.make_async_remote_copy(src_ref=outs[a].at[c], dst_ref=outs[a].at[c], send_sem=send.at[a], recv_sem=recv.at[a],
                                              device_id=sib, device_id_type=MESH)
            cp.start()
            cps.append(cp)
        for a in range(n):
            slab = outs[a].at[1 - c]
            pltpu.make_async_remote_copy(src_ref=slab, dst_ref=slab, send_sem=send.at[a], recv_sem=recv.at[a],
                                         device_id=sib, device_id_type=MESH).wait_recv()
        for cp in cps:
            cp.wait_send()

    out_shape = tuple(jax.ShapeDtypeStruct(b.shape, b.dtype) for b in bufs)
    return pl.pallas_call(
        body, out_shape=out_shape, in_specs=[ANY] * n, out_specs=(ANY,) * n,
        input_output_aliases={a: a for a in range(n)},
        scratch_shapes=[pltpu.SemaphoreType.DMA((n,)), pltpu.SemaphoreType.DMA((n,))],
        name="grads_share_cores")(*bufs)


def _add_core_halves(g, ra, c_idx, *, name):
    _, _, r, c = g.shape
    tr = _tile(r, max(8, (1 << 19) // c // 8 * 8), 8)

    def body(c_ref, g_ref, ra_ref, o_ref):
        o_ref[...] = g_ref[...] + ra_ref[...]

    gs = pltpu.PrefetchScalarGridSpec(
        num_scalar_prefetch=1, grid=(4, r // tr),
        in_specs=[pl.BlockSpec((None, None, tr, c), lambda j, i, cr: (cr[0], j, i, 0)),
                  pl.BlockSpec((None, tr, c), lambda j, i, cr: (j, i, 0))],
        out_specs=pl.BlockSpec((None, tr, c), lambda j, i, cr: (j, i, 0)))
    return pl.pallas_call(body, out_shape=jax.ShapeDtypeStruct((4, r, c), F32), grid_spec=gs,
                          compiler_params=_cparams(("parallel", "parallel")), name=name)(c_idx, g, ra)


def _add_chips(p, r3, k_idx, c_idx, *, name):
    _, r, c = p.shape
    tr = _tile(r, max(8, (1 << 18) // c // 8 * 8), 8)

    def body(k_ref, c_ref, p_ref, r_ref, o_ref):
        o_ref[...] = ((p_ref[...] + r_ref[0]) + r_ref[1]) + r_ref[2]

    gs = pltpu.PrefetchScalarGridSpec(
        num_scalar_prefetch=2, grid=(r // tr,),
        in_specs=[pl.BlockSpec((None, tr, c), lambda i, kr, cr: (kr[0], i, 0)), pl.BlockSpec((3, tr, c), lambda i, kr, cr: (0, i, 0))],
        out_specs=pl.BlockSpec((None, tr, c), lambda i, kr, cr: (cr[0], i, 0)))
    return pl.pallas_call(body, out_shape=jax.ShapeDtypeStruct((2, r, c), F32), grid_spec=gs,
                          compiler_params=_cparams(("parallel",)), name=name)(k_idx, c_idx, p, r3)


def _all_reduce_small(vec):
    r = vec.shape[0]

    def body(v_ref, gath_ref, sum_ref, send, recv):
        x, y, c = lax.axis_index("x"), lax.axis_index("y"), lax.axis_index("c")
        me = 4 * x + 2 * y + c
        gath_ref[me] = v_ref[...]
        cps = []
        for f in range(1, 8):
            fx, fy, fc = (f >> 2) & 1, (f >> 1) & 1, f & 1
            peer = (x ^ fx, y ^ fy, c ^ fc)
            cp = pltpu.make_async_remote_copy(src_ref=v_ref, dst_ref=gath_ref.at[me], send_sem=send.at[f - 1], recv_sem=recv.at[f - 1],
                                              device_id=peer, device_id_type=MESH)
            cp.start()
            cps.append(cp)
        for f in range(1, 8):
            fx, fy, fc = (f >> 2) & 1, (f >> 1) & 1, f & 1
            src = 4 * (x ^ fx) + 2 * (y ^ fy) + (c ^ fc)
            pltpu.make_async_remote_copy(src_ref=v_ref, dst_ref=gath_ref.at[src], send_sem=send.at[f - 1], recv_sem=recv.at[f - 1],
                                         device_id=(x ^ fx, y ^ fy, c ^ fc), device_id_type=MESH).wait_recv()
        for cp in cps:
            cp.wait_send()
        acc = gath_ref[0]
        for i in range(1, 8):
            acc = acc + gath_ref[i]
        sum_ref[...] = acc

    vm = pl.BlockSpec(memory_space=pltpu.VMEM)
    _, total = pl.pallas_call(
        body, out_shape=(jax.ShapeDtypeStruct((8, r, 128), F32), jax.ShapeDtypeStruct((r, 128), F32)),
        in_specs=[vm], out_specs=(vm, vm),
        scratch_shapes=[pltpu.SemaphoreType.DMA((7,)), pltpu.SemaphoreType.DMA((7,))],
        name="small_all_reduce")(vec)
    return total


def _full_weight(gw, name, layer):
    gathered, own, chip = gw[name]
    return jnp.concatenate([jnp.where(chip == k, own[layer], gathered[layer, k]) for k in range(4)], axis=SHARD_AXIS[name])


def _to_shards(full, name):
    return jnp.stack(jnp.split(full, 4, axis=SHARD_AXIS[name]), axis=0)


def _rope_tables(positions):
    inv = ROPE_THETA ** (-jnp.arange(0, QK_ROPE, 2, dtype=F32) / QK_ROPE)
    ang = positions.astype(F32)[:, None] * inv
    cos, sin = jnp.cos(ang), jnp.sin(ang)
    s = positions.shape[0]
    pad = jnp.zeros((s, HEAD_PAD - QK_HEAD), F32)
    ctab = jnp.concatenate([jnp.ones((s, QK_NOPE), F32), cos, cos, pad], axis=1)
    stab = jnp.concatenate([jnp.zeros((s, QK_NOPE), F32), -sin, sin, pad], axis=1)
    return ctab, stab


def _pad_gain(g):
    return jnp.concatenate([g, jnp.zeros((HEAD_PAD - QK_HEAD,), F32)])[None, :]


def _layer_weights(gw, rep, l, ql, kvl):
    d = rep["norm_g"].shape[1]
    w_in = _full_weight(gw, "w_in", l)
    o_kr = 2 * BRANCH_W + ql + kvl
    o_g = o_kr + QK_ROPE + 7 * BRANCH_W
    w = {}
    w["w1"] = jnp.concatenate([w_in[:, :o_kr], w_in[:, o_kr + QK_ROPE:o_g]], axis=1)
    w["wg"] = w_in[:, o_g:]
    w["wkr"] = jnp.concatenate([w_in[:, o_kr:o_kr + QK_ROPE], jnp.zeros((d, 128 - QK_ROPE), BF16)], axis=1)
    wuq = _full_weight(gw, "w_uq", l).reshape(ql, MLA_HEADS, QK_HEAD)
    w["w_uq"] = jnp.pad(wuq, ((0, 0), (0, 0), (0, HEAD_PAD - QK_HEAD))).reshape(ql, MLA_HEADS * HEAD_PAD)
    for nme in ("w_ukv", "pool_w", "conv_w", "w_mem_kv", "w_branch", "w_out"):
        w[nme] = _full_weight(gw, nme, l)
    for nme in ("norm_g", "gate_b", "pool_scale", "q_a_norm_g", "kv_a_norm_g", "mem_norm_g", "xattn_q_norm_g", "xattn_k_norm_g"):
        w[nme] = rep[nme][l][None, :]
    w["mla_q_norm_g"] = _pad_gain(rep["mla_q_norm_g"][l])
    w["mla_k_norm_g"] = _pad_gain(rep["mla_k_norm_g"][l])
    return w


def _forward_layer(x, mem, ctab, stab, w, tq, l):
    sfx = f"_l{l}"
    h, ht = _norm_fwd(x, w["norm_g"], name="norm_fwd" + sfx)
    p1 = _mm(h, w["w1"], name="proj_main" + sfx)
    gp = _mm(h, w["wg"], name="proj_gates" + sfx)
    kr = _mm(h, w["wkr"], name="proj_krope" + sfx)
    y_pool, yt_pool = _pool_fwd(p1, w["pool_w"], w["pool_scale"])
    qf, kf, vv = _mla_prep_fwd(p1, kr, ctab, stab, w["q_a_norm_g"], w["kv_a_norm_g"], w["w_uq"], w["w_ukv"],
                               w["mla_q_norm_g"], w["mla_k_norm_g"])
    y_mla, yt_mla, o_att, lse, lset = _attn_fwd(qf, kf, vv, p1, tq=tq)
    y_conv, yt_conv = _conv_fwd(p1, w["conv_w"])
    memn, memnt = _norm_fwd(mem, w["mem_norm_g"], name="mem_norm" + sfx)
    mem_kv = _mm(memn, w["w_mem_kv"], name="mem_kv" + sfx)
    kx, vx = _memkv_prep(mem_kv, w["xattn_k_norm_g"])
    y_mem, yt_mem = _xattn_fwd(p1, kx, vx, w["xattn_q_norm_g"])
    ystack = jnp.stack([y_pool, y_mla, y_conv, y_mem])
    ytstack = jnp.stack([yt_pool, yt_mla, yt_conv, yt_mem])
    merged, mergedt = _gate_fwd(ystack, w["w_branch"], gp, w["gate_b"])
    x_out = _mm(merged, w["w_out"], add=x, name="out_proj" + sfx)
    saved = dict(x=x, ht=ht, p1=p1, gp=gp, kr=kr, qf=qf, kf=kf, vv=vv, o_att=o_att, lse=lse, lset=lset, memnt=memnt,
                 mem_kv=mem_kv, kx=kx, vx=vx, ystack=ystack, ytstack=ytstack, mergedt=mergedt)
    return x_out, saved


def _backward_layer(dx_out, sv, mem, ctab, stab, w, tq, l, ql, kvl):
    sfx = f"_l{l}"
    g = {}
    g["w_out"] = _mm(sv["mergedt"], dx_out, name="g_w_out" + sfx)
    dm = _mm(dx_out, w["w_out"], trans_b=True, name="d_merged" + sfx)
    dproj, dgp, g_gate_b = _gate_bwd(sv["ystack"], w["w_branch"], sv["gp"], w["gate_b"], dm)
    g["gate_b"] = g_gate_b[0]
    g["w_branch"] = _mm(sv["ytstack"], dproj, name="g_w_branch" + sfx)
    dy4 = _mm(dproj, w["w_branch"], trans_b=True, name="d_branches" + sfx)
    p1, kr = sv["p1"], sv["kr"]
    d_pool, g_pw, g_ps = _pool_bwd(p1, dy4[0], w["pool_w"], w["pool_scale"])
    g["pool_w"], g["pool_scale"] = g_pw, g_ps[0]
    do, d_mz, delta, deltat = _attn_bwd_pre(dy4, sv["o_att"], p1)
    dqf = _attn_bwd_dq(sv["qf"], sv["kf"], sv["vv"], do, sv["lse"], delta, tq=tq)
    dkf, dvv = _attn_bwd_dkv(sv["qf"], sv["kf"], sv["vv"], do, sv["lset"], deltat, tq=tq)
    (d_c, d_kr, dq_raw, dkv_raw, cqnt, ckvnt, g_qa, g_kva, g_qg, g_kg) = _mla_prep_bwd(
        p1, kr, ctab, stab, w["q_a_norm_g"], w["kv_a_norm_g"], w["w_uq"], w["w_ukv"], w["mla_q_norm_g"], w["mla_k_norm_g"],
        dqf, dkf, dvv)
    g["q_a_norm_g"], g["kv_a_norm_g"] = g_qa[0], g_kva[0]
    g["mla_q_norm_g"], g["mla_k_norm_g"] = g_qg[0, :QK_HEAD], g_kg[0, :QK_HEAD]
    g_wuq = _mm(cqnt, dq_raw, name="g_w_uq" + sfx)
    g["w_uq"] = g_wuq.reshape(ql, MLA_HEADS, HEAD_PAD)[:, :, :QK_HEAD].reshape(ql, MLA_HEADS * QK_HEAD)
    g["w_ukv"] = _mm(ckvnt, dkv_raw, name="g_w_ukv" + sfx)
    d_conv, gc0, gc1, gc2 = _conv_bwd(p1, dy4[2], w["conv_w"])
    g["conv_w"] = jnp.concatenate([gc0, gc1, gc2], axis=0)
    d_x, dkx, dvx, g_xq = _xattn_bwd(p1, dy4, sv["kx"], sv["vx"], w["xattn_q_norm_g"])
    g["xattn_q_norm_g"] = g_xq[0]
    d_memkv, g_xk = _memkv_prep_bwd(sv["mem_kv"], w["xattn_k_norm_g"], dkx, dvx)
    g["xattn_k_norm_g"] = g_xk[0]
    g["w_mem_kv"] = _mm(sv["memnt"], d_memkv, name="g_w_mem_kv" + sfx)
    d_memn = _mm(d_memkv, w["w_mem_kv"], trans_b=True, name="d_memn" + sfx)
    _, g_mn = _norm_bwd(mem, w["mem_norm_g"], d_memn, d_memn, name="mem_norm_bwd" + sfx)
    g["mem_norm_g"] = g_mn[0]
    dp1 = jnp.concatenate([d_pool, d_c, d_mz, d_conv, d_x], axis=1)
    ht = sv["ht"]
    g_w1 = _mm(ht, dp1, name="g_w1" + sfx)
    g_wg = _mm(ht, dgp, name="g_wg" + sfx)
    g_wkr = _mm(ht, d_kr, name="g_wkr" + sfx)
    o_kr = 2 * BRANCH_W + ql + kvl
    g["w_in"] = jnp.concatenate([g_w1[:, :o_kr], g_wkr[:, :QK_ROPE], g_w1[:, o_kr:], g_wg], axis=1)
    dh = _mm(dp1, w["w1"], trans_b=True, name="dh_main" + sfx)
    dh = _mm(dgp, w["wg"], trans_b=True, add=dh, name="dh_gates" + sfx)
    dh = _mm(d_kr, w["wkr"], trans_b=True, add=dh, name="dh_krope" + sfx)
    dx, g_ng = _norm_bwd(sv["x"], w["norm_g"], dh, dx_out, name="norm_bwd" + sfx)
    g["norm_g"] = g_ng[0]
    return dx, g


def _as4(a):
    rest = a.shape[2:]
    r = 1
    for n in rest[:-1]:
        r *= n
    return a.reshape(2, 4, r, rest[-1])


def kernel(x, mem, positions, norm_g, w_in, gate_b, pool_w, pool_scale, q_a_norm_g, kv_a_norm_g, w_uq, w_ukv, mla_q_norm_g, mla_k_norm_g, conv_w, mem_norm_g, w_mem_kv, xattn_q_norm_g, xattn_k_norm_g, w_branch, w_out, loss_target, m_norm_g, m_w_in, m_gate_b, m_pool_w, m_pool_scale, m_q_a_norm_g, m_kv_a_norm_g, m_w_uq, m_w_ukv, m_mla_q_norm_g, m_mla_k_norm_g, m_conv_w, m_mem_norm_g, m_w_mem_kv, m_xattn_q_norm_g, m_xattn_k_norm_g, m_w_branch, m_w_out, v_norm_g, v_w_in, v_gate_b, v_pool_w, v_pool_scale, v_q_a_norm_g, v_kv_a_norm_g, v_w_uq, v_w_ukv, v_mla_q_norm_g, v_mla_k_norm_g, v_conv_w, v_mem_norm_g, v_w_mem_kv, v_xattn_q_norm_g, v_xattn_k_norm_g, v_w_branch, v_w_out):
    wts = dict(norm_g=norm_g, w_in=w_in, gate_b=gate_b, pool_w=pool_w, pool_scale=pool_scale, q_a_norm_g=q_a_norm_g,
               kv_a_norm_g=kv_a_norm_g, w_uq=w_uq, w_ukv=w_ukv, mla_q_norm_g=mla_q_norm_g, mla_k_norm_g=mla_k_norm_g,
               conv_w=conv_w, mem_norm_g=mem_norm_g, w_mem_kv=w_mem_kv, xattn_q_norm_g=xattn_q_norm_g,
               xattn_k_norm_g=xattn_k_norm_g, w_branch=w_branch, w_out=w_out)
    mom = dict(norm_g=m_norm_g, w_in=m_w_in, gate_b=m_gate_b, pool_w=m_pool_w, pool_scale=m_pool_scale, q_a_norm_g=m_q_a_norm_g,
               kv_a_norm_g=m_kv_a_norm_g, w_uq=m_w_uq, w_ukv=m_w_ukv, mla_q_norm_g=m_mla_q_norm_g, mla_k_norm_g=m_mla_k_norm_g,
               conv_w=m_conv_w, mem_norm_g=m_mem_norm_g, w_mem_kv=m_w_mem_kv, xattn_q_norm_g=m_xattn_q_norm_g,
               xattn_k_norm_g=m_xattn_k_norm_g, w_branch=m_w_branch, w_out=m_w_out)
    vel = dict(norm_g=v_norm_g, w_in=v_w_in, gate_b=v_gate_b, pool_w=v_pool_w, pool_scale=v_pool_scale, q_a_norm_g=v_q_a_norm_g,
               kv_a_norm_g=v_kv_a_norm_g, w_uq=v_w_uq, w_ukv=v_w_ukv, mla_q_norm_g=v_mla_q_norm_g, mla_k_norm_g=v_mla_k_norm_g,
               conv_w=v_conv_w, mem_norm_g=v_mem_norm_g, w_mem_kv=v_w_mem_kv, xattn_q_norm_g=v_xattn_q_norm_g,
               xattn_k_norm_g=v_xattn_k_norm_g, w_branch=v_w_branch, w_out=v_w_out)
    depth = norm_g.shape[0]
    assert depth == 2 and x.shape[0] == 1
    xs, mems, tgt = x[0], mem[0], loss_target[0]
    s = xs.shape[0]
    ql, kvl = q_a_norm_g.shape[1], kv_a_norm_g.shape[1]
    tq = _tile(s, 512, 128)
    ctab, stab = _rope_tables(positions[0])

    send = [wts[n].astype(F32 if n == "conv_w" else BF16) for n in SHARDED]
    chip = 2 * lax.axis_index("x") + lax.axis_index("y")
    gathered = {n: (g, own, chip) for n, g, own in zip(SHARDED, _all_gather(send), send)}
    rep = {n: wts[n] for n in REPLICATED}
    lw = [_layer_weights(gathered, rep, l, ql, kvl) for l in range(depth)]

    act, saved = xs, []
    for l in range(depth):
        act, sv = _forward_layer(act, mems, ctab, stab, lw[l], tq, l)
        saved.append(sv)
    dy, loss_part = _loss_head(act, tgt)

    grads = [None] * depth
    dxl = dy
    for l in reversed(range(depth)):
        dxl, grads[l] = _backward_layer(dxl, saved[l], mems, ctab, stab, lw[l], tq, l, ql, kvl)
    grad_x = dxl[None]

    c_idx = lax.axis_index("c").astype(jnp.int32).reshape(1)
    g_full = [_as4(jnp.stack([_to_shards(grads[l][n], n) for l in range(depth)], axis=0)) for n in SHARDED]
    from_sib = _rs_exchange_cores(g_full)
    parts = [_add_core_halves(g, r, c_idx, name=f"add_cores_{n}") for g, r, n in zip(g_full, from_sib, SHARDED)]
    from_chips = _rs_exchange_chips(parts)
    k_idx = chip.astype(jnp.int32).reshape(1)
    halves = [_add_chips(p, r3, k_idx, c_idx, name=f"add_chips_{n}") for p, r3, n in zip(parts, from_chips, SHARDED)]
    reduced = _rs_share_cores(halves)
    gsum = {n: r.reshape(wts[n].shape) for n, r in zip(SHARDED, reduced)}

    flat = [jnp.stack([grads[l][n] for l in range(depth)], axis=0).reshape(-1) for n in REPLICATED]
    sizes = [f.shape[0] for f in flat]
    total = sum(sizes) + 1
    rows = -(-total // 1024) * 8
    vec = jnp.concatenate(flat + [loss_part[0, :1], jnp.zeros((rows * 128 - total,), F32)]).reshape(rows, 128)
    red = _all_reduce_small(vec).reshape(-1)
    off = 0
    for n, sz in zip(REPLICATED, sizes):
        gsum[n] = red[off:off + sz].reshape(wts[n].shape)
        off += sz
    loss = red[off]

    delta, new_m, new_v = {}, {}, {}
    for n in WEIGHTS:
        delta[n], new_m[n], new_v[n] = _adamw(wts[n], gsum[n], mom[n], vel[n], name=f"adamw_{n}")
    return (loss, grad_x, *[gsum[n] for n in WEIGHTS], *[delta[n] for n in WEIGHTS],
            *[new_m[n] for n in WEIGHTS], *[new_v[n] for n in WEIGHTS])
```

```python
import functools

import jax
import jax.numpy as jnp
from jax import lax
from jax.experimental import pallas as pl
from jax.experimental.pallas import tpu as pltpu

F32 = jnp.float32
BF16 = jnp.bfloat16
MESH = pl.DeviceIdType.MESH

EPS = 1e-6
N_BRANCH = 4
BRANCH_W = 1024
POOL_GROUPS = 4
POOL_GW = BRANCH_W // POOL_GROUPS
POOL_HALO = 16
CONV_HALO = 8
MLA_HEADS = 8
QK_NOPE = 128
QK_ROPE = 64
QK_HEAD = QK_NOPE + QK_ROPE
HEAD_PAD = 256
V_HEAD = 128
ROPE_THETA = 10000.0
XATTN_HEADS = 4
XATTN_HEAD_DIM = BRANCH_W // XATTN_HEADS
ADAM_LR, ADAM_B1, ADAM_B2, ADAM_EPS, ADAM_WD, ADAM_STEP = 0.001, 0.9, 0.999, 1e-08, 0.01, 10
NEG = -1e30
VMEM_LIMIT = 48 * 1024 * 1024

SHARDED = ("w_in", "pool_w", "w_uq", "w_ukv", "conv_w", "w_mem_kv", "w_branch", "w_out")
REPLICATED = ("norm_g", "gate_b", "pool_scale", "q_a_norm_g", "kv_a_norm_g", "mla_q_norm_g", "mla_k_norm_g",
              "mem_norm_g", "xattn_q_norm_g", "xattn_k_norm_g")
WEIGHTS = ("norm_g", "w_in", "gate_b", "pool_w", "pool_scale", "q_a_norm_g", "kv_a_norm_g", "w_uq", "w_ukv",
           "mla_q_norm_g", "mla_k_norm_g", "conv_w", "mem_norm_g", "w_mem_kv", "xattn_q_norm_g", "xattn_k_norm_g",
           "w_branch", "w_out")
SHARD_AXIS = {"w_in": 1, "pool_w": 1, "w_uq": 1, "w_ukv": 1, "conv_w": 1, "w_mem_kv": 0, "w_branch": 2, "w_out": 0}


def _cparams(sem=None):
    return pltpu.CompilerParams(dimension_semantics=sem, vmem_limit_bytes=VMEM_LIMIT)


def _tile(n, pref, unit=128):
    if n <= pref:
        return n
    t = (pref // unit) * unit
    while t >= unit:
        if n % t == 0:
            return t
        t -= unit
    return n


def _silu(z):
    return z * jax.nn.sigmoid(z)


def _dsilu(z):
    s = jax.nn.sigmoid(z)
    return s * (1.0 + z * (1.0 - s))


def _dot(a, b):
    return jnp.dot(a, b, preferred_element_type=F32)


def _dot_nt(a, b):
    return lax.dot_general(a, b, (((1,), (1,)), ((), ())), preferred_element_type=F32)


def _dot_tn(a, b):
    return lax.dot_general(a, b, (((0,), (0,)), ((), ())), preferred_element_type=F32)


def _rms(x, g, n):
    r = lax.rsqrt(jnp.sum(x * x, axis=-1, keepdims=True) * (1.0 / n) + EPS)
    return x * r * g


def _rms_bwd(x, g, n, dout):
    r = lax.rsqrt(jnp.sum(x * x, axis=-1, keepdims=True) * (1.0 / n) + EPS)
    y = x * r
    dy = dout * g
    dx = r * (dy - y * (jnp.sum(dy * y, axis=-1, keepdims=True) * (1.0 / n)))
    return dx, dout * y


def _rope(x, ctab, stab):
    lane = lax.broadcasted_iota(jnp.int32, x.shape, 1)
    partner = jnp.where(lane < QK_NOPE + QK_ROPE // 2, pltpu.roll(x, HEAD_PAD - QK_ROPE // 2, 1),
                        pltpu.roll(x, QK_ROPE // 2, 1))
    return x * ctab + partner * stab


def _rope_bwd(d, ctab, stab):
    lane = lax.broadcasted_iota(jnp.int32, d.shape, 1)
    ds = d * stab
    partner = jnp.where(lane < QK_NOPE + QK_ROPE // 2, pltpu.roll(ds, HEAD_PAD - QK_ROPE // 2, 1),
                        pltpu.roll(ds, QK_ROPE // 2, 1))
    return d * ctab + jnp.where((lane >= QK_NOPE) & (lane < QK_HEAD), partner, 0.0)


def _mm(a, b, *, name, trans_b=False, add=None, out_dtype=F32, tm=512, tn=1024, tk=2048):
    batched = a.ndim == 3
    if batched:
        nb, m, k = a.shape
    else:
        m, k = a.shape
    n = b.shape[-2] if trans_b else b.shape[-1]
    tm, tn, tk = _tile(m, tm, 8), _tile(n, tn), _tile(k, tk)
    nk = k // tk

    def body(*refs):
        if add is None:
            a_ref, b_ref, o_ref = refs[:3]
            add_ref = None
            rest = refs[3:]
        else:
            a_ref, b_ref, add_ref, o_ref = refs[:4]
            rest = refs[4:]
        av = a_ref[...].astype(BF16)
        bv = b_ref[...].astype(BF16)
        part = _dot_nt(av, bv) if trans_b else _dot(av, bv)

        def finish(acc):
            if add_ref is not None:
                acc = acc + add_ref[...]
            o_ref[...] = acc.astype(o_ref.dtype)

        if nk == 1:
            finish(part)
        else:
            acc_ref = rest[0]
            kk = pl.program_id(3 if batched else 2)

            @pl.when(kk == 0)
            def _():
                acc_ref[...] = part

            @pl.when(kk > 0)
            def _():
                acc_ref[...] += part

            @pl.when(kk == nk - 1)
            def _():
                finish(acc_ref[...])

    if batched:
        a_spec = pl.BlockSpec((None, tm, tk), lambda bb, i, j, kk: (bb, i, kk))
        b_spec = (pl.BlockSpec((None, tn, tk), lambda bb, i, j, kk: (bb, j, kk)) if trans_b
                  else pl.BlockSpec((None, tk, tn), lambda bb, i, j, kk: (bb, kk, j)))
        o_spec = pl.BlockSpec((None, tm, tn), lambda bb, i, j, kk: (bb, i, j))
        grid = (nb, m // tm, n // tn, nk)
        out_shape = jax.ShapeDtypeStruct((nb, m, n), out_dtype)
        sem = ("parallel", "parallel", "parallel", "arbitrary")
    else:
        a_spec = pl.BlockSpec((tm, tk), lambda i, j, kk: (i, kk))
        b_spec = (pl.BlockSpec((tn, tk), lambda i, j, kk: (j, kk)) if trans_b
                  else pl.BlockSpec((tk, tn), lambda i, j, kk: (kk, j)))
        o_spec = pl.BlockSpec((tm, tn), lambda i, j, kk: (i, j))
        grid = (m // tm, n // tn, nk)
        out_shape = jax.ShapeDtypeStruct((m, n), out_dtype)
        sem = ("parallel", "parallel", "arbitrary")
    in_specs = [a_spec, b_spec] + ([o_spec] if add is not None else [])
    args = (a, b) + ((add,) if add is not None else ())
    scratch = [pltpu.VMEM((tm, tn), F32)] if nk > 1 else []
    return pl.pallas_call(body, out_shape=out_shape, grid=grid, in_specs=in_specs, out_specs=o_spec,
                          scratch_shapes=scratch, compiler_params=_cparams(sem), name=name)(*args)


def _norm_fwd(x, g, *, name, t=256):
    s, d = x.shape
    t = _tile(s, t, 128)

    def body(x_ref, g_ref, h_ref, ht_ref):
        h = _rms(x_ref[...], g_ref[...], d)
        h_ref[...] = h.astype(BF16)
        ht_ref[...] = h.T.astype(BF16)

    return pl.pallas_call(
        body, out_shape=(jax.ShapeDtypeStruct((s, d), BF16), jax.ShapeDtypeStruct((d, s), BF16)),
        grid=(s // t,),
        in_specs=[pl.BlockSpec((t, d), lambda i: (i, 0)), pl.BlockSpec((1, d), lambda i: (0, 0))],
        out_specs=(pl.BlockSpec((t, d), lambda i: (i, 0)), pl.BlockSpec((d, t), lambda i: (0, i))),
        compiler_params=_cparams(("parallel",)), name=name)(x, g)


def _norm_bwd(x, g, dh, dres, *, name, t=256):
    s, d = x.shape
    t = _tile(s, t, 8)

    def body(x_ref, g_ref, dh_ref, dres_ref, dx_ref, dg_ref):
        dx, dgt = _rms_bwd(x_ref[...], g_ref[...], d, dh_ref[...])
        dx_ref[...] = dx + dres_ref[...]
        part = jnp.sum(dgt, axis=0, keepdims=True)

        @pl.when(pl.program_id(0) == 0)
        def _():
            dg_ref[...] = part

        @pl.when(pl.program_id(0) > 0)
        def _():
            dg_ref[...] += part

    row = pl.BlockSpec((t, d), lambda i: (i, 0))
    vec = pl.BlockSpec((1, d), lambda i: (0, 0))
    return pl.pallas_call(
        body, out_shape=(jax.ShapeDtypeStruct((s, d), F32), jax.ShapeDtypeStruct((1, d), F32)),
        grid=(s // t,), in_specs=[row, vec, row, row], out_specs=(row, vec),
        compiler_params=_cparams(("arbitrary",)), name=name)(x, g, dh, dres)


def _loss_head(y, tgt, *, t=256):
    s, d = y.shape
    t = _tile(s, t, 8)

    def body(y_ref, t_ref, dy_ref, l_ref):
        e = y_ref[...] - t_ref[...]
        dy_ref[...] = e * (1.0 / d)
        part = jnp.zeros((1, 128), F32) + jnp.sum(e * e) * (0.5 / d)

        @pl.when(pl.program_id(0) == 0)
        def _():
            l_ref[...] = part

        @pl.when(pl.program_id(0) > 0)
        def _():
            l_ref[...] += part

    row = pl.BlockSpec((t, d), lambda i: (i, 0))
    return pl.pallas_call(
        body, out_shape=(jax.ShapeDtypeStruct((s, d), F32), jax.ShapeDtypeStruct((1, 128), F32)),
        grid=(s // t,), in_specs=[row, row], out_specs=(row, pl.BlockSpec((1, 128), lambda i: (0, 0))),
        compiler_params=_cparams(("arbitrary",)), name="loss_head")(y, tgt)


def _pool_mixed(scr, v, halo, first, row0, t):
    scr[0:POOL_HALO, :] = jnp.where(first, 0.0, halo)
    scr[POOL_HALO:POOL_HALO + t, :] = v
    row = row0 + lax.broadcasted_iota(jnp.int32, (t, 1), 0)
    mixed = []
    for g in range(POOL_GROUPS):
        w = 2 ** (g + 1)
        acc = scr[:, g * POOL_GW:(g + 1) * POOL_GW]
        sh = 1
        while sh < w:
            acc = acc + pltpu.roll(acc, sh, 0)
            sh *= 2
        cnt = jnp.minimum(row + 1, w).astype(F32)
        mixed.append(acc[POOL_HALO:POOL_HALO + t, :] / cnt - v[:, g * POOL_GW:(g + 1) * POOL_GW])
    return mixed


def _pool_fwd(p1, pool_w, pool_scale, *, t=256):
    s = p1.shape[0]
    t = _tile(s, t, 128)
    hb = t // POOL_HALO

    def body(pv_ref, halo_ref, pz_ref, pw_ref, sc_ref, y_ref, yt_ref, scr):
        i = pl.program_id(0)
        mixed = _pool_mixed(scr, pv_ref[...], halo_ref[...], i == 0, i * t, t)
        outs = [_dot(mixed[g].astype(BF16), pw_ref[g]) for g in range(POOL_GROUPS)]
        y = jnp.concatenate(outs, axis=1) * sc_ref[...] * _silu(pz_ref[...])
        y_ref[...] = y.astype(BF16)
        yt_ref[...] = y.T.astype(BF16)

    return pl.pallas_call(
        body, out_shape=(jax.ShapeDtypeStruct((s, BRANCH_W), BF16), jax.ShapeDtypeStruct((BRANCH_W, s), BF16)),
        grid=(s // t,),
        in_specs=[pl.BlockSpec((t, BRANCH_W), lambda i: (i, 0)),
                  pl.BlockSpec((POOL_HALO, BRANCH_W), lambda i: (jnp.maximum(i * hb - 1, 0), 0)),
                  pl.BlockSpec((t, BRANCH_W), lambda i: (i, 1)),
                  pl.BlockSpec((POOL_GROUPS, POOL_GW, POOL_GW), lambda i: (0, 0, 0)),
                  pl.BlockSpec((1, BRANCH_W), lambda i: (0, 0))],
        out_specs=(pl.BlockSpec((t, BRANCH_W), lambda i: (i, 0)), pl.BlockSpec((BRANCH_W, t), lambda i: (0, i))),
        scratch_shapes=[pltpu.VMEM((t + POOL_HALO, BRANCH_W), F32)],
        compiler_params=_cparams(("parallel",)), name="pool_fwd")(p1, p1, p1, pool_w, pool_scale)


def _pool_bwd(p1, dy, pool_w, pool_scale, *, t=256):
    s = p1.shape[0]
    t = _tile(s, t, 128)
    hb = t // POOL_HALO
    nt = s // t
    last_hb = s // POOL_HALO - 1

    def body(pv_ref, halo_ref, pz_ref, pzn_ref, dy_ref, dyn_ref, pw_ref, sc_ref, d_ref, gw_ref, gs_ref, scr, scr2, scr3):
        i = pl.program_id(0)
        mixed = _pool_mixed(scr, pv_ref[...], halo_ref[...], i == 0, i * t, t)
        scale = sc_ref[...]
        pz = pz_ref[...]
        dy = dy_ref[...]
        raw = jnp.concatenate([_dot(mixed[g].astype(BF16), pw_ref[g]) for g in range(POOL_GROUPS)], axis=1)
        d_pool = dy * _silu(pz)
        d_ref[:, BRANCH_W:2 * BRANCH_W] = (dy * raw * scale * _dsilu(pz)).astype(BF16)
        gs_part = jnp.sum(d_pool * raw, axis=0, keepdims=True)
        scr2[0:t, :] = d_pool * scale
        scr2[t:t + POOL_HALO, :] = jnp.where(i == nt - 1, 0.0, dyn_ref[...] * _silu(pzn_ref[...]) * scale)
        row = i * t + lax.broadcasted_iota(jnp.int32, (t + POOL_HALO, 1), 0)
        gw_parts = []
        for g in range(POOL_GROUPS):
            w = 2 ** (g + 1)
            sl = slice(g * POOL_GW, (g + 1) * POOL_GW)
            do_g = scr2[:, sl].astype(BF16)
            dm = _dot_nt(do_g, pw_ref[g])
            gw_parts.append(_dot_tn(mixed[g].astype(BF16), do_g[0:t, :]))
            cnt = jnp.minimum(row + 1, w).astype(F32)
            acc = dm / cnt
            sh = 1
            while sh < w:
                acc = acc + pltpu.roll(acc, t + POOL_HALO - sh, 0)
                sh *= 2
            scr3[:, sl] = acc - dm
        d_ref[:, 0:BRANCH_W] = scr3[0:t, :].astype(BF16)

        @pl.when(i == 0)
        def _():
            for g in range(POOL_GROUPS):
                gw_ref[g] = gw_parts[g]
            gs_ref[...] = gs_part

        @pl.when(i > 0)
        def _():
            for g in range(POOL_GROUPS):
                gw_ref[g] += gw_parts[g]
            gs_ref[...] += gs_part

    tile = lambda col: pl.BlockSpec((t, BRANCH_W), lambda i: (i, col))
    nxt = lambda col: pl.BlockSpec((POOL_HALO, BRANCH_W), lambda i: (jnp.minimum((i + 1) * hb, last_hb), col))
    return pl.pallas_call(
        body,
        out_shape=(jax.ShapeDtypeStruct((s, 2 * BRANCH_W), BF16),
                   jax.ShapeDtypeStruct((POOL_GROUPS, POOL_GW, POOL_GW), F32),
                   jax.ShapeDtypeStruct((1, BRANCH_W), F32)),
        grid=(nt,),
        in_specs=[tile(0), pl.BlockSpec((POOL_HALO, BRANCH_W), lambda i: (jnp.maximum(i * hb - 1, 0), 0)),
                  tile(1), nxt(1),
                  pl.BlockSpec((None, t, BRANCH_W), lambda i: (0, i, 0)),
                  pl.BlockSpec((None, POOL_HALO, BRANCH_W), lambda i: (0, jnp.minimum((i + 1) * hb, last_hb), 0)),
                  pl.BlockSpec((POOL_GROUPS, POOL_GW, POOL_GW), lambda i: (0, 0, 0)),
                  pl.BlockSpec((1, BRANCH_W), lambda i: (0, 0))],
        out_specs=(pl.BlockSpec((t, 2 * BRANCH_W), lambda i: (i, 0)),
                   pl.BlockSpec((POOL_GROUPS, POOL_GW, POOL_GW), lambda i: (0, 0, 0)),
                   pl.BlockSpec((1, BRANCH_W), lambda i: (0, 0))),
        scratch_shapes=[pltpu.VMEM((t + POOL_HALO, BRANCH_W), F32)] * 3,
        compiler_params=_cparams(("arbitrary",)), name="pool_bwd")(p1, p1, p1, p1, dy, dy, pool_w, pool_scale)


CONV_COL = 4


def _conv_taps(scr, u, uh, first, t):
    scr[0:CONV_HALO, :] = jnp.where(first, 0.0, uh)
    scr[CONV_HALO:CONV_HALO + t, :] = u
    e = scr[...]
    u1 = pltpu.roll(e, 1, 0)[CONV_HALO:CONV_HALO + t, :]
    u2 = pltpu.roll(e, 2, 0)[CONV_HALO:CONV_HALO + t, :]
    return u2, u1, u


def _conv_fwd(p1, conv_w, *, t=256):
    s = p1.shape[0]
    t = _tile(s, t, 128)
    hb = t // CONV_HALO

    def body(cb_ref, cc_ref, cx_ref, cz_ref, cch_ref, cxh_ref, w_ref, y_ref, yt_ref, scr):
        i = pl.program_id(0)
        u0, u1, u2 = _conv_taps(scr, cc_ref[...] * cx_ref[...], cch_ref[...] * cxh_ref[...], i == 0, t)
        w = w_ref[...]
        y = (w[0:1, :] * u0 + w[1:2, :] * u1 + w[2:3, :] * u2) * cb_ref[...] * _silu(cz_ref[...])
        y_ref[...] = y.astype(BF16)
        yt_ref[...] = y.T.astype(BF16)

    tile = lambda col: pl.BlockSpec((t, BRANCH_W), lambda i: (i, CONV_COL + col))
    prev = lambda col: pl.BlockSpec((CONV_HALO, BRANCH_W), lambda i: (jnp.maximum(i * hb - 1, 0), CONV_COL + col))
    return pl.pallas_call(
        body, out_shape=(jax.ShapeDtypeStruct((s, BRANCH_W), BF16), jax.ShapeDtypeStruct((BRANCH_W, s), BF16)),
        grid=(s // t,),
        in_specs=[tile(0), tile(1), tile(2), tile(3), prev(1), prev(2), pl.BlockSpec((3, BRANCH_W), lambda i: (0, 0))],
        out_specs=(pl.BlockSpec((t, BRANCH_W), lambda i: (i, 0)), pl.BlockSpec((BRANCH_W, t), lambda i: (0, i))),
        scratch_shapes=[pltpu.VMEM((t + CONV_HALO, BRANCH_W), F32)],
        compiler_params=_cparams(("parallel",)), name="conv_fwd")(p1, p1, p1, p1, p1, p1, conv_w)


def _conv_bwd(p1, dy, conv_w, *, t=256):
    s = p1.shape[0]
    t = _tile(s, t, 128)
    hb = t // CONV_HALO
    nt = s // t
    last_hb = s // CONV_HALO - 1

    def body(cb_ref, cc_ref, cx_ref, cz_ref, cch_ref, cxh_ref, cbn_ref, czn_ref, dy_ref, dyn_ref, w_ref,
             d_ref, g0_ref, g1_ref, g2_ref, scr, scr2):
        i = pl.program_id(0)
        cb, cc, cx, cz = cb_ref[...], cc_ref[...], cx_ref[...], cz_ref[...]
        u0, u1, u2 = _conv_taps(scr, cc * cx, cch_ref[...] * cxh_ref[...], i == 0, t)
        w = w_ref[...]
        w0, w1, w2 = w[0:1, :], w[1:2, :], w[2:3, :]
        y = w0 * u0 + w1 * u1 + w2 * u2
        dy = dy_ref[...]
        sz = _silu(cz)
        d_ref[:, 0:BRANCH_W] = (dy * sz * y).astype(BF16)
        d_ref[:, 3 * BRANCH_W:4 * BRANCH_W] = (dy * cb * y * _dsilu(cz)).astype(BF16)
        d_y = dy * sz * cb
        parts = [jnp.sum(d_y * u, axis=0, keepdims=True) for u in (u0, u1, u2)]
        scr2[0:t, :] = d_y
        scr2[t:t + CONV_HALO, :] = jnp.where(i == nt - 1, 0.0, dyn_ref[...] * _silu(czn_ref[...]) * cbn_ref[...])
        e = scr2[...]
        n = t + CONV_HALO
        du = (w2 * e + w1 * pltpu.roll(e, n - 1, 0) + w0 * pltpu.roll(e, n - 2, 0))[0:t, :]
        d_ref[:, BRANCH_W:2 * BRANCH_W] = (du * cx).astype(BF16)
        d_ref[:, 2 * BRANCH_W:3 * BRANCH_W] = (du * cc).astype(BF16)

        @pl.when(i == 0)
        def _():
            g0_ref[...] = parts[0]
            g1_ref[...] = parts[1]
            g2_ref[...] = parts[2]

        @pl.when(i > 0)
        def _():
            g0_ref[...] += parts[0]
            g1_ref[...] += parts[1]
            g2_ref[...] += parts[2]

    tile = lambda col: pl.BlockSpec((t, BRANCH_W), lambda i: (i, CONV_COL + col))
    prev = lambda col: pl.BlockSpec((CONV_HALO, BRANCH_W), lambda i: (jnp.maximum(i * hb - 1, 0), CONV_COL + col))
    nxt = lambda col: pl.BlockSpec((CONV_HALO, BRANCH_W), lambda i: (jnp.minimum((i + 1) * hb, last_hb), CONV_COL + col))
    vec = pl.BlockSpec((1, BRANCH_W), lambda i: (0, 0))
    gshape = jax.ShapeDtypeStruct((1, BRANCH_W), F32)
    return pl.pallas_call(
        body, out_shape=(jax.ShapeDtypeStruct((s, 4 * BRANCH_W), BF16), gshape, gshape, gshape),
        grid=(nt,),
        in_specs=[tile(0), tile(1), tile(2), tile(3), prev(1), prev(2), nxt(0), nxt(3),
                  pl.BlockSpec((None, t, BRANCH_W), lambda i: (2, i, 0)),
                  pl.BlockSpec((None, CONV_HALO, BRANCH_W), lambda i: (2, jnp.minimum((i + 1) * hb, last_hb), 0)),
                  pl.BlockSpec((3, BRANCH_W), lambda i: (0, 0))],
        out_specs=(pl.BlockSpec((t, 4 * BRANCH_W), lambda i: (i, 0)), vec, vec, vec),
        scratch_shapes=[pltpu.VMEM((t + CONV_HALO, BRANCH_W), F32)] * 2,
        compiler_params=_cparams(("arbitrary",)), name="conv_bwd")(p1, p1, p1, p1, p1, p1, p1, p1, dy, dy, conv_w)


def _mla_prep_fwd(p1, kr, ctab, stab, qa_g, kva_g, w_uq, w_ukv, q_g, k_g, *, t=256):
    s = p1.shape[0]
    t = _tile(s, t, 128)
    ql, kvl = qa_g.shape[1], kva_g.shape[1]
    assert ql == kvl and 2048 % ql == 0
    cq_blk = 2048 // ql

    def body(cq_ref, ckv_ref, kr_ref, c_ref, s_ref, qag_ref, kvag_ref, wuq_ref, wukv_ref, qg_ref, kg_ref,
             qf_ref, kf_ref, v_ref, vt_ref):
        q_raw = _dot(_rms(cq_ref[...], qag_ref[...], ql).astype(BF16), wuq_ref[...])
        kv_raw = _dot(_rms(ckv_ref[...], kvag_ref[...], kvl).astype(BF16), wukv_ref[...])
        krp = kr_ref[...]
        ct, st = c_ref[...], s_ref[...]
        for h in range(MLA_HEADS):
            qh = q_raw[:, h * HEAD_PAD:(h + 1) * HEAD_PAD]
            qf_ref[h] = _rope(_rms(qh, qg_ref[...], QK_HEAD), ct, st).astype(BF16)
            kh = jnp.concatenate([kv_raw[:, h * HEAD_PAD:h * HEAD_PAD + QK_NOPE], krp], axis=1)
            kf_ref[h] = _rope(_rms(kh, kg_ref[...], QK_HEAD), ct, st).astype(BF16)
            vh = kv_raw[:, h * HEAD_PAD + QK_NOPE:(h + 1) * HEAD_PAD]
            v_ref[h] = vh.astype(BF16)
            vt_ref[h] = vh.T.astype(BF16)

    full = lambda shp: pl.BlockSpec(shp, lambda i: tuple(0 for _ in shp))
    return pl.pallas_call(
        body,
        out_shape=(jax.ShapeDtypeStruct((MLA_HEADS, s, HEAD_PAD), BF16), jax.ShapeDtypeStruct((MLA_HEADS, s, HEAD_PAD), BF16),
                   jax.ShapeDtypeStruct((MLA_HEADS, s, V_HEAD), BF16), jax.ShapeDtypeStruct((MLA_HEADS, V_HEAD, s), BF16)),
        grid=(s // t,),
        in_specs=[pl.BlockSpec((t, ql), lambda i: (i, cq_blk)), pl.BlockSpec((t, kvl), lambda i: (i, cq_blk + 1)),
                  pl.BlockSpec((t, 128), lambda i: (i, 0)),
                  pl.BlockSpec((t, HEAD_PAD), lambda i: (i, 0)), pl.BlockSpec((t, HEAD_PAD), lambda i: (i, 0)),
                  full((1, ql)), full((1, kvl)), full(w_uq.shape), full(w_ukv.shape), full((1, HEAD_PAD)), full((1, HEAD_PAD))],
        out_specs=(pl.BlockSpec((MLA_HEADS, t, HEAD_PAD), lambda i: (0, i, 0)),
                   pl.BlockSpec((MLA_HEADS, t, HEAD_PAD), lambda i: (0, i, 0)),
                   pl.BlockSpec((MLA_HEADS, t, V_HEAD), lambda i: (0, i, 0)),
                   pl.BlockSpec((MLA_HEADS, V_HEAD, t), lambda i: (0, 0, i))),
        compiler_params=_cparams(("parallel",)), name="mla_prep_fwd")(
            p1, p1, kr, ctab, stab, qa_g, kva_g, w_uq, w_ukv, q_g, k_g)


def _mla_prep_bwd(p1, kr, ctab, stab, qa_g, kva_g, w_uq, w_ukv, q_g, k_g, dqf, dkf, dv, *, t=256):
    s = p1.shape[0]
    t = _tile(s, t, 128)
    ql, kvl = qa_g.shape[1], kva_g.shape[1]
    cq_blk = 2048 // ql
    nq = MLA_HEADS * HEAD_PAD

    def body(cq_ref, ckv_ref, kr_ref, c_ref, s_ref, qag_ref, kvag_ref, wuq_ref, wukv_ref, qg_ref, kg_ref,
             dqf_ref, dkf_ref, dv_ref,
             dc_ref, dkr_ref, dqraw_ref, dkvraw_ref, cqnt_ref, ckvnt_ref, gqa_ref, gkva_ref, gqg_ref, gkg_ref):
        i = pl.program_id(0)
        cq, ckv = cq_ref[...], ckv_ref[...]
        cqn = _rms(cq, qag_ref[...], ql)
        ckvn = _rms(ckv, kvag_ref[...], kvl)
        cqnt_ref[...] = cqn.T.astype(BF16)
        ckvnt_ref[...] = ckvn.T.astype(BF16)
        q_raw = _dot(cqn.astype(BF16), wuq_ref[...])
        kv_raw = _dot(ckvn.astype(BF16), wukv_ref[...])
        krp = kr_ref[...]
        ct, st = c_ref[...], s_ref[...]
        gqg = jnp.zeros((1, HEAD_PAD), F32)
        gkg = jnp.zeros((1, HEAD_PAD), F32)
        dkr = jnp.zeros((t, HEAD_PAD - QK_NOPE), F32)
        dq_parts, dkv_parts = [], []
        for h in range(MLA_HEADS):
            qh = q_raw[:, h * HEAD_PAD:(h + 1) * HEAD_PAD]
            dx, dg = _rms_bwd(qh, qg_ref[...], QK_HEAD, _rope_bwd(dqf_ref[h], ct, st))
            gqg = gqg + jnp.sum(dg, axis=0, keepdims=True)
            dq_parts.append(dx)
            kh = jnp.concatenate([kv_raw[:, h * HEAD_PAD:h * HEAD_PAD + QK_NOPE], krp], axis=1)
            dx, dg = _rms_bwd(kh, kg_ref[...], QK_HEAD, _rope_bwd(dkf_ref[h], ct, st))
            gkg = gkg + jnp.sum(dg, axis=0, keepdims=True)
            dkr = dkr + dx[:, QK_NOPE:HEAD_PAD]
            dkv_parts += [dx[:, 0:QK_NOPE], dv_ref[h]]
        dq_raw = jnp.concatenate(dq_parts, axis=1).astype(BF16)
        dkv_raw = jnp.concatenate(dkv_parts, axis=1).astype(BF16)
        dqraw_ref[...] = dq_raw
        dkvraw_ref[...] = dkv_raw
        dkr_ref[...] = dkr.astype(BF16)
        dcq, gqa = _rms_bwd(cq, qag_ref[...], ql, _dot_nt(dq_raw, wuq_ref[...]))
        dckv, gkva = _rms_bwd(ckv, kvag_ref[...], kvl, _dot_nt(dkv_raw, wukv_ref[...]))
        dc_ref[:, 0:ql] = dcq.astype(BF16)
        dc_ref[:, ql:ql + kvl] = dckv.astype(BF16)
        gqa = jnp.sum(gqa, axis=0, keepdims=True)
        gkva = jnp.sum(gkva, axis=0, keepdims=True)

        @pl.when(i == 0)
        def _():
            gqa_ref[...] = gqa
            gkva_ref[...] = gkva
            gqg_ref[...] = gqg
            gkg_ref[...] = gkg

        @pl.when(i > 0)
        def _():
            gqa_ref[...] += gqa
            gkva_ref[...] += gkva
            gqg_ref[...] += gqg
            gkg_ref[...] += gkg

    full = lambda shp: pl.BlockSpec(shp, lambda i: tuple(0 for _ in shp))
    hblk = lambda w: pl.BlockSpec((MLA_HEADS, t, w), lambda i: (0, i, 0))
    sds = jax.ShapeDtypeStruct
    return pl.pallas_call(
        body,
        out_shape=(sds((s, ql + kvl), BF16), sds((s, 128), BF16), sds((s, nq), BF16), sds((s, nq), BF16),
                   sds((ql, s), BF16), sds((kvl, s), BF16),
                   sds((1, ql), F32), sds((1, kvl), F32), sds((1, HEAD_PAD), F32), sds((1, HEAD_PAD), F32)),
        grid=(s // t,),
        in_specs=[pl.BlockSpec((t, ql), lambda i: (i, cq_blk)), pl.BlockSpec((t, kvl), lambda i: (i, cq_blk + 1)),
                  pl.BlockSpec((t, 128), lambda i: (i, 0)),
                  pl.BlockSpec((t, HEAD_PAD), lambda i: (i, 0)), pl.BlockSpec((t, HEAD_PAD), lambda i: (i, 0)),
                  full((1, ql)), full((1, kvl)), full(w_uq.shape), full(w_ukv.shape), full((1, HEAD_PAD)), full((1, HEAD_PAD)),
                  hblk(HEAD_PAD), hblk(HEAD_PAD), hblk(V_HEAD)],
        out_specs=(pl.BlockSpec((t, ql + kvl), lambda i: (i, 0)), pl.BlockSpec((t, 128), lambda i: (i, 0)),
                   pl.BlockSpec((t, nq), lambda i: (i, 0)), pl.BlockSpec((t, nq), lambda i: (i, 0)),
                   pl.BlockSpec((ql, t), lambda i: (0, i)), pl.BlockSpec((kvl, t), lambda i: (0, i)),
                   full((1, ql)), full((1, kvl)), full((1, HEAD_PAD)), full((1, HEAD_PAD))),
        compiler_params=_cparams(("arbitrary",)), name="mla_prep_bwd")(
            p1, p1, kr, ctab, stab, qa_g, kva_g, w_uq, w_ukv, q_g, k_g, dqf, dkf, dv)


MZ_BLK128 = 3072 // 128
ATT_SCALE = QK_HEAD ** -0.5


HPS = 2


def _causal_pairs(nq, by_query):
    if by_query:
        prs = [(qi, ki) for qi in range(nq) for ki in range(qi + 1)]
    else:
        prs = [(qi, ki) for ki in range(nq) for qi in range(ki, nq)]
    return (jnp.asarray([p[0] for p in prs], jnp.int32), jnp.asarray([p[1] for p in prs], jnp.int32), len(prs))


def _attn_fwd(qf, kf, vt, p1, *, tq):
    nh, s, _ = qf.shape
    nq = s // tq
    qtab, ktab, npairs = _causal_pairs(nq, True)
    wv = HPS * V_HEAD

    def body(qt_ref, kt_ref, q_ref, k_ref, vt_ref, mz_ref, y_ref, yt_ref, o_ref, lse_ref, lset_ref, m_scr, l_scr, acc_scr):
        pr = pl.program_id(1)
        qi, ki = qt_ref[pr], kt_ref[pr]

        @pl.when(ki == 0)
        def _():
            m_scr[...] = jnp.full((HPS, 1, tq), NEG, F32)
            l_scr[...] = jnp.zeros((HPS, 1, tq), F32)
            acc_scr[...] = jnp.zeros((HPS, V_HEAD, tq), F32)

        def step(diagonal):
            for u in range(HPS):
                st = _dot_nt(k_ref[u], q_ref[u]) * ATT_SCALE
                if diagonal:
                    r = lax.broadcasted_iota(jnp.int32, (tq, tq), 0)
                    c = lax.broadcasted_iota(jnp.int32, (tq, tq), 1)
                    st = jnp.where(r <= c, st, NEG)
                m_old = m_scr[u]
                m_new = jnp.maximum(m_old, jnp.max(st, axis=0, keepdims=True))
                alpha = jnp.exp(m_old - m_new)
                pt = jnp.exp(st - m_new)
                l_scr[u] = alpha * l_scr[u] + jnp.sum(pt, axis=0, keepdims=True)
                acc_scr[u] = alpha * acc_scr[u] + _dot(vt_ref[u], pt.astype(BF16))
                m_scr[u] = m_new

        @pl.when(ki < qi)
        def _():
            step(False)

        @pl.when(ki == qi)
        def _():
            step(True)
            outs = []
            for u in range(HPS):
                l = l_scr[u]
                outs.append((acc_scr[u] / l).T)
                lset = m_scr[u] + jnp.log(l)
                lset_ref[u] = lset
                lse_ref[u] = jnp.broadcast_to(lset, (128, tq)).T[:, 0:1]
            o = jnp.concatenate(outs, axis=1)
            o_ref[...] = o
            y = o * _silu(mz_ref[...])
            y_ref[...] = y.astype(BF16)
            yt_ref[...] = y.T.astype(BF16)

    sds = jax.ShapeDtypeStruct
    gs = pltpu.PrefetchScalarGridSpec(
        num_scalar_prefetch=2, grid=(nh // HPS, npairs),
        in_specs=[pl.BlockSpec((HPS, tq, HEAD_PAD), lambda h, p, qt, kt: (h, qt[p], 0)),
                  pl.BlockSpec((HPS, tq, HEAD_PAD), lambda h, p, qt, kt: (h, kt[p], 0)),
                  pl.BlockSpec((HPS, V_HEAD, tq), lambda h, p, qt, kt: (h, 0, kt[p])),
                  pl.BlockSpec((tq, wv), lambda h, p, qt, kt: (qt[p], MZ_BLK128 // HPS + h))],
        out_specs=(pl.BlockSpec((tq, wv), lambda h, p, qt, kt: (qt[p], h)),
                   pl.BlockSpec((wv, tq), lambda h, p, qt, kt: (h, qt[p])),
                   pl.BlockSpec((tq, wv), lambda h, p, qt, kt: (qt[p], h)),
                   pl.BlockSpec((HPS, tq, 1), lambda h, p, qt, kt: (h, qt[p], 0)),
                   pl.BlockSpec((HPS, 1, tq), lambda h, p, qt, kt: (h, 0, qt[p]))),
        scratch_shapes=[pltpu.VMEM((HPS, 1, tq), F32), pltpu.VMEM((HPS, 1, tq), F32), pltpu.VMEM((HPS, V_HEAD, tq), F32)])
    return pl.pallas_call(
        body,
        out_shape=(sds((s, BRANCH_W), BF16), sds((BRANCH_W, s), BF16), sds((s, BRANCH_W), F32),
                   sds((nh, s, 1), F32), sds((nh, 1, s), F32)),
        grid_spec=gs, compiler_params=_cparams(("parallel", "arbitrary")), name="attn_fwd")(qtab, ktab, qf, kf, vt, p1)


def _attn_bwd_pre(dy4, o, p1, *, t=256):
    s = o.shape[0]
    t = _tile(s, t, 128)

    def body(dy_ref, o_ref, mz_ref, do_ref, dmz_ref, dl_ref, dlt_ref):
        dy, o_, mz = dy_ref[...], o_ref[...], mz_ref[...]
        do = dy * _silu(mz)
        do_ref[...] = do.astype(BF16)
        dmz_ref[...] = (dy * o_ * _dsilu(mz)).astype(BF16)
        prod = do * o_
        for h in range(MLA_HEADS):
            dl = jnp.sum(prod[:, h * V_HEAD:(h + 1) * V_HEAD], axis=-1, keepdims=True)
            dl_ref[h] = dl
            dlt_ref[h] = jnp.broadcast_to(dl, (t, 128)).T[0:1, :]

    sds = jax.ShapeDtypeStruct
    return pl.pallas_call(
        body,
        out_shape=(sds((s, BRANCH_W), BF16), sds((s, BRANCH_W), BF16), sds((MLA_HEADS, s, 1), F32), sds((MLA_HEADS, 1, s), F32)),
        grid=(s // t,),
        in_specs=[pl.BlockSpec((None, t, BRANCH_W), lambda i: (1, i, 0)), pl.BlockSpec((t, BRANCH_W), lambda i: (i, 0)),
                  pl.BlockSpec((t, BRANCH_W), lambda i: (i, 3))],
        out_specs=(pl.BlockSpec((t, BRANCH_W), lambda i: (i, 0)), pl.BlockSpec((t, BRANCH_W), lambda i: (i, 0)),
                   pl.BlockSpec((MLA_HEADS, t, 1), lambda i: (0, i, 0)), pl.BlockSpec((MLA_HEADS, 1, t), lambda i: (0, 0, i))),
        compiler_params=_cparams(("parallel",)), name="attn_bwd_pre")(dy4, o, p1)


def _attn_bwd_dq(qf, kf, vv, do, lse, delta, *, tq):
    nh, s, _ = qf.shape
    nq = s // tq
    qtab, ktab, npairs = _causal_pairs(nq, True)
    wv = HPS * V_HEAD

    def body(qt_ref, kt_ref, q_ref, k_ref, v_ref, do_ref, lse_ref, dl_ref, dq_ref, acc_scr):
        pr = pl.program_id(1)
        qi, ki = qt_ref[pr], kt_ref[pr]

        @pl.when(ki == 0)
        def _():
            acc_scr[...] = jnp.zeros((HPS, tq, HEAD_PAD), F32)

        def step(diagonal):
            for u in range(HPS):
                k = k_ref[u]
                sc = _dot_nt(q_ref[u], k) * ATT_SCALE
                p = jnp.exp(sc - lse_ref[u])
                if diagonal:
                    r = lax.broadcasted_iota(jnp.int32, (tq, tq), 0)
                    c = lax.broadcasted_iota(jnp.int32, (tq, tq), 1)
                    p = jnp.where(c <= r, p, 0.0)
                dp = _dot_nt(do_ref[:, u * V_HEAD:(u + 1) * V_HEAD], v_ref[u])
                ds = p * (dp - dl_ref[u]) * ATT_SCALE
                acc_scr[u] += _dot(ds.astype(BF16), k)

        @pl.when(ki < qi)
        def _():
            step(False)

        @pl.when(ki == qi)
        def _():
            step(True)
            dq_ref[...] = acc_scr[...]

    gs = pltpu.PrefetchScalarGridSpec(
        num_scalar_prefetch=2, grid=(nh // HPS, npairs),
        in_specs=[pl.BlockSpec((HPS, tq, HEAD_PAD), lambda h, p, qt, kt: (h, qt[p], 0)),
                  pl.BlockSpec((HPS, tq, HEAD_PAD), lambda h, p, qt, kt: (h, kt[p], 0)),
                  pl.BlockSpec((HPS, tq, V_HEAD), lambda h, p, qt, kt: (h, kt[p], 0)),
                  pl.BlockSpec((tq, wv), lambda h, p, qt, kt: (qt[p], h)),
                  pl.BlockSpec((HPS, tq, 1), lambda h, p, qt, kt: (h, qt[p], 0)),
                  pl.BlockSpec((HPS, tq, 1), lambda h, p, qt, kt: (h, qt[p], 0))],
        out_specs=pl.BlockSpec((HPS, tq, HEAD_PAD), lambda h, p, qt, kt: (h, qt[p], 0)),
        scratch_shapes=[pltpu.VMEM((HPS, tq, HEAD_PAD), F32)])
    return pl.pallas_call(
        body, out_shape=jax.ShapeDtypeStruct((nh, s, HEAD_PAD), F32), grid_spec=gs,
        compiler_params=_cparams(("parallel", "arbitrary")), name="attn_bwd_dq")(qtab, ktab, qf, kf, vv, do, lse, delta)


def _attn_bwd_dkv(qf, kf, vv, do, lset, deltat, *, tq):
    nh, s, _ = qf.shape
    nq = s // tq
    qtab, ktab, npairs = _causal_pairs(nq, False)
    wv = HPS * V_HEAD

    def body(qt_ref, kt_ref, k_ref, v_ref, q_ref, do_ref, lse_ref, dl_ref, dk_ref, dv_ref, dk_scr, dv_scr):
        pr = pl.program_id(1)
        qi, ki = qt_ref[pr], kt_ref[pr]

        def step(diagonal):
            for u in range(HPS):
                q = q_ref[u]
                do_ = do_ref[:, u * V_HEAD:(u + 1) * V_HEAD]
                st = _dot_nt(k_ref[u], q) * ATT_SCALE
                pt = jnp.exp(st - lse_ref[u])
                if diagonal:
                    r = lax.broadcasted_iota(jnp.int32, (tq, tq), 0)
                    c = lax.broadcasted_iota(jnp.int32, (tq, tq), 1)
                    pt = jnp.where(r <= c, pt, 0.0)
                dpt = _dot_nt(v_ref[u], do_)
                dst = pt * (dpt - dl_ref[u]) * ATT_SCALE
                if diagonal:
                    dv_scr[u] = _dot(pt.astype(BF16), do_)
                    dk_scr[u] = _dot(dst.astype(BF16), q)
                else:
                    dv_scr[u] += _dot(pt.astype(BF16), do_)
                    dk_scr[u] += _dot(dst.astype(BF16), q)

        @pl.when(qi == ki)
        def _():
            step(True)

        @pl.when(qi > ki)
        def _():
            step(False)

        @pl.when(qi == nq - 1)
        def _():
            dk_ref[...] = dk_scr[...]
            dv_ref[...] = dv_scr[...]

    sds = jax.ShapeDtypeStruct
    gs = pltpu.PrefetchScalarGridSpec(
        num_scalar_prefetch=2, grid=(nh // HPS, npairs),
        in_specs=[pl.BlockSpec((HPS, tq, HEAD_PAD), lambda h, p, qt, kt: (h, kt[p], 0)),
                  pl.BlockSpec((HPS, tq, V_HEAD), lambda h, p, qt, kt: (h, kt[p], 0)),
                  pl.BlockSpec((HPS, tq, HEAD_PAD), lambda h, p, qt, kt: (h, qt[p], 0)),
                  pl.BlockSpec((tq, wv), lambda h, p, qt, kt: (qt[p], h)),
                  pl.BlockSpec((HPS, 1, tq), lambda h, p, qt, kt: (h, 0, qt[p])),
                  pl.BlockSpec((HPS, 1, tq), lambda h, p, qt, kt: (h, 0, qt[p]))],
        out_specs=(pl.BlockSpec((HPS, tq, HEAD_PAD), lambda h, p, qt, kt: (h, kt[p], 0)),
                   pl.BlockSpec((HPS, tq, V_HEAD), lambda h, p, qt, kt: (h, kt[p], 0))),
        scratch_shapes=[pltpu.VMEM((HPS, tq, HEAD_PAD), F32), pltpu.VMEM((HPS, tq, V_HEAD), F32)])
    return pl.pallas_call(
        body, out_shape=(sds((nh, s, HEAD_PAD), F32), sds((nh, s, V_HEAD), F32)), grid_spec=gs,
        compiler_params=_cparams(("parallel", "arbitrary")), name="attn_bwd_dkv")(qtab, ktab, kf, vv, qf, do, lset, deltat)


XATT_SCALE = XATTN_HEAD_DIM ** -0.5
XQ_COL = 8


def _memkv_prep(mem_kv, k_g):
    m = mem_kv.shape[0]

    def body(kv_ref, g_ref, k_ref, v_ref):
        for h in range(XATTN_HEADS):
            sl = slice(h * XATTN_HEAD_DIM, (h + 1) * XATTN_HEAD_DIM)
            k_ref[:, sl] = _rms(kv_ref[:, sl], g_ref[...], XATTN_HEAD_DIM).astype(BF16)
        v_ref[...] = kv_ref[:, BRANCH_W:2 * BRANCH_W].astype(BF16)

    sds = jax.ShapeDtypeStruct
    return pl.pallas_call(body, out_shape=(sds((m, BRANCH_W), BF16), sds((m, BRANCH_W), BF16)),
                          compiler_params=_cparams(), name="memkv_prep")(mem_kv, k_g)


def _memkv_prep_bwd(mem_kv, k_g, dk, dv):
    m = mem_kv.shape[0]

    def body(kv_ref, g_ref, dk_ref, dv_ref, d_ref, gk_ref):
        gk = jnp.zeros((1, XATTN_HEAD_DIM), F32)
        for h in range(XATTN_HEADS):
            sl = slice(h * XATTN_HEAD_DIM, (h + 1) * XATTN_HEAD_DIM)
            dx, dg = _rms_bwd(kv_ref[:, sl], g_ref[...], XATTN_HEAD_DIM, dk_ref[:, sl])
            d_ref[:, sl] = dx.astype(BF16)
            gk = gk + jnp.sum(dg, axis=0, keepdims=True)
        d_ref[:, BRANCH_W:2 * BRANCH_W] = dv_ref[...].astype(BF16)
        gk_ref[...] = gk

    sds = jax.ShapeDtypeStruct
    return pl.pallas_call(body, out_shape=(sds((m, 2 * BRANCH_W), BF16), sds((1, XATTN_HEAD_DIM), F32)),
                          compiler_params=_cparams(), name="memkv_prep_bwd")(mem_kv, k_g, dk, dv)


def _xattn_probs(xq, k_ref, qg, h):
    sl = slice(h * XATTN_HEAD_DIM, (h + 1) * XATTN_HEAD_DIM)
    q = _rms(xq[:, sl], qg, XATTN_HEAD_DIM).astype(BF16)
    sc = _dot_nt(q, k_ref[:, sl]) * XATT_SCALE
    e = jnp.exp(sc - jnp.max(sc, axis=-1, keepdims=True))
    return q, e / jnp.sum(e, axis=-1, keepdims=True)


def _xattn_fwd(p1, kx, vx, q_g, *, t=256):
    s = p1.shape[0]
    m = kx.shape[0]
    t = _tile(s, t, 128)

    def body(xq_ref, xz_ref, k_ref, v_ref, g_ref, y_ref, yt_ref):
        xq = xq_ref[...]
        outs = []
        for h in range(XATTN_HEADS):
            _, p = _xattn_probs(xq, k_ref, g_ref[...], h)
            outs.append(_dot(p.astype(BF16), v_ref[:, h * XATTN_HEAD_DIM:(h + 1) * XATTN_HEAD_DIM]))
        y = jnp.concatenate(outs, axis=1) * _silu(xz_ref[...])
        y_ref[...] = y.astype(BF16)
        yt_ref[...] = y.T.astype(BF16)

    full = lambda shp: pl.BlockSpec(shp, lambda i: tuple(0 for _ in shp))
    sds = jax.ShapeDtypeStruct
    return pl.pallas_call(
        body, out_shape=(sds((s, BRANCH_W), BF16), sds((BRANCH_W, s), BF16)), grid=(s // t,),
        in_specs=[pl.BlockSpec((t, BRANCH_W), lambda i: (i, XQ_COL)), pl.BlockSpec((t, BRANCH_W), lambda i: (i, XQ_COL + 1)),
                  full((m, BRANCH_W)), full((m, BRANCH_W)), full((1, XATTN_HEAD_DIM))],
        out_specs=(pl.BlockSpec((t, BRANCH_W), lambda i: (i, 0)), pl.BlockSpec((BRANCH_W, t), lambda i: (0, i))),
        compiler_params=_cparams(("parallel",)), name="xattn_fwd")(p1, p1, kx, vx, q_g)


def _xattn_bwd(p1, dy4, kx, vx, q_g, *, t=256):
    s = p1.shape[0]
    m = kx.shape[0]
    t = _tile(s, t, 128)

    def body(xq_ref, xz_ref, dy_ref, k_ref, v_ref, g_ref, d_ref, dk_ref, dv_ref, gq_ref):
        i = pl.program_id(0)
        xq, xz, dy = xq_ref[...], xz_ref[...], dy_ref[...]
        do = dy * _silu(xz)
        gq = jnp.zeros((1, XATTN_HEAD_DIM), F32)
        outs, dks, dvs = [], [], []
        for h in range(XATTN_HEADS):
            sl = slice(h * XATTN_HEAD_DIM, (h + 1) * XATTN_HEAD_DIM)
            q, p = _xattn_probs(xq, k_ref, g_ref[...], h)
            pb = p.astype(BF16)
            outs.append(_dot(pb, v_ref[:, sl]))
            do_h = do[:, sl].astype(BF16)
            dvs.append(_dot_tn(pb, do_h))
            dp = _dot_nt(do_h, v_ref[:, sl])
            ds = (p * (dp - jnp.sum(p * dp, axis=-1, keepdims=True)) * XATT_SCALE).astype(BF16)
            dks.append(_dot_tn(ds, q))
            dx, dg = _rms_bwd(xq[:, sl], g_ref[...], XATTN_HEAD_DIM, _dot(ds, k_ref[:, sl]))
            d_ref[:, sl] = dx.astype(BF16)
            gq = gq + jnp.sum(dg, axis=0, keepdims=True)
        o = jnp.concatenate(outs, axis=1)
        d_ref[:, BRANCH_W:2 * BRANCH_W] = (dy * o * _dsilu(xz)).astype(BF16)
        dk = jnp.concatenate(dks, axis=1)
        dv = jnp.concatenate(dvs, axis=1)

        @pl.when(i == 0)
        def _():
            dk_ref[...] = dk
            dv_ref[...] = dv
            gq_ref[...] = gq

        @pl.when(i > 0)
        def _():
            dk_ref[...] += dk
            dv_ref[...] += dv
            gq_ref[...] += gq

    full = lambda shp: pl.BlockSpec(shp, lambda i: tuple(0 for _ in shp))
    sds = jax.ShapeDtypeStruct
    return pl.pallas_call(
        body, out_shape=(sds((s, 2 * BRANCH_W), BF16), sds((m, BRANCH_W), F32), sds((m, BRANCH_W), F32), sds((1, XATTN_HEAD_DIM), F32)),
        grid=(s // t,),
        in_specs=[pl.BlockSpec((t, BRANCH_W), lambda i: (i, XQ_COL)), pl.BlockSpec((t, BRANCH_W), lambda i: (i, XQ_COL + 1)),
                  pl.BlockSpec((None, t, BRANCH_W), lambda i: (3, i, 0)),
                  full((m, BRANCH_W)), full((m, BRANCH_W)), full((1, XATTN_HEAD_DIM))],
        out_specs=(pl.BlockSpec((t, 2 * BRANCH_W), lambda i: (i, 0)), full((m, BRANCH_W)), full((m, BRANCH_W)),
                   full((1, XATTN_HEAD_DIM))),
        compiler_params=_cparams(("arbitrary",)), name="xattn_bwd")(p1, p1, dy4, kx, vx, q_g)


def _gate_fwd(ystack, w_branch, gp, gate_b, *, tm=512, tn=1024):
    _, s, _ = ystack.shape
    d = w_branch.shape[2]
    tm, tn = _tile(s, tm, 128), _tile(d, tn)
    nj = d // tn

    def body(y_ref, w_ref, gp_ref, gb_ref, o_ref, ot_ref, acc_scr):
        b = pl.program_id(2)
        part = jax.nn.sigmoid(gp_ref[...] + gb_ref[...]) * _dot(y_ref[...], w_ref[...])

        @pl.when(b == 0)
        def _():
            acc_scr[...] = part

        @pl.when(b > 0)
        def _():
            acc_scr[...] += part

        @pl.when(b == N_BRANCH - 1)
        def _():
            acc = acc_scr[...]
            o_ref[...] = acc.astype(BF16)
            ot_ref[...] = acc.T.astype(BF16)

    sds = jax.ShapeDtypeStruct
    return pl.pallas_call(
        body, out_shape=(sds((s, d), BF16), sds((d, s), BF16)), grid=(s // tm, nj, N_BRANCH),
        in_specs=[pl.BlockSpec((None, tm, BRANCH_W), lambda i, j, b: (b, i, 0)),
                  pl.BlockSpec((None, BRANCH_W, tn), lambda i, j, b: (b, 0, j)),
                  pl.BlockSpec((tm, tn), lambda i, j, b: (i, b * nj + j)),
                  pl.BlockSpec((1, tn), lambda i, j, b: (0, b * nj + j))],
        out_specs=(pl.BlockSpec((tm, tn), lambda i, j, b: (i, j)), pl.BlockSpec((tn, tm), lambda i, j, b: (j, i))),
        scratch_shapes=[pltpu.VMEM((tm, tn), F32)],
        compiler_params=_cparams(("parallel", "parallel", "arbitrary")), name="gate_fwd")(ystack, w_branch, gp, gate_b)


def _gate_bwd(ystack, w_branch, gp, gate_b, dm, *, tm=512, tn=1024):
    _, s, _ = ystack.shape
    d = w_branch.shape[2]
    tm, tn = _tile(s, tm, 128), _tile(d, tn)
    nj = d // tn

    def body(y_ref, w_ref, gp_ref, gb_ref, dm_ref, dp_ref, dg_ref, gb_out_ref):
        i = pl.program_id(2)
        proj = _dot(y_ref[...], w_ref[...])
        gate = jax.nn.sigmoid(gp_ref[...] + gb_ref[...])
        dmv = dm_ref[...]
        dp_ref[...] = (dmv * gate).astype(BF16)
        dpre = dmv * proj * gate * (1.0 - gate)
        dg_ref[...] = dpre.astype(BF16)
        part = jnp.sum(dpre, axis=0, keepdims=True)

        @pl.when(i == 0)
        def _():
            gb_out_ref[...] = part

        @pl.when(i > 0)
        def _():
            gb_out_ref[...] += part

    sds = jax.ShapeDtypeStruct
    return pl.pallas_call(
        body, out_shape=(sds((N_BRANCH, s, d), BF16), sds((s, N_BRANCH * d), BF16), sds((1, N_BRANCH * d), F32)),
        grid=(N_BRANCH, nj, s // tm),
        in_specs=[pl.BlockSpec((None, tm, BRANCH_W), lambda b, j, i: (b, i, 0)),
                  pl.BlockSpec((None, BRANCH_W, tn), lambda b, j, i: (b, 0, j)),
                  pl.BlockSpec((tm, tn), lambda b, j, i: (i, b * nj + j)),
                  pl.BlockSpec((1, tn), lambda b, j, i: (0, b * nj + j)),
                  pl.BlockSpec((tm, tn), lambda b, j, i: (i, j))],
        out_specs=(pl.BlockSpec((None, tm, tn), lambda b, j, i: (b, i, j)),
                   pl.BlockSpec((tm, tn), lambda b, j, i: (i, b * nj + j)),
                   pl.BlockSpec((1, tn), lambda b, j, i: (0, b * nj + j))),
        compiler_params=_cparams(("parallel", "parallel", "arbitrary")), name="gate_bwd")(ystack, w_branch, gp, gate_b, dm)


def _adamw(w, g, m, v, *, name):
    shape = w.shape
    c = shape[-1]
    r = 1
    for n in shape[:-1]:
        r *= n
    w2, g2, m2, v2 = (a.reshape(r, c) for a in (w, g, m, v))
    tr = _tile(r, max(8, (1 << 19) // c // 8 * 8), 8)
    c1 = 1.0 / (1.0 - ADAM_B1 ** ADAM_STEP)
    c2 = 1.0 / (1.0 - ADAM_B2 ** ADAM_STEP)

    def body(w_ref, g_ref, m_ref, v_ref, d_ref, nm_ref, nv_ref):
        gv = g_ref[...]
        nm = ADAM_B1 * m_ref[...] + (1.0 - ADAM_B1) * gv
        nv = ADAM_B2 * v_ref[...] + (1.0 - ADAM_B2) * (gv * gv)
        nm_ref[...] = nm
        nv_ref[...] = nv
        d_ref[...] = -ADAM_LR * ((nm * c1) / (jnp.sqrt(nv * c2) + ADAM_EPS) + ADAM_WD * w_ref[...])

    blk = pl.BlockSpec((tr, c), lambda i: (i, 0))
    sd = jax.ShapeDtypeStruct((r, c), F32)
    d2, nm2, nv2 = pl.pallas_call(body, out_shape=(sd, sd, sd), grid=(r // tr,), in_specs=[blk] * 4, out_specs=(blk,) * 3,
                                  compiler_params=_cparams(("parallel",)), name=name)(w2, g2, m2, v2)
    return d2.reshape(shape), nm2.reshape(shape), nv2.reshape(shape)


def _place():
    x, y, c = lax.axis_index("x"), lax.axis_index("y"), lax.axis_index("c")
    chips = [(1 - x, y), (x, 1 - y), (1 - x, 1 - y)]
    return x, y, c, 2 * x + y, chips, [2 * cx + cy for cx, cy in chips]


ANY = pl.BlockSpec(memory_space=pl.ANY)


def _all_gather(shards):
    n = len(shards)

    def body(*refs):
        ins, outs = refs[:n], refs[n:2 * n]
        send, recv = refs[2 * n:]
        x, y, c, k, chips, ks = _place()
        sib = (x, y, 1 - c)
        sends = []
        for a in range(n):
            for j in range(3):
                cp = pltpu.make_async_remote_copy(src_ref=ins[a].at[c], dst_ref=outs[a].at[c, k], send_sem=send.at[6 * a + j],
                                                  recv_sem=recv.at[6 * a + j], device_id=(*chips[j], c), device_id_type=MESH)
                cp.start()
                sends.append(cp)
        for a in range(n):
            for j in range(3):
                slab = outs[a].at[c, ks[j]]
                pltpu.make_async_remote_copy(src_ref=slab, dst_ref=slab, send_sem=send.at[6 * a + j], recv_sem=recv.at[6 * a + j],
                                             device_id=(*chips[j], c), device_id_type=MESH).wait_recv()
                cp = pltpu.make_async_remote_copy(src_ref=slab, dst_ref=slab, send_sem=send.at[6 * a + 3 + j],
                                                  recv_sem=recv.at[6 * a + 3 + j], device_id=sib, device_id_type=MESH)
                cp.start()
                sends.append(cp)
        for a in range(n):
            for j in range(3):
                slab = outs[a].at[1 - c, ks[j]]
                pltpu.make_async_remote_copy(src_ref=slab, dst_ref=slab, send_sem=send.at[6 * a + 3 + j],
                                             recv_sem=recv.at[6 * a + 3 + j], device_id=sib, device_id_type=MESH).wait_recv()
        for cp in sends:
            cp.wait_send()

    out_shape = tuple(jax.ShapeDtypeStruct((2, 4) + s.shape[1:], s.dtype) for s in shards)
    return pl.pallas_call(
        body, out_shape=out_shape, in_specs=[ANY] * n, out_specs=(ANY,) * n,
        scratch_shapes=[pltpu.SemaphoreType.DMA((6 * n,)), pltpu.SemaphoreType.DMA((6 * n,))],
        name="weights_all_gather")(*shards)


def _rs_exchange_cores(grads):
    n = len(grads)

    def body(*refs):
        ins, outs = refs[:n], refs[n:2 * n]
        send, recv = refs[2 * n:]
        x, y, c, _, _, _ = _place()
        sib = (x, y, 1 - c)
        cps = []
        for a in range(n):
            cp = pltpu.make_async_remote_copy(src_ref=ins[a].at[1 - c], dst_ref=outs[a], send_sem=send.at[a], recv_sem=recv.at[a],
                                              device_id=sib, device_id_type=MESH)
            cp.start()
            cps.append(cp)
        for cp in cps:
            cp.wait()

    out_shape = tuple(jax.ShapeDtypeStruct(g.shape[1:], g.dtype) for g in grads)
    return pl.pallas_call(body, out_shape=out_shape, in_specs=[ANY] * n, out_specs=(ANY,) * n,
                          scratch_shapes=[pltpu.SemaphoreType.DMA((n,)), pltpu.SemaphoreType.DMA((n,))],
                          name="grads_exchange_cores")(*grads)


def _rs_exchange_chips(parts):
    n = len(parts)

    def body(*refs):
        ins, outs = refs[:n], refs[n:2 * n]
        send, recv = refs[2 * n:]
        x, y, c, k, chips, ks = _place()
        sends = []
        for a in range(n):
            for j in range(3):
                cp = pltpu.make_async_remote_copy(src_ref=ins[a].at[ks[j]], dst_ref=outs[a].at[j], send_sem=send.at[3 * a + j],
                                                  recv_sem=recv.at[3 * a + j], device_id=(*chips[j], c), device_id_type=MESH)
                cp.start()
                sends.append(cp)
        for cp in sends:
            cp.wait()

    out_shape = tuple(jax.ShapeDtypeStruct((3,) + p.shape[1:], p.dtype) for p in parts)
    return pl.pallas_call(
        body, out_shape=out_shape, in_specs=[ANY] * n, out_specs=(ANY,) * n,
        scratch_shapes=[pltpu.SemaphoreType.DMA((3 * n,)), pltpu.SemaphoreType.DMA((3 * n,))],
        name="grads_exchange_chips")(*parts)


def _rs_share_cores(bufs):
    n = len(bufs)

    def body(*refs):
        outs = refs[n:2 * n]
        send, recv = refs[2 * n:]
        x, y, c, _, _, _ = _place()
        sib = (x, y, 1 - c)
        cps = []
        for a in range(n):
            cp = pltpu.make_async_remote_copy(src_ref=outs[a].at[c], dst_ref=outs[a].at[c], send_sem=send.at[a], recv_sem=recv.at[a],
                                              device_id=sib, device_id_type=MESH)
            cp.start()
            cps.append(cp)
        for a in range(n):
            slab = outs[a].at[1 - c]
            pltpu.make_async_remote_copy(src_ref=slab, dst_ref=slab, send_sem=send.at[a], recv_sem=recv.at[a],
                                         device_id=sib, device_id_type=MESH).wait_recv()
        for cp in cps:
            cp.wait_send()

    out_shape = tuple(jax.ShapeDtypeStruct(b.shape, b.dtype) for b in bufs)
    return pl.pallas_call(
        body, out_shape=out_shape, in_specs=[ANY] * n, out_specs=(ANY,) * n,
        input_output_aliases={a: a for a in range(n)},
        scratch_shapes=[pltpu.SemaphoreType.DMA((n,)), pltpu.SemaphoreType.DMA((n,))],
        name="grads_share_cores")(*bufs)


def _add_core_halves(g, ra, c_idx, *, name):
    _, _, r, c = g.shape
    tr = _tile(r, max(16, (1 << 19) // c // 16 * 16), 16)

    def body(c_ref, g_ref, ra_ref, o_ref, ob_ref):
        tot = g_ref[...] + ra_ref[...]
        o_ref[...] = tot
        ob_ref[...] = tot.astype(BF16)

    blk = pl.BlockSpec((None, tr, c), lambda j, i, cr: (j, i, 0))
    gs = pltpu.PrefetchScalarGridSpec(
        num_scalar_prefetch=1, grid=(4, r // tr),
        in_specs=[pl.BlockSpec((None, None, tr, c), lambda j, i, cr: (cr[0], j, i, 0)), blk],
        out_specs=(blk, blk))
    return pl.pallas_call(body, out_shape=(jax.ShapeDtypeStruct((4, r, c), F32), jax.ShapeDtypeStruct((4, r, c), BF16)), grid_spec=gs,
                          compiler_params=_cparams(("parallel", "parallel")), name=name)(c_idx, g, ra)


def _add_chips(p, r3, k_idx, c_idx, *, name):
    _, r, c = p.shape
    tr = _tile(r, max(16, (1 << 18) // c // 16 * 16), 16)

    def body(k_ref, c_ref, p_ref, r_ref, o_ref):
        o_ref[...] = ((p_ref[...] + r_ref[0].astype(F32)) + r_ref[1].astype(F32)) + r_ref[2].astype(F32)

    gs = pltpu.PrefetchScalarGridSpec(
        num_scalar_prefetch=2, grid=(r // tr,),
        in_specs=[pl.BlockSpec((None, tr, c), lambda i, kr, cr: (kr[0], i, 0)), pl.BlockSpec((3, tr, c), lambda i, kr, cr: (0, i, 0))],
        out_specs=pl.BlockSpec((None, tr, c), lambda i, kr, cr: (cr[0], i, 0)))
    return pl.pallas_call(body, out_shape=jax.ShapeDtypeStruct((2, r, c), F32), grid_spec=gs,
                          compiler_params=_cparams(("parallel",)), name=name)(k_idx, c_idx, p, r3)


def _all_reduce_small(vec):
    r = vec.shape[0]

    def body(v_ref, gath_ref, sum_ref, send, recv):
        x, y, c = lax.axis_index("x"), lax.axis_index("y"), lax.axis_index("c")
        me = 4 * x + 2 * y + c
        gath_ref[me] = v_ref[...]
        cps = []
        for f in range(1, 8):
            fx, fy, fc = (f >> 2) & 1, (f >> 1) & 1, f & 1
            peer = (x ^ fx, y ^ fy, c ^ fc)
            cp = pltpu.make_async_remote_copy(src_ref=v_ref, dst_ref=gath_ref.at[me], send_sem=send.at[f - 1], recv_sem=recv.at[f - 1],
                                              device_id=peer, device_id_type=MESH)
            cp.start()
            cps.append(cp)
        for f in range(1, 8):
            fx, fy, fc = (f >> 2) & 1, (f >> 1) & 1, f & 1
            src = 4 * (x ^ fx) + 2 * (y ^ fy) + (c ^ fc)
            pltpu.make_async_remote_copy(src_ref=v_ref, dst_ref=gath_ref.at[src], send_sem=send.at[f - 1], recv_sem=recv.at[f - 1],
                                         device_id=(x ^ fx, y ^ fy, c ^ fc), device_id_type=MESH).wait_recv()
        for cp in cps:
            cp.wait_send()
        acc = gath_ref[0]
        for i in range(1, 8):
            acc = acc + gath_ref[i]
        sum_ref[...] = acc

    vm = pl.BlockSpec(memory_space=pltpu.VMEM)
    _, total = pl.pallas_call(
        body, out_shape=(jax.ShapeDtypeStruct((8, r, 128), F32), jax.ShapeDtypeStruct((r, 128), F32)),
        in_specs=[vm], out_specs=(vm, vm),
        scratch_shapes=[pltpu.SemaphoreType.DMA((7,)), pltpu.SemaphoreType.DMA((7,))],
        name="small_all_reduce")(vec)
    return total


def _full_weight(gw, name, layer):
    gathered, own, chip = gw[name]
    return jnp.concatenate([jnp.where(chip == k, own[layer], gathered[layer, k]) for k in range(4)], axis=SHARD_AXIS[name])


def _to_shards(full, name):
    return jnp.stack(jnp.split(full, 4, axis=SHARD_AXIS[name]), axis=0)


def _rope_tables(positions):
    inv = ROPE_THETA ** (-jnp.arange(0, QK_ROPE, 2, dtype=F32) / QK_ROPE)
    ang = positions.astype(F32)[:, None] * inv
    cos, sin = jnp.cos(ang), jnp.sin(ang)
    s = positions.shape[0]
    pad = jnp.zeros((s, HEAD_PAD - QK_HEAD), F32)
    ctab = jnp.concatenate([jnp.ones((s, QK_NOPE), F32), cos, cos, pad], axis=1)
    stab = jnp.concatenate([jnp.zeros((s, QK_NOPE), F32), -sin, sin, pad], axis=1)
    return ctab, stab


def _pad_gain(g):
    return jnp.concatenate([g, jnp.zeros((HEAD_PAD - QK_HEAD,), F32)])[None, :]


def _layer_weights(gw, rep, l, ql, kvl):
    d = rep["norm_g"].shape[1]
    w_in = _full_weight(gw, "w_in", l)
    o_kr = 2 * BRANCH_W + ql + kvl
    o_g = o_kr + QK_ROPE + 7 * BRANCH_W
    w = {}
    w["w1"] = jnp.concatenate([w_in[:, :o_kr], w_in[:, o_kr + QK_ROPE:o_g]], axis=1)
    w["wg"] = w_in[:, o_g:]
    w["wkr"] = jnp.concatenate([w_in[:, o_kr:o_kr + QK_ROPE], jnp.zeros((d, 128 - QK_ROPE), BF16)], axis=1)
    wuq = _full_weight(gw, "w_uq", l).reshape(ql, MLA_HEADS, QK_HEAD)
    w["w_uq"] = jnp.pad(wuq, ((0, 0), (0, 0), (0, HEAD_PAD - QK_HEAD))).reshape(ql, MLA_HEADS * HEAD_PAD)
    for nme in ("w_ukv", "pool_w", "conv_w", "w_mem_kv", "w_branch", "w_out"):
        w[nme] = _full_weight(gw, nme, l)
    for nme in ("norm_g", "gate_b", "pool_scale", "q_a_norm_g", "kv_a_norm_g", "mem_norm_g", "xattn_q_norm_g", "xattn_k_norm_g"):
        w[nme] = rep[nme][l][None, :]
    w["mla_q_norm_g"] = _pad_gain(rep["mla_q_norm_g"][l])
    w["mla_k_norm_g"] = _pad_gain(rep["mla_k_norm_g"][l])
    return w


def _forward_layer(x, mem, ctab, stab, w, tq, l):
    sfx = f"_l{l}"
    h, ht = _norm_fwd(x, w["norm_g"], name="norm_fwd" + sfx)
    p1 = _mm(h, w["w1"], name="proj_main" + sfx)
    gp = _mm(h, w["wg"], name="proj_gates" + sfx)
    kr = _mm(h, w["wkr"], name="proj_krope" + sfx)
    y_pool, yt_pool = _pool_fwd(p1, w["pool_w"], w["pool_scale"])
    qf, kf, vv, vt = _mla_prep_fwd(p1, kr, ctab, stab, w["q_a_norm_g"], w["kv_a_norm_g"], w["w_uq"], w["w_ukv"],
                               w["mla_q_norm_g"], w["mla_k_norm_g"])
    y_mla, yt_mla, o_att, lse, lset = _attn_fwd(qf, kf, vt, p1, tq=tq)
    y_conv, yt_conv = _conv_fwd(p1, w["conv_w"])
    memn, memnt = _norm_fwd(mem, w["mem_norm_g"], name="mem_norm" + sfx)
    mem_kv = _mm(memn, w["w_mem_kv"], name="mem_kv" + sfx)
    kx, vx = _memkv_prep(mem_kv, w["xattn_k_norm_g"])
    y_mem, yt_mem = _xattn_fwd(p1, kx, vx, w["xattn_q_norm_g"])
    ystack = jnp.stack([y_pool, y_mla, y_conv, y_mem])
    ytstack = jnp.stack([yt_pool, yt_mla, yt_conv, yt_mem])
    merged, mergedt = _gate_fwd(ystack, w["w_branch"], gp, w["gate_b"])
    x_out = _mm(merged, w["w_out"], add=x, name="out_proj" + sfx)
    saved = dict(x=x, ht=ht, p1=p1, gp=gp, kr=kr, qf=qf, kf=kf, vv=vv, o_att=o_att, lse=lse, lset=lset, memnt=memnt,
                 mem_kv=mem_kv, kx=kx, vx=vx, ystack=ystack, ytstack=ytstack, mergedt=mergedt)
    return x_out, saved


def _backward_layer(dx_out, sv, mem, ctab, stab, w, tq, l, ql, kvl):
    sfx = f"_l{l}"
    g = {}
    g["w_out"] = _mm(sv["mergedt"], dx_out, name="g_w_out" + sfx)
    dm = _mm(dx_out, w["w_out"], trans_b=True, name="d_merged" + sfx)
    dproj, dgp, g_gate_b = _gate_bwd(sv["ystack"], w["w_branch"], sv["gp"], w["gate_b"], dm)
    g["gate_b"] = g_gate_b[0]
    g["w_branch"] = _mm(sv["ytstack"], dproj, name="g_w_branch" + sfx)
    dy4 = _mm(dproj, w["w_branch"], trans_b=True, name="d_branches" + sfx)
    p1, kr = sv["p1"], sv["kr"]
    d_pool, g_pw, g_ps = _pool_bwd(p1, dy4, w["pool_w"], w["pool_scale"])
    g["pool_w"], g["pool_scale"] = g_pw, g_ps[0]
    do, d_mz, delta, deltat = _attn_bwd_pre(dy4, sv["o_att"], p1)
    dqf = _attn_bwd_dq(sv["qf"], sv["kf"], sv["vv"], do, sv["lse"], delta, tq=tq)
    dkf, dvv = _attn_bwd_dkv(sv["qf"], sv["kf"], sv["vv"], do, sv["lset"], deltat, tq=tq)
    (d_c, d_kr, dq_raw, dkv_raw, cqnt, ckvnt, g_qa, g_kva, g_qg, g_kg) = _mla_prep_bwd(
        p1, kr, ctab, stab, w["q_a_norm_g"], w["kv_a_norm_g"], w["w_uq"], w["w_ukv"], w["mla_q_norm_g"], w["mla_k_norm_g"],
        dqf, dkf, dvv)
    g["q_a_norm_g"], g["kv_a_norm_g"] = g_qa[0], g_kva[0]
    g["mla_q_norm_g"], g["mla_k_norm_g"] = g_qg[0, :QK_HEAD], g_kg[0, :QK_HEAD]
    g_wuq = _mm(cqnt, dq_raw, name="g_w_uq" + sfx)
    g["w_uq"] = g_wuq.reshape(ql, MLA_HEADS, HEAD_PAD)[:, :, :QK_HEAD].reshape(ql, MLA_HEADS * QK_HEAD)
    g["w_ukv"] = _mm(ckvnt, dkv_raw, name="g_w_ukv" + sfx)
    d_conv, gc0, gc1, gc2 = _conv_bwd(p1, dy4, w["conv_w"])
    g["conv_w"] = jnp.concatenate([gc0, gc1, gc2], axis=0)
    d_x, dkx, dvx, g_xq = _xattn_bwd(p1, dy4, sv["kx"], sv["vx"], w["xattn_q_norm_g"])
    g["xattn_q_norm_g"] = g_xq[0]
    d_memkv, g_xk = _memkv_prep_bwd(sv["mem_kv"], w["xattn_k_norm_g"], dkx, dvx)
    g["xattn_k_norm_g"] = g_xk[0]
    g["w_mem_kv"] = _mm(sv["memnt"], d_memkv, name="g_w_mem_kv" + sfx)
    d_memn = _mm(d_memkv, w["w_mem_kv"], trans_b=True, name="d_memn" + sfx)
    _, g_mn = _norm_bwd(mem, w["mem_norm_g"], d_memn, d_memn, name="mem_norm_bwd" + sfx)
    g["mem_norm_g"] = g_mn[0]
    dp1 = jnp.concatenate([d_pool, d_c, d_mz, d_conv, d_x], axis=1)
    ht = sv["ht"]
    g_w1 = _mm(ht, dp1, name="g_w1" + sfx)
    g_wg = _mm(ht, dgp, name="g_wg" + sfx)
    g_wkr = _mm(ht, d_kr, name="g_wkr" + sfx)
    o_kr = 2 * BRANCH_W + ql + kvl
    g["w_in"] = jnp.concatenate([g_w1[:, :o_kr], g_wkr[:, :QK_ROPE], g_w1[:, o_kr:], g_wg], axis=1)
    dh = _mm(dp1, w["w1"], trans_b=True, name="dh_main" + sfx)
    dh = _mm(dgp, w["wg"], trans_b=True, add=dh, name="dh_gates" + sfx)
    dh = _mm(d_kr, w["wkr"], trans_b=True, add=dh, name="dh_krope" + sfx)
    dx, g_ng = _norm_bwd(sv["x"], w["norm_g"], dh, dx_out, name="norm_bwd" + sfx)
    g["norm_g"] = g_ng[0]
    return dx, g


def _as4(a):
    rest = a.shape[2:]
    r = 1
    for n in rest[:-1]:
        r *= n
    return a.reshape(2, 4, r, rest[-1])


def kernel(x, mem, positions, norm_g, w_in, gate_b, pool_w, pool_scale, q_a_norm_g, kv_a_norm_g, w_uq, w_ukv, mla_q_norm_g, mla_k_norm_g, conv_w, mem_norm_g, w_mem_kv, xattn_q_norm_g, xattn_k_norm_g, w_branch, w_out, loss_target, m_norm_g, m_w_in, m_gate_b, m_pool_w, m_pool_scale, m_q_a_norm_g, m_kv_a_norm_g, m_w_uq, m_w_ukv, m_mla_q_norm_g, m_mla_k_norm_g, m_conv_w, m_mem_norm_g, m_w_mem_kv, m_xattn_q_norm_g, m_xattn_k_norm_g, m_w_branch, m_w_out, v_norm_g, v_w_in, v_gate_b, v_pool_w, v_pool_scale, v_q_a_norm_g, v_kv_a_norm_g, v_w_uq, v_w_ukv, v_mla_q_norm_g, v_mla_k_norm_g, v_conv_w, v_mem_norm_g, v_w_mem_kv, v_xattn_q_norm_g, v_xattn_k_norm_g, v_w_branch, v_w_out):
    wts = dict(norm_g=norm_g, w_in=w_in, gate_b=gate_b, pool_w=pool_w, pool_scale=pool_scale, q_a_norm_g=q_a_norm_g,
               kv_a_norm_g=kv_a_norm_g, w_uq=w_uq, w_ukv=w_ukv, mla_q_norm_g=mla_q_norm_g, mla_k_norm_g=mla_k_norm_g,
               conv_w=conv_w, mem_norm_g=mem_norm_g, w_mem_kv=w_mem_kv, xattn_q_norm_g=xattn_q_norm_g,
               xattn_k_norm_g=xattn_k_norm_g, w_branch=w_branch, w_out=w_out)
    mom = dict(norm_g=m_norm_g, w_in=m_w_in, gate_b=m_gate_b, pool_w=m_pool_w, pool_scale=m_pool_scale, q_a_norm_g=m_q_a_norm_g,
               kv_a_norm_g=m_kv_a_norm_g, w_uq=m_w_uq, w_ukv=m_w_ukv, mla_q_norm_g=m_mla_q_norm_g, mla_k_norm_g=m_mla_k_norm_g,
               conv_w=m_conv_w, mem_norm_g=m_mem_norm_g, w_mem_kv=m_w_mem_kv, xattn_q_norm_g=m_xattn_q_norm_g,
               xattn_k_norm_g=m_xattn_k_norm_g, w_branch=m_w_branch, w_out=m_w_out)
    vel = dict(norm_g=v_norm_g, w_in=v_w_in, gate_b=v_gate_b, pool_w=v_pool_w, pool_scale=v_pool_scale, q_a_norm_g=v_q_a_norm_g,
               kv_a_norm_g=v_kv_a_norm_g, w_uq=v_w_uq, w_ukv=v_w_ukv, mla_q_norm_g=v_mla_q_norm_g, mla_k_norm_g=v_mla_k_norm_g,
               conv_w=v_conv_w, mem_norm_g=v_mem_norm_g, w_mem_kv=v_w_mem_kv, xattn_q_norm_g=v_xattn_q_norm_g,
               xattn_k_norm_g=v_xattn_k_norm_g, w_branch=v_w_branch, w_out=v_w_out)
    depth = norm_g.shape[0]
    assert depth == 2 and x.shape[0] == 1
    xs, mems, tgt = x[0], mem[0], loss_target[0]
    s = xs.shape[0]
    ql, kvl = q_a_norm_g.shape[1], kv_a_norm_g.shape[1]
    tq = _tile(s, 512, 128)
    ctab, stab = _rope_tables(positions[0])

    send = [wts[n].astype(F32 if n == "conv_w" else BF16) for n in SHARDED]
    chip = 2 * lax.axis_index("x") + lax.axis_index("y")
    gathered = {n: (g, own, chip) for n, g, own in zip(SHARDED, _all_gather(send), send)}
    rep = {n: wts[n] for n in REPLICATED}
    lw = [_layer_weights(gathered, rep, l, ql, kvl) for l in range(depth)]

    act, saved = xs, []
    for l in range(depth):
        act, sv = _forward_layer(act, mems, ctab, stab, lw[l], tq, l)
        saved.append(sv)
    dy, loss_part = _loss_head(act, tgt)

    grads = [None] * depth
    dxl = dy
    for l in reversed(range(depth)):
        dxl, grads[l] = _backward_layer(dxl, saved[l], mems, ctab, stab, lw[l], tq, l, ql, kvl)
    grad_x = dxl[None]

    c_idx = lax.axis_index("c").astype(jnp.int32).reshape(1)
    g_full = [_as4(jnp.stack([_to_shards(grads[l][n], n) for l in range(depth)], axis=0)) for n in SHARDED]
    from_sib = _rs_exchange_cores(g_full)
    parts = [_add_core_halves(g, r, c_idx, name=f"add_cores_{n}") for g, r, n in zip(g_full, from_sib, SHARDED)]
    from_chips = _rs_exchange_chips([pb for _, pb in parts])
    k_idx = chip.astype(jnp.int32).reshape(1)
    halves = [_add_chips(p, r3, k_idx, c_idx, name=f"add_chips_{n}") for (p, _), r3, n in zip(parts, from_chips, SHARDED)]
    reduced = _rs_share_cores(halves)
    gsum = {n: r.reshape(wts[n].shape) for n, r in zip(SHARDED, reduced)}

    flat = [jnp.stack([grads[l][n] for l in range(depth)], axis=0).reshape(-1) for n in REPLICATED]
    sizes = [f.shape[0] for f in flat]
    total = sum(sizes) + 1
    rows = -(-total // 1024) * 8
    vec = jnp.concatenate(flat + [loss_part[0, :1], jnp.zeros((rows * 128 - total,), F32)]).reshape(rows, 128)
    red = _all_reduce_small(vec).reshape(-1)
    off = 0
    for n, sz in zip(REPLICATED, sizes):
        gsum[n] = red[off:off + sz].reshape(wts[n].shape)
        off += sz
    loss = red[off]

    delta, new_m, new_v = {}, {}, {}
    for n in WEIGHTS:
        delta[n], new_m[n], new_v[n] = _adamw(wts[n], gsum[n], mom[n], vel[n], name=f"adamw_{n}")
    return (loss, grad_x, *[gsum[n] for n in WEIGHTS], *[delta[n] for n in WEIGHTS],
            *[new_m[n] for n in WEIGHTS], *[new_v[n] for n in WEIGHTS])
```

```python
import functools

import jax
import jax.numpy as jnp
from jax import lax
from jax.experimental import pallas as pl
from jax.experimental.pallas import tpu as pltpu

F32 = jnp.float32
BF16 = jnp.bfloat16
MESH = pl.DeviceIdType.MESH

EPS = 1e-6
N_BRANCH = 4
BRANCH_W = 1024
POOL_GROUPS = 4
POOL_GW = BRANCH_W // POOL_GROUPS
POOL_HALO = 16
CONV_HALO = 8
MLA_HEADS = 8
QK_NOPE = 128
QK_ROPE = 64
QK_HEAD = QK_NOPE + QK_ROPE
HEAD_PAD = 256
V_HEAD = 128
ROPE_THETA = 10000.0
XATTN_HEADS = 4
XATTN_HEAD_DIM = BRANCH_W // XATTN_HEADS
ADAM_LR, ADAM_B1, ADAM_B2, ADAM_EPS, ADAM_WD, ADAM_STEP = 0.001, 0.9, 0.999, 1e-08, 0.01, 10
NEG = -1e30
VMEM_LIMIT = 48 * 1024 * 1024

SHARDED = ("w_in", "pool_w", "w_uq", "w_ukv", "conv_w", "w_mem_kv", "w_branch", "w_out")
REPLICATED = ("norm_g", "gate_b", "pool_scale", "q_a_norm_g", "kv_a_norm_g", "mla_q_norm_g", "mla_k_norm_g",
              "mem_norm_g", "xattn_q_norm_g", "xattn_k_norm_g")
WEIGHTS = ("norm_g", "w_in", "gate_b", "pool_w", "pool_scale", "q_a_norm_g", "kv_a_norm_g", "w_uq", "w_ukv",
           "mla_q_norm_g", "mla_k_norm_g", "conv_w", "mem_norm_g", "w_mem_kv", "xattn_q_norm_g", "xattn_k_norm_g",
           "w_branch", "w_out")
SHARD_AXIS = {"w_in": 1, "pool_w": 1, "w_uq": 1, "w_ukv": 1, "conv_w": 1, "w_mem_kv": 0, "w_branch": 2, "w_out": 0}


def _cparams(sem=None):
    return pltpu.CompilerParams(dimension_semantics=sem, vmem_limit_bytes=VMEM_LIMIT)


def _tile(n, pref, unit=128):
    if n <= pref:
        return n
    t = (pref // unit) * unit
    while t >= unit:
        if n % t == 0:
            return t
        t -= unit
    return n


def _silu(z):
    return z * jax.nn.sigmoid(z)


def _dsilu(z):
    s = jax.nn.sigmoid(z)
    return s * (1.0 + z * (1.0 - s))


def _dot(a, b):
    return jnp.dot(a, b, preferred_element_type=F32)


def _dot_nt(a, b):
    return lax.dot_general(a, b, (((1,), (1,)), ((), ())), preferred_element_type=F32)


def _dot_tn(a, b):
    return lax.dot_general(a, b, (((0,), (0,)), ((), ())), preferred_element_type=F32)


def _rms(x, g, n):
    r = lax.rsqrt(jnp.sum(x * x, axis=-1, keepdims=True) * (1.0 / n) + EPS)
    return x * r * g


def _rms_bwd(x, g, n, dout):
    r = lax.rsqrt(jnp.sum(x * x, axis=-1, keepdims=True) * (1.0 / n) + EPS)
    y = x * r
    dy = dout * g
    dx = r * (dy - y * (jnp.sum(dy * y, axis=-1, keepdims=True) * (1.0 / n)))
    return dx, dout * y


def _rope(x, ctab, stab):
    lane = lax.broadcasted_iota(jnp.int32, x.shape, 1)
    partner = jnp.where(lane < QK_NOPE + QK_ROPE // 2, pltpu.roll(x, HEAD_PAD - QK_ROPE // 2, 1),
                        pltpu.roll(x, QK_ROPE // 2, 1))
    return x * ctab + partner * stab


def _rope_bwd(d, ctab, stab):
    lane = lax.broadcasted_iota(jnp.int32, d.shape, 1)
    ds = d * stab
    partner = jnp.where(lane < QK_NOPE + QK_ROPE // 2, pltpu.roll(ds, HEAD_PAD - QK_ROPE // 2, 1),
                        pltpu.roll(ds, QK_ROPE // 2, 1))
    return d * ctab + jnp.where((lane >= QK_NOPE) & (lane < QK_HEAD), partner, 0.0)


def _mm(a, b, *, name, trans_b=False, add=None, out_dtype=F32, tm=512, tn=1024, tk=2048):
    batched = a.ndim == 3
    if batched:
        nb, m, k = a.shape
    else:
        m, k = a.shape
    n = b.shape[-2] if trans_b else b.shape[-1]
    tm, tn, tk = _tile(m, tm, 8), _tile(n, tn), _tile(k, tk)
    nk = k // tk

    def body(*refs):
        if add is None:
            a_ref, b_ref, o_ref = refs[:3]
            add_ref = None
            rest = refs[3:]
        else:
            a_ref, b_ref, add_ref, o_ref = refs[:4]
            rest = refs[4:]
        av = a_ref[...].astype(BF16)
        bv = b_ref[...].astype(BF16)
        part = _dot_nt(av, bv) if trans_b else _dot(av, bv)

        def finish(acc):
            if add_ref is not None:
                acc = acc + add_ref[...]
            o_ref[...] = acc.astype(o_ref.dtype)

        if nk == 1:
            finish(part)
        else:
            acc_ref = rest[0]
            kk = pl.program_id(3 if batched else 2)

            @pl.when(kk == 0)
            def _():
                acc_ref[...] = part

            @pl.when(kk > 0)
            def _():
                acc_ref[...] += part

            @pl.when(kk == nk - 1)
            def _():
                finish(acc_ref[...])

    if batched:
        a_spec = pl.BlockSpec((None, tm, tk), lambda bb, i, j, kk: (bb, i, kk))
        b_spec = (pl.BlockSpec((None, tn, tk), lambda bb, i, j, kk: (bb, j, kk)) if trans_b
                  else pl.BlockSpec((None, tk, tn), lambda bb, i, j, kk: (bb, kk, j)))
        o_spec = pl.BlockSpec((None, tm, tn), lambda bb, i, j, kk: (bb, i, j))
        grid = (nb, m // tm, n // tn, nk)
        out_shape = jax.ShapeDtypeStruct((nb, m, n), out_dtype)
        sem = ("parallel", "parallel", "parallel", "arbitrary")
    else:
        a_spec = pl.BlockSpec((tm, tk), lambda i, j, kk: (i, kk))
        b_spec = (pl.BlockSpec((tn, tk), lambda i, j, kk: (j, kk)) if trans_b
                  else pl.BlockSpec((tk, tn), lambda i, j, kk: (kk, j)))
        o_spec = pl.BlockSpec((tm, tn), lambda i, j, kk: (i, j))
        grid = (m // tm, n // tn, nk)
        out_shape = jax.ShapeDtypeStruct((m, n), out_dtype)
        sem = ("parallel", "parallel", "arbitrary")
    in_specs = [a_spec, b_spec] + ([o_spec] if add is not None else [])
    args = (a, b) + ((add,) if add is not None else ())
    scratch = [pltpu.VMEM((tm, tn), F32)] if nk > 1 else []
    return pl.pallas_call(body, out_shape=out_shape, grid=grid, in_specs=in_specs, out_specs=o_spec,
                          scratch_shapes=scratch, compiler_params=_cparams(sem), name=name)(*args)


def _norm_fwd(x, g, *, name, t=256):
    s, d = x.shape
    t = _tile(s, t, 128)

    def body(x_ref, g_ref, h_ref, ht_ref):
        h = _rms(x_ref[...], g_ref[...], d)
        h_ref[...] = h.astype(BF16)
        ht_ref[...] = h.T.astype(BF16)

    return pl.pallas_call(
        body, out_shape=(jax.ShapeDtypeStruct((s, d), BF16), jax.ShapeDtypeStruct((d, s), BF16)),
        grid=(s // t,),
        in_specs=[pl.BlockSpec((t, d), lambda i: (i, 0)), pl.BlockSpec((1, d), lambda i: (0, 0))],
        out_specs=(pl.BlockSpec((t, d), lambda i: (i, 0)), pl.BlockSpec((d, t), lambda i: (0, i))),
        compiler_params=_cparams(("parallel",)), name=name)(x, g)


def _norm_bwd(x, g, dh, dres, *, name, t=256):
    s, d = x.shape
    t = _tile(s, t, 8)

    def body(x_ref, g_ref, dh_ref, dres_ref, dx_ref, dg_ref):
        dx, dgt = _rms_bwd(x_ref[...], g_ref[...], d, dh_ref[...])
        dx_ref[...] = dx + dres_ref[...]
        part = jnp.sum(dgt, axis=0, keepdims=True)

        @pl.when(pl.program_id(0) == 0)
        def _():
            dg_ref[...] = part

        @pl.when(pl.program_id(0) > 0)
        def _():
            dg_ref[...] += part

    row = pl.BlockSpec((t, d), lambda i: (i, 0))
    vec = pl.BlockSpec((1, d), lambda i: (0, 0))
    return pl.pallas_call(
        body, out_shape=(jax.ShapeDtypeStruct((s, d), F32), jax.ShapeDtypeStruct((1, d), F32)),
        grid=(s // t,), in_specs=[row, vec, row, row], out_specs=(row, vec),
        compiler_params=_cparams(("arbitrary",)), name=name)(x, g, dh, dres)


def _loss_head(y, tgt, *, t=256):
    s, d = y.shape
    t = _tile(s, t, 8)

    def body(y_ref, t_ref, dy_ref, l_ref):
        e = y_ref[...] - t_ref[...]
        dy_ref[...] = e * (1.0 / d)
        part = jnp.zeros((1, 128), F32) + jnp.sum(e * e) * (0.5 / d)

        @pl.when(pl.program_id(0) == 0)
        def _():
            l_ref[...] = part

        @pl.when(pl.program_id(0) > 0)
        def _():
            l_ref[...] += part

    row = pl.BlockSpec((t, d), lambda i: (i, 0))
    return pl.pallas_call(
        body, out_shape=(jax.ShapeDtypeStruct((s, d), F32), jax.ShapeDtypeStruct((1, 128), F32)),
        grid=(s // t,), in_specs=[row, row], out_specs=(row, pl.BlockSpec((1, 128), lambda i: (0, 0))),
        compiler_params=_cparams(("arbitrary",)), name="loss_head")(y, tgt)


def _pool_mixed(scr, v, halo, first, row0, t):
    scr[0:POOL_HALO, :] = jnp.where(first, 0.0, halo)
    scr[POOL_HALO:POOL_HALO + t, :] = v
    row = row0 + lax.broadcasted_iota(jnp.int32, (t, 1), 0)
    mixed = []
    for g in range(POOL_GROUPS):
        w = 2 ** (g + 1)
        acc = scr[:, g * POOL_GW:(g + 1) * POOL_GW]
        sh = 1
        while sh < w:
            acc = acc + pltpu.roll(acc, sh, 0)
            sh *= 2
        cnt = jnp.minimum(row + 1, w).astype(F32)
        mixed.append(acc[POOL_HALO:POOL_HALO + t, :] / cnt - v[:, g * POOL_GW:(g + 1) * POOL_GW])
    return mixed


def _pool_fwd(p1, pool_w, pool_scale, *, t=256):
    s = p1.shape[0]
    t = _tile(s, t, 128)
    hb = t // POOL_HALO

    def body(pv_ref, halo_ref, pz_ref, pw_ref, sc_ref, y_ref, yt_ref, scr):
        i = pl.program_id(0)
        mixed = _pool_mixed(scr, pv_ref[...], halo_ref[...], i == 0, i * t, t)
        outs = [_dot(mixed[g].astype(BF16), pw_ref[g]) for g in range(POOL_GROUPS)]
        y = jnp.concatenate(outs, axis=1) * sc_ref[...] * _silu(pz_ref[...])
        y_ref[...] = y.astype(BF16)
        yt_ref[...] = y.T.astype(BF16)

    return pl.pallas_call(
        body, out_shape=(jax.ShapeDtypeStruct((s, BRANCH_W), BF16), jax.ShapeDtypeStruct((BRANCH_W, s), BF16)),
        grid=(s // t,),
        in_specs=[pl.BlockSpec((t, BRANCH_W), lambda i: (i, 0)),
                  pl.BlockSpec((POOL_HALO, BRANCH_W), lambda i: (jnp.maximum(i * hb - 1, 0), 0)),
                  pl.BlockSpec((t, BRANCH_W), lambda i: (i, 1)),
                  pl.BlockSpec((POOL_GROUPS, POOL_GW, POOL_GW), lambda i: (0, 0, 0)),
                  pl.BlockSpec((1, BRANCH_W), lambda i: (0, 0))],
        out_specs=(pl.BlockSpec((t, BRANCH_W), lambda i: (i, 0)), pl.BlockSpec((BRANCH_W, t), lambda i: (0, i))),
        scratch_shapes=[pltpu.VMEM((t + POOL_HALO, BRANCH_W), F32)],
        compiler_params=_cparams(("parallel",)), name="pool_fwd")(p1, p1, p1, pool_w, pool_scale)


def _pool_bwd(p1, dy, pool_w, pool_scale, *, t=256):
    s = p1.shape[0]
    t = _tile(s, t, 128)
    hb = t // POOL_HALO
    nt = s // t
    last_hb = s // POOL_HALO - 1

    def body(pv_ref, halo_ref, pz_ref, pzn_ref, dy_ref, dyn_ref, pw_ref, sc_ref, d_ref, gw_ref, gs_ref, scr, scr2, scr3):
        i = pl.program_id(0)
        mixed = _pool_mixed(scr, pv_ref[...], halo_ref[...], i == 0, i * t, t)
        scale = sc_ref[...]
        pz = pz_ref[...]
        dy = dy_ref[...]
        raw = jnp.concatenate([_dot(mixed[g].astype(BF16), pw_ref[g]) for g in range(POOL_GROUPS)], axis=1)
        d_pool = dy * _silu(pz)
        d_ref[:, BRANCH_W:2 * BRANCH_W] = (dy * raw * scale * _dsilu(pz)).astype(BF16)
        gs_part = jnp.sum(d_pool * raw, axis=0, keepdims=True)
        scr2[0:t, :] = d_pool * scale
        scr2[t:t + POOL_HALO, :] = jnp.where(i == nt - 1, 0.0, dyn_ref[...] * _silu(pzn_ref[...]) * scale)
        row = i * t + lax.broadcasted_iota(jnp.int32, (t + POOL_HALO, 1), 0)
        gw_parts = []
        for g in range(POOL_GROUPS):
            w = 2 ** (g + 1)
            sl = slice(g * POOL_GW, (g + 1) * POOL_GW)
            do_g = scr2[:, sl].astype(BF16)
            dm = _dot_nt(do_g, pw_ref[g])
            gw_parts.append(_dot_tn(mixed[g].astype(BF16), do_g[0:t, :]))
            cnt = jnp.minimum(row + 1, w).astype(F32)
            acc = dm / cnt
            sh = 1
            while sh < w:
                acc = acc + pltpu.roll(acc, t + POOL_HALO - sh, 0)
                sh *= 2
            scr3[:, sl] = acc - dm
        d_ref[:, 0:BRANCH_W] = scr3[0:t, :].astype(BF16)

        @pl.when(i == 0)
        def _():
            for g in range(POOL_GROUPS):
                gw_ref[g] = gw_parts[g]
            gs_ref[...] = gs_part

        @pl.when(i > 0)
        def _():
            for g in range(POOL_GROUPS):
                gw_ref[g] += gw_parts[g]
            gs_ref[...] += gs_part

    tile = lambda col: pl.BlockSpec((t, BRANCH_W), lambda i: (i, col))
    nxt = lambda col: pl.BlockSpec((POOL_HALO, BRANCH_W), lambda i: (jnp.minimum((i + 1) * hb, last_hb), col))
    return pl.pallas_call(
        body,
        out_shape=(jax.ShapeDtypeStruct((s, 2 * BRANCH_W), BF16),
                   jax.ShapeDtypeStruct((POOL_GROUPS, POOL_GW, POOL_GW), F32),
                   jax.ShapeDtypeStruct((1, BRANCH_W), F32)),
        grid=(nt,),
        in_specs=[tile(0), pl.BlockSpec((POOL_HALO, BRANCH_W), lambda i: (jnp.maximum(i * hb - 1, 0), 0)),
                  tile(1), nxt(1),
                  pl.BlockSpec((None, t, BRANCH_W), lambda i: (0, i, 0)),
                  pl.BlockSpec((None, POOL_HALO, BRANCH_W), lambda i: (0, jnp.minimum((i + 1) * hb, last_hb), 0)),
                  pl.BlockSpec((POOL_GROUPS, POOL_GW, POOL_GW), lambda i: (0, 0, 0)),
                  pl.BlockSpec((1, BRANCH_W), lambda i: (0, 0))],
        out_specs=(pl.BlockSpec((t, 2 * BRANCH_W), lambda i: (i, 0)),
                   pl.BlockSpec((POOL_GROUPS, POOL_GW, POOL_GW), lambda i: (0, 0, 0)),
                   pl.BlockSpec((1, BRANCH_W), lambda i: (0, 0))),
        scratch_shapes=[pltpu.VMEM((t + POOL_HALO, BRANCH_W), F32)] * 3,
        compiler_params=_cparams(("arbitrary",)), name="pool_bwd")(p1, p1, p1, p1, dy, dy, pool_w, pool_scale)


CONV_COL = 4


def _conv_taps(scr, u, uh, first, t):
    scr[0:CONV_HALO, :] = jnp.where(first, 0.0, uh)
    scr[CONV_HALO:CONV_HALO + t, :] = u
    e = scr[...]
    u1 = pltpu.roll(e, 1, 0)[CONV_HALO:CONV_HALO + t, :]
    u2 = pltpu.roll(e, 2, 0)[CONV_HALO:CONV_HALO + t, :]
    return u2, u1, u


def _conv_fwd(p1, conv_w, *, t=256):
    s = p1.shape[0]
    t = _tile(s, t, 128)
    hb = t // CONV_HALO

    def body(cb_ref, cc_ref, cx_ref, cz_ref, cch_ref, cxh_ref, w_ref, y_ref, yt_ref, scr):
        i = pl.program_id(0)
        u0, u1, u2 = _conv_taps(scr, cc_ref[...] * cx_ref[...], cch_ref[...] * cxh_ref[...], i == 0, t)
        w = w_ref[...]
        y = (w[0:1, :] * u0 + w[1:2, :] * u1 + w[2:3, :] * u2) * cb_ref[...] * _silu(cz_ref[...])
        y_ref[...] = y.astype(BF16)
        yt_ref[...] = y.T.astype(BF16)

    tile = lambda col: pl.BlockSpec((t, BRANCH_W), lambda i: (i, CONV_COL + col))
    prev = lambda col: pl.BlockSpec((CONV_HALO, BRANCH_W), lambda i: (jnp.maximum(i * hb - 1, 0), CONV_COL + col))
    return pl.pallas_call(
        body, out_shape=(jax.ShapeDtypeStruct((s, BRANCH_W), BF16), jax.ShapeDtypeStruct((BRANCH_W, s), BF16)),
        grid=(s // t,),
        in_specs=[tile(0), tile(1), tile(2), tile(3), prev(1), prev(2), pl.BlockSpec((3, BRANCH_W), lambda i: (0, 0))],
        out_specs=(pl.BlockSpec((t, BRANCH_W), lambda i: (i, 0)), pl.BlockSpec((BRANCH_W, t), lambda i: (0, i))),
        scratch_shapes=[pltpu.VMEM((t + CONV_HALO, BRANCH_W), F32)],
        compiler_params=_cparams(("parallel",)), name="conv_fwd")(p1, p1, p1, p1, p1, p1, conv_w)


def _conv_bwd(p1, dy, conv_w, *, t=256):
    s = p1.shape[0]
    t = _tile(s, t, 128)
    hb = t // CONV_HALO
    nt = s // t
    last_hb = s // CONV_HALO - 1

    def body(cb_ref, cc_ref, cx_ref, cz_ref, cch_ref, cxh_ref, cbn_ref, czn_ref, dy_ref, dyn_ref, w_ref,
             d_ref, g0_ref, g1_ref, g2_ref, scr, scr2):
        i = pl.program_id(0)
        cb, cc, cx, cz = cb_ref[...], cc_ref[...], cx_ref[...], cz_ref[...]
        u0, u1, u2 = _conv_taps(scr, cc * cx, cch_ref[...] * cxh_ref[...], i == 0, t)
        w = w_ref[...]
        w0, w1, w2 = w[0:1, :], w[1:2, :], w[2:3, :]
        y = w0 * u0 + w1 * u1 + w2 * u2
        dy = dy_ref[...]
        sz = _silu(cz)
        d_ref[:, 0:BRANCH_W] = (dy * sz * y).astype(BF16)
        d_ref[:, 3 * BRANCH_W:4 * BRANCH_W] = (dy * cb * y * _dsilu(cz)).astype(BF16)
        d_y = dy * sz * cb
        parts = [jnp.sum(d_y * u, axis=0, keepdims=True) for u in (u0, u1, u2)]
        scr2[0:t, :] = d_y
        scr2[t:t + CONV_HALO, :] = jnp.where(i == nt - 1, 0.0, dyn_ref[...] * _silu(czn_ref[...]) * cbn_ref[...])
        e = scr2[...]
        n = t + CONV_HALO
        du = (w2 * e + w1 * pltpu.roll(e, n - 1, 0) + w0 * pltpu.roll(e, n - 2, 0))[0:t, :]
        d_ref[:, BRANCH_W:2 * BRANCH_W] = (du * cx).astype(BF16)
        d_ref[:, 2 * BRANCH_W:3 * BRANCH_W] = (du * cc).astype(BF16)

        @pl.when(i == 0)
        def _():
            g0_ref[...] = parts[0]
            g1_ref[...] = parts[1]
            g2_ref[...] = parts[2]

        @pl.when(i > 0)
        def _():
            g0_ref[...] += parts[0]
            g1_ref[...] += parts[1]
            g2_ref[...] += parts[2]

    tile = lambda col: pl.BlockSpec((t, BRANCH_W), lambda i: (i, CONV_COL + col))
    prev = lambda col: pl.BlockSpec((CONV_HALO, BRANCH_W), lambda i: (jnp.maximum(i * hb - 1, 0), CONV_COL + col))
    nxt = lambda col: pl.BlockSpec((CONV_HALO, BRANCH_W), lambda i: (jnp.minimum((i + 1) * hb, last_hb), CONV_COL + col))
    vec = pl.BlockSpec((1, BRANCH_W), lambda i: (0, 0))
    gshape = jax.ShapeDtypeStruct((1, BRANCH_W), F32)
    return pl.pallas_call(
        body, out_shape=(jax.ShapeDtypeStruct((s, 4 * BRANCH_W), BF16), gshape, gshape, gshape),
        grid=(nt,),
        in_specs=[tile(0), tile(1), tile(2), tile(3), prev(1), prev(2), nxt(0), nxt(3),
                  pl.BlockSpec((None, t, BRANCH_W), lambda i: (2, i, 0)),
                  pl.BlockSpec((None, CONV_HALO, BRANCH_W), lambda i: (2, jnp.minimum((i + 1) * hb, last_hb), 0)),
                  pl.BlockSpec((3, BRANCH_W), lambda i: (0, 0))],
        out_specs=(pl.BlockSpec((t, 4 * BRANCH_W), lambda i: (i, 0)), vec, vec, vec),
        scratch_shapes=[pltpu.VMEM((t + CONV_HALO, BRANCH_W), F32)] * 2,
        compiler_params=_cparams(("arbitrary",)), name="conv_bwd")(p1, p1, p1, p1, p1, p1, p1, p1, dy, dy, conv_w)


def _mla_prep_fwd(p1, kr, ctab, stab, qa_g, kva_g, w_uq, w_ukv, q_g, k_g, *, t=256):
    s = p1.shape[0]
    t = _tile(s, t, 128)
    ql, kvl = qa_g.shape[1], kva_g.shape[1]
    assert ql == kvl and 2048 % ql == 0
    cq_blk = 2048 // ql

    def body(cq_ref, ckv_ref, kr_ref, c_ref, s_ref, qag_ref, kvag_ref, wuq_ref, wukv_ref, qg_ref, kg_ref,
             qf_ref, kf_ref, v_ref, vt_ref):
        q_raw = _dot(_rms(cq_ref[...], qag_ref[...], ql).astype(BF16), wuq_ref[...])
        kv_raw = _dot(_rms(ckv_ref[...], kvag_ref[...], kvl).astype(BF16), wukv_ref[...])
        krp = kr_ref[...]
        ct, st = c_ref[...], s_ref[...]
        for h in range(MLA_HEADS):
            qh = q_raw[:, h * HEAD_PAD:(h + 1) * HEAD_PAD]
            qf_ref[h] = _rope(_rms(qh, qg_ref[...], QK_HEAD), ct, st).astype(BF16)
            kh = jnp.concatenate([kv_raw[:, h * HEAD_PAD:h * HEAD_PAD + QK_NOPE], krp], axis=1)
            kf_ref[h] = _rope(_rms(kh, kg_ref[...], QK_HEAD), ct, st).astype(BF16)
            vh = kv_raw[:, h * HEAD_PAD + QK_NOPE:(h + 1) * HEAD_PAD]
            v_ref[h] = vh.astype(BF16)
            vt_ref[h] = vh.T.astype(BF16)

    full = lambda shp: pl.BlockSpec(shp, lambda i: tuple(0 for _ in shp))
    return pl.pallas_call(
        body,
        out_shape=(jax.ShapeDtypeStruct((MLA_HEADS, s, HEAD_PAD), BF16), jax.ShapeDtypeStruct((MLA_HEADS, s, HEAD_PAD), BF16),
                   jax.ShapeDtypeStruct((MLA_HEADS, s, V_HEAD), BF16), jax.ShapeDtypeStruct((MLA_HEADS, V_HEAD, s), BF16)),
        grid=(s // t,),
        in_specs=[pl.BlockSpec((t, ql), lambda i: (i, cq_blk)), pl.BlockSpec((t, kvl), lambda i: (i, cq_blk + 1)),
                  pl.BlockSpec((t, 128), lambda i: (i, 0)),
                  pl.BlockSpec((t, HEAD_PAD), lambda i: (i, 0)), pl.BlockSpec((t, HEAD_PAD), lambda i: (i, 0)),
                  full((1, ql)), full((1, kvl)), full(w_uq.shape), full(w_ukv.shape), full((1, HEAD_PAD)), full((1, HEAD_PAD))],
        out_specs=(pl.BlockSpec((MLA_HEADS, t, HEAD_PAD), lambda i: (0, i, 0)),
                   pl.BlockSpec((MLA_HEADS, t, HEAD_PAD), lambda i: (0, i, 0)),
                   pl.BlockSpec((MLA_HEADS, t, V_HEAD), lambda i: (0, i, 0)),
                   pl.BlockSpec((MLA_HEADS, V_HEAD, t), lambda i: (0, 0, i))),
        compiler_params=_cparams(("parallel",)), name="mla_prep_fwd")(
            p1, p1, kr, ctab, stab, qa_g, kva_g, w_uq, w_ukv, q_g, k_g)


def _mla_prep_bwd(p1, kr, ctab, stab, qa_g, kva_g, w_uq, w_ukv, q_g, k_g, dqf, dkf, dv, *, t=256):
    s = p1.shape[0]
    t = _tile(s, t, 128)
    ql, kvl = qa_g.shape[1], kva_g.shape[1]
    cq_blk = 2048 // ql
    nq = MLA_HEADS * HEAD_PAD

    def body(cq_ref, ckv_ref, kr_ref, c_ref, s_ref, qag_ref, kvag_ref, wuq_ref, wukv_ref, qg_ref, kg_ref,
             dqf_ref, dkf_ref, dv_ref,
             dc_ref, dkr_ref, dqraw_ref, dkvraw_ref, cqnt_ref, ckvnt_ref, gqa_ref, gkva_ref, gqg_ref, gkg_ref):
        i = pl.program_id(0)
        cq, ckv = cq_ref[...], ckv_ref[...]
        cqn = _rms(cq, qag_ref[...], ql)
        ckvn = _rms(ckv, kvag_ref[...], kvl)
        cqnt_ref[...] = cqn.T.astype(BF16)
        ckvnt_ref[...] = ckvn.T.astype(BF16)
        q_raw = _dot(cqn.astype(BF16), wuq_ref[...])
        kv_raw = _dot(ckvn.astype(BF16), wukv_ref[...])
        krp = kr_ref[...]
        ct, st = c_ref[...], s_ref[...]
        gqg = jnp.zeros((1, HEAD_PAD), F32)
        gkg = jnp.zeros((1, HEAD_PAD), F32)
        dkr = jnp.zeros((t, HEAD_PAD - QK_NOPE), F32)
        dq_parts, dkv_parts = [], []
        for h in range(MLA_HEADS):
            qh = q_raw[:, h * HEAD_PAD:(h + 1) * HEAD_PAD]
            dx, dg = _rms_bwd(qh, qg_ref[...], QK_HEAD, _rope_bwd(dqf_ref[h], ct, st))
            gqg = gqg + jnp.sum(dg, axis=0, keepdims=True)
            dq_parts.append(dx)
            kh = jnp.concatenate([kv_raw[:, h * HEAD_PAD:h * HEAD_PAD + QK_NOPE], krp], axis=1)
            dx, dg = _rms_bwd(kh, kg_ref[...], QK_HEAD, _rope_bwd(dkf_ref[h], ct, st))
            gkg = gkg + jnp.sum(dg, axis=0, keepdims=True)
            dkr = dkr + dx[:, QK_NOPE:HEAD_PAD]
            dkv_parts += [dx[:, 0:QK_NOPE], dv_ref[h]]
        dq_raw = jnp.concatenate(dq_parts, axis=1).astype(BF16)
        dkv_raw = jnp.concatenate(dkv_parts, axis=1).astype(BF16)
        dqraw_ref[...] = dq_raw
        dkvraw_ref[...] = dkv_raw
        dkr_ref[...] = dkr.astype(BF16)
        dcq, gqa = _rms_bwd(cq, qag_ref[...], ql, _dot_nt(dq_raw, wuq_ref[...]))
        dckv, gkva = _rms_bwd(ckv, kvag_ref[...], kvl, _dot_nt(dkv_raw, wukv_ref[...]))
        dc_ref[:, 0:ql] = dcq.astype(BF16)
        dc_ref[:, ql:ql + kvl] = dckv.astype(BF16)
        gqa = jnp.sum(gqa, axis=0, keepdims=True)
        gkva = jnp.sum(gkva, axis=0, keepdims=True)

        @pl.when(i == 0)
        def _():
            gqa_ref[...] = gqa
            gkva_ref[...] = gkva
            gqg_ref[...] = gqg
            gkg_ref[...] = gkg

        @pl.when(i > 0)
        def _():
            gqa_ref[...] += gqa
            gkva_ref[...] += gkva
            gqg_ref[...] += gqg
            gkg_ref[...] += gkg

    full = lambda shp: pl.BlockSpec(shp, lambda i: tuple(0 for _ in shp))
    hblk = lambda w: pl.BlockSpec((MLA_HEADS, t, w), lambda i: (0, i, 0))
    sds = jax.ShapeDtypeStruct
    return pl.pallas_call(
        body,
        out_shape=(sds((s, ql + kvl), BF16), sds((s, 128), BF16), sds((s, nq), BF16), sds((s, nq), BF16),
                   sds((ql, s), BF16), sds((kvl, s), BF16),
                   sds((1, ql), F32), sds((1, kvl), F32), sds((1, HEAD_PAD), F32), sds((1, HEAD_PAD), F32)),
        grid=(s // t,),
        in_specs=[pl.BlockSpec((t, ql), lambda i: (i, cq_blk)), pl.BlockSpec((t, kvl), lambda i: (i, cq_blk + 1)),
                  pl.BlockSpec((t, 128), lambda i: (i, 0)),
                  pl.BlockSpec((t, HEAD_PAD), lambda i: (i, 0)), pl.BlockSpec((t, HEAD_PAD), lambda i: (i, 0)),
                  full((1, ql)), full((1, kvl)), full(w_uq.shape), full(w_ukv.shape), full((1, HEAD_PAD)), full((1, HEAD_PAD)),
                  hblk(HEAD_PAD), hblk(HEAD_PAD), hblk(V_HEAD)],
        out_specs=(pl.BlockSpec((t, ql + kvl), lambda i: (i, 0)), pl.BlockSpec((t, 128), lambda i: (i, 0)),
                   pl.BlockSpec((t, nq), lambda i: (i, 0)), pl.BlockSpec((t, nq), lambda i: (i, 0)),
                   pl.BlockSpec((ql, t), lambda i: (0, i)), pl.BlockSpec((kvl, t), lambda i: (0, i)),
                   full((1, ql)), full((1, kvl)), full((1, HEAD_PAD)), full((1, HEAD_PAD))),
        compiler_params=_cparams(("arbitrary",)), name="mla_prep_bwd")(
            p1, p1, kr, ctab, stab, qa_g, kva_g, w_uq, w_ukv, q_g, k_g, dqf, dkf, dv)


MZ_BLK128 = 3072 // 128
ATT_SCALE = QK_HEAD ** -0.5


HPS = 2


def _causal_pairs(nq, by_query):
    if by_query:
        prs = [(qi, ki) for qi in range(nq) for ki in range(qi + 1)]
    else:
        prs = [(qi, ki) for ki in range(nq) for qi in range(ki, nq)]
    return (jnp.asarray([p[0] for p in prs], jnp.int32), jnp.asarray([p[1] for p in prs], jnp.int32), len(prs))


def _comm_hooks(comm, cins, couts, csems, first, middle, last):
    ph = comm["phases"]

    def at(pred, phase):
        @pl.when(pred)
        def _():
            phase(cins, couts, *csems)

    return (lambda: at(first, ph[0])), (lambda: [at(middle, ph[1])] if len(ph) == 3 else None), (lambda: at(last, ph[-1]))


def _comm_extras(comm):
    if comm is None:
        return [], [], [], ()
    sems = [pltpu.SemaphoreType.DMA((comm["nsem"],)), pltpu.SemaphoreType.DMA((comm["nsem"],))]
    return [ANY] * len(comm["ins"]), [ANY] * len(comm["outs"]), sems, tuple(comm["outs"])


def _attn_fwd(qf, kf, vt, p1, *, tq, comm=None):
    nh, s, _ = qf.shape
    nq = s // tq
    qtab, ktab, npairs = _causal_pairs(nq, True)
    wv = HPS * V_HEAD
    c_in, c_out, c_sem, c_shapes = _comm_extras(comm)
    nci, nco = len(c_in), len(c_out)

    def body(*refs):
        qt_ref, kt_ref, q_ref, k_ref, vt_ref, mz_ref = refs[:6]
        y_ref, yt_ref, o_ref, lse_ref, lset_ref = refs[6 + nci:11 + nci]
        m_scr, l_scr, acc_scr = refs[11 + nci + nco:14 + nci + nco]
        pr = pl.program_id(1)
        qi, ki = qt_ref[pr], kt_ref[pr]
        if comm is not None:
            hg = pl.program_id(0)
            last_hg = nh // HPS - 1
            c_first, c_mid, c_last = _comm_hooks(comm, refs[6:6 + nci], refs[11 + nci:11 + nci + nco], refs[14 + nci + nco:],
                                                 (hg == 0) & (pr == 0), (hg == last_hg) & (pr == 0),
                                                 (hg == last_hg) & (pr == npairs - 1))
            c_first()

        @pl.when(ki == 0)
        def _():
            m_scr[...] = jnp.full((HPS, 1, tq), NEG, F32)
            l_scr[...] = jnp.zeros((HPS, 1, tq), F32)
            acc_scr[...] = jnp.zeros((HPS, V_HEAD, tq), F32)

        def step(diagonal):
            for u in range(HPS):
                st = _dot_nt(k_ref[u], q_ref[u]) * ATT_SCALE
                if diagonal:
                    r = lax.broadcasted_iota(jnp.int32, (tq, tq), 0)
                    c = lax.broadcasted_iota(jnp.int32, (tq, tq), 1)
                    st = jnp.where(r <= c, st, NEG)
                m_old = m_scr[u]
                m_new = jnp.maximum(m_old, jnp.max(st, axis=0, keepdims=True))
                alpha = jnp.exp(m_old - m_new)
                pt = jnp.exp(st - m_new)
                l_scr[u] = alpha * l_scr[u] + jnp.sum(pt, axis=0, keepdims=True)
                acc_scr[u] = alpha * acc_scr[u] + _dot(vt_ref[u], pt.astype(BF16))
                m_scr[u] = m_new

        @pl.when(ki < qi)
        def _():
            step(False)

        @pl.when(ki == qi)
        def _():
            step(True)
            outs = []
            for u in range(HPS):
                l = l_scr[u]
                outs.append((acc_scr[u] / l).T)
                lset = m_scr[u] + jnp.log(l)
                lset_ref[u] = lset
                lse_ref[u] = jnp.broadcast_to(lset, (128, tq)).T[:, 0:1]
            o = jnp.concatenate(outs, axis=1)
            o_ref[...] = o
            y = o * _silu(mz_ref[...])
            y_ref[...] = y.astype(BF16)
            yt_ref[...] = y.T.astype(BF16)

        if comm is not None:
            c_mid()
            c_last()

    sds = jax.ShapeDtypeStruct
    gs = pltpu.PrefetchScalarGridSpec(
        num_scalar_prefetch=2, grid=(nh // HPS, npairs),
        in_specs=[pl.BlockSpec((HPS, tq, HEAD_PAD), lambda h, p, qt, kt: (h, qt[p], 0)),
                  pl.BlockSpec((HPS, tq, HEAD_PAD), lambda h, p, qt, kt: (h, kt[p], 0)),
                  pl.BlockSpec((HPS, V_HEAD, tq), lambda h, p, qt, kt: (h, 0, kt[p])),
                  pl.BlockSpec((tq, wv), lambda h, p, qt, kt: (qt[p], MZ_BLK128 // HPS + h))] + c_in,
        out_specs=(pl.BlockSpec((tq, wv), lambda h, p, qt, kt: (qt[p], h)),
                   pl.BlockSpec((wv, tq), lambda h, p, qt, kt: (h, qt[p])),
                   pl.BlockSpec((tq, wv), lambda h, p, qt, kt: (qt[p], h)),
                   pl.BlockSpec((HPS, tq, 1), lambda h, p, qt, kt: (h, qt[p], 0)),
                   pl.BlockSpec((HPS, 1, tq), lambda h, p, qt, kt: (h, 0, qt[p]))) + tuple(c_out),
        scratch_shapes=[pltpu.VMEM((HPS, 1, tq), F32), pltpu.VMEM((HPS, 1, tq), F32), pltpu.VMEM((HPS, V_HEAD, tq), F32)] + c_sem)
    sem = ("parallel", "arbitrary") if comm is None else ("arbitrary", "arbitrary")
    res = pl.pallas_call(
        body,
        out_shape=(sds((s, BRANCH_W), BF16), sds((BRANCH_W, s), BF16), sds((s, BRANCH_W), F32),
                   sds((nh, s, 1), F32), sds((nh, 1, s), F32)) + c_shapes,
        grid_spec=gs, compiler_params=_cparams(sem),
        name="attn_fwd" if comm is None else "attn_fwd_with_gather")(qtab, ktab, qf, kf, vt, p1, *(comm["ins"] if comm else ()))
    return res[:5], list(res[5:])


def _attn_bwd_pre(dy4, o, p1, *, t=256):
    s = o.shape[0]
    t = _tile(s, t, 128)

    def body(dy_ref, o_ref, mz_ref, do_ref, dmz_ref, dl_ref, dlt_ref):
        dy, o_, mz = dy_ref[...], o_ref[...], mz_ref[...]
        do = dy * _silu(mz)
        do_ref[...] = do.astype(BF16)
        dmz_ref[...] = (dy * o_ * _dsilu(mz)).astype(BF16)
        prod = do * o_
        for h in range(MLA_HEADS):
            dl = jnp.sum(prod[:, h * V_HEAD:(h + 1) * V_HEAD], axis=-1, keepdims=True)
            dl_ref[h] = dl
            dlt_ref[h] = jnp.broadcast_to(dl, (t, 128)).T[0:1, :]

    sds = jax.ShapeDtypeStruct
    return pl.pallas_call(
        body,
        out_shape=(sds((s, BRANCH_W), BF16), sds((s, BRANCH_W), BF16), sds((MLA_HEADS, s, 1), F32), sds((MLA_HEADS, 1, s), F32)),
        grid=(s // t,),
        in_specs=[pl.BlockSpec((None, t, BRANCH_W), lambda i: (1, i, 0)), pl.BlockSpec((t, BRANCH_W), lambda i: (i, 0)),
                  pl.BlockSpec((t, BRANCH_W), lambda i: (i, 3))],
        out_specs=(pl.BlockSpec((t, BRANCH_W), lambda i: (i, 0)), pl.BlockSpec((t, BRANCH_W), lambda i: (i, 0)),
                   pl.BlockSpec((MLA_HEADS, t, 1), lambda i: (0, i, 0)), pl.BlockSpec((MLA_HEADS, 1, t), lambda i: (0, 0, i))),
        compiler_params=_cparams(("parallel",)), name="attn_bwd_pre")(dy4, o, p1)


def _attn_bwd_dq(qf, kf, vv, do, lse, delta, *, tq):
    nh, s, _ = qf.shape
    nq = s // tq
    qtab, ktab, npairs = _causal_pairs(nq, True)
    wv = HPS * V_HEAD

    def body(qt_ref, kt_ref, q_ref, k_ref, v_ref, do_ref, lse_ref, dl_ref, dq_ref, acc_scr):
        pr = pl.program_id(1)
        qi, ki = qt_ref[pr], kt_ref[pr]

        @pl.when(ki == 0)
        def _():
            acc_scr[...] = jnp.zeros((HPS, tq, HEAD_PAD), F32)

        def step(diagonal):
            for u in range(HPS):
                k = k_ref[u]
                sc = _dot_nt(q_ref[u], k) * ATT_SCALE
                p = jnp.exp(sc - lse_ref[u])
                if diagonal:
                    r = lax.broadcasted_iota(jnp.int32, (tq, tq), 0)
                    c = lax.broadcasted_iota(jnp.int32, (tq, tq), 1)
                    p = jnp.where(c <= r, p, 0.0)
                dp = _dot_nt(do_ref[:, u * V_HEAD:(u + 1) * V_HEAD], v_ref[u])
                ds = p * (dp - dl_ref[u]) * ATT_SCALE
                acc_scr[u] += _dot(ds.astype(BF16), k)

        @pl.when(ki < qi)
        def _():
            step(False)

        @pl.when(ki == qi)
        def _():
            step(True)
            dq_ref[...] = acc_scr[...]

    gs = pltpu.PrefetchScalarGridSpec(
        num_scalar_prefetch=2, grid=(nh // HPS, npairs),
        in_specs=[pl.BlockSpec((HPS, tq, HEAD_PAD), lambda h, p, qt, kt: (h, qt[p], 0)),
                  pl.BlockSpec((HPS, tq, HEAD_PAD), lambda h, p, qt, kt: (h, kt[p], 0)),
                  pl.BlockSpec((HPS, tq, V_HEAD), lambda h, p, qt, kt: (h, kt[p], 0)),
                  pl.BlockSpec((tq, wv), lambda h, p, qt, kt: (qt[p], h)),
                  pl.BlockSpec((HPS, tq, 1), lambda h, p, qt, kt: (h, qt[p], 0)),
                  pl.BlockSpec((HPS, tq, 1), lambda h, p, qt, kt: (h, qt[p], 0))],
        out_specs=pl.BlockSpec((HPS, tq, HEAD_PAD), lambda h, p, qt, kt: (h, qt[p], 0)),
        scratch_shapes=[pltpu.VMEM((HPS, tq, HEAD_PAD), F32)])
    return pl.pallas_call(
        body, out_shape=jax.ShapeDtypeStruct((nh, s, HEAD_PAD), F32), grid_spec=gs,
        compiler_params=_cparams(("parallel", "arbitrary")), name="attn_bwd_dq")(qtab, ktab, qf, kf, vv, do, lse, delta)


def _attn_bwd_dkv(qf, kf, vv, do, lset, deltat, *, tq, comm=None):
    nh, s, _ = qf.shape
    nq = s // tq
    qtab, ktab, npairs = _causal_pairs(nq, False)
    wv = HPS * V_HEAD
    c_in, c_out, c_sem, c_shapes = _comm_extras(comm)
    nci, nco = len(c_in), len(c_out)

    def body(*refs):
        qt_ref, kt_ref, k_ref, v_ref, q_ref, do_ref, lse_ref, dl_ref = refs[:8]
        dk_ref, dv_ref = refs[8 + nci:10 + nci]
        dk_scr, dv_scr = refs[10 + nci + nco:12 + nci + nco]
        pr = pl.program_id(1)
        qi, ki = qt_ref[pr], kt_ref[pr]
        if comm is not None:
            hg = pl.program_id(0)
            last_hg = nh // HPS - 1
            c_first, c_mid, c_last = _comm_hooks(comm, refs[8:8 + nci], refs[10 + nci:10 + nci + nco], refs[12 + nci + nco:],
                                                 (hg == 0) & (pr == 0), (hg == last_hg) & (pr == 0),
                                                 (hg == last_hg) & (pr == npairs - 1))
            c_first()

        def step(diagonal):
            for u in range(HPS):
                q = q_ref[u]
                do_ = do_ref[:, u * V_HEAD:(u + 1) * V_HEAD]
                st = _dot_nt(k_ref[u], q) * ATT_SCALE
                pt = jnp.exp(st - lse_ref[u])
                if diagonal:
                    r = lax.broadcasted_iota(jnp.int32, (tq, tq), 0)
                    c = lax.broadcasted_iota(jnp.int32, (tq, tq), 1)
                    pt = jnp.where(r <= c, pt, 0.0)
                dpt = _dot_nt(v_ref[u], do_)
                dst = pt * (dpt - dl_ref[u]) * ATT_SCALE
                if diagonal:
                    dv_scr[u] = _dot(pt.astype(BF16), do_)
                    dk_scr[u] = _dot(dst.astype(BF16), q)
                else:
                    dv_scr[u] += _dot(pt.astype(BF16), do_)
                    dk_scr[u] += _dot(dst.astype(BF16), q)

        @pl.when(qi == ki)
        def _():
            step(True)

        @pl.when(qi > ki)
        def _():
            step(False)

        @pl.when(qi == nq - 1)
        def _():
            dk_ref[...] = dk_scr[...]
            dv_ref[...] = dv_scr[...]

        if comm is not None:
            c_mid()
            c_last()

    sds = jax.ShapeDtypeStruct
    gs = pltpu.PrefetchScalarGridSpec(
        num_scalar_prefetch=2, grid=(nh // HPS, npairs),
        in_specs=[pl.BlockSpec((HPS, tq, HEAD_PAD), lambda h, p, qt, kt: (h, kt[p], 0)),
                  pl.BlockSpec((HPS, tq, V_HEAD), lambda h, p, qt, kt: (h, kt[p], 0)),
                  pl.BlockSpec((HPS, tq, HEAD_PAD), lambda h, p, qt, kt: (h, qt[p], 0)),
                  pl.BlockSpec((tq, wv), lambda h, p, qt, kt: (qt[p], h)),
                  pl.BlockSpec((HPS, 1, tq), lambda h, p, qt, kt: (h, 0, qt[p])),
                  pl.BlockSpec((HPS, 1, tq), lambda h, p, qt, kt: (h, 0, qt[p]))] + c_in,
        out_specs=(pl.BlockSpec((HPS, tq, HEAD_PAD), lambda h, p, qt, kt: (h, kt[p], 0)),
                   pl.BlockSpec((HPS, tq, V_HEAD), lambda h, p, qt, kt: (h, kt[p], 0))) + tuple(c_out),
        scratch_shapes=[pltpu.VMEM((HPS, tq, HEAD_PAD), F32), pltpu.VMEM((HPS, tq, V_HEAD), F32)] + c_sem)
    sem = ("parallel", "arbitrary") if comm is None else ("arbitrary", "arbitrary")
    res = pl.pallas_call(
        body, out_shape=(sds((nh, s, HEAD_PAD), F32), sds((nh, s, V_HEAD), F32)) + c_shapes, grid_spec=gs,
        compiler_params=_cparams(sem), name="attn_bwd_dkv" if comm is None else "attn_bwd_dkv_with_exchange")(
            qtab, ktab, kf, vv, qf, do, lset, deltat, *(comm["ins"] if comm else ()))
    return res[0], res[1], list(res[2:])


XATT_SCALE = XATTN_HEAD_DIM ** -0.5
XQ_COL = 8


def _memkv_prep(mem_kv, k_g):
    m = mem_kv.shape[0]

    def body(kv_ref, g_ref, k_ref, v_ref):
        for h in range(XATTN_HEADS):
            sl = slice(h * XATTN_HEAD_DIM, (h + 1) * XATTN_HEAD_DIM)
            k_ref[:, sl] = _rms(kv_ref[:, sl], g_ref[...], XATTN_HEAD_DIM).astype(BF16)
        v_ref[...] = kv_ref[:, BRANCH_W:2 * BRANCH_W].astype(BF16)

    sds = jax.ShapeDtypeStruct
    return pl.pallas_call(body, out_shape=(sds((m, BRANCH_W), BF16), sds((m, BRANCH_W), BF16)),
                          compiler_params=_cparams(), name="memkv_prep")(mem_kv, k_g)


def _memkv_prep_bwd(mem_kv, k_g, dk, dv):
    m = mem_kv.shape[0]

    def body(kv_ref, g_ref, dk_ref, dv_ref, d_ref, gk_ref):
        gk = jnp.zeros((1, XATTN_HEAD_DIM), F32)
        for h in range(XATTN_HEADS):
            sl = slice(h * XATTN_HEAD_DIM, (h + 1) * XATTN_HEAD_DIM)
            dx, dg = _rms_bwd(kv_ref[:, sl], g_ref[...], XATTN_HEAD_DIM, dk_ref[:, sl])
            d_ref[:, sl] = dx.astype(BF16)
            gk = gk + jnp.sum(dg, axis=0, keepdims=True)
        d_ref[:, BRANCH_W:2 * BRANCH_W] = dv_ref[...].astype(BF16)
        gk_ref[...] = gk

    sds = jax.ShapeDtypeStruct
    return pl.pallas_call(body, out_shape=(sds((m, 2 * BRANCH_W), BF16), sds((1, XATTN_HEAD_DIM), F32)),
                          compiler_params=_cparams(), name="memkv_prep_bwd")(mem_kv, k_g, dk, dv)


def _xattn_probs(xq, k_ref, qg, h):
    sl = slice(h * XATTN_HEAD_DIM, (h + 1) * XATTN_HEAD_DIM)
    q = _rms(xq[:, sl], qg, XATTN_HEAD_DIM).astype(BF16)
    sc = _dot_nt(q, k_ref[:, sl]) * XATT_SCALE
    e = jnp.exp(sc - jnp.max(sc, axis=-1, keepdims=True))
    return q, e / jnp.sum(e, axis=-1, keepdims=True)


def _xattn_fwd(p1, kx, vx, q_g, *, t=256):
    s = p1.shape[0]
    m = kx.shape[0]
    t = _tile(s, t, 128)

    def body(xq_ref, xz_ref, k_ref, v_ref, g_ref, y_ref, yt_ref):
        xq = xq_ref[...]
        outs = []
        for h in range(XATTN_HEADS):
            _, p = _xattn_probs(xq, k_ref, g_ref[...], h)
            outs.append(_dot(p.astype(BF16), v_ref[:, h * XATTN_HEAD_DIM:(h + 1) * XATTN_HEAD_DIM]))
        y = jnp.concatenate(outs, axis=1) * _silu(xz_ref[...])
        y_ref[...] = y.astype(BF16)
        yt_ref[...] = y.T.astype(BF16)

    full = lambda shp: pl.BlockSpec(shp, lambda i: tuple(0 for _ in shp))
    sds = jax.ShapeDtypeStruct
    return pl.pallas_call(
        body, out_shape=(sds((s, BRANCH_W), BF16), sds((BRANCH_W, s), BF16)), grid=(s // t,),
        in_specs=[pl.BlockSpec((t, BRANCH_W), lambda i: (i, XQ_COL)), pl.BlockSpec((t, BRANCH_W), lambda i: (i, XQ_COL + 1)),
                  full((m, BRANCH_W)), full((m, BRANCH_W)), full((1, XATTN_HEAD_DIM))],
        out_specs=(pl.BlockSpec((t, BRANCH_W), lambda i: (i, 0)), pl.BlockSpec((BRANCH_W, t), lambda i: (0, i))),
        compiler_params=_cparams(("parallel",)), name="xattn_fwd")(p1, p1, kx, vx, q_g)


def _xattn_bwd(p1, dy4, kx, vx, q_g, *, t=256):
    s = p1.shape[0]
    m = kx.shape[0]
    t = _tile(s, t, 128)

    def body(xq_ref, xz_ref, dy_ref, k_ref, v_ref, g_ref, d_ref, dk_ref, dv_ref, gq_ref):
        i = pl.program_id(0)
        xq, xz, dy = xq_ref[...], xz_ref[...], dy_ref[...]
        do = dy * _silu(xz)
        gq = jnp.zeros((1, XATTN_HEAD_DIM), F32)
        outs, dks, dvs = [], [], []
        for h in range(XATTN_HEADS):
            sl = slice(h * XATTN_HEAD_DIM, (h + 1) * XATTN_HEAD_DIM)
            q, p = _xattn_probs(xq, k_ref, g_ref[...], h)
            pb = p.astype(BF16)
            outs.append(_dot(pb, v_ref[:, sl]))
            do_h = do[:, sl].astype(BF16)
            dvs.append(_dot_tn(pb, do_h))
            dp = _dot_nt(do_h, v_ref[:, sl])
            ds = (p * (dp - jnp.sum(p * dp, axis=-1, keepdims=True)) * XATT_SCALE).astype(BF16)
            dks.append(_dot_tn(ds, q))
            dx, dg = _rms_bwd(xq[:, sl], g_ref[...], XATTN_HEAD_DIM, _dot(ds, k_ref[:, sl]))
            d_ref[:, sl] = dx.astype(BF16)
            gq = gq + jnp.sum(dg, axis=0, keepdims=True)
        o = jnp.concatenate(outs, axis=1)
        d_ref[:, BRANCH_W:2 * BRANCH_W] = (dy * o * _dsilu(xz)).astype(BF16)
        dk = jnp.concatenate(dks, axis=1)
        dv = jnp.concatenate(dvs, axis=1)

        @pl.when(i == 0)
        def _():
            dk_ref[...] = dk
            dv_ref[...] = dv
            gq_ref[...] = gq

        @pl.when(i > 0)
        def _():
            dk_ref[...] += dk
            dv_ref[...] += dv
            gq_ref[...] += gq

    full = lambda shp: pl.BlockSpec(shp, lambda i: tuple(0 for _ in shp))
    sds = jax.ShapeDtypeStruct
    return pl.pallas_call(
        body, out_shape=(sds((s, 2 * BRANCH_W), BF16), sds((m, BRANCH_W), F32), sds((m, BRANCH_W), F32), sds((1, XATTN_HEAD_DIM), F32)),
        grid=(s // t,),
        in_specs=[pl.BlockSpec((t, BRANCH_W), lambda i: (i, XQ_COL)), pl.BlockSpec((t, BRANCH_W), lambda i: (i, XQ_COL + 1)),
                  pl.BlockSpec((None, t, BRANCH_W), lambda i: (3, i, 0)),
                  full((m, BRANCH_W)), full((m, BRANCH_W)), full((1, XATTN_HEAD_DIM))],
        out_specs=(pl.BlockSpec((t, 2 * BRANCH_W), lambda i: (i, 0)), full((m, BRANCH_W)), full((m, BRANCH_W)),
                   full((1, XATTN_HEAD_DIM))),
        compiler_params=_cparams(("arbitrary",)), name="xattn_bwd")(p1, p1, dy4, kx, vx, q_g)


def _gate_fwd(ystack, w_branch, gp, gate_b, *, tm=512, tn=1024):
    _, s, _ = ystack.shape
    d = w_branch.shape[2]
    tm, tn = _tile(s, tm, 128), _tile(d, tn)
    nj = d // tn

    def body(y_ref, w_ref, gp_ref, gb_ref, o_ref, ot_ref, acc_scr):
        b = pl.program_id(2)
        part = jax.nn.sigmoid(gp_ref[...] + gb_ref[...]) * _dot(y_ref[...], w_ref[...])

        @pl.when(b == 0)
        def _():
            acc_scr[...] = part

        @pl.when(b > 0)
        def _():
            acc_scr[...] += part

        @pl.when(b == N_BRANCH - 1)
        def _():
            acc = acc_scr[...]
            o_ref[...] = acc.astype(BF16)
            ot_ref[...] = acc.T.astype(BF16)

    sds = jax.ShapeDtypeStruct
    return pl.pallas_call(
        body, out_shape=(sds((s, d), BF16), sds((d, s), BF16)), grid=(s // tm, nj, N_BRANCH),
        in_specs=[pl.BlockSpec((None, tm, BRANCH_W), lambda i, j, b: (b, i, 0)),
                  pl.BlockSpec((None, BRANCH_W, tn), lambda i, j, b: (b, 0, j)),
                  pl.BlockSpec((tm, tn), lambda i, j, b: (i, b * nj + j)),
                  pl.BlockSpec((1, tn), lambda i, j, b: (0, b * nj + j))],
        out_specs=(pl.BlockSpec((tm, tn), lambda i, j, b: (i, j)), pl.BlockSpec((tn, tm), lambda i, j, b: (j, i))),
        scratch_shapes=[pltpu.VMEM((tm, tn), F32)],
        compiler_params=_cparams(("parallel", "parallel", "arbitrary")), name="gate_fwd")(ystack, w_branch, gp, gate_b)


def _gate_bwd(ystack, w_branch, gp, gate_b, dm, *, tm=512, tn=1024):
    _, s, _ = ystack.shape
    d = w_branch.shape[2]
    tm, tn = _tile(s, tm, 128), _tile(d, tn)
    nj = d // tn

    def body(y_ref, w_ref, gp_ref, gb_ref, dm_ref, dp_ref, dg_ref, gb_out_ref):
        i = pl.program_id(2)
        proj = _dot(y_ref[...], w_ref[...])
        gate = jax.nn.sigmoid(gp_ref[...] + gb_ref[...])
        dmv = dm_ref[...]
        dp_ref[...] = (dmv * gate).astype(BF16)
        dpre = dmv * proj * gate * (1.0 - gate)
        dg_ref[...] = dpre.astype(BF16)
        part = jnp.sum(dpre, axis=0, keepdims=True)

        @pl.when(i == 0)
        def _():
            gb_out_ref[...] = part

        @pl.when(i > 0)
        def _():
            gb_out_ref[...] += part

    sds = jax.ShapeDtypeStruct
    return pl.pallas_call(
        body, out_shape=(sds((N_BRANCH, s, d), BF16), sds((s, N_BRANCH * d), BF16), sds((1, N_BRANCH * d), F32)),
        grid=(N_BRANCH, nj, s // tm),
        in_specs=[pl.BlockSpec((None, tm, BRANCH_W), lambda b, j, i: (b, i, 0)),
                  pl.BlockSpec((None, BRANCH_W, tn), lambda b, j, i: (b, 0, j)),
                  pl.BlockSpec((tm, tn), lambda b, j, i: (i, b * nj + j)),
                  pl.BlockSpec((1, tn), lambda b, j, i: (0, b * nj + j)),
                  pl.BlockSpec((tm, tn), lambda b, j, i: (i, j))],
        out_specs=(pl.BlockSpec((None, tm, tn), lambda b, j, i: (b, i, j)),
                   pl.BlockSpec((tm, tn), lambda b, j, i: (i, b * nj + j)),
                   pl.BlockSpec((1, tn), lambda b, j, i: (0, b * nj + j))),
        compiler_params=_cparams(("parallel", "parallel", "arbitrary")), name="gate_bwd")(ystack, w_branch, gp, gate_b, dm)


def _adamw(w, g, m, v, *, name):
    shape = w.shape
    c = shape[-1]
    r = 1
    for n in shape[:-1]:
        r *= n
    w2, g2, m2, v2 = (a.reshape(r, c) for a in (w, g, m, v))
    tr = _tile(r, max(8, (1 << 19) // c // 8 * 8), 8)
    c1 = 1.0 / (1.0 - ADAM_B1 ** ADAM_STEP)
    c2 = 1.0 / (1.0 - ADAM_B2 ** ADAM_STEP)

    def body(w_ref, g_ref, m_ref, v_ref, d_ref, nm_ref, nv_ref):
        gv = g_ref[...]
        nm = ADAM_B1 * m_ref[...] + (1.0 - ADAM_B1) * gv
        nv = ADAM_B2 * v_ref[...] + (1.0 - ADAM_B2) * (gv * gv)
        nm_ref[...] = nm
        nv_ref[...] = nv
        d_ref[...] = -ADAM_LR * ((nm * c1) / (jnp.sqrt(nv * c2) + ADAM_EPS) + ADAM_WD * w_ref[...])

    blk = pl.BlockSpec((tr, c), lambda i: (i, 0))
    sd = jax.ShapeDtypeStruct((r, c), F32)
    d2, nm2, nv2 = pl.pallas_call(body, out_shape=(sd, sd, sd), grid=(r // tr,), in_specs=[blk] * 4, out_specs=(blk,) * 3,
                                  compiler_params=_cparams(("parallel",)), name=name)(w2, g2, m2, v2)
    return d2.reshape(shape), nm2.reshape(shape), nv2.reshape(shape)


def _place():
    x, y, c = lax.axis_index("x"), lax.axis_index("y"), lax.axis_index("c")
    chips = [(1 - x, y), (x, 1 - y), (1 - x, 1 - y)]
    return x, y, c, 2 * x + y, chips, [2 * cx + cy for cx, cy in chips]


ANY = pl.BlockSpec(memory_space=pl.ANY)


def _all_gather(shards):
    n = len(shards)

    def body(*refs):
        ins, outs = refs[:n], refs[n:2 * n]
        send, recv = refs[2 * n:]
        x, y, c, k, chips, ks = _place()
        sib = (x, y, 1 - c)
        sends = []
        for a in range(n):
            for j in range(3):
                cp = pltpu.make_async_remote_copy(src_ref=ins[a].at[c], dst_ref=outs[a].at[c, k], send_sem=send.at[6 * a + j],
                                                  recv_sem=recv.at[6 * a + j], device_id=(*chips[j], c), device_id_type=MESH)
                cp.start()
                sends.append(cp)
        for a in range(n):
            for j in range(3):
                slab = outs[a].at[c, ks[j]]
                pltpu.make_async_remote_copy(src_ref=slab, dst_ref=slab, send_sem=send.at[6 * a + j], recv_sem=recv.at[6 * a + j],
                                             device_id=(*chips[j], c), device_id_type=MESH).wait_recv()
                cp = pltpu.make_async_remote_copy(src_ref=slab, dst_ref=slab, send_sem=send.at[6 * a + 3 + j],
                                                  recv_sem=recv.at[6 * a + 3 + j], device_id=sib, device_id_type=MESH)
                cp.start()
                sends.append(cp)
        for a in range(n):
            for j in range(3):
                slab = outs[a].at[1 - c, ks[j]]
                pltpu.make_async_remote_copy(src_ref=slab, dst_ref=slab, send_sem=send.at[6 * a + 3 + j],
                                             recv_sem=recv.at[6 * a + 3 + j], device_id=sib, device_id_type=MESH).wait_recv()
        for cp in sends:
            cp.wait_send()

    out_shape = tuple(jax.ShapeDtypeStruct((2, 4) + s.shape[1:], s.dtype) for s in shards)
    return pl.pallas_call(
        body, out_shape=out_shape, in_specs=[ANY] * n, out_specs=(ANY,) * n,
        scratch_shapes=[pltpu.SemaphoreType.DMA((6 * n,)), pltpu.SemaphoreType.DMA((6 * n,))],
        name="weights_all_gather")(*shards)


def _rs_exchange_cores(grads):
    n = len(grads)

    def body(*refs):
        ins, outs = refs[:n], refs[n:2 * n]
        send, recv = refs[2 * n:]
        x, y, c, _, _, _ = _place()
        sib = (x, y, 1 - c)
        cps = []
        for a in range(n):
            cp = pltpu.make_async_remote_copy(src_ref=ins[a].at[1 - c], dst_ref=outs[a], send_sem=send.at[a], recv_sem=recv.at[a],
                                              device_id=sib, device_id_type=MESH)
            cp.start()
            cps.append(cp)
        for cp in cps:
            cp.wait()

    out_shape = tuple(jax.ShapeDtypeStruct(g.shape[1:], g.dtype) for g in grads)
    return pl.pallas_call(body, out_shape=out_shape, in_specs=[ANY] * n, out_specs=(ANY,) * n,
                          scratch_shapes=[pltpu.SemaphoreType.DMA((n,)), pltpu.SemaphoreType.DMA((n,))],
                          name="grads_exchange_cores")(*grads)


def _rs_exchange_chips(parts):
    n = len(parts)

    def body(*refs):
        ins, outs = refs[:n], refs[n:2 * n]
        send, recv = refs[2 * n:]
        x, y, c, k, chips, ks = _place()
        sends = []
        for a in range(n):
            for j in range(3):
                cp = pltpu.make_async_remote_copy(src_ref=ins[a].at[ks[j]], dst_ref=outs[a].at[j], send_sem=send.at[3 * a + j],
                                                  recv_sem=recv.at[3 * a + j], device_id=(*chips[j], c), device_id_type=MESH)
                cp.start()
                sends.append(cp)
        for cp in sends:
            cp.wait()

    out_shape = tuple(jax.ShapeDtypeStruct((3,) + p.shape[1:], p.dtype) for p in parts)
    return pl.pallas_call(
        body, out_shape=out_shape, in_specs=[ANY] * n, out_specs=(ANY,) * n,
        scratch_shapes=[pltpu.SemaphoreType.DMA((3 * n,)), pltpu.SemaphoreType.DMA((3 * n,))],
        name="grads_exchange_chips")(*parts)


def _rs_share_cores(bufs):
    n = len(bufs)

    def body(*refs):
        outs = refs[n:2 * n]
        send, recv = refs[2 * n:]
        x, y, c, _, _, _ = _place()
        sib = (x, y, 1 - c)
        cps = []
        for a in range(n):
            cp = pltpu.make_async_remote_copy(src_ref=outs[a].at[c], dst_ref=outs[a].at[c], send_sem=send.at[a], recv_sem=recv.at[a],
                                              device_id=sib, device_id_type=MESH)
            cp.start()
            cps.append(cp)
        for a in range(n):
            slab = outs[a].at[1 - c]
            pltpu.make_async_remote_copy(src_ref=slab, dst_ref=slab, send_sem=send.at[a], recv_sem=recv.at[a],
                                         device_id=sib, device_id_type=MESH).wait_recv()
        for cp in cps:
            cp.wait_send()

    out_shape = tuple(jax.ShapeDtypeStruct(b.shape, b.dtype) for b in bufs)
    return pl.pallas_call(
        body, out_shape=out_shape, in_specs=[ANY] * n, out_specs=(ANY,) * n,
        input_output_aliases={a: a for a in range(n)},
        scratch_shapes=[pltpu.SemaphoreType.DMA((n,)), pltpu.SemaphoreType.DMA((n,))],
        name="grads_share_cores")(*bufs)


def _add_core_halves(g, ra, c_idx, *, name):
    _, _, r, c = g.shape
    tr = _tile(r, max(16, (1 << 19) // c // 16 * 16), 16)

    def body(c_ref, g_ref, ra_ref, o_ref, ob_ref):
        tot = g_ref[...] + ra_ref[...]
        o_ref[...] = tot
        ob_ref[...] = tot.astype(BF16)

    blk = pl.BlockSpec((None, tr, c), lambda j, i, cr: (j, i, 0))
    gs = pltpu.PrefetchScalarGridSpec(
        num_scalar_prefetch=1, grid=(4, r // tr),
        in_specs=[pl.BlockSpec((None, None, tr, c), lambda j, i, cr: (cr[0], j, i, 0)), blk],
        out_specs=(blk, blk))
    return pl.pallas_call(body, out_shape=(jax.ShapeDtypeStruct((4, r, c), F32), jax.ShapeDtypeStruct((4, r, c), BF16)), grid_spec=gs,
                          compiler_params=_cparams(("parallel", "parallel")), name=name)(c_idx, g, ra)


def _add_chips(p, r3, k_idx, c_idx, *, name):
    _, r, c = p.shape
    tr = _tile(r, max(16, (1 << 18) // c // 16 * 16), 16)

    def body(k_ref, c_ref, p_ref, r_ref, o_ref):
        o_ref[...] = ((p_ref[...] + r_ref[0].astype(F32)) + r_ref[1].astype(F32)) + r_ref[2].astype(F32)

    gs = pltpu.PrefetchScalarGridSpec(
        num_scalar_prefetch=2, grid=(r // tr,),
        in_specs=[pl.BlockSpec((None, tr, c), lambda i, kr, cr: (kr[0], i, 0)), pl.BlockSpec((3, tr, c), lambda i, kr, cr: (0, i, 0))],
        out_specs=pl.BlockSpec((None, tr, c), lambda i, kr, cr: (cr[0], i, 0)))
    return pl.pallas_call(body, out_shape=jax.ShapeDtypeStruct((2, r, c), F32), grid_spec=gs,
                          compiler_params=_cparams(("parallel",)), name=name)(k_idx, c_idx, p, r3)


def _rdma(src, dst, send, recv, idx, dev):
    return pltpu.make_async_remote_copy(src_ref=src, dst_ref=dst, send_sem=send.at[idx], recv_sem=recv.at[idx],
                                        device_id=dev, device_id_type=MESH)


def _run_comm(comm, name):
    n_in, n_out = len(comm["ins"]), len(comm["outs"])

    def body(*refs):
        ins, outs = refs[:n_in], refs[n_in:n_in + n_out]
        send, recv = refs[n_in + n_out:]
        for phase in comm["phases"]:
            phase(ins, outs, send, recv)

    return pl.pallas_call(
        body, out_shape=tuple(comm["outs"]), in_specs=[ANY] * n_in, out_specs=(ANY,) * n_out,
        input_output_aliases=comm.get("aliases", {}),
        scratch_shapes=[pltpu.SemaphoreType.DMA((comm["nsem"],)), pltpu.SemaphoreType.DMA((comm["nsem"],))],
        name=name)(*comm["ins"])


def _gather_comm(shards, layer):
    n = len(shards)

    def start(ins, outs, send, recv):
        x, y, c, k, chips, ks = _place()

        @pl.when(c == layer)
        def _():
            for a in range(n):
                for j in range(3):
                    _rdma(ins[a], outs[a].at[k], send, recv, 6 * a + j, (*chips[j], c)).start()

    def forward(ins, outs, send, recv):
        x, y, c, k, chips, ks = _place()

        @pl.when(c == layer)
        def _():
            for a in range(n):
                for j in range(3):
                    slab = outs[a].at[ks[j]]
                    _rdma(slab, slab, send, recv, 6 * a + j, (*chips[j], c)).wait_recv()
                    _rdma(slab, slab, send, recv, 6 * a + 3 + j, (x, y, 1 - c)).start()

    def finish(ins, outs, send, recv):
        x, y, c, k, chips, ks = _place()

        @pl.when(c == layer)
        def _():
            for a in range(n):
                for j in range(3):
                    slab = outs[a].at[ks[j]]
                    _rdma(ins[a], outs[a].at[k], send, recv, 6 * a + j, (*chips[j], c)).wait_send()
                    _rdma(slab, slab, send, recv, 6 * a + 3 + j, (x, y, 1 - c)).wait_send()

        @pl.when(c != layer)
        def _():
            for a in range(n):
                for j in range(3):
                    slab = outs[a].at[ks[j]]
                    _rdma(slab, slab, send, recv, 6 * a + 3 + j, (x, y, 1 - c)).wait_recv()

    return dict(ins=list(shards), outs=[jax.ShapeDtypeStruct((4,) + s.shape, s.dtype) for s in shards], nsem=6 * n,
                phases=[start, forward, finish])


def _to_owner_comm(grads, layer):
    n = len(grads)

    def go(ins, outs, send, recv):
        x, y, c, _, _, _ = _place()

        @pl.when(c != layer)
        def _():
            for a in range(n):
                _rdma(ins[a], outs[a], send, recv, a, (x, y, 1 - c)).start()
            for a in range(n):
                _rdma(ins[a], outs[a], send, recv, a, (x, y, 1 - c)).wait_send()

        @pl.when(c == layer)
        def _():
            for a in range(n):
                _rdma(ins[a], outs[a], send, recv, a, (x, y, 1 - c)).wait_recv()

    return dict(ins=list(grads), outs=[jax.ShapeDtypeStruct(g.shape, g.dtype) for g in grads], nsem=n, phases=[go])


def _exchange_comm(parts, layer):
    n = len(parts)

    def start(ins, outs, send, recv):
        x, y, c, k, chips, ks = _place()

        @pl.when(c == layer)
        def _():
            for a in range(n):
                for j in range(3):
                    _rdma(ins[a].at[ks[j]], outs[a].at[j], send, recv, 3 * a + j, (*chips[j], c)).start()

    def finish(ins, outs, send, recv):
        x, y, c, k, chips, ks = _place()

        @pl.when(c == layer)
        def _():
            for a in range(n):
                for j in range(3):
                    _rdma(ins[a].at[ks[j]], outs[a].at[j], send, recv, 3 * a + j, (*chips[j], c)).wait()

    return dict(ins=list(parts), outs=[jax.ShapeDtypeStruct((3,) + p.shape[1:], p.dtype) for p in parts], nsem=3 * n,
                phases=[start, finish])


def _share_comm(bufs, layer):
    n = len(bufs)

    def go(ins, outs, send, recv):
        x, y, c, _, _, _ = _place()

        @pl.when(c == layer)
        def _():
            for a in range(n):
                _rdma(outs[a].at[layer], outs[a].at[layer], send, recv, a, (x, y, 1 - c)).start()
            for a in range(n):
                _rdma(outs[a].at[layer], outs[a].at[layer], send, recv, a, (x, y, 1 - c)).wait_send()

        @pl.when(c != layer)
        def _():
            for a in range(n):
                _rdma(outs[a].at[layer], outs[a].at[layer], send, recv, a, (x, y, 1 - c)).wait_recv()

    return dict(ins=list(bufs), outs=[jax.ShapeDtypeStruct(b.shape, b.dtype) for b in bufs], nsem=n, phases=[go],
                aliases={a: a for a in range(n)})


def _add_owner(g, ra, *, name):
    _, r, c = g.shape
    tr = _tile(r, max(16, (1 << 19) // c // 16 * 16), 16)

    def body(g_ref, ra_ref, o_ref, ob_ref):
        tot = g_ref[...] + ra_ref[...]
        o_ref[...] = tot
        ob_ref[...] = tot.astype(BF16)

    blk = pl.BlockSpec((None, tr, c), lambda j, i: (j, i, 0))
    return pl.pallas_call(body, out_shape=(jax.ShapeDtypeStruct((4, r, c), F32), jax.ShapeDtypeStruct((4, r, c), BF16)),
                          grid=(4, r // tr), in_specs=[blk, blk], out_specs=(blk, blk),
                          compiler_params=_cparams(("parallel", "parallel")), name=name)(g, ra)


def _add_chips_layer(p, r3, k_idx, layer, buf, *, name):
    _, r, c = p.shape
    tr = _tile(r, max(16, (1 << 18) // c // 16 * 16), 16)

    def body(k_ref, p_ref, r_ref, *rest):
        o_ref = rest[-1]
        o_ref[...] = ((p_ref[...] + r_ref[0].astype(F32)) + r_ref[1].astype(F32)) + r_ref[2].astype(F32)

    in_specs = [pl.BlockSpec((None, tr, c), lambda i, kr: (kr[0], i, 0)), pl.BlockSpec((3, tr, c), lambda i, kr: (0, i, 0))]
    args = [k_idx, p, r3]
    aliases = {}
    if buf is not None:
        in_specs.append(ANY)
        args.append(buf)
        aliases = {3: 0}
    gs = pltpu.PrefetchScalarGridSpec(num_scalar_prefetch=1, grid=(r // tr,), in_specs=in_specs,
                                      out_specs=pl.BlockSpec((None, tr, c), lambda i, kr: (layer, i, 0)))
    return pl.pallas_call(body, out_shape=jax.ShapeDtypeStruct((2, r, c), F32), grid_spec=gs, input_output_aliases=aliases,
                          compiler_params=_cparams(("arbitrary",)), name=name)(*args)


def _all_reduce_small(vec):
    r = vec.shape[0]

    def body(v_ref, gath_ref, sum_ref, send, recv):
        x, y, c = lax.axis_index("x"), lax.axis_index("y"), lax.axis_index("c")
        me = 4 * x + 2 * y + c
        gath_ref[me] = v_ref[...]
        cps = []
        for f in range(1, 8):
            fx, fy, fc = (f >> 2) & 1, (f >> 1) & 1, f & 1
            peer = (x ^ fx, y ^ fy, c ^ fc)
            cp = pltpu.make_async_remote_copy(src_ref=v_ref, dst_ref=gath_ref.at[me], send_sem=send.at[f - 1], recv_sem=recv.at[f - 1],
                                              device_id=peer, device_id_type=MESH)
            cp.start()
            cps.append(cp)
        for f in range(1, 8):
            fx, fy, fc = (f >> 2) & 1, (f >> 1) & 1, f & 1
            src = 4 * (x ^ fx) + 2 * (y ^ fy) + (c ^ fc)
            pltpu.make_async_remote_copy(src_ref=v_ref, dst_ref=gath_ref.at[src], send_sem=send.at[f - 1], recv_sem=recv.at[f - 1],
                                         device_id=(x ^ fx, y ^ fy, c ^ fc), device_id_type=MESH).wait_recv()
        for cp in cps:
            cp.wait_send()
        acc = gath_ref[0]
        for i in range(1, 8):
            acc = acc + gath_ref[i]
        sum_ref[...] = acc

    vm = pl.BlockSpec(memory_space=pltpu.VMEM)
    _, total = pl.pallas_call(
        body, out_shape=(jax.ShapeDtypeStruct((8, r, 128), F32), jax.ShapeDtypeStruct((r, 128), F32)),
        in_specs=[vm], out_specs=(vm, vm),
        scratch_shapes=[pltpu.SemaphoreType.DMA((7,)), pltpu.SemaphoreType.DMA((7,))],
        name="small_all_reduce")(vec)
    return total


def _full_weight(gw, name):
    gathered, own, chip = gw[name]
    return jnp.concatenate([jnp.where(chip == k, own, gathered[k]) for k in range(4)], axis=SHARD_AXIS[name])


def _to_shards(full, name):
    return jnp.stack(jnp.split(full, 4, axis=SHARD_AXIS[name]), axis=0)


def _rope_tables(positions):
    inv = ROPE_THETA ** (-jnp.arange(0, QK_ROPE, 2, dtype=F32) / QK_ROPE)
    ang = positions.astype(F32)[:, None] * inv
    cos, sin = jnp.cos(ang), jnp.sin(ang)
    s = positions.shape[0]
    pad = jnp.zeros((s, HEAD_PAD - QK_HEAD), F32)
    ctab = jnp.concatenate([jnp.ones((s, QK_NOPE), F32), cos, cos, pad], axis=1)
    stab = jnp.concatenate([jnp.zeros((s, QK_NOPE), F32), -sin, sin, pad], axis=1)
    return ctab, stab


def _pad_gain(g):
    return jnp.concatenate([g, jnp.zeros((HEAD_PAD - QK_HEAD,), F32)])[None, :]


def _layer_weights(gw, rep, l, ql, kvl):
    d = rep["norm_g"].shape[1]
    w_in = _full_weight(gw, "w_in")
    o_kr = 2 * BRANCH_W + ql + kvl
    o_g = o_kr + QK_ROPE + 7 * BRANCH_W
    w = {}
    w["w1"] = jnp.concatenate([w_in[:, :o_kr], w_in[:, o_kr + QK_ROPE:o_g]], axis=1)
    w["wg"] = w_in[:, o_g:]
    w["wkr"] = jnp.concatenate([w_in[:, o_kr:o_kr + QK_ROPE], jnp.zeros((d, 128 - QK_ROPE), BF16)], axis=1)
    wuq = _full_weight(gw, "w_uq").reshape(ql, MLA_HEADS, QK_HEAD)
    w["w_uq"] = jnp.pad(wuq, ((0, 0), (0, 0), (0, HEAD_PAD - QK_HEAD))).reshape(ql, MLA_HEADS * HEAD_PAD)
    for nme in ("w_ukv", "pool_w", "conv_w", "w_mem_kv", "w_branch", "w_out"):
        w[nme] = _full_weight(gw, nme)
    for nme in ("norm_g", "gate_b", "pool_scale", "q_a_norm_g", "kv_a_norm_g", "mem_norm_g", "xattn_q_norm_g", "xattn_k_norm_g"):
        w[nme] = rep[nme][l][None, :]
    w["mla_q_norm_g"] = _pad_gain(rep["mla_q_norm_g"][l])
    w["mla_k_norm_g"] = _pad_gain(rep["mla_k_norm_g"][l])
    return w


def _forward_layer(x, mem, ctab, stab, w, tq, l, comm=None):
    sfx = f"_l{l}"
    h, ht = _norm_fwd(x, w["norm_g"], name="norm_fwd" + sfx)
    p1 = _mm(h, w["w1"], name="proj_main" + sfx)
    gp = _mm(h, w["wg"], name="proj_gates" + sfx)
    kr = _mm(h, w["wkr"], name="proj_krope" + sfx)
    y_pool, yt_pool = _pool_fwd(p1, w["pool_w"], w["pool_scale"])
    qf, kf, vv, vt = _mla_prep_fwd(p1, kr, ctab, stab, w["q_a_norm_g"], w["kv_a_norm_g"], w["w_uq"], w["w_ukv"],
                               w["mla_q_norm_g"], w["mla_k_norm_g"])
    (y_mla, yt_mla, o_att, lse, lset), comm_out = _attn_fwd(qf, kf, vt, p1, tq=tq, comm=comm)
    y_conv, yt_conv = _conv_fwd(p1, w["conv_w"])
    memn, memnt = _norm_fwd(mem, w["mem_norm_g"], name="mem_norm" + sfx)
    mem_kv = _mm(memn, w["w_mem_kv"], name="mem_kv" + sfx)
    kx, vx = _memkv_prep(mem_kv, w["xattn_k_norm_g"])
    y_mem, yt_mem = _xattn_fwd(p1, kx, vx, w["xattn_q_norm_g"])
    ystack = jnp.stack([y_pool, y_mla, y_conv, y_mem])
    ytstack = jnp.stack([yt_pool, yt_mla, yt_conv, yt_mem])
    merged, mergedt = _gate_fwd(ystack, w["w_branch"], gp, w["gate_b"])
    x_out = _mm(merged, w["w_out"], add=x, name="out_proj" + sfx)
    saved = dict(x=x, ht=ht, p1=p1, gp=gp, kr=kr, qf=qf, kf=kf, vv=vv, o_att=o_att, lse=lse, lset=lset, memnt=memnt,
                 mem_kv=mem_kv, kx=kx, vx=vx, ystack=ystack, ytstack=ytstack, mergedt=mergedt)
    return x_out, saved, comm_out


def _backward_layer(dx_out, sv, mem, ctab, stab, w, tq, l, ql, kvl, comm=None):
    sfx = f"_l{l}"
    g = {}
    g["w_out"] = _mm(sv["mergedt"], dx_out, name="g_w_out" + sfx)
    dm = _mm(dx_out, w["w_out"], trans_b=True, name="d_merged" + sfx)
    dproj, dgp, g_gate_b = _gate_bwd(sv["ystack"], w["w_branch"], sv["gp"], w["gate_b"], dm)
    g["gate_b"] = g_gate_b[0]
    g["w_branch"] = _mm(sv["ytstack"], dproj, name="g_w_branch" + sfx)
    dy4 = _mm(dproj, w["w_branch"], trans_b=True, name="d_branches" + sfx)
    p1, kr = sv["p1"], sv["kr"]
    d_pool, g_pw, g_ps = _pool_bwd(p1, dy4, w["pool_w"], w["pool_scale"])
    g["pool_w"], g["pool_scale"] = g_pw, g_ps[0]
    do, d_mz, delta, deltat = _attn_bwd_pre(dy4, sv["o_att"], p1)
    dqf = _attn_bwd_dq(sv["qf"], sv["kf"], sv["vv"], do, sv["lse"], delta, tq=tq)
    dkf, dvv, comm_out = _attn_bwd_dkv(sv["qf"], sv["kf"], sv["vv"], do, sv["lset"], deltat, tq=tq, comm=comm)
    (d_c, d_kr, dq_raw, dkv_raw, cqnt, ckvnt, g_qa, g_kva, g_qg, g_kg) = _mla_prep_bwd(
        p1, kr, ctab, stab, w["q_a_norm_g"], w["kv_a_norm_g"], w["w_uq"], w["w_ukv"], w["mla_q_norm_g"], w["mla_k_norm_g"],
        dqf, dkf, dvv)
    g["q_a_norm_g"], g["kv_a_norm_g"] = g_qa[0], g_kva[0]
    g["mla_q_norm_g"], g["mla_k_norm_g"] = g_qg[0, :QK_HEAD], g_kg[0, :QK_HEAD]
    g_wuq = _mm(cqnt, dq_raw, name="g_w_uq" + sfx)
    g["w_uq"] = g_wuq.reshape(ql, MLA_HEADS, HEAD_PAD)[:, :, :QK_HEAD].reshape(ql, MLA_HEADS * QK_HEAD)
    g["w_ukv"] = _mm(ckvnt, dkv_raw, name="g_w_ukv" + sfx)
    d_conv, gc0, gc1, gc2 = _conv_bwd(p1, dy4, w["conv_w"])
    g["conv_w"] = jnp.concatenate([gc0, gc1, gc2], axis=0)
    d_x, dkx, dvx, g_xq = _xattn_bwd(p1, dy4, sv["kx"], sv["vx"], w["xattn_q_norm_g"])
    g["xattn_q_norm_g"] = g_xq[0]
    d_memkv, g_xk = _memkv_prep_bwd(sv["mem_kv"], w["xattn_k_norm_g"], dkx, dvx)
    g["xattn_k_norm_g"] = g_xk[0]
    g["w_mem_kv"] = _mm(sv["memnt"], d_memkv, name="g_w_mem_kv" + sfx)
    d_memn = _mm(d_memkv, w["w_mem_kv"], trans_b=True, name="d_memn" + sfx)
    _, g_mn = _norm_bwd(mem, w["mem_norm_g"], d_memn, d_memn, name="mem_norm_bwd" + sfx)
    g["mem_norm_g"] = g_mn[0]
    dp1 = jnp.concatenate([d_pool, d_c, d_mz, d_conv, d_x], axis=1)
    ht = sv["ht"]
    g_w1 = _mm(ht, dp1, name="g_w1" + sfx)
    g_wg = _mm(ht, dgp, name="g_wg" + sfx)
    g_wkr = _mm(ht, d_kr, name="g_wkr" + sfx)
    o_kr = 2 * BRANCH_W + ql + kvl
    g["w_in"] = jnp.concatenate([g_w1[:, :o_kr], g_wkr[:, :QK_ROPE], g_w1[:, o_kr:], g_wg], axis=1)
    dh = _mm(dp1, w["w1"], trans_b=True, name="dh_main" + sfx)
    dh = _mm(dgp, w["wg"], trans_b=True, add=dh, name="dh_gates" + sfx)
    dh = _mm(d_kr, w["wkr"], trans_b=True, add=dh, name="dh_krope" + sfx)
    dx, g_ng = _norm_bwd(sv["x"], w["norm_g"], dh, dx_out, name="norm_bwd" + sfx)
    g["norm_g"] = g_ng[0]
    return dx, g, comm_out


def _as4(a):
    rest = a.shape[2:]
    r = 1
    for n in rest[:-1]:
        r *= n
    return a.reshape(2, 4, r, rest[-1])


def kernel(x, mem, positions, norm_g, w_in, gate_b, pool_w, pool_scale, q_a_norm_g, kv_a_norm_g, w_uq, w_ukv, mla_q_norm_g, mla_k_norm_g, conv_w, mem_norm_g, w_mem_kv, xattn_q_norm_g, xattn_k_norm_g, w_branch, w_out, loss_target, m_norm_g, m_w_in, m_gate_b, m_pool_w, m_pool_scale, m_q_a_norm_g, m_kv_a_norm_g, m_w_uq, m_w_ukv, m_mla_q_norm_g, m_mla_k_norm_g, m_conv_w, m_mem_norm_g, m_w_mem_kv, m_xattn_q_norm_g, m_xattn_k_norm_g, m_w_branch, m_w_out, v_norm_g, v_w_in, v_gate_b, v_pool_w, v_pool_scale, v_q_a_norm_g, v_kv_a_norm_g, v_w_uq, v_w_ukv, v_mla_q_norm_g, v_mla_k_norm_g, v_conv_w, v_mem_norm_g, v_w_mem_kv, v_xattn_q_norm_g, v_xattn_k_norm_g, v_w_branch, v_w_out):
    wts = dict(norm_g=norm_g, w_in=w_in, gate_b=gate_b, pool_w=pool_w, pool_scale=pool_scale, q_a_norm_g=q_a_norm_g,
               kv_a_norm_g=kv_a_norm_g, w_uq=w_uq, w_ukv=w_ukv, mla_q_norm_g=mla_q_norm_g, mla_k_norm_g=mla_k_norm_g,
               conv_w=conv_w, mem_norm_g=mem_norm_g, w_mem_kv=w_mem_kv, xattn_q_norm_g=xattn_q_norm_g,
               xattn_k_norm_g=xattn_k_norm_g, w_branch=w_branch, w_out=w_out)
    mom = dict(norm_g=m_norm_g, w_in=m_w_in, gate_b=m_gate_b, pool_w=m_pool_w, pool_scale=m_pool_scale, q_a_norm_g=m_q_a_norm_g,
               kv_a_norm_g=m_kv_a_norm_g, w_uq=m_w_uq, w_ukv=m_w_ukv, mla_q_norm_g=m_mla_q_norm_g, mla_k_norm_g=m_mla_k_norm_g,
               conv_w=m_conv_w, mem_norm_g=m_mem_norm_g, w_mem_kv=m_w_mem_kv, xattn_q_norm_g=m_xattn_q_norm_g,
               xattn_k_norm_g=m_xattn_k_norm_g, w_branch=m_w_branch, w_out=m_w_out)
    vel = dict(norm_g=v_norm_g, w_in=v_w_in, gate_b=v_gate_b, pool_w=v_pool_w, pool_scale=v_pool_scale, q_a_norm_g=v_q_a_norm_g,
               kv_a_norm_g=v_kv_a_norm_g, w_uq=v_w_uq, w_ukv=v_w_ukv, mla_q_norm_g=v_mla_q_norm_g, mla_k_norm_g=v_mla_k_norm_g,
               conv_w=v_conv_w, mem_norm_g=v_mem_norm_g, w_mem_kv=v_w_mem_kv, xattn_q_norm_g=v_xattn_q_norm_g,
               xattn_k_norm_g=v_xattn_k_norm_g, w_branch=v_w_branch, w_out=v_w_out)
    depth = norm_g.shape[0]
    assert depth == 2 and x.shape[0] == 1
    xs, mems, tgt = x[0], mem[0], loss_target[0]
    s = xs.shape[0]
    ql, kvl = q_a_norm_g.shape[1], kv_a_norm_g.shape[1]
    tq = _tile(s, 512, 128)
    ctab, stab = _rope_tables(positions[0])

    chip = 2 * lax.axis_index("x") + lax.axis_index("y")
    k_idx = chip.astype(jnp.int32).reshape(1)
    rep = {n: wts[n] for n in REPLICATED}
    send = [[wts[n][l].astype(F32 if n == "conv_w" else BF16) for n in SHARDED] for l in range(depth)]

    def layer_weights(got, l):
        return _layer_weights({n: (g, own, chip) for n, g, own in zip(SHARDED, got, send[l])}, rep, l, ql, kvl)

    lw = [layer_weights(_run_comm(_gather_comm(send[0], 0), "weights_gather_l0"), 0), None]

    act, sv0, got1 = _forward_layer(xs, mems, ctab, stab, lw[0], tq, 0, comm=_gather_comm(send[1], 1))
    lw[1] = layer_weights(got1, 1)
    act, sv1, _ = _forward_layer(act, mems, ctab, stab, lw[1], tq, 1)
    dy, loss_part = _loss_head(act, tgt)

    def shard_layout(g):
        shards = [_to_shards(g[n], n) for n in SHARDED]
        return [t.reshape(4, -1, t.shape[-1]) for t in shards]

    def to_owner_and_add(g, l):
        gl = shard_layout(g)
        ra = _run_comm(_to_owner_comm(gl, l), f"grads_to_owner_l{l}")
        return [_add_owner(a, b, name=f"add_owner_{n}_l{l}") for a, b, n in zip(gl, ra, SHARDED)]

    def finish_layer(parts, r3s, l, bufs):
        bufs = [_add_chips_layer(p, r3, k_idx, l, None if bufs is None else bufs[i], name=f"add_chips_{n}_l{l}")
                for i, ((p, _), r3, n) in enumerate(zip(parts, r3s, SHARDED))]
        return _run_comm(_share_comm(bufs, l), f"grads_share_l{l}")

    grads = [None] * depth
    dxl, grads[1], _ = _backward_layer(dy, sv1, mems, ctab, stab, lw[1], tq, 1, ql, kvl)
    parts1 = to_owner_and_add(grads[1], 1)
    dxl, grads[0], r3_1 = _backward_layer(dxl, sv0, mems, ctab, stab, lw[0], tq, 0, ql, kvl,
                                          comm=_exchange_comm([pb for _, pb in parts1], 1))
    grad_x = dxl[None]
    bufs = finish_layer(parts1, r3_1, 1, None)
    parts0 = to_owner_and_add(grads[0], 0)
    r3_0 = _run_comm(_exchange_comm([pb for _, pb in parts0], 0), "grads_exchange_l0")
    reduced = finish_layer(parts0, r3_0, 0, bufs)
    gsum = {n: r.reshape(wts[n].shape) for n, r in zip(SHARDED, reduced)}

    flat = [jnp.stack([grads[l][n] for l in range(depth)], axis=0).reshape(-1) for n in REPLICATED]
    sizes = [f.shape[0] for f in flat]
    total = sum(sizes) + 1
    rows = -(-total // 1024) * 8
    vec = jnp.concatenate(flat + [loss_part[0, :1], jnp.zeros((rows * 128 - total,), F32)]).reshape(rows, 128)
    red = _all_reduce_small(vec).reshape(-1)
    off = 0
    for n, sz in zip(REPLICATED, sizes):
        gsum[n] = red[off:off + sz].reshape(wts[n].shape)
        off += sz
    loss = red[off]

    delta, new_m, new_v = {}, {}, {}
    for n in WEIGHTS:
        delta[n], new_m[n], new_v[n] = _adamw(wts[n], gsum[n], mom[n], vel[n], name=f"adamw_{n}")
    return (loss, grad_x, *[gsum[n] for n in WEIGHTS], *[delta[n] for n in WEIGHTS],
            *[new_m[n] for n in WEIGHTS], *[new_v[n] for n in WEIGHTS])
```

```python
import functools

import jax
import jax.numpy as jnp
from jax import lax
from jax.experimental import pallas as pl
from jax.experimental.pallas import tpu as pltpu

F32 = jnp.float32
BF16 = jnp.bfloat16
MESH = pl.DeviceIdType.MESH

EPS = 1e-6
N_BRANCH = 4
BRANCH_W = 1024
POOL_GROUPS = 4
POOL_GW = BRANCH_W // POOL_GROUPS
POOL_HALO = 16
CONV_HALO = 8
MLA_HEADS = 8
QK_NOPE = 128
QK_ROPE = 64
QK_HEAD = QK_NOPE + QK_ROPE
HEAD_PAD = 256
V_HEAD = 128
ROPE_THETA = 10000.0
XATTN_HEADS = 4
XATTN_HEAD_DIM = BRANCH_W // XATTN_HEADS
ADAM_LR, ADAM_B1, ADAM_B2, ADAM_EPS, ADAM_WD, ADAM_STEP = 0.001, 0.9, 0.999, 1e-08, 0.01, 10
NEG = -1e30
VMEM_LIMIT = 48 * 1024 * 1024

SHARDED = ("w_in", "pool_w", "w_uq", "w_ukv", "conv_w", "w_mem_kv", "w_branch", "w_out")
REPLICATED = ("norm_g", "gate_b", "pool_scale", "q_a_norm_g", "kv_a_norm_g", "mla_q_norm_g", "mla_k_norm_g",
              "mem_norm_g", "xattn_q_norm_g", "xattn_k_norm_g")
WEIGHTS = ("norm_g", "w_in", "gate_b", "pool_w", "pool_scale", "q_a_norm_g", "kv_a_norm_g", "w_uq", "w_ukv",
           "mla_q_norm_g", "mla_k_norm_g", "conv_w", "mem_norm_g", "w_mem_kv", "xattn_q_norm_g", "xattn_k_norm_g",
           "w_branch", "w_out")
SHARD_AXIS = {"w_in": 1, "pool_w": 1, "w_uq": 1, "w_ukv": 1, "conv_w": 1, "w_mem_kv": 0, "w_branch": 2, "w_out": 0}


def _cparams(sem=None):
    return pltpu.CompilerParams(dimension_semantics=sem, vmem_limit_bytes=VMEM_LIMIT)


def _tile(n, pref, unit=128):
    if n <= pref:
        return n
    t = (pref // unit) * unit
    while t >= unit:
        if n % t == 0:
            return t
        t -= unit
    return n


def _silu(z):
    return z * jax.nn.sigmoid(z)


def _dsilu(z):
    s = jax.nn.sigmoid(z)
    return s * (1.0 + z * (1.0 - s))


def _dot(a, b):
    return jnp.dot(a, b, preferred_element_type=F32)


def _dot_nt(a, b):
    return lax.dot_general(a, b, (((1,), (1,)), ((), ())), preferred_element_type=F32)


def _dot_tn(a, b):
    return lax.dot_general(a, b, (((0,), (0,)), ((), ())), preferred_element_type=F32)


def _rms(x, g, n):
    r = lax.rsqrt(jnp.sum(x * x, axis=-1, keepdims=True) * (1.0 / n) + EPS)
    return x * r * g


def _rms_bwd(x, g, n, dout):
    r = lax.rsqrt(jnp.sum(x * x, axis=-1, keepdims=True) * (1.0 / n) + EPS)
    y = x * r
    dy = dout * g
    dx = r * (dy - y * (jnp.sum(dy * y, axis=-1, keepdims=True) * (1.0 / n)))
    return dx, dout * y


def _rope(x, ctab, stab):
    lane = lax.broadcasted_iota(jnp.int32, x.shape, 1)
    partner = jnp.where(lane < QK_NOPE + QK_ROPE // 2, pltpu.roll(x, HEAD_PAD - QK_ROPE // 2, 1),
                        pltpu.roll(x, QK_ROPE // 2, 1))
    return x * ctab + partner * stab


def _rope_bwd(d, ctab, stab):
    lane = lax.broadcasted_iota(jnp.int32, d.shape, 1)
    ds = d * stab
    partner = jnp.where(lane < QK_NOPE + QK_ROPE // 2, pltpu.roll(ds, HEAD_PAD - QK_ROPE // 2, 1),
                        pltpu.roll(ds, QK_ROPE // 2, 1))
    return d * ctab + jnp.where((lane >= QK_NOPE) & (lane < QK_HEAD), partner, 0.0)


def _mm(a, b, *, name, trans_b=False, add=None, out_dtype=F32, tm=512, tn=1024, tk=2048):
    batched = a.ndim == 3
    if batched:
        nb, m, k = a.shape
    else:
        m, k = a.shape
    n = b.shape[-2] if trans_b else b.shape[-1]
    tm, tn, tk = _tile(m, tm, 8), _tile(n, tn), _tile(k, tk)
    nk = k // tk

    def body(*refs):
        if add is None:
            a_ref, b_ref, o_ref = refs[:3]
            add_ref = None
            rest = refs[3:]
        else:
            a_ref, b_ref, add_ref, o_ref = refs[:4]
            rest = refs[4:]
        av = a_ref[...].astype(BF16)
        bv = b_ref[...].astype(BF16)
        part = _dot_nt(av, bv) if trans_b else _dot(av, bv)

        def finish(acc):
            if add_ref is not None:
                acc = acc + add_ref[...]
            o_ref[...] = acc.astype(o_ref.dtype)

        if nk == 1:
            finish(part)
        else:
            acc_ref = rest[0]
            kk = pl.program_id(3 if batched else 2)

            @pl.when(kk == 0)
            def _():
                acc_ref[...] = part

            @pl.when(kk > 0)
            def _():
                acc_ref[...] += part

            @pl.when(kk == nk - 1)
            def _():
                finish(acc_ref[...])

    if batched:
        a_spec = pl.BlockSpec((None, tm, tk), lambda bb, i, j, kk: (bb, i, kk))
        b_spec = (pl.BlockSpec((None, tn, tk), lambda bb, i, j, kk: (bb, j, kk)) if trans_b
                  else pl.BlockSpec((None, tk, tn), lambda bb, i, j, kk: (bb, kk, j)))
        o_spec = pl.BlockSpec((None, tm, tn), lambda bb, i, j, kk: (bb, i, j))
        grid = (nb, m // tm, n // tn, nk)
        out_shape = jax.ShapeDtypeStruct((nb, m, n), out_dtype)
        sem = ("parallel", "parallel", "parallel", "arbitrary")
    else:
        a_spec = pl.BlockSpec((tm, tk), lambda i, j, kk: (i, kk))
        b_spec = (pl.BlockSpec((tn, tk), lambda i, j, kk: (j, kk)) if trans_b
                  else pl.BlockSpec((tk, tn), lambda i, j, kk: (kk, j)))
        o_spec = pl.BlockSpec((tm, tn), lambda i, j, kk: (i, j))
        grid = (m // tm, n // tn, nk)
        out_shape = jax.ShapeDtypeStruct((m, n), out_dtype)
        sem = ("parallel", "parallel", "arbitrary")
    in_specs = [a_spec, b_spec] + ([o_spec] if add is not None else [])
    args = (a, b) + ((add,) if add is not None else ())
    scratch = [pltpu.VMEM((tm, tn), F32)] if nk > 1 else []
    return pl.pallas_call(body, out_shape=out_shape, grid=grid, in_specs=in_specs, out_specs=o_spec,
                          scratch_shapes=scratch, compiler_params=_cparams(sem), name=name)(*args)


def _norm_fwd(x, g, *, name, t=256):
    s, d = x.shape
    t = _tile(s, t, 128)

    def body(x_ref, g_ref, h_ref, ht_ref):
        h = _rms(x_ref[...], g_ref[...], d)
        h_ref[...] = h.astype(BF16)
        ht_ref[...] = h.T.astype(BF16)

    return pl.pallas_call(
        body, out_shape=(jax.ShapeDtypeStruct((s, d), BF16), jax.ShapeDtypeStruct((d, s), BF16)),
        grid=(s // t,),
        in_specs=[pl.BlockSpec((t, d), lambda i: (i, 0)), pl.BlockSpec((1, d), lambda i: (0, 0))],
        out_specs=(pl.BlockSpec((t, d), lambda i: (i, 0)), pl.BlockSpec((d, t), lambda i: (0, i))),
        compiler_params=_cparams(("parallel",)), name=name)(x, g)


def _norm_bwd(x, g, dh, dres, *, name, t=256):
    s, d = x.shape
    t = _tile(s, t, 8)

    def body(x_ref, g_ref, dh_ref, dres_ref, dx_ref, dg_ref):
        dx, dgt = _rms_bwd(x_ref[...], g_ref[...], d, dh_ref[...])
        dx_ref[...] = dx + dres_ref[...]
        part = jnp.sum(dgt, axis=0, keepdims=True)

        @pl.when(pl.program_id(0) == 0)
        def _():
            dg_ref[...] = part

        @pl.when(pl.program_id(0) > 0)
        def _():
            dg_ref[...] += part

    row = pl.BlockSpec((t, d), lambda i: (i, 0))
    vec = pl.BlockSpec((1, d), lambda i: (0, 0))
    return pl.pallas_call(
        body, out_shape=(jax.ShapeDtypeStruct((s, d), F32), jax.ShapeDtypeStruct((1, d), F32)),
        grid=(s // t,), in_specs=[row, vec, row, row], out_specs=(row, vec),
        compiler_params=_cparams(("arbitrary",)), name=name)(x, g, dh, dres)


def _loss_head(y, tgt, *, t=256):
    s, d = y.shape
    t = _tile(s, t, 8)

    def body(y_ref, t_ref, dy_ref, l_ref):
        e = y_ref[...] - t_ref[...]
        dy_ref[...] = e * (1.0 / d)
        part = jnp.zeros((1, 128), F32) + jnp.sum(e * e) * (0.5 / d)

        @pl.when(pl.program_id(0) == 0)
        def _():
            l_ref[...] = part

        @pl.when(pl.program_id(0) > 0)
        def _():
            l_ref[...] += part

    row = pl.BlockSpec((t, d), lambda i: (i, 0))
    return pl.pallas_call(
        body, out_shape=(jax.ShapeDtypeStruct((s, d), F32), jax.ShapeDtypeStruct((1, 128), F32)),
        grid=(s // t,), in_specs=[row, row], out_specs=(row, pl.BlockSpec((1, 128), lambda i: (0, 0))),
        compiler_params=_cparams(("arbitrary",)), name="loss_head")(y, tgt)


def _pool_mixed(scr, v, halo, first, row0, t):
    scr[0:POOL_HALO, :] = jnp.where(first, 0.0, halo)
    scr[POOL_HALO:POOL_HALO + t, :] = v
    row = row0 + lax.broadcasted_iota(jnp.int32, (t, 1), 0)
    mixed = []
    for g in range(POOL_GROUPS):
        w = 2 ** (g + 1)
        acc = scr[:, g * POOL_GW:(g + 1) * POOL_GW]
        sh = 1
        while sh < w:
            acc = acc + pltpu.roll(acc, sh, 0)
            sh *= 2
        cnt = jnp.minimum(row + 1, w).astype(F32)
        mixed.append(acc[POOL_HALO:POOL_HALO + t, :] / cnt - v[:, g * POOL_GW:(g + 1) * POOL_GW])
    return mixed


def _pool_fwd(p1, pool_w, pool_scale, *, t=256):
    s = p1.shape[0]
    t = _tile(s, t, 128)
    hb = t // POOL_HALO

    def body(pv_ref, halo_ref, pz_ref, pw_ref, sc_ref, y_ref, yt_ref, scr):
        i = pl.program_id(0)
        mixed = _pool_mixed(scr, pv_ref[...], halo_ref[...], i == 0, i * t, t)
        outs = [_dot(mixed[g].astype(BF16), pw_ref[g]) for g in range(POOL_GROUPS)]
        y = jnp.concatenate(outs, axis=1) * sc_ref[...] * _silu(pz_ref[...])
        y_ref[...] = y.astype(BF16)
        yt_ref[...] = y.T.astype(BF16)

    return pl.pallas_call(
        body, out_shape=(jax.ShapeDtypeStruct((s, BRANCH_W), BF16), jax.ShapeDtypeStruct((BRANCH_W, s), BF16)),
        grid=(s // t,),
        in_specs=[pl.BlockSpec((t, BRANCH_W), lambda i: (i, 0)),
                  pl.BlockSpec((POOL_HALO, BRANCH_W), lambda i: (jnp.maximum(i * hb - 1, 0), 0)),
                  pl.BlockSpec((t, BRANCH_W), lambda i: (i, 1)),
                  pl.BlockSpec((POOL_GROUPS, POOL_GW, POOL_GW), lambda i: (0, 0, 0)),
                  pl.BlockSpec((1, BRANCH_W), lambda i: (0, 0))],
        out_specs=(pl.BlockSpec((t, BRANCH_W), lambda i: (i, 0)), pl.BlockSpec((BRANCH_W, t), lambda i: (0, i))),
        scratch_shapes=[pltpu.VMEM((t + POOL_HALO, BRANCH_W), F32)],
        compiler_params=_cparams(("parallel",)), name="pool_fwd")(p1, p1, p1, pool_w, pool_scale)


def _pool_bwd(p1, dy, pool_w, pool_scale, *, t=256):
    s = p1.shape[0]
    t = _tile(s, t, 128)
    hb = t // POOL_HALO
    nt = s // t
    last_hb = s // POOL_HALO - 1

    def body(pv_ref, halo_ref, pz_ref, pzn_ref, dy_ref, dyn_ref, pw_ref, sc_ref, d_ref, gw_ref, gs_ref, scr, scr2, scr3):
        i = pl.program_id(0)
        mixed = _pool_mixed(scr, pv_ref[...], halo_ref[...], i == 0, i * t, t)
        scale = sc_ref[...]
        pz = pz_ref[...]
        dy = dy_ref[...]
        raw = jnp.concatenate([_dot(mixed[g].astype(BF16), pw_ref[g]) for g in range(POOL_GROUPS)], axis=1)
        d_pool = dy * _silu(pz)
        d_ref[:, BRANCH_W:2 * BRANCH_W] = (dy * raw * scale * _dsilu(pz)).astype(BF16)
        gs_part = jnp.sum(d_pool * raw, axis=0, keepdims=True)
        scr2[0:t, :] = d_pool * scale
        scr2[t:t + POOL_HALO, :] = jnp.where(i == nt - 1, 0.0, dyn_ref[...] * _silu(pzn_ref[...]) * scale)
        row = i * t + lax.broadcasted_iota(jnp.int32, (t + POOL_HALO, 1), 0)
        gw_parts = []
        for g in range(POOL_GROUPS):
            w = 2 ** (g + 1)
            sl = slice(g * POOL_GW, (g + 1) * POOL_GW)
            do_g = scr2[:, sl].astype(BF16)
            dm = _dot_nt(do_g, pw_ref[g])
            gw_parts.append(_dot_tn(mixed[g].astype(BF16), do_g[0:t, :]))
            cnt = jnp.minimum(row + 1, w).astype(F32)
            acc = dm / cnt
            sh = 1
            while sh < w:
                acc = acc + pltpu.roll(acc, t + POOL_HALO - sh, 0)
                sh *= 2
            scr3[:, sl] = acc - dm
        d_ref[:, 0:BRANCH_W] = scr3[0:t, :].astype(BF16)

        @pl.when(i == 0)
        def _():
            for g in range(POOL_GROUPS):
                gw_ref[g] = gw_parts[g]
            gs_ref[...] = gs_part

        @pl.when(i > 0)
        def _():
            for g in range(POOL_GROUPS):
                gw_ref[g] += gw_parts[g]
            gs_ref[...] += gs_part

    tile = lambda col: pl.BlockSpec((t, BRANCH_W), lambda i: (i, col))
    nxt = lambda col: pl.BlockSpec((POOL_HALO, BRANCH_W), lambda i: (jnp.minimum((i + 1) * hb, last_hb), col))
    return pl.pallas_call(
        body,
        out_shape=(jax.ShapeDtypeStruct((s, 2 * BRANCH_W), BF16),
                   jax.ShapeDtypeStruct((POOL_GROUPS, POOL_GW, POOL_GW), F32),
                   jax.ShapeDtypeStruct((1, BRANCH_W), F32)),
        grid=(nt,),
        in_specs=[tile(0), pl.BlockSpec((POOL_HALO, BRANCH_W), lambda i: (jnp.maximum(i * hb - 1, 0), 0)),
                  tile(1), nxt(1),
                  pl.BlockSpec((None, t, BRANCH_W), lambda i: (0, i, 0)),
                  pl.BlockSpec((None, POOL_HALO, BRANCH_W), lambda i: (0, jnp.minimum((i + 1) * hb, last_hb), 0)),
                  pl.BlockSpec((POOL_GROUPS, POOL_GW, POOL_GW), lambda i: (0, 0, 0)),
                  pl.BlockSpec((1, BRANCH_W), lambda i: (0, 0))],
        out_specs=(pl.BlockSpec((t, 2 * BRANCH_W), lambda i: (i, 0)),
                   pl.BlockSpec((POOL_GROUPS, POOL_GW, POOL_GW), lambda i: (0, 0, 0)),
                   pl.BlockSpec((1, BRANCH_W), lambda i: (0, 0))),
        scratch_shapes=[pltpu.VMEM((t + POOL_HALO, BRANCH_W), F32)] * 3,
        compiler_params=_cparams(("arbitrary",)), name="pool_bwd")(p1, p1, p1, p1, dy, dy, pool_w, pool_scale)


CONV_COL = 4


def _conv_taps(scr, u, uh, first, t):
    scr[0:CONV_HALO, :] = jnp.where(first, 0.0, uh)
    scr[CONV_HALO:CONV_HALO + t, :] = u
    e = scr[...]
    u1 = pltpu.roll(e, 1, 0)[CONV_HALO:CONV_HALO + t, :]
    u2 = pltpu.roll(e, 2, 0)[CONV_HALO:CONV_HALO + t, :]
    return u2, u1, u


def _conv_fwd(p1, conv_w, *, t=256):
    s = p1.shape[0]
    t = _tile(s, t, 128)
    hb = t // CONV_HALO

    def body(cb_ref, cc_ref, cx_ref, cz_ref, cch_ref, cxh_ref, w_ref, y_ref, yt_ref, scr):
        i = pl.program_id(0)
        u0, u1, u2 = _conv_taps(scr, cc_ref[...] * cx_ref[...], cch_ref[...] * cxh_ref[...], i == 0, t)
        w = w_ref[...]
        y = (w[0:1, :] * u0 + w[1:2, :] * u1 + w[2:3, :] * u2) * cb_ref[...] * _silu(cz_ref[...])
        y_ref[...] = y.astype(BF16)
        yt_ref[...] = y.T.astype(BF16)

    tile = lambda col: pl.BlockSpec((t, BRANCH_W), lambda i: (i, CONV_COL + col))
    prev = lambda col: pl.BlockSpec((CONV_HALO, BRANCH_W), lambda i: (jnp.maximum(i * hb - 1, 0), CONV_COL + col))
    return pl.pallas_call(
        body, out_shape=(jax.ShapeDtypeStruct((s, BRANCH_W), BF16), jax.ShapeDtypeStruct((BRANCH_W, s), BF16)),
        grid=(s // t,),
        in_specs=[tile(0), tile(1), tile(2), tile(3), prev(1), prev(2), pl.BlockSpec((3, BRANCH_W), lambda i: (0, 0))],
        out_specs=(pl.BlockSpec((t, BRANCH_W), lambda i: (i, 0)), pl.BlockSpec((BRANCH_W, t), lambda i: (0, i))),
        scratch_shapes=[pltpu.VMEM((t + CONV_HALO, BRANCH_W), F32)],
        compiler_params=_cparams(("parallel",)), name="conv_fwd")(p1, p1, p1, p1, p1, p1, conv_w)


def _conv_bwd(p1, dy, conv_w, *, t=256):
    s = p1.shape[0]
    t = _tile(s, t, 128)
    hb = t // CONV_HALO
    nt = s // t
    last_hb = s // CONV_HALO - 1

    def body(cb_ref, cc_ref, cx_ref, cz_ref, cch_ref, cxh_ref, cbn_ref, czn_ref, dy_ref, dyn_ref, w_ref,
             d_ref, g0_ref, g1_ref, g2_ref, scr, scr2):
        i = pl.program_id(0)
        cb, cc, cx, cz = cb_ref[...], cc_ref[...], cx_ref[...], cz_ref[...]
        u0, u1, u2 = _conv_taps(scr, cc * cx, cch_ref[...] * cxh_ref[...], i == 0, t)
        w = w_ref[...]
        w0, w1, w2 = w[0:1, :], w[1:2, :], w[2:3, :]
        y = w0 * u0 + w1 * u1 + w2 * u2
        dy = dy_ref[...]
        sz = _silu(cz)
        d_ref[:, 0:BRANCH_W] = (dy * sz * y).astype(BF16)
        d_ref[:, 3 * BRANCH_W:4 * BRANCH_W] = (dy * cb * y * _dsilu(cz)).astype(BF16)
        d_y = dy * sz * cb
        parts = [jnp.sum(d_y * u, axis=0, keepdims=True) for u in (u0, u1, u2)]
        scr2[0:t, :] = d_y
        scr2[t:t + CONV_HALO, :] = jnp.where(i == nt - 1, 0.0, dyn_ref[...] * _silu(czn_ref[...]) * cbn_ref[...])
        e = scr2[...]
        n = t + CONV_HALO
        du = (w2 * e + w1 * pltpu.roll(e, n - 1, 0) + w0 * pltpu.roll(e, n - 2, 0))[0:t, :]
        d_ref[:, BRANCH_W:2 * BRANCH_W] = (du * cx).astype(BF16)
        d_ref[:, 2 * BRANCH_W:3 * BRANCH_W] = (du * cc).astype(BF16)

        @pl.when(i == 0)
        def _():
            g0_ref[...] = parts[0]
            g1_ref[...] = parts[1]
            g2_ref[...] = parts[2]

        @pl.when(i > 0)
        def _():
            g0_ref[...] += parts[0]
            g1_ref[...] += parts[1]
            g2_ref[...] += parts[2]

    tile = lambda col: pl.BlockSpec((t, BRANCH_W), lambda i: (i, CONV_COL + col))
    prev = lambda col: pl.BlockSpec((CONV_HALO, BRANCH_W), lambda i: (jnp.maximum(i * hb - 1, 0), CONV_COL + col))
    nxt = lambda col: pl.BlockSpec((CONV_HALO, BRANCH_W), lambda i: (jnp.minimum((i + 1) * hb, last_hb), CONV_COL + col))
    vec = pl.BlockSpec((1, BRANCH_W), lambda i: (0, 0))
    gshape = jax.ShapeDtypeStruct((1, BRANCH_W), F32)
    return pl.pallas_call(
        body, out_shape=(jax.ShapeDtypeStruct((s, 4 * BRANCH_W), BF16), gshape, gshape, gshape),
        grid=(nt,),
        in_specs=[tile(0), tile(1), tile(2), tile(3), prev(1), prev(2), nxt(0), nxt(3),
                  pl.BlockSpec((None, t, BRANCH_W), lambda i: (2, i, 0)),
                  pl.BlockSpec((None, CONV_HALO, BRANCH_W), lambda i: (2, jnp.minimum((i + 1) * hb, last_hb), 0)),
                  pl.BlockSpec((3, BRANCH_W), lambda i: (0, 0))],
        out_specs=(pl.BlockSpec((t, 4 * BRANCH_W), lambda i: (i, 0)), vec, vec, vec),
        scratch_shapes=[pltpu.VMEM((t + CONV_HALO, BRANCH_W), F32)] * 2,
        compiler_params=_cparams(("arbitrary",)), name="conv_bwd")(p1, p1, p1, p1, p1, p1, p1, p1, dy, dy, conv_w)


def _mla_prep_fwd(p1, kr, ctab, stab, qa_g, kva_g, w_uq, w_ukv, q_g, k_g, *, t=256):
    s = p1.shape[0]
    t = _tile(s, t, 128)
    ql, kvl = qa_g.shape[1], kva_g.shape[1]
    assert ql == kvl and 2048 % ql == 0
    cq_blk = 2048 // ql

    def body(cq_ref, ckv_ref, kr_ref, c_ref, s_ref, qag_ref, kvag_ref, wuq_ref, wukv_ref, qg_ref, kg_ref,
             qf_ref, kf_ref, v_ref, vt_ref):
        q_raw = _dot(_rms(cq_ref[...], qag_ref[...], ql).astype(BF16), wuq_ref[...])
        kv_raw = _dot(_rms(ckv_ref[...], kvag_ref[...], kvl).astype(BF16), wukv_ref[...])
        krp = kr_ref[...]
        ct, st = c_ref[...], s_ref[...]
        for h in range(MLA_HEADS):
            qh = q_raw[:, h * HEAD_PAD:(h + 1) * HEAD_PAD]
            qf_ref[h] = _rope(_rms(qh, qg_ref[...], QK_HEAD), ct, st).astype(BF16)
            kh = jnp.concatenate([kv_raw[:, h * HEAD_PAD:h * HEAD_PAD + QK_NOPE], krp], axis=1)
            kf_ref[h] = _rope(_rms(kh, kg_ref[...], QK_HEAD), ct, st).astype(BF16)
            vh = kv_raw[:, h * HEAD_PAD + QK_NOPE:(h + 1) * HEAD_PAD]
            v_ref[h] = vh.astype(BF16)
            vt_ref[h] = vh.T.astype(BF16)

    full = lambda shp: pl.BlockSpec(shp, lambda i: tuple(0 for _ in shp))
    return pl.pallas_call(
        body,
        out_shape=(jax.ShapeDtypeStruct((MLA_HEADS, s, HEAD_PAD), BF16), jax.ShapeDtypeStruct((MLA_HEADS, s, HEAD_PAD), BF16),
                   jax.ShapeDtypeStruct((MLA_HEADS, s, V_HEAD), BF16), jax.ShapeDtypeStruct((MLA_HEADS, V_HEAD, s), BF16)),
        grid=(s // t,),
        in_specs=[pl.BlockSpec((t, ql), lambda i: (i, cq_blk)), pl.BlockSpec((t, kvl), lambda i: (i, cq_blk + 1)),
                  pl.BlockSpec((t, 128), lambda i: (i, 0)),
                  pl.BlockSpec((t, HEAD_PAD), lambda i: (i, 0)), pl.BlockSpec((t, HEAD_PAD), lambda i: (i, 0)),
                  full((1, ql)), full((1, kvl)), full(w_uq.shape), full(w_ukv.shape), full((1, HEAD_PAD)), full((1, HEAD_PAD))],
        out_specs=(pl.BlockSpec((MLA_HEADS, t, HEAD_PAD), lambda i: (0, i, 0)),
                   pl.BlockSpec((MLA_HEADS, t, HEAD_PAD), lambda i: (0, i, 0)),
                   pl.BlockSpec((MLA_HEADS, t, V_HEAD), lambda i: (0, i, 0)),
                   pl.BlockSpec((MLA_HEADS, V_HEAD, t), lambda i: (0, 0, i))),
        compiler_params=_cparams(("parallel",)), name="mla_prep_fwd")(
            p1, p1, kr, ctab, stab, qa_g, kva_g, w_uq, w_ukv, q_g, k_g)


def _mla_prep_bwd(p1, kr, ctab, stab, qa_g, kva_g, w_uq, w_ukv, q_g, k_g, dqf, dkf, dv, *, t=256):
    s = p1.shape[0]
    t = _tile(s, t, 128)
    ql, kvl = qa_g.shape[1], kva_g.shape[1]
    cq_blk = 2048 // ql
    nq = MLA_HEADS * HEAD_PAD

    def body(cq_ref, ckv_ref, kr_ref, c_ref, s_ref, qag_ref, kvag_ref, wuq_ref, wukv_ref, qg_ref, kg_ref,
             dqf_ref, dkf_ref, dv_ref,
             dc_ref, dkr_ref, dqraw_ref, dkvraw_ref, cqnt_ref, ckvnt_ref, gqa_ref, gkva_ref, gqg_ref, gkg_ref):
        i = pl.program_id(0)
        cq, ckv = cq_ref[...], ckv_ref[...]
        cqn = _rms(cq, qag_ref[...], ql)
        ckvn = _rms(ckv, kvag_ref[...], kvl)
        cqnt_ref[...] = cqn.T.astype(BF16)
        ckvnt_ref[...] = ckvn.T.astype(BF16)
        q_raw = _dot(cqn.astype(BF16), wuq_ref[...])
        kv_raw = _dot(ckvn.astype(BF16), wukv_ref[...])
        krp = kr_ref[...]
        ct, st = c_ref[...], s_ref[...]
        gqg = jnp.zeros((1, HEAD_PAD), F32)
        gkg = jnp.zeros((1, HEAD_PAD), F32)
        dkr = jnp.zeros((t, HEAD_PAD - QK_NOPE), F32)
        dq_parts, dkv_parts = [], []
        for h in range(MLA_HEADS):
            qh = q_raw[:, h * HEAD_PAD:(h + 1) * HEAD_PAD]
            dx, dg = _rms_bwd(qh, qg_ref[...], QK_HEAD, _rope_bwd(dqf_ref[h], ct, st))
            gqg = gqg + jnp.sum(dg, axis=0, keepdims=True)
            dq_parts.append(dx)
            kh = jnp.concatenate([kv_raw[:, h * HEAD_PAD:h * HEAD_PAD + QK_NOPE], krp], axis=1)
            dx, dg = _rms_bwd(kh, kg_ref[...], QK_HEAD, _rope_bwd(dkf_ref[h], ct, st))
            gkg = gkg + jnp.sum(dg, axis=0, keepdims=True)
            dkr = dkr + dx[:, QK_NOPE:HEAD_PAD]
            dkv_parts += [dx[:, 0:QK_NOPE], dv_ref[h]]
        dq_raw = jnp.concatenate(dq_parts, axis=1).astype(BF16)
        dkv_raw = jnp.concatenate(dkv_parts, axis=1).astype(BF16)
        dqraw_ref[...] = dq_raw
        dkvraw_ref[...] = dkv_raw
        dkr_ref[...] = dkr.astype(BF16)
        dcq, gqa = _rms_bwd(cq, qag_ref[...], ql, _dot_nt(dq_raw, wuq_ref[...]))
        dckv, gkva = _rms_bwd(ckv, kvag_ref[...], kvl, _dot_nt(dkv_raw, wukv_ref[...]))
        dc_ref[:, 0:ql] = dcq.astype(BF16)
        dc_ref[:, ql:ql + kvl] = dckv.astype(BF16)
        gqa = jnp.sum(gqa, axis=0, keepdims=True)
        gkva = jnp.sum(gkva, axis=0, keepdims=True)

        @pl.when(i == 0)
        def _():
            gqa_ref[...] = gqa
            gkva_ref[...] = gkva
            gqg_ref[...] = gqg
            gkg_ref[...] = gkg

        @pl.when(i > 0)
        def _():
            gqa_ref[...] += gqa
            gkva_ref[...] += gkva
            gqg_ref[...] += gqg
            gkg_ref[...] += gkg

    full = lambda shp: pl.BlockSpec(shp, lambda i: tuple(0 for _ in shp))
    hblk = lambda w: pl.BlockSpec((MLA_HEADS, t, w), lambda i: (0, i, 0))
    sds = jax.ShapeDtypeStruct
    return pl.pallas_call(
        body,
        out_shape=(sds((s, ql + kvl), BF16), sds((s, 128), BF16), sds((s, nq), BF16), sds((s, nq), BF16),
                   sds((ql, s), BF16), sds((kvl, s), BF16),
                   sds((1, ql), F32), sds((1, kvl), F32), sds((1, HEAD_PAD), F32), sds((1, HEAD_PAD), F32)),
        grid=(s // t,),
        in_specs=[pl.BlockSpec((t, ql), lambda i: (i, cq_blk)), pl.BlockSpec((t, kvl), lambda i: (i, cq_blk + 1)),
                  pl.BlockSpec((t, 128), lambda i: (i, 0)),
                  pl.BlockSpec((t, HEAD_PAD), lambda i: (i, 0)), pl.BlockSpec((t, HEAD_PAD), lambda i: (i, 0)),
                  full((1, ql)), full((1, kvl)), full(w_uq.shape), full(w_ukv.shape), full((1, HEAD_PAD)), full((1, HEAD_PAD)),
                  hblk(HEAD_PAD), hblk(HEAD_PAD), hblk(V_HEAD)],
        out_specs=(pl.BlockSpec((t, ql + kvl), lambda i: (i, 0)), pl.BlockSpec((t, 128), lambda i: (i, 0)),
                   pl.BlockSpec((t, nq), lambda i: (i, 0)), pl.BlockSpec((t, nq), lambda i: (i, 0)),
                   pl.BlockSpec((ql, t), lambda i: (0, i)), pl.BlockSpec((kvl, t), lambda i: (0, i)),
                   full((1, ql)), full((1, kvl)), full((1, HEAD_PAD)), full((1, HEAD_PAD))),
        compiler_params=_cparams(("arbitrary",)), name="mla_prep_bwd")(
            p1, p1, kr, ctab, stab, qa_g, kva_g, w_uq, w_ukv, q_g, k_g, dqf, dkf, dv)


MZ_BLK128 = 3072 // 128
ATT_SCALE = QK_HEAD ** -0.5


HPS = 2


def _causal_pairs(nq, by_query):
    if by_query:
        prs = [(qi, ki) for qi in range(nq) for ki in range(qi + 1)]
    else:
        prs = [(qi, ki) for ki in range(nq) for qi in range(ki, nq)]
    return (jnp.asarray([p[0] for p in prs], jnp.int32), jnp.asarray([p[1] for p in prs], jnp.int32), len(prs))


def _comm_hooks(comm, cins, couts, csems, first, middle, last):
    ph = comm["phases"]
    assert len(ph) in (2, 3)

    def at(pred, phase):
        @pl.when(pred)
        def _():
            phase(cins, couts, *csems)

    return (lambda: at(first, ph[0])), (lambda: [at(middle, ph[1])] if len(ph) == 3 else None), (lambda: at(last, ph[-1]))


def _comm_extras(comm):
    if comm is None:
        return [], [], [], ()
    sems = [pltpu.SemaphoreType.DMA((comm["nsem"],)), pltpu.SemaphoreType.DMA((comm["nsem"],))]
    return [ANY] * len(comm["ins"]), [ANY] * len(comm["outs"]), sems, tuple(comm["outs"])


def _attn_fwd(qf, kf, vt, p1, *, tq, comm=None):
    nh, s, _ = qf.shape
    nq = s // tq
    qtab, ktab, npairs = _causal_pairs(nq, True)
    wv = HPS * V_HEAD
    c_in, c_out, c_sem, c_shapes = _comm_extras(comm)
    nci, nco = len(c_in), len(c_out)

    def body(*refs):
        qt_ref, kt_ref, q_ref, k_ref, vt_ref, mz_ref = refs[:6]
        y_ref, yt_ref, o_ref, lse_ref, lset_ref = refs[6 + nci:11 + nci]
        m_scr, l_scr, acc_scr = refs[11 + nci + nco:14 + nci + nco]
        pr = pl.program_id(1)
        qi, ki = qt_ref[pr], kt_ref[pr]
        if comm is not None:
            hg = pl.program_id(0)
            last_hg = nh // HPS - 1
            c_first, c_mid, c_last = _comm_hooks(comm, refs[6:6 + nci], refs[11 + nci:11 + nci + nco], refs[14 + nci + nco:],
                                                 (hg == 0) & (pr == 0), (hg == last_hg) & (pr == 0),
                                                 (hg == last_hg) & (pr == npairs - 1))
            c_first()

        @pl.when(ki == 0)
        def _():
            m_scr[...] = jnp.full((HPS, 1, tq), NEG, F32)
            l_scr[...] = jnp.zeros((HPS, 1, tq), F32)
            acc_scr[...] = jnp.zeros((HPS, V_HEAD, tq), F32)

        def step(diagonal):
            for u in range(HPS):
                st = _dot_nt(k_ref[u], q_ref[u]) * ATT_SCALE
                if diagonal:
                    r = lax.broadcasted_iota(jnp.int32, (tq, tq), 0)
                    c = lax.broadcasted_iota(jnp.int32, (tq, tq), 1)
                    st = jnp.where(r <= c, st, NEG)
                m_old = m_scr[u]
                m_new = jnp.maximum(m_old, jnp.max(st, axis=0, keepdims=True))
                alpha = jnp.exp(m_old - m_new)
                pt = jnp.exp(st - m_new)
                l_scr[u] = alpha * l_scr[u] + jnp.sum(pt, axis=0, keepdims=True)
                acc_scr[u] = alpha * acc_scr[u] + _dot(vt_ref[u], pt.astype(BF16))
                m_scr[u] = m_new

        @pl.when(ki < qi)
        def _():
            step(False)

        @pl.when(ki == qi)
        def _():
            step(True)
            outs = []
            for u in range(HPS):
                l = l_scr[u]
                outs.append((acc_scr[u] / l).T)
                lset = m_scr[u] + jnp.log(l)
                lset_ref[u] = lset
                lse_ref[u] = jnp.broadcast_to(lset, (128, tq)).T[:, 0:1]
            o = jnp.concatenate(outs, axis=1)
            o_ref[...] = o
            y = o * _silu(mz_ref[...])
            y_ref[...] = y.astype(BF16)
            yt_ref[...] = y.T.astype(BF16)

        if comm is not None:
            c_mid()
            c_last()

    sds = jax.ShapeDtypeStruct
    gs = pltpu.PrefetchScalarGridSpec(
        num_scalar_prefetch=2, grid=(nh // HPS, npairs),
        in_specs=[pl.BlockSpec((HPS, tq, HEAD_PAD), lambda h, p, qt, kt: (h, qt[p], 0)),
                  pl.BlockSpec((HPS, tq, HEAD_PAD), lambda h, p, qt, kt: (h, kt[p], 0)),
                  pl.BlockSpec((HPS, V_HEAD, tq), lambda h, p, qt, kt: (h, 0, kt[p])),
                  pl.BlockSpec((tq, wv), lambda h, p, qt, kt: (qt[p], MZ_BLK128 // HPS + h))] + c_in,
        out_specs=(pl.BlockSpec((tq, wv), lambda h, p, qt, kt: (qt[p], h)),
                   pl.BlockSpec((wv, tq), lambda h, p, qt, kt: (h, qt[p])),
                   pl.BlockSpec((tq, wv), lambda h, p, qt, kt: (qt[p], h)),
                   pl.BlockSpec((HPS, tq, 1), lambda h, p, qt, kt: (h, qt[p], 0)),
                   pl.BlockSpec((HPS, 1, tq), lambda h, p, qt, kt: (h, 0, qt[p]))) + tuple(c_out),
        scratch_shapes=[pltpu.VMEM((HPS, 1, tq), F32), pltpu.VMEM((HPS, 1, tq), F32), pltpu.VMEM((HPS, V_HEAD, tq), F32)] + c_sem)
    sem = ("parallel", "arbitrary") if comm is None else ("arbitrary", "arbitrary")
    res = pl.pallas_call(
        body,
        out_shape=(sds((s, BRANCH_W), BF16), sds((BRANCH_W, s), BF16), sds((s, BRANCH_W), F32),
                   sds((nh, s, 1), F32), sds((nh, 1, s), F32)) + c_shapes,
        grid_spec=gs, compiler_params=_cparams(sem),
        name="attn_fwd" if comm is None else "attn_fwd_with_gather")(qtab, ktab, qf, kf, vt, p1, *(comm["ins"] if comm else ()))
    return res[:5], list(res[5:])


def _attn_bwd_pre(dy4, o, p1, *, t=256):
    s = o.shape[0]
    t = _tile(s, t, 128)

    def body(dy_ref, o_ref, mz_ref, do_ref, dmz_ref, dl_ref, dlt_ref):
        dy, o_, mz = dy_ref[...], o_ref[...], mz_ref[...]
        do = dy * _silu(mz)
        do_ref[...] = do.astype(BF16)
        dmz_ref[...] = (dy * o_ * _dsilu(mz)).astype(BF16)
        prod = do * o_
        for h in range(MLA_HEADS):
            dl = jnp.sum(prod[:, h * V_HEAD:(h + 1) * V_HEAD], axis=-1, keepdims=True)
            dl_ref[h] = dl
            dlt_ref[h] = jnp.broadcast_to(dl, (t, 128)).T[0:1, :]

    sds = jax.ShapeDtypeStruct
    return pl.pallas_call(
        body,
        out_shape=(sds((s, BRANCH_W), BF16), sds((s, BRANCH_W), BF16), sds((MLA_HEADS, s, 1), F32), sds((MLA_HEADS, 1, s), F32)),
        grid=(s // t,),
        in_specs=[pl.BlockSpec((None, t, BRANCH_W), lambda i: (1, i, 0)), pl.BlockSpec((t, BRANCH_W), lambda i: (i, 0)),
                  pl.BlockSpec((t, BRANCH_W), lambda i: (i, 3))],
        out_specs=(pl.BlockSpec((t, BRANCH_W), lambda i: (i, 0)), pl.BlockSpec((t, BRANCH_W), lambda i: (i, 0)),
                   pl.BlockSpec((MLA_HEADS, t, 1), lambda i: (0, i, 0)), pl.BlockSpec((MLA_HEADS, 1, t), lambda i: (0, 0, i))),
        compiler_params=_cparams(("parallel",)), name="attn_bwd_pre")(dy4, o, p1)


def _attn_bwd_dq(qf, kf, vv, do, lse, delta, *, tq, comm=None):
    nh, s, _ = qf.shape
    nq = s // tq
    qtab, ktab, npairs = _causal_pairs(nq, True)
    wv = HPS * V_HEAD
    c_in, c_out, c_sem, c_shapes = _comm_extras(comm)
    nci, nco = len(c_in), len(c_out)

    def body(*refs):
        qt_ref, kt_ref, q_ref, k_ref, v_ref, do_ref, lse_ref, dl_ref = refs[:8]
        dq_ref = refs[8 + nci]
        acc_scr = refs[9 + nci + nco]
        pr = pl.program_id(1)
        qi, ki = qt_ref[pr], kt_ref[pr]
        if comm is not None:
            hg = pl.program_id(0)
            last_hg = nh // HPS - 1
            c_first, c_mid, c_last = _comm_hooks(comm, refs[8:8 + nci], refs[9 + nci:9 + nci + nco], refs[10 + nci + nco:],
                                                 (hg == 0) & (pr == 0), (hg == last_hg) & (pr == 0),
                                                 (hg == last_hg) & (pr == npairs - 1))
            c_first()

        @pl.when(ki == 0)
        def _():
            acc_scr[...] = jnp.zeros((HPS, tq, HEAD_PAD), F32)

        def step(diagonal):
            for u in range(HPS):
                k = k_ref[u]
                sc = _dot_nt(q_ref[u], k) * ATT_SCALE
                p = jnp.exp(sc - lse_ref[u])
                if diagonal:
                    r = lax.broadcasted_iota(jnp.int32, (tq, tq), 0)
                    c = lax.broadcasted_iota(jnp.int32, (tq, tq), 1)
                    p = jnp.where(c <= r, p, 0.0)
                dp = _dot_nt(do_ref[:, u * V_HEAD:(u + 1) * V_HEAD], v_ref[u])
                ds = p * (dp - dl_ref[u]) * ATT_SCALE
                acc_scr[u] += _dot(ds.astype(BF16), k)

        @pl.when(ki < qi)
        def _():
            step(False)

        @pl.when(ki == qi)
        def _():
            step(True)
            dq_ref[...] = acc_scr[...]

        if comm is not None:
            c_mid()
            c_last()

    gs = pltpu.PrefetchScalarGridSpec(
        num_scalar_prefetch=2, grid=(nh // HPS, npairs),
        in_specs=[pl.BlockSpec((HPS, tq, HEAD_PAD), lambda h, p, qt, kt: (h, qt[p], 0)),
                  pl.BlockSpec((HPS, tq, HEAD_PAD), lambda h, p, qt, kt: (h, kt[p], 0)),
                  pl.BlockSpec((HPS, tq, V_HEAD), lambda h, p, qt, kt: (h, kt[p], 0)),
                  pl.BlockSpec((tq, wv), lambda h, p, qt, kt: (qt[p], h)),
                  pl.BlockSpec((HPS, tq, 1), lambda h, p, qt, kt: (h, qt[p], 0)),
                  pl.BlockSpec((HPS, tq, 1), lambda h, p, qt, kt: (h, qt[p], 0))] + c_in,
        out_specs=(pl.BlockSpec((HPS, tq, HEAD_PAD), lambda h, p, qt, kt: (h, qt[p], 0)),) + tuple(c_out),
        scratch_shapes=[pltpu.VMEM((HPS, tq, HEAD_PAD), F32)] + c_sem)
    sem = ("parallel", "arbitrary") if comm is None else ("arbitrary", "arbitrary")
    res = pl.pallas_call(
        body, out_shape=(jax.ShapeDtypeStruct((nh, s, HEAD_PAD), F32),) + c_shapes, grid_spec=gs,
        compiler_params=_cparams(sem), name="attn_bwd_dq" if comm is None else "attn_bwd_dq_with_to_owner")(
            qtab, ktab, qf, kf, vv, do, lse, delta, *(comm["ins"] if comm else ()))
    return res[0], list(res[1:])


def _attn_bwd_dkv(qf, kf, vv, do, lset, deltat, *, tq, comm=None):
    nh, s, _ = qf.shape
    nq = s // tq
    qtab, ktab, npairs = _causal_pairs(nq, False)
    wv = HPS * V_HEAD
    c_in, c_out, c_sem, c_shapes = _comm_extras(comm)
    nci, nco = len(c_in), len(c_out)

    def body(*refs):
        qt_ref, kt_ref, k_ref, v_ref, q_ref, do_ref, lse_ref, dl_ref = refs[:8]
        dk_ref, dv_ref = refs[8 + nci:10 + nci]
        dk_scr, dv_scr = refs[10 + nci + nco:12 + nci + nco]
        pr = pl.program_id(1)
        qi, ki = qt_ref[pr], kt_ref[pr]
        if comm is not None:
            hg = pl.program_id(0)
            last_hg = nh // HPS - 1
            c_first, c_mid, c_last = _comm_hooks(comm, refs[8:8 + nci], refs[10 + nci:10 + nci + nco], refs[12 + nci + nco:],
                                                 (hg == 0) & (pr == 0), (hg == last_hg) & (pr == 0),
                                                 (hg == last_hg) & (pr == npairs - 1))
            c_first()

        def step(diagonal):
            for u in range(HPS):
                q = q_ref[u]
                do_ = do_ref[:, u * V_HEAD:(u + 1) * V_HEAD]
                st = _dot_nt(k_ref[u], q) * ATT_SCALE
                pt = jnp.exp(st - lse_ref[u])
                if diagonal:
                    r = lax.broadcasted_iota(jnp.int32, (tq, tq), 0)
                    c = lax.broadcasted_iota(jnp.int32, (tq, tq), 1)
                    pt = jnp.where(r <= c, pt, 0.0)
                dpt = _dot_nt(v_ref[u], do_)
                dst = pt * (dpt - dl_ref[u]) * ATT_SCALE
                if diagonal:
                    dv_scr[u] = _dot(pt.astype(BF16), do_)
                    dk_scr[u] = _dot(dst.astype(BF16), q)
                else:
                    dv_scr[u] += _dot(pt.astype(BF16), do_)
                    dk_scr[u] += _dot(dst.astype(BF16), q)

        @pl.when(qi == ki)
        def _():
            step(True)

        @pl.when(qi > ki)
        def _():
            step(False)

        @pl.when(qi == nq - 1)
        def _():
            dk_ref[...] = dk_scr[...]
            dv_ref[...] = dv_scr[...]

        if comm is not None:
            c_mid()
            c_last()

    sds = jax.ShapeDtypeStruct
    gs = pltpu.PrefetchScalarGridSpec(
        num_scalar_prefetch=2, grid=(nh // HPS, npairs),
        in_specs=[pl.BlockSpec((HPS, tq, HEAD_PAD), lambda h, p, qt, kt: (h, kt[p], 0)),
                  pl.BlockSpec((HPS, tq, V_HEAD), lambda h, p, qt, kt: (h, kt[p], 0)),
                  pl.BlockSpec((HPS, tq, HEAD_PAD), lambda h, p, qt, kt: (h, qt[p], 0)),
                  pl.BlockSpec((tq, wv), lambda h, p, qt, kt: (qt[p], h)),
                  pl.BlockSpec((HPS, 1, tq), lambda h, p, qt, kt: (h, 0, qt[p])),
                  pl.BlockSpec((HPS, 1, tq), lambda h, p, qt, kt: (h, 0, qt[p]))] + c_in,
        out_specs=(pl.BlockSpec((HPS, tq, HEAD_PAD), lambda h, p, qt, kt: (h, kt[p], 0)),
                   pl.BlockSpec((HPS, tq, V_HEAD), lambda h, p, qt, kt: (h, kt[p], 0))) + tuple(c_out),
        scratch_shapes=[pltpu.VMEM((HPS, tq, HEAD_PAD), F32), pltpu.VMEM((HPS, tq, V_HEAD), F32)] + c_sem)
    sem = ("parallel", "arbitrary") if comm is None else ("arbitrary", "arbitrary")
    res = pl.pallas_call(
        body, out_shape=(sds((nh, s, HEAD_PAD), F32), sds((nh, s, V_HEAD), F32)) + c_shapes, grid_spec=gs,
        compiler_params=_cparams(sem), name="attn_bwd_dkv" if comm is None else "attn_bwd_dkv_with_exchange")(
            qtab, ktab, kf, vv, qf, do, lset, deltat, *(comm["ins"] if comm else ()))
    return res[0], res[1], list(res[2:])


XATT_SCALE = XATTN_HEAD_DIM ** -0.5
XQ_COL = 8


def _memkv_prep(mem_kv, k_g):
    m = mem_kv.shape[0]

    def body(kv_ref, g_ref, k_ref, v_ref):
        for h in range(XATTN_HEADS):
            sl = slice(h * XATTN_HEAD_DIM, (h + 1) * XATTN_HEAD_DIM)
            k_ref[:, sl] = _rms(kv_ref[:, sl], g_ref[...], XATTN_HEAD_DIM).astype(BF16)
        v_ref[...] = kv_ref[:, BRANCH_W:2 * BRANCH_W].astype(BF16)

    sds = jax.ShapeDtypeStruct
    return pl.pallas_call(body, out_shape=(sds((m, BRANCH_W), BF16), sds((m, BRANCH_W), BF16)),
                          compiler_params=_cparams(), name="memkv_prep")(mem_kv, k_g)


def _memkv_prep_bwd(mem_kv, k_g, dk, dv):
    m = mem_kv.shape[0]

    def body(kv_ref, g_ref, dk_ref, dv_ref, d_ref, gk_ref):
        gk = jnp.zeros((1, XATTN_HEAD_DIM), F32)
        for h in range(XATTN_HEADS):
            sl = slice(h * XATTN_HEAD_DIM, (h + 1) * XATTN_HEAD_DIM)
            dx, dg = _rms_bwd(kv_ref[:, sl], g_ref[...], XATTN_HEAD_DIM, dk_ref[:, sl])
            d_ref[:, sl] = dx.astype(BF16)
            gk = gk + jnp.sum(dg, axis=0, keepdims=True)
        d_ref[:, BRANCH_W:2 * BRANCH_W] = dv_ref[...].astype(BF16)
        gk_ref[...] = gk

    sds = jax.ShapeDtypeStruct
    return pl.pallas_call(body, out_shape=(sds((m, 2 * BRANCH_W), BF16), sds((1, XATTN_HEAD_DIM), F32)),
                          compiler_params=_cparams(), name="memkv_prep_bwd")(mem_kv, k_g, dk, dv)


def _xattn_probs(xq, k_ref, qg, h):
    sl = slice(h * XATTN_HEAD_DIM, (h + 1) * XATTN_HEAD_DIM)
    q = _rms(xq[:, sl], qg, XATTN_HEAD_DIM).astype(BF16)
    sc = _dot_nt(q, k_ref[:, sl]) * XATT_SCALE
    e = jnp.exp(sc - jnp.max(sc, axis=-1, keepdims=True))
    return q, e / jnp.sum(e, axis=-1, keepdims=True)


def _xattn_fwd(p1, kx, vx, q_g, *, t=256):
    s = p1.shape[0]
    m = kx.shape[0]
    t = _tile(s, t, 128)

    def body(xq_ref, xz_ref, k_ref, v_ref, g_ref, y_ref, yt_ref):
        xq = xq_ref[...]
        outs = []
        for h in range(XATTN_HEADS):
            _, p = _xattn_probs(xq, k_ref, g_ref[...], h)
            outs.append(_dot(p.astype(BF16), v_ref[:, h * XATTN_HEAD_DIM:(h + 1) * XATTN_HEAD_DIM]))
        y = jnp.concatenate(outs, axis=1) * _silu(xz_ref[...])
        y_ref[...] = y.astype(BF16)
        yt_ref[...] = y.T.astype(BF16)

    full = lambda shp: pl.BlockSpec(shp, lambda i: tuple(0 for _ in shp))
    sds = jax.ShapeDtypeStruct
    return pl.pallas_call(
        body, out_shape=(sds((s, BRANCH_W), BF16), sds((BRANCH_W, s), BF16)), grid=(s // t,),
        in_specs=[pl.BlockSpec((t, BRANCH_W), lambda i: (i, XQ_COL)), pl.BlockSpec((t, BRANCH_W), lambda i: (i, XQ_COL + 1)),
                  full((m, BRANCH_W)), full((m, BRANCH_W)), full((1, XATTN_HEAD_DIM))],
        out_specs=(pl.BlockSpec((t, BRANCH_W), lambda i: (i, 0)), pl.BlockSpec((BRANCH_W, t), lambda i: (0, i))),
        compiler_params=_cparams(("parallel",)), name="xattn_fwd")(p1, p1, kx, vx, q_g)


def _xattn_bwd(p1, dy4, kx, vx, q_g, *, t=256):
    s = p1.shape[0]
    m = kx.shape[0]
    t = _tile(s, t, 128)

    def body(xq_ref, xz_ref, dy_ref, k_ref, v_ref, g_ref, d_ref, dk_ref, dv_ref, gq_ref):
        i = pl.program_id(0)
        xq, xz, dy = xq_ref[...], xz_ref[...], dy_ref[...]
        do = dy * _silu(xz)
        gq = jnp.zeros((1, XATTN_HEAD_DIM), F32)
        outs, dks, dvs = [], [], []
        for h in range(XATTN_HEADS):
            sl = slice(h * XATTN_HEAD_DIM, (h + 1) * XATTN_HEAD_DIM)
            q, p = _xattn_probs(xq, k_ref, g_ref[...], h)
            pb = p.astype(BF16)
            outs.append(_dot(pb, v_ref[:, sl]))
            do_h = do[:, sl].astype(BF16)
            dvs.append(_dot_tn(pb, do_h))
            dp = _dot_nt(do_h, v_ref[:, sl])
            ds = (p * (dp - jnp.sum(p * dp, axis=-1, keepdims=True)) * XATT_SCALE).astype(BF16)
            dks.append(_dot_tn(ds, q))
            dx, dg = _rms_bwd(xq[:, sl], g_ref[...], XATTN_HEAD_DIM, _dot(ds, k_ref[:, sl]))
            d_ref[:, sl] = dx.astype(BF16)
            gq = gq + jnp.sum(dg, axis=0, keepdims=True)
        o = jnp.concatenate(outs, axis=1)
        d_ref[:, BRANCH_W:2 * BRANCH_W] = (dy * o * _dsilu(xz)).astype(BF16)
        dk = jnp.concatenate(dks, axis=1)
        dv = jnp.concatenate(dvs, axis=1)

        @pl.when(i == 0)
        def _():
            dk_ref[...] = dk
            dv_ref[...] = dv
            gq_ref[...] = gq

        @pl.when(i > 0)
        def _():
            dk_ref[...] += dk
            dv_ref[...] += dv
            gq_ref[...] += gq

    full = lambda shp: pl.BlockSpec(shp, lambda i: tuple(0 for _ in shp))
    sds = jax.ShapeDtypeStruct
    return pl.pallas_call(
        body, out_shape=(sds((s, 2 * BRANCH_W), BF16), sds((m, BRANCH_W), F32), sds((m, BRANCH_W), F32), sds((1, XATTN_HEAD_DIM), F32)),
        grid=(s // t,),
        in_specs=[pl.BlockSpec((t, BRANCH_W), lambda i: (i, XQ_COL)), pl.BlockSpec((t, BRANCH_W), lambda i: (i, XQ_COL + 1)),
                  pl.BlockSpec((None, t, BRANCH_W), lambda i: (3, i, 0)),
                  full((m, BRANCH_W)), full((m, BRANCH_W)), full((1, XATTN_HEAD_DIM))],
        out_specs=(pl.BlockSpec((t, 2 * BRANCH_W), lambda i: (i, 0)), full((m, BRANCH_W)), full((m, BRANCH_W)),
                   full((1, XATTN_HEAD_DIM))),
        compiler_params=_cparams(("arbitrary",)), name="xattn_bwd")(p1, p1, dy4, kx, vx, q_g)


def _gate_fwd(ystack, w_branch, gp, gate_b, *, tm=512, tn=1024):
    _, s, _ = ystack.shape
    d = w_branch.shape[2]
    tm, tn = _tile(s, tm, 128), _tile(d, tn)
    nj = d // tn

    def body(y_ref, w_ref, gp_ref, gb_ref, o_ref, ot_ref, acc_scr):
        b = pl.program_id(2)
        part = jax.nn.sigmoid(gp_ref[...] + gb_ref[...]) * _dot(y_ref[...], w_ref[...])

        @pl.when(b == 0)
        def _():
            acc_scr[...] = part

        @pl.when(b > 0)
        def _():
            acc_scr[...] += part

        @pl.when(b == N_BRANCH - 1)
        def _():
            acc = acc_scr[...]
            o_ref[...] = acc.astype(BF16)
            ot_ref[...] = acc.T.astype(BF16)

    sds = jax.ShapeDtypeStruct
    return pl.pallas_call(
        body, out_shape=(sds((s, d), BF16), sds((d, s), BF16)), grid=(s // tm, nj, N_BRANCH),
        in_specs=[pl.BlockSpec((None, tm, BRANCH_W), lambda i, j, b: (b, i, 0)),
                  pl.BlockSpec((None, BRANCH_W, tn), lambda i, j, b: (b, 0, j)),
                  pl.BlockSpec((tm, tn), lambda i, j, b: (i, b * nj + j)),
                  pl.BlockSpec((1, tn), lambda i, j, b: (0, b * nj + j))],
        out_specs=(pl.BlockSpec((tm, tn), lambda i, j, b: (i, j)), pl.BlockSpec((tn, tm), lambda i, j, b: (j, i))),
        scratch_shapes=[pltpu.VMEM((tm, tn), F32)],
        compiler_params=_cparams(("parallel", "parallel", "arbitrary")), name="gate_fwd")(ystack, w_branch, gp, gate_b)


def _gate_bwd(ystack, w_branch, gp, gate_b, dm, *, tm=512, tn=1024):
    _, s, _ = ystack.shape
    d = w_branch.shape[2]
    tm, tn = _tile(s, tm, 128), _tile(d, tn)
    nj = d // tn

    def body(y_ref, w_ref, gp_ref, gb_ref, dm_ref, dp_ref, dg_ref, gb_out_ref):
        i = pl.program_id(2)
        proj = _dot(y_ref[...], w_ref[...])
        gate = jax.nn.sigmoid(gp_ref[...] + gb_ref[...])
        dmv = dm_ref[...]
        dp_ref[...] = (dmv * gate).astype(BF16)
        dpre = dmv * proj * gate * (1.0 - gate)
        dg_ref[...] = dpre.astype(BF16)
        part = jnp.sum(dpre, axis=0, keepdims=True)

        @pl.when(i == 0)
        def _():
            gb_out_ref[...] = part

        @pl.when(i > 0)
        def _():
            gb_out_ref[...] += part

    sds = jax.ShapeDtypeStruct
    return pl.pallas_call(
        body, out_shape=(sds((N_BRANCH, s, d), BF16), sds((s, N_BRANCH * d), BF16), sds((1, N_BRANCH * d), F32)),
        grid=(N_BRANCH, nj, s // tm),
        in_specs=[pl.BlockSpec((None, tm, BRANCH_W), lambda b, j, i: (b, i, 0)),
                  pl.BlockSpec((None, BRANCH_W, tn), lambda b, j, i: (b, 0, j)),
                  pl.BlockSpec((tm, tn), lambda b, j, i: (i, b * nj + j)),
                  pl.BlockSpec((1, tn), lambda b, j, i: (0, b * nj + j)),
                  pl.BlockSpec((tm, tn), lambda b, j, i: (i, j))],
        out_specs=(pl.BlockSpec((None, tm, tn), lambda b, j, i: (b, i, j)),
                   pl.BlockSpec((tm, tn), lambda b, j, i: (i, b * nj + j)),
                   pl.BlockSpec((1, tn), lambda b, j, i: (0, b * nj + j))),
        compiler_params=_cparams(("parallel", "parallel", "arbitrary")), name="gate_bwd")(ystack, w_branch, gp, gate_b, dm)


def _adamw(w, g, m, v, *, name):
    shape = w.shape
    c = shape[-1]
    r = 1
    for n in shape[:-1]:
        r *= n
    w2, g2, m2, v2 = (a.reshape(r, c) for a in (w, g, m, v))
    tr = _tile(r, max(8, (1 << 19) // c // 8 * 8), 8)
    c1 = 1.0 / (1.0 - ADAM_B1 ** ADAM_STEP)
    c2 = 1.0 / (1.0 - ADAM_B2 ** ADAM_STEP)

    def body(w_ref, g_ref, m_ref, v_ref, d_ref, nm_ref, nv_ref):
        gv = g_ref[...]
        nm = ADAM_B1 * m_ref[...] + (1.0 - ADAM_B1) * gv
        nv = ADAM_B2 * v_ref[...] + (1.0 - ADAM_B2) * (gv * gv)
        nm_ref[...] = nm
        nv_ref[...] = nv
        d_ref[...] = -ADAM_LR * ((nm * c1) / (jnp.sqrt(nv * c2) + ADAM_EPS) + ADAM_WD * w_ref[...])

    blk = pl.BlockSpec((tr, c), lambda i: (i, 0))
    sd = jax.ShapeDtypeStruct((r, c), F32)
    d2, nm2, nv2 = pl.pallas_call(body, out_shape=(sd, sd, sd), grid=(r // tr,), in_specs=[blk] * 4, out_specs=(blk,) * 3,
                                  compiler_params=_cparams(("parallel",)), name=name)(w2, g2, m2, v2)
    return d2.reshape(shape), nm2.reshape(shape), nv2.reshape(shape)


def _place():
    x, y, c = lax.axis_index("x"), lax.axis_index("y"), lax.axis_index("c")
    chips = [(1 - x, y), (x, 1 - y), (1 - x, 1 - y)]
    return x, y, c, 2 * x + y, chips, [2 * cx + cy for cx, cy in chips]


ANY = pl.BlockSpec(memory_space=pl.ANY)


def _all_gather(shards):
    n = len(shards)

    def body(*refs):
        ins, outs = refs[:n], refs[n:2 * n]
        send, recv = refs[2 * n:]
        x, y, c, k, chips, ks = _place()
        sib = (x, y, 1 - c)
        sends = []
        for a in range(n):
            for j in range(3):
                cp = pltpu.make_async_remote_copy(src_ref=ins[a].at[c], dst_ref=outs[a].at[c, k], send_sem=send.at[6 * a + j],
                                                  recv_sem=recv.at[6 * a + j], device_id=(*chips[j], c), device_id_type=MESH)
                cp.start()
                sends.append(cp)
        for a in range(n):
            for j in range(3):
                slab = outs[a].at[c, ks[j]]
                pltpu.make_async_remote_copy(src_ref=slab, dst_ref=slab, send_sem=send.at[6 * a + j], recv_sem=recv.at[6 * a + j],
                                             device_id=(*chips[j], c), device_id_type=MESH).wait_recv()
                cp = pltpu.make_async_remote_copy(src_ref=slab, dst_ref=slab, send_sem=send.at[6 * a + 3 + j],
                                                  recv_sem=recv.at[6 * a + 3 + j], device_id=sib, device_id_type=MESH)
                cp.start()
                sends.append(cp)
        for a in range(n):
            for j in range(3):
                slab = outs[a].at[1 - c, ks[j]]
                pltpu.make_async_remote_copy(src_ref=slab, dst_ref=slab, send_sem=send.at[6 * a + 3 + j],
                                             recv_sem=recv.at[6 * a + 3 + j], device_id=sib, device_id_type=MESH).wait_recv()
        for cp in sends:
            cp.wait_send()

    out_shape = tuple(jax.ShapeDtypeStruct((2, 4) + s.shape[1:], s.dtype) for s in shards)
    return pl.pallas_call(
        body, out_shape=out_shape, in_specs=[ANY] * n, out_specs=(ANY,) * n,
        scratch_shapes=[pltpu.SemaphoreType.DMA((6 * n,)), pltpu.SemaphoreType.DMA((6 * n,))],
        name="weights_all_gather")(*shards)


def _rs_exchange_cores(grads):
    n = len(grads)

    def body(*refs):
        ins, outs = refs[:n], refs[n:2 * n]
        send, recv = refs[2 * n:]
        x, y, c, _, _, _ = _place()
        sib = (x, y, 1 - c)
        cps = []
        for a in range(n):
            cp = pltpu.make_async_remote_copy(src_ref=ins[a].at[1 - c], dst_ref=outs[a], send_sem=send.at[a], recv_sem=recv.at[a],
                                              device_id=sib, device_id_type=MESH)
            cp.start()
            cps.append(cp)
        for cp in cps:
            cp.wait()

    out_shape = tuple(jax.ShapeDtypeStruct(g.shape[1:], g.dtype) for g in grads)
    return pl.pallas_call(body, out_shape=out_shape, in_specs=[ANY] * n, out_specs=(ANY,) * n,
                          scratch_shapes=[pltpu.SemaphoreType.DMA((n,)), pltpu.SemaphoreType.DMA((n,))],
                          name="grads_exchange_cores")(*grads)


def _rs_exchange_chips(parts):
    n = len(parts)

    def body(*refs):
        ins, outs = refs[:n], refs[n:2 * n]
        send, recv = refs[2 * n:]
        x, y, c, k, chips, ks = _place()
        sends = []
        for a in range(n):
            for j in range(3):
                cp = pltpu.make_async_remote_copy(src_ref=ins[a].at[ks[j]], dst_ref=outs[a].at[j], send_sem=send.at[3 * a + j],
                                                  recv_sem=recv.at[3 * a + j], device_id=(*chips[j], c), device_id_type=MESH)
                cp.start()
                sends.append(cp)
        for cp in sends:
            cp.wait()

    out_shape = tuple(jax.ShapeDtypeStruct((3,) + p.shape[1:], p.dtype) for p in parts)
    return pl.pallas_call(
        body, out_shape=out_shape, in_specs=[ANY] * n, out_specs=(ANY,) * n,
        scratch_shapes=[pltpu.SemaphoreType.DMA((3 * n,)), pltpu.SemaphoreType.DMA((3 * n,))],
        name="grads_exchange_chips")(*parts)


def _rs_share_cores(bufs):
    n = len(bufs)

    def body(*refs):
        outs = refs[n:2 * n]
        send, recv = refs[2 * n:]
        x, y, c, _, _, _ = _place()
        sib = (x, y, 1 - c)
        cps = []
        for a in range(n):
            cp = pltpu.make_async_remote_copy(src_ref=outs[a].at[c], dst_ref=outs[a].at[c], send_sem=send.at[a], recv_sem=recv.at[a],
                                              device_id=sib, device_id_type=MESH)
            cp.start()
            cps.append(cp)
        for a in range(n):
            slab = outs[a].at[1 - c]
            pltpu.make_async_remote_copy(src_ref=slab, dst_ref=slab, send_sem=send.at[a], recv_sem=recv.at[a],
                                         device_id=sib, device_id_type=MESH).wait_recv()
        for cp in cps:
            cp.wait_send()

    out_shape = tuple(jax.ShapeDtypeStruct(b.shape, b.dtype) for b in bufs)
    return pl.pallas_call(
        body, out_shape=out_shape, in_specs=[ANY] * n, out_specs=(ANY,) * n,
        input_output_aliases={a: a for a in range(n)},
        scratch_shapes=[pltpu.SemaphoreType.DMA((n,)), pltpu.SemaphoreType.DMA((n,))],
        name="grads_share_cores")(*bufs)


def _add_core_halves(g, ra, c_idx, *, name):
    _, _, r, c = g.shape
    tr = _tile(r, max(16, (1 << 19) // c // 16 * 16), 16)

    def body(c_ref, g_ref, ra_ref, o_ref, ob_ref):
        tot = g_ref[...] + ra_ref[...]
        o_ref[...] = tot
        ob_ref[...] = tot.astype(BF16)

    blk = pl.BlockSpec((None, tr, c), lambda j, i, cr: (j, i, 0))
    gs = pltpu.PrefetchScalarGridSpec(
        num_scalar_prefetch=1, grid=(4, r // tr),
        in_specs=[pl.BlockSpec((None, None, tr, c), lambda j, i, cr: (cr[0], j, i, 0)), blk],
        out_specs=(blk, blk))
    return pl.pallas_call(body, out_shape=(jax.ShapeDtypeStruct((4, r, c), F32), jax.ShapeDtypeStruct((4, r, c), BF16)), grid_spec=gs,
                          compiler_params=_cparams(("parallel", "parallel")), name=name)(c_idx, g, ra)


def _add_chips(p, r3, k_idx, c_idx, *, name):
    _, r, c = p.shape
    tr = _tile(r, max(16, (1 << 18) // c // 16 * 16), 16)

    def body(k_ref, c_ref, p_ref, r_ref, o_ref):
        o_ref[...] = ((p_ref[...] + r_ref[0].astype(F32)) + r_ref[1].astype(F32)) + r_ref[2].astype(F32)

    gs = pltpu.PrefetchScalarGridSpec(
        num_scalar_prefetch=2, grid=(r // tr,),
        in_specs=[pl.BlockSpec((None, tr, c), lambda i, kr, cr: (kr[0], i, 0)), pl.BlockSpec((3, tr, c), lambda i, kr, cr: (0, i, 0))],
        out_specs=pl.BlockSpec((None, tr, c), lambda i, kr, cr: (cr[0], i, 0)))
    return pl.pallas_call(body, out_shape=jax.ShapeDtypeStruct((2, r, c), F32), grid_spec=gs,
                          compiler_params=_cparams(("parallel",)), name=name)(k_idx, c_idx, p, r3)


def _rdma(src, dst, send, recv, idx, dev):
    return pltpu.make_async_remote_copy(src_ref=src, dst_ref=dst, send_sem=send.at[idx], recv_sem=recv.at[idx],
                                        device_id=dev, device_id_type=MESH)


def _run_comm(comm, name):
    n_in, n_out = len(comm["ins"]), len(comm["outs"])

    def body(*refs):
        ins, outs = refs[:n_in], refs[n_in:n_in + n_out]
        send, recv = refs[n_in + n_out:]
        for phase in comm["phases"]:
            phase(ins, outs, send, recv)

    return pl.pallas_call(
        body, out_shape=tuple(comm["outs"]), in_specs=[ANY] * n_in, out_specs=(ANY,) * n_out,
        input_output_aliases=comm.get("aliases", {}),
        scratch_shapes=[pltpu.SemaphoreType.DMA((comm["nsem"],)), pltpu.SemaphoreType.DMA((comm["nsem"],))],
        name=name)(*comm["ins"])


def _gather_comm(shards, layer):
    n = len(shards)

    def start(ins, outs, send, recv):
        x, y, c, k, chips, ks = _place()

        @pl.when(c == layer)
        def _():
            for a in range(n):
                for j in range(3):
                    _rdma(ins[a], outs[a].at[k], send, recv, 6 * a + j, (*chips[j], c)).start()

    def forward(ins, outs, send, recv):
        x, y, c, k, chips, ks = _place()

        @pl.when(c == layer)
        def _():
            for a in range(n):
                for j in range(3):
                    slab = outs[a].at[ks[j]]
                    _rdma(slab, slab, send, recv, 6 * a + j, (*chips[j], c)).wait_recv()
                    _rdma(slab, slab, send, recv, 6 * a + 3 + j, (x, y, 1 - c)).start()

    def finish(ins, outs, send, recv):
        x, y, c, k, chips, ks = _place()

        @pl.when(c == layer)
        def _():
            for a in range(n):
                for j in range(3):
                    slab = outs[a].at[ks[j]]
                    _rdma(ins[a], outs[a].at[k], send, recv, 6 * a + j, (*chips[j], c)).wait_send()
                    _rdma(slab, slab, send, recv, 6 * a + 3 + j, (x, y, 1 - c)).wait_send()

        @pl.when(c != layer)
        def _():
            for a in range(n):
                for j in range(3):
                    slab = outs[a].at[ks[j]]
                    _rdma(slab, slab, send, recv, 6 * a + 3 + j, (x, y, 1 - c)).wait_recv()

    return dict(ins=list(shards), outs=[jax.ShapeDtypeStruct((4,) + s.shape, s.dtype) for s in shards], nsem=6 * n,
                phases=[start, forward, finish])


def _to_owner_comm(grads, layer):
    n = len(grads)

    def start(ins, outs, send, recv):
        x, y, c, _, _, _ = _place()

        @pl.when(c != layer)
        def _():
            for a in range(n):
                _rdma(ins[a], outs[a], send, recv, a, (x, y, 1 - c)).start()

    def finish(ins, outs, send, recv):
        x, y, c, _, _, _ = _place()

        @pl.when(c != layer)
        def _():
            for a in range(n):
                _rdma(ins[a], outs[a], send, recv, a, (x, y, 1 - c)).wait_send()

        @pl.when(c == layer)
        def _():
            for a in range(n):
                _rdma(ins[a], outs[a], send, recv, a, (x, y, 1 - c)).wait_recv()

    return dict(ins=list(grads), outs=[jax.ShapeDtypeStruct(g.shape, g.dtype) for g in grads], nsem=n, phases=[start, finish])


def _exchange_comm(parts, layer):
    n = len(parts)

    def start(ins, outs, send, recv):
        x, y, c, k, chips, ks = _place()

        @pl.when(c == layer)
        def _():
            for a in range(n):
                for j in range(3):
                    _rdma(ins[a].at[ks[j]], outs[a].at[j], send, recv, 3 * a + j, (*chips[j], c)).start()

    def finish(ins, outs, send, recv):
        x, y, c, k, chips, ks = _place()

        @pl.when(c == layer)
        def _():
            for a in range(n):
                for j in range(3):
                    _rdma(ins[a].at[ks[j]], outs[a].at[j], send, recv, 3 * a + j, (*chips[j], c)).wait()

    return dict(ins=list(parts), outs=[jax.ShapeDtypeStruct((3,) + p.shape[1:], p.dtype) for p in parts], nsem=3 * n,
                phases=[start, finish])


def _share_comm(bufs, layer):
    n = len(bufs)

    def go(ins, outs, send, recv):
        x, y, c, _, _, _ = _place()

        @pl.when(c == layer)
        def _():
            for a in range(n):
                _rdma(outs[a].at[layer], outs[a].at[layer], send, recv, a, (x, y, 1 - c)).start()
            for a in range(n):
                _rdma(outs[a].at[layer], outs[a].at[layer], send, recv, a, (x, y, 1 - c)).wait_send()

        @pl.when(c != layer)
        def _():
            for a in range(n):
                _rdma(outs[a].at[layer], outs[a].at[layer], send, recv, a, (x, y, 1 - c)).wait_recv()

    return dict(ins=list(bufs), outs=[jax.ShapeDtypeStruct(b.shape, b.dtype) for b in bufs], nsem=n, phases=[go],
                aliases={a: a for a in range(n)})


def _add_owner(g, ra, own, *, name):
    _, r, c = g.shape
    tr = _tile(r, max(16, (1 << 20) // c // 16 * 16), 16)

    def body(own_ref, g_ref, ra_ref, o_ref, ob_ref):
        tot = g_ref[...] + ra_ref[...].astype(F32)
        o_ref[...] = tot
        ob_ref[...] = tot.astype(BF16)

    blk = pl.BlockSpec((None, tr, c), lambda j, i, o: (j * o[0], i * o[0], 0))
    gs = pltpu.PrefetchScalarGridSpec(num_scalar_prefetch=1, grid=(4, r // tr), in_specs=[blk, blk], out_specs=(blk, blk))
    return pl.pallas_call(body, out_shape=(jax.ShapeDtypeStruct((4, r, c), F32), jax.ShapeDtypeStruct((4, r, c), BF16)),
                          grid_spec=gs, compiler_params=_cparams(("arbitrary", "arbitrary")), name=name)(own, g, ra)


def _add_chips_layer(p, r3, k_idx, own, layer, buf, *, name):
    _, r, c = p.shape
    tr = _tile(r, max(16, (1 << 20) // c // 16 * 16), 16)

    def body(k_ref, own_ref, p_ref, r_ref, *rest):
        o_ref = rest[-1]
        o_ref[...] = ((p_ref[...] + r_ref[0].astype(F32)) + r_ref[1].astype(F32)) + r_ref[2].astype(F32)

    in_specs = [pl.BlockSpec((None, tr, c), lambda i, kr, o: (kr[0] * o[0], i * o[0], 0)),
                pl.BlockSpec((3, tr, c), lambda i, kr, o: (0, i * o[0], 0))]
    args = [k_idx, own, p, r3]
    aliases = {}
    if buf is not None:
        in_specs.append(ANY)
        args.append(buf)
        aliases = {4: 0}
    gs = pltpu.PrefetchScalarGridSpec(num_scalar_prefetch=2, grid=(r // tr,), in_specs=in_specs,
                                      out_specs=pl.BlockSpec((None, tr, c), lambda i, kr, o: (layer, i * o[0], 0)))
    return pl.pallas_call(body, out_shape=jax.ShapeDtypeStruct((2, r, c), F32), grid_spec=gs, input_output_aliases=aliases,
                          compiler_params=_cparams(("arbitrary",)), name=name)(*args)


def _all_reduce_small(vec):
    r = vec.shape[0]

    def body(v_ref, gath_ref, sum_ref, send, recv):
        x, y, c = lax.axis_index("x"), lax.axis_index("y"), lax.axis_index("c")
        me = 4 * x + 2 * y + c
        gath_ref[me] = v_ref[...]
        cps = []
        for f in range(1, 8):
            fx, fy, fc = (f >> 2) & 1, (f >> 1) & 1, f & 1
            peer = (x ^ fx, y ^ fy, c ^ fc)
            cp = pltpu.make_async_remote_copy(src_ref=v_ref, dst_ref=gath_ref.at[me], send_sem=send.at[f - 1], recv_sem=recv.at[f - 1],
                                              device_id=peer, device_id_type=MESH)
            cp.start()
            cps.append(cp)
        for f in range(1, 8):
            fx, fy, fc = (f >> 2) & 1, (f >> 1) & 1, f & 1
            src = 4 * (x ^ fx) + 2 * (y ^ fy) + (c ^ fc)
            pltpu.make_async_remote_copy(src_ref=v_ref, dst_ref=gath_ref.at[src], send_sem=send.at[f - 1], recv_sem=recv.at[f - 1],
                                         device_id=(x ^ fx, y ^ fy, c ^ fc), device_id_type=MESH).wait_recv()
        for cp in cps:
            cp.wait_send()
        acc = gath_ref[0]
        for i in range(1, 8):
            acc = acc + gath_ref[i]
        sum_ref[...] = acc

    vm = pl.BlockSpec(memory_space=pltpu.VMEM)
    _, total = pl.pallas_call(
        body, out_shape=(jax.ShapeDtypeStruct((8, r, 128), F32), jax.ShapeDtypeStruct((r, 128), F32)),
        in_specs=[vm], out_specs=(vm, vm),
        scratch_shapes=[pltpu.SemaphoreType.DMA((7,)), pltpu.SemaphoreType.DMA((7,))],
        name="small_all_reduce")(vec)
    return total


def _full_weight(gw, name):
    gathered, own, chip = gw[name]
    return jnp.concatenate([jnp.where(chip == k, own, gathered[k]) for k in range(4)], axis=SHARD_AXIS[name])


def _to_shards(full, name):
    return jnp.stack(jnp.split(full, 4, axis=SHARD_AXIS[name]), axis=0)


def _rope_tables(positions):
    inv = ROPE_THETA ** (-jnp.arange(0, QK_ROPE, 2, dtype=F32) / QK_ROPE)
    ang = positions.astype(F32)[:, None] * inv
    cos, sin = jnp.cos(ang), jnp.sin(ang)
    s = positions.shape[0]
    pad = jnp.zeros((s, HEAD_PAD - QK_HEAD), F32)
    ctab = jnp.concatenate([jnp.ones((s, QK_NOPE), F32), cos, cos, pad], axis=1)
    stab = jnp.concatenate([jnp.zeros((s, QK_NOPE), F32), -sin, sin, pad], axis=1)
    return ctab, stab


def _pad_gain(g):
    return jnp.concatenate([g, jnp.zeros((HEAD_PAD - QK_HEAD,), F32)])[None, :]


def _layer_weights(gw, rep, l, ql, kvl):
    d = rep["norm_g"].shape[1]
    w_in = _full_weight(gw, "w_in")
    o_kr = 2 * BRANCH_W + ql + kvl
    o_g = o_kr + QK_ROPE + 7 * BRANCH_W
    w = {}
    w["w1"] = jnp.concatenate([w_in[:, :o_kr], w_in[:, o_kr + QK_ROPE:o_g]], axis=1)
    w["wg"] = w_in[:, o_g:]
    w["wkr"] = jnp.concatenate([w_in[:, o_kr:o_kr + QK_ROPE], jnp.zeros((d, 128 - QK_ROPE), BF16)], axis=1)
    wuq = _full_weight(gw, "w_uq").reshape(ql, MLA_HEADS, QK_HEAD)
    w["w_uq"] = jnp.pad(wuq, ((0, 0), (0, 0), (0, HEAD_PAD - QK_HEAD))).reshape(ql, MLA_HEADS * HEAD_PAD)
    for nme in ("w_ukv", "pool_w", "conv_w", "w_mem_kv", "w_branch", "w_out"):
        w[nme] = _full_weight(gw, nme)
    for nme in ("norm_g", "gate_b", "pool_scale", "q_a_norm_g", "kv_a_norm_g", "mem_norm_g", "xattn_q_norm_g", "xattn_k_norm_g"):
        w[nme] = rep[nme][l][None, :]
    w["mla_q_norm_g"] = _pad_gain(rep["mla_q_norm_g"][l])
    w["mla_k_norm_g"] = _pad_gain(rep["mla_k_norm_g"][l])
    return w


def _forward_layer(x, mem, ctab, stab, w, tq, l, comm=None):
    sfx = f"_l{l}"
    h, ht = _norm_fwd(x, w["norm_g"], name="norm_fwd" + sfx)
    p1 = _mm(h, w["w1"], name="proj_main" + sfx)
    gp = _mm(h, w["wg"], name="proj_gates" + sfx)
    kr = _mm(h, w["wkr"], name="proj_krope" + sfx)
    y_pool, yt_pool = _pool_fwd(p1, w["pool_w"], w["pool_scale"])
    qf, kf, vv, vt = _mla_prep_fwd(p1, kr, ctab, stab, w["q_a_norm_g"], w["kv_a_norm_g"], w["w_uq"], w["w_ukv"],
                               w["mla_q_norm_g"], w["mla_k_norm_g"])
    (y_mla, yt_mla, o_att, lse, lset), comm_out = _attn_fwd(qf, kf, vt, p1, tq=tq, comm=comm)
    y_conv, yt_conv = _conv_fwd(p1, w["conv_w"])
    memn, memnt = _norm_fwd(mem, w["mem_norm_g"], name="mem_norm" + sfx)
    mem_kv = _mm(memn, w["w_mem_kv"], name="mem_kv" + sfx)
    kx, vx = _memkv_prep(mem_kv, w["xattn_k_norm_g"])
    y_mem, yt_mem = _xattn_fwd(p1, kx, vx, w["xattn_q_norm_g"])
    ystack = jnp.stack([y_pool, y_mla, y_conv, y_mem])
    ytstack = jnp.stack([yt_pool, yt_mla, yt_conv, yt_mem])
    merged, mergedt = _gate_fwd(ystack, w["w_branch"], gp, w["gate_b"])
    x_out = _mm(merged, w["w_out"], add=x, name="out_proj" + sfx)
    saved = dict(x=x, ht=ht, p1=p1, gp=gp, kr=kr, qf=qf, kf=kf, vv=vv, o_att=o_att, lse=lse, lset=lset, memnt=memnt,
                 mem_kv=mem_kv, kx=kx, vx=vx, ystack=ystack, ytstack=ytstack, mergedt=mergedt)
    return x_out, saved, comm_out


def _backward_layer(dx_out, sv, mem, ctab, stab, w, tq, l, ql, kvl, comm=None):
    sfx = f"_l{l}"
    g = {}
    g["w_out"] = _mm(sv["mergedt"], dx_out, name="g_w_out" + sfx)
    dm = _mm(dx_out, w["w_out"], trans_b=True, name="d_merged" + sfx)
    dproj, dgp, g_gate_b = _gate_bwd(sv["ystack"], w["w_branch"], sv["gp"], w["gate_b"], dm)
    g["gate_b"] = g_gate_b[0]
    g["w_branch"] = _mm(sv["ytstack"], dproj, name="g_w_branch" + sfx)
    dy4 = _mm(dproj, w["w_branch"], trans_b=True, name="d_branches" + sfx)
    p1, kr = sv["p1"], sv["kr"]
    d_pool, g_pw, g_ps = _pool_bwd(p1, dy4, w["pool_w"], w["pool_scale"])
    g["pool_w"], g["pool_scale"] = g_pw, g_ps[0]
    do, d_mz, delta, deltat = _attn_bwd_pre(dy4, sv["o_att"], p1)
    dqf, got = _attn_bwd_dq(sv["qf"], sv["kf"], sv["vv"], do, sv["lse"], delta, tq=tq, comm=comm[0] if comm else None)
    parts, comm_dkv = comm[1](got) if comm else (None, None)
    dkf, dvv, got = _attn_bwd_dkv(sv["qf"], sv["kf"], sv["vv"], do, sv["lset"], deltat, tq=tq, comm=comm_dkv)
    comm_out = (parts, got)
    (d_c, d_kr, dq_raw, dkv_raw, cqnt, ckvnt, g_qa, g_kva, g_qg, g_kg) = _mla_prep_bwd(
        p1, kr, ctab, stab, w["q_a_norm_g"], w["kv_a_norm_g"], w["w_uq"], w["w_ukv"], w["mla_q_norm_g"], w["mla_k_norm_g"],
        dqf, dkf, dvv)
    g["q_a_norm_g"], g["kv_a_norm_g"] = g_qa[0], g_kva[0]
    g["mla_q_norm_g"], g["mla_k_norm_g"] = g_qg[0, :QK_HEAD], g_kg[0, :QK_HEAD]
    g_wuq = _mm(cqnt, dq_raw, name="g_w_uq" + sfx)
    g["w_uq"] = g_wuq.reshape(ql, MLA_HEADS, HEAD_PAD)[:, :, :QK_HEAD].reshape(ql, MLA_HEADS * QK_HEAD)
    g["w_ukv"] = _mm(ckvnt, dkv_raw, name="g_w_ukv" + sfx)
    d_conv, gc0, gc1, gc2 = _conv_bwd(p1, dy4, w["conv_w"])
    g["conv_w"] = jnp.concatenate([gc0, gc1, gc2], axis=0)
    d_x, dkx, dvx, g_xq = _xattn_bwd(p1, dy4, sv["kx"], sv["vx"], w["xattn_q_norm_g"])
    g["xattn_q_norm_g"] = g_xq[0]
    d_memkv, g_xk = _memkv_prep_bwd(sv["mem_kv"], w["xattn_k_norm_g"], dkx, dvx)
    g["xattn_k_norm_g"] = g_xk[0]
    g["w_mem_kv"] = _mm(sv["memnt"], d_memkv, name="g_w_mem_kv" + sfx)
    d_memn = _mm(d_memkv, w["w_mem_kv"], trans_b=True, name="d_memn" + sfx)
    _, g_mn = _norm_bwd(mem, w["mem_norm_g"], d_memn, d_memn, name="mem_norm_bwd" + sfx)
    g["mem_norm_g"] = g_mn[0]
    dp1 = jnp.concatenate([d_pool, d_c, d_mz, d_conv, d_x], axis=1)
    ht = sv["ht"]
    g_w1 = _mm(ht, dp1, name="g_w1" + sfx)
    g_wg = _mm(ht, dgp, name="g_wg" + sfx)
    g_wkr = _mm(ht, d_kr, name="g_wkr" + sfx)
    o_kr = 2 * BRANCH_W + ql + kvl
    g["w_in"] = jnp.concatenate([g_w1[:, :o_kr], g_wkr[:, :QK_ROPE], g_w1[:, o_kr:], g_wg], axis=1)
    dh = _mm(dp1, w["w1"], trans_b=True, name="dh_main" + sfx)
    dh = _mm(dgp, w["wg"], trans_b=True, add=dh, name="dh_gates" + sfx)
    dh = _mm(d_kr, w["wkr"], trans_b=True, add=dh, name="dh_krope" + sfx)
    dx, g_ng = _norm_bwd(sv["x"], w["norm_g"], dh, dx_out, name="norm_bwd" + sfx)
    g["norm_g"] = g_ng[0]
    return dx, g, comm_out


def _as4(a):
    rest = a.shape[2:]
    r = 1
    for n in rest[:-1]:
        r *= n
    return a.reshape(2, 4, r, rest[-1])


def kernel(x, mem, positions, norm_g, w_in, gate_b, pool_w, pool_scale, q_a_norm_g, kv_a_norm_g, w_uq, w_ukv, mla_q_norm_g, mla_k_norm_g, conv_w, mem_norm_g, w_mem_kv, xattn_q_norm_g, xattn_k_norm_g, w_branch, w_out, loss_target, m_norm_g, m_w_in, m_gate_b, m_pool_w, m_pool_scale, m_q_a_norm_g, m_kv_a_norm_g, m_w_uq, m_w_ukv, m_mla_q_norm_g, m_mla_k_norm_g, m_conv_w, m_mem_norm_g, m_w_mem_kv, m_xattn_q_norm_g, m_xattn_k_norm_g, m_w_branch, m_w_out, v_norm_g, v_w_in, v_gate_b, v_pool_w, v_pool_scale, v_q_a_norm_g, v_kv_a_norm_g, v_w_uq, v_w_ukv, v_mla_q_norm_g, v_mla_k_norm_g, v_conv_w, v_mem_norm_g, v_w_mem_kv, v_xattn_q_norm_g, v_xattn_k_norm_g, v_w_branch, v_w_out):
    wts = dict(norm_g=norm_g, w_in=w_in, gate_b=gate_b, pool_w=pool_w, pool_scale=pool_scale, q_a_norm_g=q_a_norm_g,
               kv_a_norm_g=kv_a_norm_g, w_uq=w_uq, w_ukv=w_ukv, mla_q_norm_g=mla_q_norm_g, mla_k_norm_g=mla_k_norm_g,
               conv_w=conv_w, mem_norm_g=mem_norm_g, w_mem_kv=w_mem_kv, xattn_q_norm_g=xattn_q_norm_g,
               xattn_k_norm_g=xattn_k_norm_g, w_branch=w_branch, w_out=w_out)
    mom = dict(norm_g=m_norm_g, w_in=m_w_in, gate_b=m_gate_b, pool_w=m_pool_w, pool_scale=m_pool_scale, q_a_norm_g=m_q_a_norm_g,
               kv_a_norm_g=m_kv_a_norm_g, w_uq=m_w_uq, w_ukv=m_w_ukv, mla_q_norm_g=m_mla_q_norm_g, mla_k_norm_g=m_mla_k_norm_g,
               conv_w=m_conv_w, mem_norm_g=m_mem_norm_g, w_mem_kv=m_w_mem_kv, xattn_q_norm_g=m_xattn_q_norm_g,
               xattn_k_norm_g=m_xattn_k_norm_g, w_branch=m_w_branch, w_out=m_w_out)
    vel = dict(norm_g=v_norm_g, w_in=v_w_in, gate_b=v_gate_b, pool_w=v_pool_w, pool_scale=v_pool_scale, q_a_norm_g=v_q_a_norm_g,
               kv_a_norm_g=v_kv_a_norm_g, w_uq=v_w_uq, w_ukv=v_w_ukv, mla_q_norm_g=v_mla_q_norm_g, mla_k_norm_g=v_mla_k_norm_g,
               conv_w=v_conv_w, mem_norm_g=v_mem_norm_g, w_mem_kv=v_w_mem_kv, xattn_q_norm_g=v_xattn_q_norm_g,
               xattn_k_norm_g=v_xattn_k_norm_g, w_branch=v_w_branch, w_out=v_w_out)
    depth = norm_g.shape[0]
    assert depth == 2 and x.shape[0] == 1
    xs, mems, tgt = x[0], mem[0], loss_target[0]
    s = xs.shape[0]
    ql, kvl = q_a_norm_g.shape[1], kv_a_norm_g.shape[1]
    tq = _tile(s, 512, 128)
    ctab, stab = _rope_tables(positions[0])

    chip = 2 * lax.axis_index("x") + lax.axis_index("y")
    k_idx = chip.astype(jnp.int32).reshape(1)
    rep = {n: wts[n] for n in REPLICATED}
    send = [[wts[n][l].astype(F32 if n == "conv_w" else BF16) for n in SHARDED] for l in range(depth)]

    def layer_weights(got, l):
        return _layer_weights({n: (g, own, chip) for n, g, own in zip(SHARDED, got, send[l])}, rep, l, ql, kvl)

    lw = [layer_weights(_run_comm(_gather_comm(send[0], 0), "weights_gather_l0"), 0), None]

    act, sv0, got1 = _forward_layer(xs, mems, ctab, stab, lw[0], tq, 0, comm=_gather_comm(send[1], 1))
    lw[1] = layer_weights(got1, 1)
    act, sv1, _ = _forward_layer(act, mems, ctab, stab, lw[1], tq, 1)
    dy, loss_part = _loss_head(act, tgt)

    def shard_layout(g):
        shards = [jnp.swapaxes(_to_shards(g[n], n), -1, -2) if n == "w_in" else _to_shards(g[n], n) for n in SHARDED]
        return [t.reshape(4, -1, t.shape[-1]) for t in shards]

    def own_flag(l):
        return (lax.axis_index("c") == l).astype(jnp.int32).reshape(1)

    def add_owner(gl, ra, l):
        return [_add_owner(a, b, own_flag(l), name=f"add_owner_{n}_l{l}") for a, b, n in zip(gl, ra, SHARDED)]

    def finish_layer(parts, r3s, l, bufs):
        bufs = [_add_chips_layer(p, r3, k_idx, own_flag(l), l, None if bufs is None else bufs[i], name=f"add_chips_{n}_l{l}")
                for i, ((p, _), r3, n) in enumerate(zip(parts, r3s, SHARDED))]
        return _run_comm(_share_comm(bufs, l), f"grads_share_l{l}")

    grads = [None] * depth
    dxl, grads[1], _ = _backward_layer(dy, sv1, mems, ctab, stab, lw[1], tq, 1, ql, kvl)
    gl1 = shard_layout(grads[1])

    def after_dq(ra):
        parts = add_owner(gl1, ra, 1)
        return parts, _exchange_comm([pb for _, pb in parts], 1)

    dxl, grads[0], (parts1, r3_1) = _backward_layer(dxl, sv0, mems, ctab, stab, lw[0], tq, 0, ql, kvl,
                                                    comm=(_to_owner_comm([g.astype(BF16) for g in gl1], 1), after_dq))
    grad_x = dxl[None]
    bufs = finish_layer(parts1, r3_1, 1, None)
    gl0 = shard_layout(grads[0])
    parts0 = add_owner(gl0, _run_comm(_to_owner_comm([g.astype(BF16) for g in gl0], 0), "grads_to_owner_l0"), 0)
    r3_0 = _run_comm(_exchange_comm([pb for _, pb in parts0], 0), "grads_exchange_l0")
    reduced = finish_layer(parts0, r3_0, 0, bufs)
    gsum = {n: (jnp.swapaxes(r.reshape(2, wts[n].shape[2], wts[n].shape[1]), 1, 2) if n == "w_in" else r.reshape(wts[n].shape))
            for n, r in zip(SHARDED, reduced)}

    flat = [jnp.stack([grads[l][n] for l in range(depth)], axis=0).reshape(-1) for n in REPLICATED]
    sizes = [f.shape[0] for f in flat]
    total = sum(sizes) + 1
    rows = -(-total // 1024) * 8
    vec = jnp.concatenate(flat + [loss_part[0, :1], jnp.zeros((rows * 128 - total,), F32)]).reshape(rows, 128)
    red = _all_reduce_small(vec).reshape(-1)
    off = 0
    for n, sz in zip(REPLICATED, sizes):
        gsum[n] = red[off:off + sz].reshape(wts[n].shape)
        off += sz
    loss = red[off]

    delta, new_m, new_v = {}, {}, {}
    for n in WEIGHTS:
        if n == "w_in":
            tr_ = lambda a: jnp.swapaxes(a, 1, 2)
            delta[n], new_m[n], new_v[n] = (tr_(o) for o in _adamw(tr_(wts[n]), tr_(gsum[n]), tr_(mom[n]), tr_(vel[n]),
                                                                   name=f"adamw_{n}"))
        else:
            delta[n], new_m[n], new_v[n] = _adamw(wts[n], gsum[n], mom[n], vel[n], name=f"adamw_{n}")
    return (loss, grad_x, *[gsum[n] for n in WEIGHTS], *[delta[n] for n in WEIGHTS],
            *[new_m[n] for n in WEIGHTS], *[new_v[n] for n in WEIGHTS])
```

```python
import functools

import jax
import jax.numpy as jnp
from jax import lax
from jax.experimental import pallas as pl
from jax.experimental.pallas import tpu as pltpu

F32 = jnp.float32
BF16 = jnp.bfloat16
MESH = pl.DeviceIdType.MESH

EPS = 1e-6
N_BRANCH = 4
BRANCH_W = 1024
POOL_GROUPS = 4
POOL_GW = BRANCH_W // POOL_GROUPS
POOL_HALO = 16
CONV_HALO = 8
MLA_HEADS = 8
QK_NOPE = 128
QK_ROPE = 64
QK_HEAD = QK_NOPE + QK_ROPE
HEAD_PAD = 256
V_HEAD = 128
ROPE_THETA = 10000.0
XATTN_HEADS = 4
XATTN_HEAD_DIM = BRANCH_W // XATTN_HEADS
ADAM_LR, ADAM_B1, ADAM_B2, ADAM_EPS, ADAM_WD, ADAM_STEP = 0.001, 0.9, 0.999, 1e-08, 0.01, 10
NEG = -1e30
VMEM_LIMIT = 48 * 1024 * 1024

SHARDED = ("w_in", "pool_w", "w_uq", "w_ukv", "conv_w", "w_mem_kv", "w_branch", "w_out")
REPLICATED = ("norm_g", "gate_b", "pool_scale", "q_a_norm_g", "kv_a_norm_g", "mla_q_norm_g", "mla_k_norm_g",
              "mem_norm_g", "xattn_q_norm_g", "xattn_k_norm_g")
WEIGHTS = ("norm_g", "w_in", "gate_b", "pool_w", "pool_scale", "q_a_norm_g", "kv_a_norm_g", "w_uq", "w_ukv",
           "mla_q_norm_g", "mla_k_norm_g", "conv_w", "mem_norm_g", "w_mem_kv", "xattn_q_norm_g", "xattn_k_norm_g",
           "w_branch", "w_out")
SHARD_AXIS = {"w_in": 1, "pool_w": 1, "w_uq": 1, "w_ukv": 1, "conv_w": 1, "w_mem_kv": 0, "w_branch": 2, "w_out": 0}


def _cparams(sem=None):
    return pltpu.CompilerParams(dimension_semantics=sem, vmem_limit_bytes=VMEM_LIMIT)


def _tile(n, pref, unit=128):
    if n <= pref:
        return n
    t = (pref // unit) * unit
    while t >= unit:
        if n % t == 0:
            return t
        t -= unit
    return n


def _silu(z):
    return z * jax.nn.sigmoid(z)


def _dsilu(z):
    s = jax.nn.sigmoid(z)
    return s * (1.0 + z * (1.0 - s))


def _dot(a, b):
    return jnp.dot(a, b, preferred_element_type=F32)


def _dot_nt(a, b):
    return lax.dot_general(a, b, (((1,), (1,)), ((), ())), preferred_element_type=F32)


def _dot_tn(a, b):
    return lax.dot_general(a, b, (((0,), (0,)), ((), ())), preferred_element_type=F32)


def _rms(x, g, n):
    r = lax.rsqrt(jnp.sum(x * x, axis=-1, keepdims=True) * (1.0 / n) + EPS)
    return x * r * g


def _rms_bwd(x, g, n, dout):
    r = lax.rsqrt(jnp.sum(x * x, axis=-1, keepdims=True) * (1.0 / n) + EPS)
    y = x * r
    dy = dout * g
    dx = r * (dy - y * (jnp.sum(dy * y, axis=-1, keepdims=True) * (1.0 / n)))
    return dx, dout * y


def _rope(x, ctab, stab):
    lane = lax.broadcasted_iota(jnp.int32, x.shape, 1)
    partner = jnp.where(lane < QK_NOPE + QK_ROPE // 2, pltpu.roll(x, HEAD_PAD - QK_ROPE // 2, 1),
                        pltpu.roll(x, QK_ROPE // 2, 1))
    return x * ctab + partner * stab


def _rope_bwd(d, ctab, stab):
    lane = lax.broadcasted_iota(jnp.int32, d.shape, 1)
    ds = d * stab
    partner = jnp.where(lane < QK_NOPE + QK_ROPE // 2, pltpu.roll(ds, HEAD_PAD - QK_ROPE // 2, 1),
                        pltpu.roll(ds, QK_ROPE // 2, 1))
    return d * ctab + jnp.where((lane >= QK_NOPE) & (lane < QK_HEAD), partner, 0.0)


def _mm(a, b, *, name, trans_b=False, add=None, out_dtype=F32, tm=512, tn=1024, tk=2048, comm=None):
    batched = a.ndim == 3
    if batched:
        nb, m, k = a.shape
    else:
        m, k = a.shape
    n = b.shape[-2] if trans_b else b.shape[-1]
    tm, tn, tk = _tile(m, tm, 8), _tile(n, tn), _tile(k, tk)
    nk = k // tk
    c_in, c_out, c_sem, c_shapes = _comm_extras(comm)
    nci, nco = len(c_in), len(c_out)
    assert comm is None or not batched
    n_in = 2 + (add is not None)

    def body(*refs):
        a_ref, b_ref = refs[:2]
        add_ref = refs[2] if add is not None else None
        o_ref = refs[n_in + nci]
        rest = refs[n_in + nci + 1 + nco:]
        if comm is not None:
            i, j, k3 = pl.program_id(0), pl.program_id(1), pl.program_id(2)
            ni, nj = m // tm, n // tn
            origin = (j == 0) & (k3 == 0)
            c_first, c_mid, c_last = _comm_hooks(comm, refs[n_in:n_in + nci], refs[n_in + nci + 1:n_in + nci + 1 + nco],
                                                 rest[(1 if nk > 1 else 0):], (i == 0) & origin, (i == (3 * ni) // 4) & origin,
                                                 (i == ni - 1) & (j == nj - 1) & (k3 == nk - 1))
            c_first()
        av = a_ref[...].astype(BF16)
        bv = b_ref[...].astype(BF16)
        part = _dot_nt(av, bv) if trans_b else _dot(av, bv)

        def finish(acc):
            if add_ref is not None:
                acc = acc + add_ref[...]
            o_ref[...] = acc.astype(o_ref.dtype)

        if nk == 1:
            finish(part)
        else:
            acc_ref = rest[0]
            kk = pl.program_id(3 if batched else 2)

            @pl.when(kk == 0)
            def _():
                acc_ref[...] = part

            @pl.when(kk > 0)
            def _():
                acc_ref[...] += part

            @pl.when(kk == nk - 1)
            def _():
                finish(acc_ref[...])

        if comm is not None:
            c_mid()
            c_last()

    if batched:
        a_spec = pl.BlockSpec((None, tm, tk), lambda bb, i, j, kk: (bb, i, kk))
        b_spec = (pl.BlockSpec((None, tn, tk), lambda bb, i, j, kk: (bb, j, kk)) if trans_b
                  else pl.BlockSpec((None, tk, tn), lambda bb, i, j, kk: (bb, kk, j)))
        o_spec = pl.BlockSpec((None, tm, tn), lambda bb, i, j, kk: (bb, i, j))
        grid = (nb, m // tm, n // tn, nk)
        out_shape = jax.ShapeDtypeStruct((nb, m, n), out_dtype)
        sem = ("parallel", "parallel", "parallel", "arbitrary")
    else:
        a_spec = pl.BlockSpec((tm, tk), lambda i, j, kk: (i, kk))
        b_spec = (pl.BlockSpec((tn, tk), lambda i, j, kk: (j, kk)) if trans_b
                  else pl.BlockSpec((tk, tn), lambda i, j, kk: (kk, j)))
        o_spec = pl.BlockSpec((tm, tn), lambda i, j, kk: (i, j))
        grid = (m // tm, n // tn, nk)
        out_shape = jax.ShapeDtypeStruct((m, n), out_dtype)
        sem = ("parallel", "parallel", "arbitrary")
    in_specs = [a_spec, b_spec] + ([o_spec] if add is not None else [])
    args = (a, b) + ((add,) if add is not None else ())
    scratch = [pltpu.VMEM((tm, tn), F32)] if nk > 1 else []
    if comm is None:
        return pl.pallas_call(body, out_shape=out_shape, grid=grid, in_specs=in_specs, out_specs=o_spec,
                              scratch_shapes=scratch, compiler_params=_cparams(sem), name=name)(*args)
    res = pl.pallas_call(body, out_shape=(out_shape,) + c_shapes, grid=grid, in_specs=in_specs + c_in,
                         out_specs=(o_spec,) + tuple(c_out), scratch_shapes=scratch + c_sem,
                         compiler_params=_cparams(("arbitrary",) * 3), name=name)(*args, *comm["ins"])
    return res[0], list(res[1:])


def _norm_fwd(x, g, *, name, t=256):
    s, d = x.shape
    t = _tile(s, t, 128)

    def body(x_ref, g_ref, h_ref, ht_ref):
        h = _rms(x_ref[...], g_ref[...], d)
        h_ref[...] = h.astype(BF16)
        ht_ref[...] = h.T.astype(BF16)

    return pl.pallas_call(
        body, out_shape=(jax.ShapeDtypeStruct((s, d), BF16), jax.ShapeDtypeStruct((d, s), BF16)),
        grid=(s // t,),
        in_specs=[pl.BlockSpec((t, d), lambda i: (i, 0)), pl.BlockSpec((1, d), lambda i: (0, 0))],
        out_specs=(pl.BlockSpec((t, d), lambda i: (i, 0)), pl.BlockSpec((d, t), lambda i: (0, i))),
        compiler_params=_cparams(("parallel",)), name=name)(x, g)


def _norm_bwd(x, g, dh, dres, *, name, t=256):
    s, d = x.shape
    t = _tile(s, t, 8)

    def body(x_ref, g_ref, dh_ref, dres_ref, dx_ref, dg_ref):
        dx, dgt = _rms_bwd(x_ref[...], g_ref[...], d, dh_ref[...])
        dx_ref[...] = dx + dres_ref[...]
        part = jnp.sum(dgt, axis=0, keepdims=True)

        @pl.when(pl.program_id(0) == 0)
        def _():
            dg_ref[...] = part

        @pl.when(pl.program_id(0) > 0)
        def _():
            dg_ref[...] += part

    row = pl.BlockSpec((t, d), lambda i: (i, 0))
    vec = pl.BlockSpec((1, d), lambda i: (0, 0))
    return pl.pallas_call(
        body, out_shape=(jax.ShapeDtypeStruct((s, d), F32), jax.ShapeDtypeStruct((1, d), F32)),
        grid=(s // t,), in_specs=[row, vec, row, row], out_specs=(row, vec),
        compiler_params=_cparams(("arbitrary",)), name=name)(x, g, dh, dres)


def _loss_head(y, tgt, *, t=256):
    s, d = y.shape
    t = _tile(s, t, 8)

    def body(y_ref, t_ref, dy_ref, l_ref):
        e = y_ref[...] - t_ref[...]
        dy_ref[...] = e * (1.0 / d)
        part = jnp.zeros((1, 128), F32) + jnp.sum(e * e) * (0.5 / d)

        @pl.when(pl.program_id(0) == 0)
        def _():
            l_ref[...] = part

        @pl.when(pl.program_id(0) > 0)
        def _():
            l_ref[...] += part

    row = pl.BlockSpec((t, d), lambda i: (i, 0))
    return pl.pallas_call(
        body, out_shape=(jax.ShapeDtypeStruct((s, d), F32), jax.ShapeDtypeStruct((1, 128), F32)),
        grid=(s // t,), in_specs=[row, row], out_specs=(row, pl.BlockSpec((1, 128), lambda i: (0, 0))),
        compiler_params=_cparams(("arbitrary",)), name="loss_head")(y, tgt)


def _pool_mixed(scr, v, halo, first, row0, t):
    scr[0:POOL_HALO, :] = jnp.where(first, 0.0, halo)
    scr[POOL_HALO:POOL_HALO + t, :] = v
    row = row0 + lax.broadcasted_iota(jnp.int32, (t, 1), 0)
    mixed = []
    for g in range(POOL_GROUPS):
        w = 2 ** (g + 1)
        acc = scr[:, g * POOL_GW:(g + 1) * POOL_GW]
        sh = 1
        while sh < w:
            acc = acc + pltpu.roll(acc, sh, 0)
            sh *= 2
        cnt = jnp.minimum(row + 1, w).astype(F32)
        mixed.append(acc[POOL_HALO:POOL_HALO + t, :] / cnt - v[:, g * POOL_GW:(g + 1) * POOL_GW])
    return mixed


def _pool_fwd(p1, pool_w, pool_scale, *, t=256):
    s = p1.shape[0]
    t = _tile(s, t, 128)
    hb = t // POOL_HALO

    def body(pv_ref, halo_ref, pz_ref, pw_ref, sc_ref, y_ref, yt_ref, scr):
        i = pl.program_id(0)
        mixed = _pool_mixed(scr, pv_ref[...], halo_ref[...], i == 0, i * t, t)
        outs = [_dot(mixed[g].astype(BF16), pw_ref[g]) for g in range(POOL_GROUPS)]
        y = jnp.concatenate(outs, axis=1) * sc_ref[...] * _silu(pz_ref[...])
        y_ref[...] = y.astype(BF16)
        yt_ref[...] = y.T.astype(BF16)

    return pl.pallas_call(
        body, out_shape=(jax.ShapeDtypeStruct((s, BRANCH_W), BF16), jax.ShapeDtypeStruct((BRANCH_W, s), BF16)),
        grid=(s // t,),
        in_specs=[pl.BlockSpec((t, BRANCH_W), lambda i: (i, 0)),
                  pl.BlockSpec((POOL_HALO, BRANCH_W), lambda i: (jnp.maximum(i * hb - 1, 0), 0)),
                  pl.BlockSpec((t, BRANCH_W), lambda i: (i, 1)),
                  pl.BlockSpec((POOL_GROUPS, POOL_GW, POOL_GW), lambda i: (0, 0, 0)),
                  pl.BlockSpec((1, BRANCH_W), lambda i: (0, 0))],
        out_specs=(pl.BlockSpec((t, BRANCH_W), lambda i: (i, 0)), pl.BlockSpec((BRANCH_W, t), lambda i: (0, i))),
        scratch_shapes=[pltpu.VMEM((t + POOL_HALO, BRANCH_W), F32)],
        compiler_params=_cparams(("parallel",)), name="pool_fwd")(p1, p1, p1, pool_w, pool_scale)


def _pool_bwd(p1, dy, pool_w, pool_scale, *, t=256):
    s = p1.shape[0]
    t = _tile(s, t, 128)
    hb = t // POOL_HALO
    nt = s // t
    last_hb = s // POOL_HALO - 1

    def body(pv_ref, halo_ref, pz_ref, pzn_ref, dy_ref, dyn_ref, pw_ref, sc_ref, d_ref, gw_ref, gs_ref, scr, scr2, scr3):
        i = pl.program_id(0)
        mixed = _pool_mixed(scr, pv_ref[...], halo_ref[...], i == 0, i * t, t)
        scale = sc_ref[...]
        pz = pz_ref[...]
        dy = dy_ref[...]
        raw = jnp.concatenate([_dot(mixed[g].astype(BF16), pw_ref[g]) for g in range(POOL_GROUPS)], axis=1)
        d_pool = dy * _silu(pz)
        d_ref[:, BRANCH_W:2 * BRANCH_W] = (dy * raw * scale * _dsilu(pz)).astype(BF16)
        gs_part = jnp.sum(d_pool * raw, axis=0, keepdims=True)
        scr2[0:t, :] = d_pool * scale
        scr2[t:t + POOL_HALO, :] = jnp.where(i == nt - 1, 0.0, dyn_ref[...] * _silu(pzn_ref[...]) * scale)
        row = i * t + lax.broadcasted_iota(jnp.int32, (t + POOL_HALO, 1), 0)
        gw_parts = []
        for g in range(POOL_GROUPS):
            w = 2 ** (g + 1)
            sl = slice(g * POOL_GW, (g + 1) * POOL_GW)
            do_g = scr2[:, sl].astype(BF16)
            dm = _dot_nt(do_g, pw_ref[g])
            gw_parts.append(_dot_tn(mixed[g].astype(BF16), do_g[0:t, :]))
            cnt = jnp.minimum(row + 1, w).astype(F32)
            acc = dm / cnt
            sh = 1
            while sh < w:
                acc = acc + pltpu.roll(acc, t + POOL_HALO - sh, 0)
                sh *= 2
            scr3[:, sl] = acc - dm
        d_ref[:, 0:BRANCH_W] = scr3[0:t, :].astype(BF16)

        @pl.when(i == 0)
        def _():
            for g in range(POOL_GROUPS):
                gw_ref[g] = gw_parts[g]
            gs_ref[...] = gs_part

        @pl.when(i > 0)
        def _():
            for g in range(POOL_GROUPS):
                gw_ref[g] += gw_parts[g]
            gs_ref[...] += gs_part

    tile = lambda col: pl.BlockSpec((t, BRANCH_W), lambda i: (i, col))
    nxt = lambda col: pl.BlockSpec((POOL_HALO, BRANCH_W), lambda i: (jnp.minimum((i + 1) * hb, last_hb), col))
    return pl.pallas_call(
        body,
        out_shape=(jax.ShapeDtypeStruct((s, 2 * BRANCH_W), BF16),
                   jax.ShapeDtypeStruct((POOL_GROUPS, POOL_GW, POOL_GW), F32),
                   jax.ShapeDtypeStruct((1, BRANCH_W), F32)),
        grid=(nt,),
        in_specs=[tile(0), pl.BlockSpec((POOL_HALO, BRANCH_W), lambda i: (jnp.maximum(i * hb - 1, 0), 0)),
                  tile(1), nxt(1),
                  pl.BlockSpec((None, t, BRANCH_W), lambda i: (0, i, 0)),
                  pl.BlockSpec((None, POOL_HALO, BRANCH_W), lambda i: (0, jnp.minimum((i + 1) * hb, last_hb), 0)),
                  pl.BlockSpec((POOL_GROUPS, POOL_GW, POOL_GW), lambda i: (0, 0, 0)),
                  pl.BlockSpec((1, BRANCH_W), lambda i: (0, 0))],
        out_specs=(pl.BlockSpec((t, 2 * BRANCH_W), lambda i: (i, 0)),
                   pl.BlockSpec((POOL_GROUPS, POOL_GW, POOL_GW), lambda i: (0, 0, 0)),
                   pl.BlockSpec((1, BRANCH_W), lambda i: (0, 0))),
        scratch_shapes=[pltpu.VMEM((t + POOL_HALO, BRANCH_W), F32)] * 3,
        compiler_params=_cparams(("arbitrary",)), name="pool_bwd")(p1, p1, p1, p1, dy, dy, pool_w, pool_scale)


CONV_COL = 4


def _conv_taps(scr, u, uh, first, t):
    scr[0:CONV_HALO, :] = jnp.where(first, 0.0, uh)
    scr[CONV_HALO:CONV_HALO + t, :] = u
    e = scr[...]
    u1 = pltpu.roll(e, 1, 0)[CONV_HALO:CONV_HALO + t, :]
    u2 = pltpu.roll(e, 2, 0)[CONV_HALO:CONV_HALO + t, :]
    return u2, u1, u


def _conv_fwd(p1, conv_w, *, t=256):
    s = p1.shape[0]
    t = _tile(s, t, 128)
    hb = t // CONV_HALO

    def body(cb_ref, cc_ref, cx_ref, cz_ref, cch_ref, cxh_ref, w_ref, y_ref, yt_ref, scr):
        i = pl.program_id(0)
        u0, u1, u2 = _conv_taps(scr, cc_ref[...] * cx_ref[...], cch_ref[...] * cxh_ref[...], i == 0, t)
        w = w_ref[...]
        y = (w[0:1, :] * u0 + w[1:2, :] * u1 + w[2:3, :] * u2) * cb_ref[...] * _silu(cz_ref[...])
        y_ref[...] = y.astype(BF16)
        yt_ref[...] = y.T.astype(BF16)

    tile = lambda col: pl.BlockSpec((t, BRANCH_W), lambda i: (i, CONV_COL + col))
    prev = lambda col: pl.BlockSpec((CONV_HALO, BRANCH_W), lambda i: (jnp.maximum(i * hb - 1, 0), CONV_COL + col))
    return pl.pallas_call(
        body, out_shape=(jax.ShapeDtypeStruct((s, BRANCH_W), BF16), jax.ShapeDtypeStruct((BRANCH_W, s), BF16)),
        grid=(s // t,),
        in_specs=[tile(0), tile(1), tile(2), tile(3), prev(1), prev(2), pl.BlockSpec((3, BRANCH_W), lambda i: (0, 0))],
        out_specs=(pl.BlockSpec((t, BRANCH_W), lambda i: (i, 0)), pl.BlockSpec((BRANCH_W, t), lambda i: (0, i))),
        scratch_shapes=[pltpu.VMEM((t + CONV_HALO, BRANCH_W), F32)],
        compiler_params=_cparams(("parallel",)), name="conv_fwd")(p1, p1, p1, p1, p1, p1, conv_w)


def _conv_bwd(p1, dy, conv_w, *, t=256):
    s = p1.shape[0]
    t = _tile(s, t, 128)
    hb = t // CONV_HALO
    nt = s // t
    last_hb = s // CONV_HALO - 1

    def body(cb_ref, cc_ref, cx_ref, cz_ref, cch_ref, cxh_ref, cbn_ref, czn_ref, dy_ref, dyn_ref, w_ref,
             d_ref, g0_ref, g1_ref, g2_ref, scr, scr2):
        i = pl.program_id(0)
        cb, cc, cx, cz = cb_ref[...], cc_ref[...], cx_ref[...], cz_ref[...]
        u0, u1, u2 = _conv_taps(scr, cc * cx, cch_ref[...] * cxh_ref[...], i == 0, t)
        w = w_ref[...]
        w0, w1, w2 = w[0:1, :], w[1:2, :], w[2:3, :]
        y = w0 * u0 + w1 * u1 + w2 * u2
        dy = dy_ref[...]
        sz = _silu(cz)
        d_ref[:, 0:BRANCH_W] = (dy * sz * y).astype(BF16)
        d_ref[:, 3 * BRANCH_W:4 * BRANCH_W] = (dy * cb * y * _dsilu(cz)).astype(BF16)
        d_y = dy * sz * cb
        parts = [jnp.sum(d_y * u, axis=0, keepdims=True) for u in (u0, u1, u2)]
        scr2[0:t, :] = d_y
        scr2[t:t + CONV_HALO, :] = jnp.where(i == nt - 1, 0.0, dyn_ref[...] * _silu(czn_ref[...]) * cbn_ref[...])
        e = scr2[...]
        n = t + CONV_HALO
        du = (w2 * e + w1 * pltpu.roll(e, n - 1, 0) + w0 * pltpu.roll(e, n - 2, 0))[0:t, :]
        d_ref[:, BRANCH_W:2 * BRANCH_W] = (du * cx).astype(BF16)
        d_ref[:, 2 * BRANCH_W:3 * BRANCH_W] = (du * cc).astype(BF16)

        @pl.when(i == 0)
        def _():
            g0_ref[...] = parts[0]
            g1_ref[...] = parts[1]
            g2_ref[...] = parts[2]

        @pl.when(i > 0)
        def _():
            g0_ref[...] += parts[0]
            g1_ref[...] += parts[1]
            g2_ref[...] += parts[2]

    tile = lambda col: pl.BlockSpec((t, BRANCH_W), lambda i: (i, CONV_COL + col))
    prev = lambda col: pl.BlockSpec((CONV_HALO, BRANCH_W), lambda i: (jnp.maximum(i * hb - 1, 0), CONV_COL + col))
    nxt = lambda col: pl.BlockSpec((CONV_HALO, BRANCH_W), lambda i: (jnp.minimum((i + 1) * hb, last_hb), CONV_COL + col))
    vec = pl.BlockSpec((1, BRANCH_W), lambda i: (0, 0))
    gshape = jax.ShapeDtypeStruct((1, BRANCH_W), F32)
    return pl.pallas_call(
        body, out_shape=(jax.ShapeDtypeStruct((s, 4 * BRANCH_W), BF16), gshape, gshape, gshape),
        grid=(nt,),
        in_specs=[tile(0), tile(1), tile(2), tile(3), prev(1), prev(2), nxt(0), nxt(3),
                  pl.BlockSpec((None, t, BRANCH_W), lambda i: (2, i, 0)),
                  pl.BlockSpec((None, CONV_HALO, BRANCH_W), lambda i: (2, jnp.minimum((i + 1) * hb, last_hb), 0)),
                  pl.BlockSpec((3, BRANCH_W), lambda i: (0, 0))],
        out_specs=(pl.BlockSpec((t, 4 * BRANCH_W), lambda i: (i, 0)), vec, vec, vec),
        scratch_shapes=[pltpu.VMEM((t + CONV_HALO, BRANCH_W), F32)] * 2,
        compiler_params=_cparams(("arbitrary",)), name="conv_bwd")(p1, p1, p1, p1, p1, p1, p1, p1, dy, dy, conv_w)


def _mla_prep_fwd(p1, kr, ctab, stab, qa_g, kva_g, w_uq, w_ukv, q_g, k_g, *, t=256):
    s = p1.shape[0]
    t = _tile(s, t, 128)
    ql, kvl = qa_g.shape[1], kva_g.shape[1]
    assert ql == kvl and 2048 % ql == 0
    cq_blk = 2048 // ql

    def body(cq_ref, ckv_ref, kr_ref, c_ref, s_ref, qag_ref, kvag_ref, wuq_ref, wukv_ref, qg_ref, kg_ref,
             qf_ref, kf_ref, v_ref, vt_ref):
        q_raw = _dot(_rms(cq_ref[...], qag_ref[...], ql).astype(BF16), wuq_ref[...])
        kv_raw = _dot(_rms(ckv_ref[...], kvag_ref[...], kvl).astype(BF16), wukv_ref[...])
        krp = kr_ref[...]
        ct, st = c_ref[...], s_ref[...]
        for h in range(MLA_HEADS):
            qh = q_raw[:, h * HEAD_PAD:(h + 1) * HEAD_PAD]
            qf_ref[h] = _rope(_rms(qh, qg_ref[...], QK_HEAD), ct, st).astype(BF16)
            kh = jnp.concatenate([kv_raw[:, h * HEAD_PAD:h * HEAD_PAD + QK_NOPE], krp], axis=1)
            kf_ref[h] = _rope(_rms(kh, kg_ref[...], QK_HEAD), ct, st).astype(BF16)
            vh = kv_raw[:, h * HEAD_PAD + QK_NOPE:(h + 1) * HEAD_PAD]
            v_ref[h] = vh.astype(BF16)
            vt_ref[h] = vh.T.astype(BF16)

    full = lambda shp: pl.BlockSpec(shp, lambda i: tuple(0 for _ in shp))
    return pl.pallas_call(
        body,
        out_shape=(jax.ShapeDtypeStruct((MLA_HEADS, s, HEAD_PAD), BF16), jax.ShapeDtypeStruct((MLA_HEADS, s, HEAD_PAD), BF16),
                   jax.ShapeDtypeStruct((MLA_HEADS, s, V_HEAD), BF16), jax.ShapeDtypeStruct((MLA_HEADS, V_HEAD, s), BF16)),
        grid=(s // t,),
        in_specs=[pl.BlockSpec((t, ql), lambda i: (i, cq_blk)), pl.BlockSpec((t, kvl), lambda i: (i, cq_blk + 1)),
                  pl.BlockSpec((t, 128), lambda i: (i, 0)),
                  pl.BlockSpec((t, HEAD_PAD), lambda i: (i, 0)), pl.BlockSpec((t, HEAD_PAD), lambda i: (i, 0)),
                  full((1, ql)), full((1, kvl)), full(w_uq.shape), full(w_ukv.shape), full((1, HEAD_PAD)), full((1, HEAD_PAD))],
        out_specs=(pl.BlockSpec((MLA_HEADS, t, HEAD_PAD), lambda i: (0, i, 0)),
                   pl.BlockSpec((MLA_HEADS, t, HEAD_PAD), lambda i: (0, i, 0)),
                   pl.BlockSpec((MLA_HEADS, t, V_HEAD), lambda i: (0, i, 0)),
                   pl.BlockSpec((MLA_HEADS, V_HEAD, t), lambda i: (0, 0, i))),
        compiler_params=_cparams(("parallel",)), name="mla_prep_fwd")(
            p1, p1, kr, ctab, stab, qa_g, kva_g, w_uq, w_ukv, q_g, k_g)


def _mla_prep_bwd(p1, kr, ctab, stab, qa_g, kva_g, w_uq, w_ukv, q_g, k_g, dqf, dkf, dv, *, t=256):
    s = p1.shape[0]
    t = _tile(s, t, 128)
    ql, kvl = qa_g.shape[1], kva_g.shape[1]
    cq_blk = 2048 // ql
    nq = MLA_HEADS * HEAD_PAD

    def body(cq_ref, ckv_ref, kr_ref, c_ref, s_ref, qag_ref, kvag_ref, wuq_ref, wukv_ref, qg_ref, kg_ref,
             dqf_ref, dkf_ref, dv_ref,
             dc_ref, dkr_ref, dqraw_ref, dkvraw_ref, cqnt_ref, ckvnt_ref, gqa_ref, gkva_ref, gqg_ref, gkg_ref):
        i = pl.program_id(0)
        cq, ckv = cq_ref[...], ckv_ref[...]
        cqn = _rms(cq, qag_ref[...], ql)
        ckvn = _rms(ckv, kvag_ref[...], kvl)
        cqnt_ref[...] = cqn.T.astype(BF16)
        ckvnt_ref[...] = ckvn.T.astype(BF16)
        q_raw = _dot(cqn.astype(BF16), wuq_ref[...])
        kv_raw = _dot(ckvn.astype(BF16), wukv_ref[...])
        krp = kr_ref[...]
        ct, st = c_ref[...], s_ref[...]
        gqg = jnp.zeros((1, HEAD_PAD), F32)
        gkg = jnp.zeros((1, HEAD_PAD), F32)
        dkr = jnp.zeros((t, HEAD_PAD - QK_NOPE), F32)
        dq_parts, dkv_parts = [], []
        for h in range(MLA_HEADS):
            qh = q_raw[:, h * HEAD_PAD:(h + 1) * HEAD_PAD]
            dx, dg = _rms_bwd(qh, qg_ref[...], QK_HEAD, _rope_bwd(dqf_ref[h], ct, st))
            gqg = gqg + jnp.sum(dg, axis=0, keepdims=True)
            dq_parts.append(dx)
            kh = jnp.concatenate([kv_raw[:, h * HEAD_PAD:h * HEAD_PAD + QK_NOPE], krp], axis=1)
            dx, dg = _rms_bwd(kh, kg_ref[...], QK_HEAD, _rope_bwd(dkf_ref[h], ct, st))
            gkg = gkg + jnp.sum(dg, axis=0, keepdims=True)
            dkr = dkr + dx[:, QK_NOPE:HEAD_PAD]
            dkv_parts += [dx[:, 0:QK_NOPE], dv_ref[h]]
        dq_raw = jnp.concatenate(dq_parts, axis=1).astype(BF16)
        dkv_raw = jnp.concatenate(dkv_parts, axis=1).astype(BF16)
        dqraw_ref[...] = dq_raw
        dkvraw_ref[...] = dkv_raw
        dkr_ref[...] = dkr.astype(BF16)
        dcq, gqa = _rms_bwd(cq, qag_ref[...], ql, _dot_nt(dq_raw, wuq_ref[...]))
        dckv, gkva = _rms_bwd(ckv, kvag_ref[...], kvl, _dot_nt(dkv_raw, wukv_ref[...]))
        dc_ref[:, 0:ql] = dcq.astype(BF16)
        dc_ref[:, ql:ql + kvl] = dckv.astype(BF16)
        gqa = jnp.sum(gqa, axis=0, keepdims=True)
        gkva = jnp.sum(gkva, axis=0, keepdims=True)

        @pl.when(i == 0)
        def _():
            gqa_ref[...] = gqa
            gkva_ref[...] = gkva
            gqg_ref[...] = gqg
            gkg_ref[...] = gkg

        @pl.when(i > 0)
        def _():
            gqa_ref[...] += gqa
            gkva_ref[...] += gkva
            gqg_ref[...] += gqg
            gkg_ref[...] += gkg

    full = lambda shp: pl.BlockSpec(shp, lambda i: tuple(0 for _ in shp))
    hblk = lambda w: pl.BlockSpec((MLA_HEADS, t, w), lambda i: (0, i, 0))
    sds = jax.ShapeDtypeStruct
    return pl.pallas_call(
        body,
        out_shape=(sds((s, ql + kvl), BF16), sds((s, 128), BF16), sds((s, nq), BF16), sds((s, nq), BF16),
                   sds((ql, s), BF16), sds((kvl, s), BF16),
                   sds((1, ql), F32), sds((1, kvl), F32), sds((1, HEAD_PAD), F32), sds((1, HEAD_PAD), F32)),
        grid=(s // t,),
        in_specs=[pl.BlockSpec((t, ql), lambda i: (i, cq_blk)), pl.BlockSpec((t, kvl), lambda i: (i, cq_blk + 1)),
                  pl.BlockSpec((t, 128), lambda i: (i, 0)),
                  pl.BlockSpec((t, HEAD_PAD), lambda i: (i, 0)), pl.BlockSpec((t, HEAD_PAD), lambda i: (i, 0)),
                  full((1, ql)), full((1, kvl)), full(w_uq.shape), full(w_ukv.shape), full((1, HEAD_PAD)), full((1, HEAD_PAD)),
                  hblk(HEAD_PAD), hblk(HEAD_PAD), hblk(V_HEAD)],
        out_specs=(pl.BlockSpec((t, ql + kvl), lambda i: (i, 0)), pl.BlockSpec((t, 128), lambda i: (i, 0)),
                   pl.BlockSpec((t, nq), lambda i: (i, 0)), pl.BlockSpec((t, nq), lambda i: (i, 0)),
                   pl.BlockSpec((ql, t), lambda i: (0, i)), pl.BlockSpec((kvl, t), lambda i: (0, i)),
                   full((1, ql)), full((1, kvl)), full((1, HEAD_PAD)), full((1, HEAD_PAD))),
        compiler_params=_cparams(("arbitrary",)), name="mla_prep_bwd")(
            p1, p1, kr, ctab, stab, qa_g, kva_g, w_uq, w_ukv, q_g, k_g, dqf, dkf, dv)


MZ_BLK128 = 3072 // 128
ATT_SCALE = QK_HEAD ** -0.5


HPS = 2


def _causal_pairs(nq, by_query):
    if by_query:
        prs = [(qi, ki) for qi in range(nq) for ki in range(qi + 1)]
    else:
        prs = [(qi, ki) for ki in range(nq) for qi in range(ki, nq)]
    return (jnp.asarray([p[0] for p in prs], jnp.int32), jnp.asarray([p[1] for p in prs], jnp.int32), len(prs))


def _comm_hooks(comm, cins, couts, csems, first, middle, last):
    ph = comm["phases"]
    assert len(ph) in (2, 3)

    def at(pred, phase):
        @pl.when(pred)
        def _():
            phase(cins, couts, *csems)

    return (lambda: at(first, ph[0])), (lambda: [at(middle, ph[1])] if len(ph) == 3 else None), (lambda: at(last, ph[-1]))


def _comm_extras(comm):
    if comm is None:
        return [], [], [], ()
    sems = [pltpu.SemaphoreType.DMA((comm["nsem"],)), pltpu.SemaphoreType.DMA((comm["nsem"],))]
    return [ANY] * len(comm["ins"]), [ANY] * len(comm["outs"]), sems, tuple(comm["outs"])


def _attn_fwd(qf, kf, vt, p1, *, tq, comm=None):
    nh, s, _ = qf.shape
    nq = s // tq
    qtab, ktab, npairs = _causal_pairs(nq, True)
    wv = HPS * V_HEAD
    c_in, c_out, c_sem, c_shapes = _comm_extras(comm)
    nci, nco = len(c_in), len(c_out)

    def body(*refs):
        qt_ref, kt_ref, q_ref, k_ref, vt_ref, mz_ref = refs[:6]
        y_ref, yt_ref, o_ref, lse_ref, lset_ref = refs[6 + nci:11 + nci]
        m_scr, l_scr, acc_scr = refs[11 + nci + nco:14 + nci + nco]
        pr = pl.program_id(1)
        qi, ki = qt_ref[pr], kt_ref[pr]
        if comm is not None:
            hg = pl.program_id(0)
            last_hg = nh // HPS - 1
            c_first, c_mid, c_last = _comm_hooks(comm, refs[6:6 + nci], refs[11 + nci:11 + nci + nco], refs[14 + nci + nco:],
                                                 (hg == 0) & (pr == 0), (hg == last_hg) & (pr == 0),
                                                 (hg == last_hg) & (pr == npairs - 1))
            c_first()

        @pl.when(ki == 0)
        def _():
            m_scr[...] = jnp.full((HPS, 1, tq), NEG, F32)
            l_scr[...] = jnp.zeros((HPS, 1, tq), F32)
            acc_scr[...] = jnp.zeros((HPS, V_HEAD, tq), F32)

        def step(diagonal):
            for u in range(HPS):
                st = _dot_nt(k_ref[u], q_ref[u]) * ATT_SCALE
                if diagonal:
                    r = lax.broadcasted_iota(jnp.int32, (tq, tq), 0)
                    c = lax.broadcasted_iota(jnp.int32, (tq, tq), 1)
                    st = jnp.where(r <= c, st, NEG)
                m_old = m_scr[u]
                m_new = jnp.maximum(m_old, jnp.max(st, axis=0, keepdims=True))
                alpha = jnp.exp(m_old - m_new)
                pt = jnp.exp(st - m_new)
                l_scr[u] = alpha * l_scr[u] + jnp.sum(pt, axis=0, keepdims=True)
                acc_scr[u] = alpha * acc_scr[u] + _dot(vt_ref[u], pt.astype(BF16))
                m_scr[u] = m_new

        @pl.when(ki < qi)
        def _():
            step(False)

        @pl.when(ki == qi)
        def _():
            step(True)
            outs = []
            for u in range(HPS):
                l = l_scr[u]
                outs.append((acc_scr[u] / l).T)
                lset = m_scr[u] + jnp.log(l)
                lset_ref[u] = lset
                lse_ref[u] = jnp.broadcast_to(lset, (128, tq)).T[:, 0:1]
            o = jnp.concatenate(outs, axis=1)
            o_ref[...] = o
            y = o * _silu(mz_ref[...])
            y_ref[...] = y.astype(BF16)
            yt_ref[...] = y.T.astype(BF16)

        if comm is not None:
            c_mid()
            c_last()

    sds = jax.ShapeDtypeStruct
    gs = pltpu.PrefetchScalarGridSpec(
        num_scalar_prefetch=2, grid=(nh // HPS, npairs),
        in_specs=[pl.BlockSpec((HPS, tq, HEAD_PAD), lambda h, p, qt, kt: (h, qt[p], 0)),
                  pl.BlockSpec((HPS, tq, HEAD_PAD), lambda h, p, qt, kt: (h, kt[p], 0)),
                  pl.BlockSpec((HPS, V_HEAD, tq), lambda h, p, qt, kt: (h, 0, kt[p])),
                  pl.BlockSpec((tq, wv), lambda h, p, qt, kt: (qt[p], MZ_BLK128 // HPS + h))] + c_in,
        out_specs=(pl.BlockSpec((tq, wv), lambda h, p, qt, kt: (qt[p], h)),
                   pl.BlockSpec((wv, tq), lambda h, p, qt, kt: (h, qt[p])),
                   pl.BlockSpec((tq, wv), lambda h, p, qt, kt: (qt[p], h)),
                   pl.BlockSpec((HPS, tq, 1), lambda h, p, qt, kt: (h, qt[p], 0)),
                   pl.BlockSpec((HPS, 1, tq), lambda h, p, qt, kt: (h, 0, qt[p]))) + tuple(c_out),
        scratch_shapes=[pltpu.VMEM((HPS, 1, tq), F32), pltpu.VMEM((HPS, 1, tq), F32), pltpu.VMEM((HPS, V_HEAD, tq), F32)] + c_sem)
    sem = ("parallel", "arbitrary") if comm is None else ("arbitrary", "arbitrary")
    res = pl.pallas_call(
        body,
        out_shape=(sds((s, BRANCH_W), BF16), sds((BRANCH_W, s), BF16), sds((s, BRANCH_W), F32),
                   sds((nh, s, 1), F32), sds((nh, 1, s), F32)) + c_shapes,
        grid_spec=gs, compiler_params=_cparams(sem),
        name="attn_fwd" if comm is None else "attn_fwd_with_gather")(qtab, ktab, qf, kf, vt, p1, *(comm["ins"] if comm else ()))
    return res[:5], list(res[5:])


def _attn_bwd_pre(dy4, o, p1, *, t=256):
    s = o.shape[0]
    t = _tile(s, t, 128)

    def body(dy_ref, o_ref, mz_ref, do_ref, dmz_ref, dl_ref, dlt_ref):
        dy, o_, mz = dy_ref[...], o_ref[...], mz_ref[...]
        do = dy * _silu(mz)
        do_ref[...] = do.astype(BF16)
        dmz_ref[...] = (dy * o_ * _dsilu(mz)).astype(BF16)
        prod = do * o_
        for h in range(MLA_HEADS):
            dl = jnp.sum(prod[:, h * V_HEAD:(h + 1) * V_HEAD], axis=-1, keepdims=True)
            dl_ref[h] = dl
            dlt_ref[h] = jnp.broadcast_to(dl, (t, 128)).T[0:1, :]

    sds = jax.ShapeDtypeStruct
    return pl.pallas_call(
        body,
        out_shape=(sds((s, BRANCH_W), BF16), sds((s, BRANCH_W), BF16), sds((MLA_HEADS, s, 1), F32), sds((MLA_HEADS, 1, s), F32)),
        grid=(s // t,),
        in_specs=[pl.BlockSpec((None, t, BRANCH_W), lambda i: (1, i, 0)), pl.BlockSpec((t, BRANCH_W), lambda i: (i, 0)),
                  pl.BlockSpec((t, BRANCH_W), lambda i: (i, 3))],
        out_specs=(pl.BlockSpec((t, BRANCH_W), lambda i: (i, 0)), pl.BlockSpec((t, BRANCH_W), lambda i: (i, 0)),
                   pl.BlockSpec((MLA_HEADS, t, 1), lambda i: (0, i, 0)), pl.BlockSpec((MLA_HEADS, 1, t), lambda i: (0, 0, i))),
        compiler_params=_cparams(("parallel",)), name="attn_bwd_pre")(dy4, o, p1)


def _attn_bwd_dq(qf, kf, vv, do, lse, delta, *, tq, comm=None):
    nh, s, _ = qf.shape
    nq = s // tq
    qtab, ktab, npairs = _causal_pairs(nq, True)
    wv = HPS * V_HEAD
    c_in, c_out, c_sem, c_shapes = _comm_extras(comm)
    nci, nco = len(c_in), len(c_out)

    def body(*refs):
        qt_ref, kt_ref, q_ref, k_ref, v_ref, do_ref, lse_ref, dl_ref = refs[:8]
        dq_ref = refs[8 + nci]
        acc_scr = refs[9 + nci + nco]
        pr = pl.program_id(1)
        qi, ki = qt_ref[pr], kt_ref[pr]
        if comm is not None:
            hg = pl.program_id(0)
            last_hg = nh // HPS - 1
            c_first, c_mid, c_last = _comm_hooks(comm, refs[8:8 + nci], refs[9 + nci:9 + nci + nco], refs[10 + nci + nco:],
                                                 (hg == 0) & (pr == 0), (hg == last_hg) & (pr == 0),
                                                 (hg == last_hg) & (pr == npairs - 1))
            c_first()

        @pl.when(ki == 0)
        def _():
            acc_scr[...] = jnp.zeros((HPS, tq, HEAD_PAD), F32)

        def step(diagonal):
            for u in range(HPS):
                k = k_ref[u]
                sc = _dot_nt(q_ref[u], k) * ATT_SCALE
                p = jnp.exp(sc - lse_ref[u])
                if diagonal:
                    r = lax.broadcasted_iota(jnp.int32, (tq, tq), 0)
                    c = lax.broadcasted_iota(jnp.int32, (tq, tq), 1)
                    p = jnp.where(c <= r, p, 0.0)
                dp = _dot_nt(do_ref[:, u * V_HEAD:(u + 1) * V_HEAD], v_ref[u])
                ds = p * (dp - dl_ref[u]) * ATT_SCALE
                acc_scr[u] += _dot(ds.astype(BF16), k)

        @pl.when(ki < qi)
        def _():
            step(False)

        @pl.when(ki == qi)
        def _():
            step(True)
            dq_ref[...] = acc_scr[...]

        if comm is not None:
            c_mid()
            c_last()

    gs = pltpu.PrefetchScalarGridSpec(
        num_scalar_prefetch=2, grid=(nh // HPS, npairs),
        in_specs=[pl.BlockSpec((HPS, tq, HEAD_PAD), lambda h, p, qt, kt: (h, qt[p], 0)),
                  pl.BlockSpec((HPS, tq, HEAD_PAD), lambda h, p, qt, kt: (h, kt[p], 0)),
                  pl.BlockSpec((HPS, tq, V_HEAD), lambda h, p, qt, kt: (h, kt[p], 0)),
                  pl.BlockSpec((tq, wv), lambda h, p, qt, kt: (qt[p], h)),
                  pl.BlockSpec((HPS, tq, 1), lambda h, p, qt, kt: (h, qt[p], 0)),
                  pl.BlockSpec((HPS, tq, 1), lambda h, p, qt, kt: (h, qt[p], 0))] + c_in,
        out_specs=(pl.BlockSpec((HPS, tq, HEAD_PAD), lambda h, p, qt, kt: (h, qt[p], 0)),) + tuple(c_out),
        scratch_shapes=[pltpu.VMEM((HPS, tq, HEAD_PAD), F32)] + c_sem)
    sem = ("parallel", "arbitrary") if comm is None else ("arbitrary", "arbitrary")
    res = pl.pallas_call(
        body, out_shape=(jax.ShapeDtypeStruct((nh, s, HEAD_PAD), F32),) + c_shapes, grid_spec=gs,
        compiler_params=_cparams(sem), name="attn_bwd_dq" if comm is None else "attn_bwd_dq_with_to_owner")(
            qtab, ktab, qf, kf, vv, do, lse, delta, *(comm["ins"] if comm else ()))
    return res[0], list(res[1:])


def _attn_bwd_dkv(qf, kf, vv, do, lset, deltat, *, tq, comm=None):
    nh, s, _ = qf.shape
    nq = s // tq
    qtab, ktab, npairs = _causal_pairs(nq, False)
    wv = HPS * V_HEAD
    c_in, c_out, c_sem, c_shapes = _comm_extras(comm)
    nci, nco = len(c_in), len(c_out)

    def body(*refs):
        qt_ref, kt_ref, k_ref, v_ref, q_ref, do_ref, lse_ref, dl_ref = refs[:8]
        dk_ref, dv_ref = refs[8 + nci:10 + nci]
        dk_scr, dv_scr = refs[10 + nci + nco:12 + nci + nco]
        pr = pl.program_id(1)
        qi, ki = qt_ref[pr], kt_ref[pr]
        if comm is not None:
            hg = pl.program_id(0)
            last_hg = nh // HPS - 1
            c_first, c_mid, c_last = _comm_hooks(comm, refs[8:8 + nci], refs[10 + nci:10 + nci + nco], refs[12 + nci + nco:],
                                                 (hg == 0) & (pr == 0), (hg == last_hg) & (pr == 0),
                                                 (hg == last_hg) & (pr == npairs - 1))
            c_first()

        def step(diagonal):
            for u in range(HPS):
                q = q_ref[u]
                do_ = do_ref[:, u * V_HEAD:(u + 1) * V_HEAD]
                st = _dot_nt(k_ref[u], q) * ATT_SCALE
                pt = jnp.exp(st - lse_ref[u])
                if diagonal:
                    r = lax.broadcasted_iota(jnp.int32, (tq, tq), 0)
                    c = lax.broadcasted_iota(jnp.int32, (tq, tq), 1)
                    pt = jnp.where(r <= c, pt, 0.0)
                dpt = _dot_nt(v_ref[u], do_)
                dst = pt * (dpt - dl_ref[u]) * ATT_SCALE
                if diagonal:
                    dv_scr[u] = _dot(pt.astype(BF16), do_)
                    dk_scr[u] = _dot(dst.astype(BF16), q)
                else:
                    dv_scr[u] += _dot(pt.astype(BF16), do_)
                    dk_scr[u] += _dot(dst.astype(BF16), q)

        @pl.when(qi == ki)
        def _():
            step(True)

        @pl.when(qi > ki)
        def _():
            step(False)

        @pl.when(qi == nq - 1)
        def _():
            dk_ref[...] = dk_scr[...]
            dv_ref[...] = dv_scr[...]

        if comm is not None:
            c_mid()
            c_last()

    sds = jax.ShapeDtypeStruct
    gs = pltpu.PrefetchScalarGridSpec(
        num_scalar_prefetch=2, grid=(nh // HPS, npairs),
        in_specs=[pl.BlockSpec((HPS, tq, HEAD_PAD), lambda h, p, qt, kt: (h, kt[p], 0)),
                  pl.BlockSpec((HPS, tq, V_HEAD), lambda h, p, qt, kt: (h, kt[p], 0)),
                  pl.BlockSpec((HPS, tq, HEAD_PAD), lambda h, p, qt, kt: (h, qt[p], 0)),
                  pl.BlockSpec((tq, wv), lambda h, p, qt, kt: (qt[p], h)),
                  pl.BlockSpec((HPS, 1, tq), lambda h, p, qt, kt: (h, 0, qt[p])),
                  pl.BlockSpec((HPS, 1, tq), lambda h, p, qt, kt: (h, 0, qt[p]))] + c_in,
        out_specs=(pl.BlockSpec((HPS, tq, HEAD_PAD), lambda h, p, qt, kt: (h, kt[p], 0)),
                   pl.BlockSpec((HPS, tq, V_HEAD), lambda h, p, qt, kt: (h, kt[p], 0))) + tuple(c_out),
        scratch_shapes=[pltpu.VMEM((HPS, tq, HEAD_PAD), F32), pltpu.VMEM((HPS, tq, V_HEAD), F32)] + c_sem)
    sem = ("parallel", "arbitrary") if comm is None else ("arbitrary", "arbitrary")
    res = pl.pallas_call(
        body, out_shape=(sds((nh, s, HEAD_PAD), F32), sds((nh, s, V_HEAD), F32)) + c_shapes, grid_spec=gs,
        compiler_params=_cparams(sem), name="attn_bwd_dkv" if comm is None else "attn_bwd_dkv_with_exchange")(
            qtab, ktab, kf, vv, qf, do, lset, deltat, *(comm["ins"] if comm else ()))
    return res[0], res[1], list(res[2:])


XATT_SCALE = XATTN_HEAD_DIM ** -0.5
XQ_COL = 8


def _memkv_prep(mem_kv, k_g):
    m = mem_kv.shape[0]

    def body(kv_ref, g_ref, k_ref, v_ref):
        for h in range(XATTN_HEADS):
            sl = slice(h * XATTN_HEAD_DIM, (h + 1) * XATTN_HEAD_DIM)
            k_ref[:, sl] = _rms(kv_ref[:, sl], g_ref[...], XATTN_HEAD_DIM).astype(BF16)
        v_ref[...] = kv_ref[:, BRANCH_W:2 * BRANCH_W].astype(BF16)

    sds = jax.ShapeDtypeStruct
    return pl.pallas_call(body, out_shape=(sds((m, BRANCH_W), BF16), sds((m, BRANCH_W), BF16)),
                          compiler_params=_cparams(), name="memkv_prep")(mem_kv, k_g)


def _memkv_prep_bwd(mem_kv, k_g, dk, dv):
    m = mem_kv.shape[0]

    def body(kv_ref, g_ref, dk_ref, dv_ref, d_ref, gk_ref):
        gk = jnp.zeros((1, XATTN_HEAD_DIM), F32)
        for h in range(XATTN_HEADS):
            sl = slice(h * XATTN_HEAD_DIM, (h + 1) * XATTN_HEAD_DIM)
            dx, dg = _rms_bwd(kv_ref[:, sl], g_ref[...], XATTN_HEAD_DIM, dk_ref[:, sl])
            d_ref[:, sl] = dx.astype(BF16)
            gk = gk + jnp.sum(dg, axis=0, keepdims=True)
        d_ref[:, BRANCH_W:2 * BRANCH_W] = dv_ref[...].astype(BF16)
        gk_ref[...] = gk

    sds = jax.ShapeDtypeStruct
    return pl.pallas_call(body, out_shape=(sds((m, 2 * BRANCH_W), BF16), sds((1, XATTN_HEAD_DIM), F32)),
                          compiler_params=_cparams(), name="memkv_prep_bwd")(mem_kv, k_g, dk, dv)


def _xattn_probs(xq, k_ref, qg, h):
    sl = slice(h * XATTN_HEAD_DIM, (h + 1) * XATTN_HEAD_DIM)
    q = _rms(xq[:, sl], qg, XATTN_HEAD_DIM).astype(BF16)
    sc = _dot_nt(q, k_ref[:, sl]) * XATT_SCALE
    e = jnp.exp(sc - jnp.max(sc, axis=-1, keepdims=True))
    return q, e / jnp.sum(e, axis=-1, keepdims=True)


def _xattn_fwd(p1, kx, vx, q_g, *, t=256):
    s = p1.shape[0]
    m = kx.shape[0]
    t = _tile(s, t, 128)

    def body(xq_ref, xz_ref, k_ref, v_ref, g_ref, y_ref, yt_ref):
        xq = xq_ref[...]
        outs = []
        for h in range(XATTN_HEADS):
            _, p = _xattn_probs(xq, k_ref, g_ref[...], h)
            outs.append(_dot(p.astype(BF16), v_ref[:, h * XATTN_HEAD_DIM:(h + 1) * XATTN_HEAD_DIM]))
        y = jnp.concatenate(outs, axis=1) * _silu(xz_ref[...])
        y_ref[...] = y.astype(BF16)
        yt_ref[...] = y.T.astype(BF16)

    full = lambda shp: pl.BlockSpec(shp, lambda i: tuple(0 for _ in shp))
    sds = jax.ShapeDtypeStruct
    return pl.pallas_call(
        body, out_shape=(sds((s, BRANCH_W), BF16), sds((BRANCH_W, s), BF16)), grid=(s // t,),
        in_specs=[pl.BlockSpec((t, BRANCH_W), lambda i: (i, XQ_COL)), pl.BlockSpec((t, BRANCH_W), lambda i: (i, XQ_COL + 1)),
                  full((m, BRANCH_W)), full((m, BRANCH_W)), full((1, XATTN_HEAD_DIM))],
        out_specs=(pl.BlockSpec((t, BRANCH_W), lambda i: (i, 0)), pl.BlockSpec((BRANCH_W, t), lambda i: (0, i))),
        compiler_params=_cparams(("parallel",)), name="xattn_fwd")(p1, p1, kx, vx, q_g)


def _xattn_bwd(p1, dy4, kx, vx, q_g, *, t=256):
    s = p1.shape[0]
    m = kx.shape[0]
    t = _tile(s, t, 128)

    def body(xq_ref, xz_ref, dy_ref, k_ref, v_ref, g_ref, d_ref, dk_ref, dv_ref, gq_ref):
        i = pl.program_id(0)
        xq, xz, dy = xq_ref[...], xz_ref[...], dy_ref[...]
        do = dy * _silu(xz)
        gq = jnp.zeros((1, XATTN_HEAD_DIM), F32)
        outs, dks, dvs = [], [], []
        for h in range(XATTN_HEADS):
            sl = slice(h * XATTN_HEAD_DIM, (h + 1) * XATTN_HEAD_DIM)
            q, p = _xattn_probs(xq, k_ref, g_ref[...], h)
            pb = p.astype(BF16)
            outs.append(_dot(pb, v_ref[:, sl]))
            do_h = do[:, sl].astype(BF16)
            dvs.append(_dot_tn(pb, do_h))
            dp = _dot_nt(do_h, v_ref[:, sl])
            ds = (p * (dp - jnp.sum(p * dp, axis=-1, keepdims=True)) * XATT_SCALE).astype(BF16)
            dks.append(_dot_tn(ds, q))
            dx, dg = _rms_bwd(xq[:, sl], g_ref[...], XATTN_HEAD_DIM, _dot(ds, k_ref[:, sl]))
            d_ref[:, sl] = dx.astype(BF16)
            gq = gq + jnp.sum(dg, axis=0, keepdims=True)
        o = jnp.concatenate(outs, axis=1)
        d_ref[:, BRANCH_W:2 * BRANCH_W] = (dy * o * _dsilu(xz)).astype(BF16)
        dk = jnp.concatenate(dks, axis=1)
        dv = jnp.concatenate(dvs, axis=1)

        @pl.when(i == 0)
        def _():
            dk_ref[...] = dk
            dv_ref[...] = dv
            gq_ref[...] = gq

        @pl.when(i > 0)
        def _():
            dk_ref[...] += dk
            dv_ref[...] += dv
            gq_ref[...] += gq

    full = lambda shp: pl.BlockSpec(shp, lambda i: tuple(0 for _ in shp))
    sds = jax.ShapeDtypeStruct
    return pl.pallas_call(
        body, out_shape=(sds((s, 2 * BRANCH_W), BF16), sds((m, BRANCH_W), F32), sds((m, BRANCH_W), F32), sds((1, XATTN_HEAD_DIM), F32)),
        grid=(s // t,),
        in_specs=[pl.BlockSpec((t, BRANCH_W), lambda i: (i, XQ_COL)), pl.BlockSpec((t, BRANCH_W), lambda i: (i, XQ_COL + 1)),
                  pl.BlockSpec((None, t, BRANCH_W), lambda i: (3, i, 0)),
                  full((m, BRANCH_W)), full((m, BRANCH_W)), full((1, XATTN_HEAD_DIM))],
        out_specs=(pl.BlockSpec((t, 2 * BRANCH_W), lambda i: (i, 0)), full((m, BRANCH_W)), full((m, BRANCH_W)),
                   full((1, XATTN_HEAD_DIM))),
        compiler_params=_cparams(("arbitrary",)), name="xattn_bwd")(p1, p1, dy4, kx, vx, q_g)


def _gate_fwd(ystack, w_branch, gp, gate_b, *, tm=512, tn=1024):
    _, s, _ = ystack.shape
    d = w_branch.shape[2]
    tm, tn = _tile(s, tm, 128), _tile(d, tn)
    nj = d // tn

    def body(y_ref, w_ref, gp_ref, gb_ref, o_ref, ot_ref, acc_scr):
        b = pl.program_id(2)
        part = jax.nn.sigmoid(gp_ref[...] + gb_ref[...]) * _dot(y_ref[...], w_ref[...])

        @pl.when(b == 0)
        def _():
            acc_scr[...] = part

        @pl.when(b > 0)
        def _():
            acc_scr[...] += part

        @pl.when(b == N_BRANCH - 1)
        def _():
            acc = acc_scr[...]
            o_ref[...] = acc.astype(BF16)
            ot_ref[...] = acc.T.astype(BF16)

    sds = jax.ShapeDtypeStruct
    return pl.pallas_call(
        body, out_shape=(sds((s, d), BF16), sds((d, s), BF16)), grid=(s // tm, nj, N_BRANCH),
        in_specs=[pl.BlockSpec((None, tm, BRANCH_W), lambda i, j, b: (b, i, 0)),
                  pl.BlockSpec((None, BRANCH_W, tn), lambda i, j, b: (b, 0, j)),
                  pl.BlockSpec((tm, tn), lambda i, j, b: (i, b * nj + j)),
                  pl.BlockSpec((1, tn), lambda i, j, b: (0, b * nj + j))],
        out_specs=(pl.BlockSpec((tm, tn), lambda i, j, b: (i, j)), pl.BlockSpec((tn, tm), lambda i, j, b: (j, i))),
        scratch_shapes=[pltpu.VMEM((tm, tn), F32)],
        compiler_params=_cparams(("parallel", "parallel", "arbitrary")), name="gate_fwd")(ystack, w_branch, gp, gate_b)


def _gate_bwd(ystack, w_branch, gp, gate_b, dm, *, tm=512, tn=1024):
    _, s, _ = ystack.shape
    d = w_branch.shape[2]
    tm, tn = _tile(s, tm, 128), _tile(d, tn)
    nj = d // tn

    def body(y_ref, w_ref, gp_ref, gb_ref, dm_ref, dp_ref, dg_ref, gb_out_ref):
        i = pl.program_id(2)
        proj = _dot(y_ref[...], w_ref[...])
        gate = jax.nn.sigmoid(gp_ref[...] + gb_ref[...])
        dmv = dm_ref[...]
        dp_ref[...] = (dmv * gate).astype(BF16)
        dpre = dmv * proj * gate * (1.0 - gate)
        dg_ref[...] = dpre.astype(BF16)
        part = jnp.sum(dpre, axis=0, keepdims=True)

        @pl.when(i == 0)
        def _():
            gb_out_ref[...] = part

        @pl.when(i > 0)
        def _():
            gb_out_ref[...] += part

    sds = jax.ShapeDtypeStruct
    return pl.pallas_call(
        body, out_shape=(sds((N_BRANCH, s, d), BF16), sds((s, N_BRANCH * d), BF16), sds((1, N_BRANCH * d), F32)),
        grid=(N_BRANCH, nj, s // tm),
        in_specs=[pl.BlockSpec((None, tm, BRANCH_W), lambda b, j, i: (b, i, 0)),
                  pl.BlockSpec((None, BRANCH_W, tn), lambda b, j, i: (b, 0, j)),
                  pl.BlockSpec((tm, tn), lambda b, j, i: (i, b * nj + j)),
                  pl.BlockSpec((1, tn), lambda b, j, i: (0, b * nj + j)),
                  pl.BlockSpec((tm, tn), lambda b, j, i: (i, j))],
        out_specs=(pl.BlockSpec((None, tm, tn), lambda b, j, i: (b, i, j)),
                   pl.BlockSpec((tm, tn), lambda b, j, i: (i, b * nj + j)),
                   pl.BlockSpec((1, tn), lambda b, j, i: (0, b * nj + j))),
        compiler_params=_cparams(("parallel", "parallel", "arbitrary")), name="gate_bwd")(ystack, w_branch, gp, gate_b, dm)


def _adamw(w, g, m, v, *, name):
    shape = w.shape
    c = shape[-1]
    r = 1
    for n in shape[:-1]:
        r *= n
    w2, g2, m2, v2 = (a.reshape(r, c) for a in (w, g, m, v))
    tr = _tile(r, max(8, (1 << 19) // c // 8 * 8), 8)
    c1 = 1.0 / (1.0 - ADAM_B1 ** ADAM_STEP)
    c2 = 1.0 / (1.0 - ADAM_B2 ** ADAM_STEP)

    def body(w_ref, g_ref, m_ref, v_ref, d_ref, nm_ref, nv_ref):
        gv = g_ref[...]
        nm = ADAM_B1 * m_ref[...] + (1.0 - ADAM_B1) * gv
        nv = ADAM_B2 * v_ref[...] + (1.0 - ADAM_B2) * (gv * gv)
        nm_ref[...] = nm
        nv_ref[...] = nv
        d_ref[...] = -ADAM_LR * ((nm * c1) / (jnp.sqrt(nv * c2) + ADAM_EPS) + ADAM_WD * w_ref[...])

    blk = pl.BlockSpec((tr, c), lambda i: (i, 0))
    sd = jax.ShapeDtypeStruct((r, c), F32)
    d2, nm2, nv2 = pl.pallas_call(body, out_shape=(sd, sd, sd), grid=(r // tr,), in_specs=[blk] * 4, out_specs=(blk,) * 3,
                                  compiler_params=_cparams(("parallel",)), name=name)(w2, g2, m2, v2)
    return d2.reshape(shape), nm2.reshape(shape), nv2.reshape(shape)


def _place():
    x, y, c = lax.axis_index("x"), lax.axis_index("y"), lax.axis_index("c")
    chips = [(1 - x, y), (x, 1 - y), (1 - x, 1 - y)]
    return x, y, c, 2 * x + y, chips, [2 * cx + cy for cx, cy in chips]


ANY = pl.BlockSpec(memory_space=pl.ANY)


def _all_gather(shards):
    n = len(shards)

    def body(*refs):
        ins, outs = refs[:n], refs[n:2 * n]
        send, recv = refs[2 * n:]
        x, y, c, k, chips, ks = _place()
        sib = (x, y, 1 - c)
        sends = []
        for a in range(n):
            for j in range(3):
                cp = pltpu.make_async_remote_copy(src_ref=ins[a].at[c], dst_ref=outs[a].at[c, k], send_sem=send.at[6 * a + j],
                                                  recv_sem=recv.at[6 * a + j], device_id=(*chips[j], c), device_id_type=MESH)
                cp.start()
                sends.append(cp)
        for a in range(n):
            for j in range(3):
                slab = outs[a].at[c, ks[j]]
                pltpu.make_async_remote_copy(src_ref=slab, dst_ref=slab, send_sem=send.at[6 * a + j], recv_sem=recv.at[6 * a + j],
                                             device_id=(*chips[j], c), device_id_type=MESH).wait_recv()
                cp = pltpu.make_async_remote_copy(src_ref=slab, dst_ref=slab, send_sem=send.at[6 * a + 3 + j],
                                                  recv_sem=recv.at[6 * a + 3 + j], device_id=sib, device_id_type=MESH)
                cp.start()
                sends.append(cp)
        for a in range(n):
            for j in range(3):
                slab = outs[a].at[1 - c, ks[j]]
                pltpu.make_async_remote_copy(src_ref=slab, dst_ref=slab, send_sem=send.at[6 * a + 3 + j],
                                             recv_sem=recv.at[6 * a + 3 + j], device_id=sib, device_id_type=MESH).wait_recv()
        for cp in sends:
            cp.wait_send()

    out_shape = tuple(jax.ShapeDtypeStruct((2, 4) + s.shape[1:], s.dtype) for s in shards)
    return pl.pallas_call(
        body, out_shape=out_shape, in_specs=[ANY] * n, out_specs=(ANY,) * n,
        scratch_shapes=[pltpu.SemaphoreType.DMA((6 * n,)), pltpu.SemaphoreType.DMA((6 * n,))],
        name="weights_all_gather")(*shards)


def _rs_exchange_cores(grads):
    n = len(grads)

    def body(*refs):
        ins, outs = refs[:n], refs[n:2 * n]
        send, recv = refs[2 * n:]
        x, y, c, _, _, _ = _place()
        sib = (x, y, 1 - c)
        cps = []
        for a in range(n):
            cp = pltpu.make_async_remote_copy(src_ref=ins[a].at[1 - c], dst_ref=outs[a], send_sem=send.at[a], recv_sem=recv.at[a],
                                              device_id=sib, device_id_type=MESH)
            cp.start()
            cps.append(cp)
        for cp in cps:
            cp.wait()

    out_shape = tuple(jax.ShapeDtypeStruct(g.shape[1:], g.dtype) for g in grads)
    return pl.pallas_call(body, out_shape=out_shape, in_specs=[ANY] * n, out_specs=(ANY,) * n,
                          scratch_shapes=[pltpu.SemaphoreType.DMA((n,)), pltpu.SemaphoreType.DMA((n,))],
                          name="grads_exchange_cores")(*grads)


def _rs_exchange_chips(parts):
    n = len(parts)

    def body(*refs):
        ins, outs = refs[:n], refs[n:2 * n]
        send, recv = refs[2 * n:]
        x, y, c, k, chips, ks = _place()
        sends = []
        for a in range(n):
            for j in range(3):
                cp = pltpu.make_async_remote_copy(src_ref=ins[a].at[ks[j]], dst_ref=outs[a].at[j], send_sem=send.at[3 * a + j],
                                                  recv_sem=recv.at[3 * a + j], device_id=(*chips[j], c), device_id_type=MESH)
                cp.start()
                sends.append(cp)
        for cp in sends:
            cp.wait()

    out_shape = tuple(jax.ShapeDtypeStruct((3,) + p.shape[1:], p.dtype) for p in parts)
    return pl.pallas_call(
        body, out_shape=out_shape, in_specs=[ANY] * n, out_specs=(ANY,) * n,
        scratch_shapes=[pltpu.SemaphoreType.DMA((3 * n,)), pltpu.SemaphoreType.DMA((3 * n,))],
        name="grads_exchange_chips")(*parts)


def _rs_share_cores(bufs):
    n = len(bufs)

    def body(*refs):
        outs = refs[n:2 * n]
        send, recv = refs[2 * n:]
        x, y, c, _, _, _ = _place()
        sib = (x, y, 1 - c)
        cps = []
        for a in range(n):
            cp = pltpu.make_async_remote_copy(src_ref=outs[a].at[c], dst_ref=outs[a].at[c], send_sem=send.at[a], recv_sem=recv.at[a],
                                              device_id=sib, device_id_type=MESH)
            cp.start()
            cps.append(cp)
        for a in range(n):
            slab = outs[a].at[1 - c]
            pltpu.make_async_remote_copy(src_ref=slab, dst_ref=slab, send_sem=send.at[a], recv_sem=recv.at[a],
                                         device_id=sib, device_id_type=MESH).wait_recv()
        for cp in cps:
            cp.wait_send()

    out_shape = tuple(jax.ShapeDtypeStruct(b.shape, b.dtype) for b in bufs)
    return pl.pallas_call(
        body, out_shape=out_shape, in_specs=[ANY] * n, out_specs=(ANY,) * n,
        input_output_aliases={a: a for a in range(n)},
        scratch_shapes=[pltpu.SemaphoreType.DMA((n,)), pltpu.SemaphoreType.DMA((n,))],
        name="grads_share_cores")(*bufs)


def _add_core_halves(g, ra, c_idx, *, name):
    _, _, r, c = g.shape
    tr = _tile(r, max(16, (1 << 19) // c // 16 * 16), 16)

    def body(c_ref, g_ref, ra_ref, o_ref, ob_ref):
        tot = g_ref[...] + ra_ref[...]
        o_ref[...] = tot
        ob_ref[...] = tot.astype(BF16)

    blk = pl.BlockSpec((None, tr, c), lambda j, i, cr: (j, i, 0))
    gs = pltpu.PrefetchScalarGridSpec(
        num_scalar_prefetch=1, grid=(4, r // tr),
        in_specs=[pl.BlockSpec((None, None, tr, c), lambda j, i, cr: (cr[0], j, i, 0)), blk],
        out_specs=(blk, blk))
    return pl.pallas_call(body, out_shape=(jax.ShapeDtypeStruct((4, r, c), F32), jax.ShapeDtypeStruct((4, r, c), BF16)), grid_spec=gs,
                          compiler_params=_cparams(("parallel", "parallel")), name=name)(c_idx, g, ra)


def _add_chips(p, r3, k_idx, c_idx, *, name):
    _, r, c = p.shape
    tr = _tile(r, max(16, (1 << 18) // c // 16 * 16), 16)

    def body(k_ref, c_ref, p_ref, r_ref, o_ref):
        o_ref[...] = ((p_ref[...] + r_ref[0].astype(F32)) + r_ref[1].astype(F32)) + r_ref[2].astype(F32)

    gs = pltpu.PrefetchScalarGridSpec(
        num_scalar_prefetch=2, grid=(r // tr,),
        in_specs=[pl.BlockSpec((None, tr, c), lambda i, kr, cr: (kr[0], i, 0)), pl.BlockSpec((3, tr, c), lambda i, kr, cr: (0, i, 0))],
        out_specs=pl.BlockSpec((None, tr, c), lambda i, kr, cr: (cr[0], i, 0)))
    return pl.pallas_call(body, out_shape=jax.ShapeDtypeStruct((2, r, c), F32), grid_spec=gs,
                          compiler_params=_cparams(("parallel",)), name=name)(k_idx, c_idx, p, r3)


def _rdma(src, dst, send, recv, idx, dev):
    return pltpu.make_async_remote_copy(src_ref=src, dst_ref=dst, send_sem=send.at[idx], recv_sem=recv.at[idx],
                                        device_id=dev, device_id_type=MESH)


def _run_comm(comm, name):
    n_in, n_out = len(comm["ins"]), len(comm["outs"])

    def body(*refs):
        ins, outs = refs[:n_in], refs[n_in:n_in + n_out]
        send, recv = refs[n_in + n_out:]
        for phase in comm["phases"]:
            phase(ins, outs, send, recv)

    return pl.pallas_call(
        body, out_shape=tuple(comm["outs"]), in_specs=[ANY] * n_in, out_specs=(ANY,) * n_out,
        input_output_aliases=comm.get("aliases", {}),
        scratch_shapes=[pltpu.SemaphoreType.DMA((comm["nsem"],)), pltpu.SemaphoreType.DMA((comm["nsem"],))],
        name=name)(*comm["ins"])


def _gather_comm(shards, layer):
    n = len(shards)

    def start(ins, outs, send, recv):
        x, y, c, k, chips, ks = _place()

        @pl.when(c == layer)
        def _():
            for a in range(n):
                for j in range(3):
                    _rdma(ins[a], outs[a].at[k], send, recv, 6 * a + j, (*chips[j], c)).start()

    def forward(ins, outs, send, recv):
        x, y, c, k, chips, ks = _place()

        @pl.when(c == layer)
        def _():
            for a in range(n):
                for j in range(3):
                    slab = outs[a].at[ks[j]]
                    _rdma(slab, slab, send, recv, 6 * a + j, (*chips[j], c)).wait_recv()
                    _rdma(slab, slab, send, recv, 6 * a + 3 + j, (x, y, 1 - c)).start()

    def finish(ins, outs, send, recv):
        x, y, c, k, chips, ks = _place()

        @pl.when(c == layer)
        def _():
            for a in range(n):
                for j in range(3):
                    slab = outs[a].at[ks[j]]
                    _rdma(ins[a], outs[a].at[k], send, recv, 6 * a + j, (*chips[j], c)).wait_send()
                    _rdma(slab, slab, send, recv, 6 * a + 3 + j, (x, y, 1 - c)).wait_send()

        @pl.when(c != layer)
        def _():
            for a in range(n):
                for j in range(3):
                    slab = outs[a].at[ks[j]]
                    _rdma(slab, slab, send, recv, 6 * a + 3 + j, (x, y, 1 - c)).wait_recv()

    return dict(ins=list(shards), outs=[jax.ShapeDtypeStruct((4,) + s.shape, s.dtype) for s in shards], nsem=6 * n,
                phases=[start, forward, finish])


def _to_owner_comm(grads, layer):
    n = len(grads)

    def start(ins, outs, send, recv):
        x, y, c, _, _, _ = _place()

        @pl.when(c != layer)
        def _():
            for a in range(n):
                _rdma(ins[a], outs[a], send, recv, a, (x, y, 1 - c)).start()

    def finish(ins, outs, send, recv):
        x, y, c, _, _, _ = _place()

        @pl.when(c != layer)
        def _():
            for a in range(n):
                _rdma(ins[a], outs[a], send, recv, a, (x, y, 1 - c)).wait_send()

        @pl.when(c == layer)
        def _():
            for a in range(n):
                _rdma(ins[a], outs[a], send, recv, a, (x, y, 1 - c)).wait_recv()

    return dict(ins=list(grads), outs=[jax.ShapeDtypeStruct(g.shape, g.dtype) for g in grads], nsem=n, phases=[start, finish])


def _exchange_comm(parts, layer):
    n = len(parts)

    def start(ins, outs, send, recv):
        x, y, c, k, chips, ks = _place()

        @pl.when(c == layer)
        def _():
            for a in range(n):
                for j in range(3):
                    _rdma(ins[a].at[ks[j]], outs[a].at[j], send, recv, 3 * a + j, (*chips[j], c)).start()

    def finish(ins, outs, send, recv):
        x, y, c, k, chips, ks = _place()

        @pl.when(c == layer)
        def _():
            for a in range(n):
                for j in range(3):
                    _rdma(ins[a].at[ks[j]], outs[a].at[j], send, recv, 3 * a + j, (*chips[j], c)).wait()

    return dict(ins=list(parts), outs=[jax.ShapeDtypeStruct((3,) + p.shape[1:], p.dtype) for p in parts], nsem=3 * n,
                phases=[start, finish])


def _share_comm(bufs, layer):
    n = len(bufs)

    def go(ins, outs, send, recv):
        x, y, c, _, _, _ = _place()

        @pl.when(c == layer)
        def _():
            for a in range(n):
                _rdma(outs[a].at[layer], outs[a].at[layer], send, recv, a, (x, y, 1 - c)).start()
            for a in range(n):
                _rdma(outs[a].at[layer], outs[a].at[layer], send, recv, a, (x, y, 1 - c)).wait_send()

        @pl.when(c != layer)
        def _():
            for a in range(n):
                _rdma(outs[a].at[layer], outs[a].at[layer], send, recv, a, (x, y, 1 - c)).wait_recv()

    return dict(ins=list(bufs), outs=[jax.ShapeDtypeStruct(b.shape, b.dtype) for b in bufs], nsem=n, phases=[go],
                aliases={a: a for a in range(n)})


def _add_owner(g, ra, own, *, name):
    _, r, c = g.shape
    tr = _tile(r, max(16, (1 << 20) // c // 16 * 16), 16)

    def body(own_ref, g_ref, ra_ref, o_ref, ob_ref):
        tot = g_ref[...] + ra_ref[...].astype(F32)
        o_ref[...] = tot
        ob_ref[...] = tot.astype(BF16)

    blk = pl.BlockSpec((None, tr, c), lambda j, i, o: (j * o[0], i * o[0], 0))
    gs = pltpu.PrefetchScalarGridSpec(num_scalar_prefetch=1, grid=(4, r // tr), in_specs=[blk, blk], out_specs=(blk, blk))
    return pl.pallas_call(body, out_shape=(jax.ShapeDtypeStruct((4, r, c), F32), jax.ShapeDtypeStruct((4, r, c), BF16)),
                          grid_spec=gs, compiler_params=_cparams(("arbitrary", "arbitrary")), name=name)(own, g, ra)


def _add_chips_layer(p, r3, k_idx, own, layer, buf, *, name):
    _, r, c = p.shape
    tr = _tile(r, max(16, (1 << 20) // c // 16 * 16), 16)

    def body(k_ref, own_ref, p_ref, r_ref, *rest):
        o_ref = rest[-1]
        o_ref[...] = ((p_ref[...] + r_ref[0].astype(F32)) + r_ref[1].astype(F32)) + r_ref[2].astype(F32)

    in_specs = [pl.BlockSpec((None, tr, c), lambda i, kr, o: (kr[0] * o[0], i * o[0], 0)),
                pl.BlockSpec((3, tr, c), lambda i, kr, o: (0, i * o[0], 0))]
    args = [k_idx, own, p, r3]
    aliases = {}
    if buf is not None:
        in_specs.append(ANY)
        args.append(buf)
        aliases = {4: 0}
    gs = pltpu.PrefetchScalarGridSpec(num_scalar_prefetch=2, grid=(r // tr,), in_specs=in_specs,
                                      out_specs=pl.BlockSpec((None, tr, c), lambda i, kr, o: (layer, i * o[0], 0)))
    return pl.pallas_call(body, out_shape=jax.ShapeDtypeStruct((2, r, c), F32), grid_spec=gs, input_output_aliases=aliases,
                          compiler_params=_cparams(("arbitrary",)), name=name)(*args)


def _all_reduce_small(vec):
    r = vec.shape[0]

    def body(v_ref, gath_ref, sum_ref, send, recv):
        x, y, c = lax.axis_index("x"), lax.axis_index("y"), lax.axis_index("c")
        me = 4 * x + 2 * y + c
        gath_ref[me] = v_ref[...]
        cps = []
        for f in range(1, 8):
            fx, fy, fc = (f >> 2) & 1, (f >> 1) & 1, f & 1
            peer = (x ^ fx, y ^ fy, c ^ fc)
            cp = pltpu.make_async_remote_copy(src_ref=v_ref, dst_ref=gath_ref.at[me], send_sem=send.at[f - 1], recv_sem=recv.at[f - 1],
                                              device_id=peer, device_id_type=MESH)
            cp.start()
            cps.append(cp)
        for f in range(1, 8):
            fx, fy, fc = (f >> 2) & 1, (f >> 1) & 1, f & 1
            src = 4 * (x ^ fx) + 2 * (y ^ fy) + (c ^ fc)
            pltpu.make_async_remote_copy(src_ref=v_ref, dst_ref=gath_ref.at[src], send_sem=send.at[f - 1], recv_sem=recv.at[f - 1],
                                         device_id=(x ^ fx, y ^ fy, c ^ fc), device_id_type=MESH).wait_recv()
        for cp in cps:
            cp.wait_send()
        acc = gath_ref[0]
        for i in range(1, 8):
            acc = acc + gath_ref[i]
        sum_ref[...] = acc

    vm = pl.BlockSpec(memory_space=pltpu.VMEM)
    _, total = pl.pallas_call(
        body, out_shape=(jax.ShapeDtypeStruct((8, r, 128), F32), jax.ShapeDtypeStruct((r, 128), F32)),
        in_specs=[vm], out_specs=(vm, vm),
        scratch_shapes=[pltpu.SemaphoreType.DMA((7,)), pltpu.SemaphoreType.DMA((7,))],
        name="small_all_reduce")(vec)
    return total


def _full_weight(gw, name):
    gathered, own, chip = gw[name]
    return jnp.concatenate([jnp.where(chip == k, own, gathered[k]) for k in range(4)], axis=SHARD_AXIS[name])


def _to_shards(full, name):
    return jnp.stack(jnp.split(full, 4, axis=SHARD_AXIS[name]), axis=0)


def _rope_tables(positions):
    inv = ROPE_THETA ** (-jnp.arange(0, QK_ROPE, 2, dtype=F32) / QK_ROPE)
    ang = positions.astype(F32)[:, None] * inv
    cos, sin = jnp.cos(ang), jnp.sin(ang)
    s = positions.shape[0]
    pad = jnp.zeros((s, HEAD_PAD - QK_HEAD), F32)
    ctab = jnp.concatenate([jnp.ones((s, QK_NOPE), F32), cos, cos, pad], axis=1)
    stab = jnp.concatenate([jnp.zeros((s, QK_NOPE), F32), -sin, sin, pad], axis=1)
    return ctab, stab


def _pad_gain(g):
    return jnp.concatenate([g, jnp.zeros((HEAD_PAD - QK_HEAD,), F32)])[None, :]


def _layer_weights(gw, rep, l, ql, kvl):
    d = rep["norm_g"].shape[1]
    w_in = _full_weight(gw, "w_in")
    o_kr = 2 * BRANCH_W + ql + kvl
    o_g = o_kr + QK_ROPE + 7 * BRANCH_W
    w = {}
    w["w1"] = jnp.concatenate([w_in[:, :o_kr], w_in[:, o_kr + QK_ROPE:o_g]], axis=1)
    w["wg"] = w_in[:, o_g:]
    w["wkr"] = jnp.concatenate([w_in[:, o_kr:o_kr + QK_ROPE], jnp.zeros((d, 128 - QK_ROPE), BF16)], axis=1)
    for nme in ("norm_g", "gate_b", "pool_scale", "q_a_norm_g", "kv_a_norm_g", "mem_norm_g", "xattn_q_norm_g", "xattn_k_norm_g"):
        w[nme] = rep[nme][l][None, :]
    w["mla_q_norm_g"] = _pad_gain(rep["mla_q_norm_g"][l])
    w["mla_k_norm_g"] = _pad_gain(rep["mla_k_norm_g"][l])
    return w


def _other_weights(gw, ql):
    w = {}
    wuq = _full_weight(gw, "w_uq").reshape(ql, MLA_HEADS, QK_HEAD)
    w["w_uq"] = jnp.pad(wuq, ((0, 0), (0, 0), (0, HEAD_PAD - QK_HEAD))).reshape(ql, MLA_HEADS * HEAD_PAD)
    for nme in ("w_ukv", "pool_w", "conv_w", "w_mem_kv", "w_branch", "w_out"):
        w[nme] = _full_weight(gw, nme)
    return w


def _forward_layer(x, mem, ctab, stab, w, tq, l, comm=None, late=None):
    sfx = f"_l{l}"
    h, ht = _norm_fwd(x, w["norm_g"], name="norm_fwd" + sfx)
    if late is None:
        p1 = _mm(h, w["w1"], name="proj_main" + sfx)
    else:
        p1, got = _mm(h, w["w1"], name="proj_main_with_gather" + sfx, comm=late[0])
        w.update(late[1](got))
    gp = _mm(h, w["wg"], name="proj_gates" + sfx)
    kr = _mm(h, w["wkr"], name="proj_krope" + sfx)
    y_pool, yt_pool = _pool_fwd(p1, w["pool_w"], w["pool_scale"])
    qf, kf, vv, vt = _mla_prep_fwd(p1, kr, ctab, stab, w["q_a_norm_g"], w["kv_a_norm_g"], w["w_uq"], w["w_ukv"],
                               w["mla_q_norm_g"], w["mla_k_norm_g"])
    (y_mla, yt_mla, o_att, lse, lset), comm_out = _attn_fwd(qf, kf, vt, p1, tq=tq, comm=comm)
    y_conv, yt_conv = _conv_fwd(p1, w["conv_w"])
    memn, memnt = _norm_fwd(mem, w["mem_norm_g"], name="mem_norm" + sfx)
    mem_kv = _mm(memn, w["w_mem_kv"], name="mem_kv" + sfx)
    kx, vx = _memkv_prep(mem_kv, w["xattn_k_norm_g"])
    y_mem, yt_mem = _xattn_fwd(p1, kx, vx, w["xattn_q_norm_g"])
    ystack = jnp.stack([y_pool, y_mla, y_conv, y_mem])
    ytstack = jnp.stack([yt_pool, yt_mla, yt_conv, yt_mem])
    merged, mergedt = _gate_fwd(ystack, w["w_branch"], gp, w["gate_b"])
    x_out = _mm(merged, w["w_out"], add=x, name="out_proj" + sfx)
    saved = dict(x=x, ht=ht, p1=p1, gp=gp, kr=kr, qf=qf, kf=kf, vv=vv, o_att=o_att, lse=lse, lset=lset, memnt=memnt,
                 mem_kv=mem_kv, kx=kx, vx=vx, ystack=ystack, ytstack=ytstack, mergedt=mergedt)
    return x_out, saved, comm_out


def _backward_layer(dx_out, sv, mem, ctab, stab, w, tq, l, ql, kvl, comm=None, own=None):
    sfx = f"_l{l}"
    g = {}
    g["w_out"] = _mm(sv["mergedt"], dx_out, name="g_w_out" + sfx)
    dm = _mm(dx_out, w["w_out"], trans_b=True, name="d_merged" + sfx)
    dproj, dgp, g_gate_b = _gate_bwd(sv["ystack"], w["w_branch"], sv["gp"], w["gate_b"], dm)
    g["gate_b"] = g_gate_b[0]
    g["w_branch"] = _mm(sv["ytstack"], dproj, name="g_w_branch" + sfx)
    dy4 = _mm(dproj, w["w_branch"], trans_b=True, name="d_branches" + sfx)
    p1, kr = sv["p1"], sv["kr"]
    d_pool, g_pw, g_ps = _pool_bwd(p1, dy4, w["pool_w"], w["pool_scale"])
    g["pool_w"], g["pool_scale"] = g_pw, g_ps[0]
    do, d_mz, delta, deltat = _attn_bwd_pre(dy4, sv["o_att"], p1)
    dqf, got = _attn_bwd_dq(sv["qf"], sv["kf"], sv["vv"], do, sv["lse"], delta, tq=tq, comm=comm[0] if comm else None)
    parts, comm_dkv = comm[1](got) if comm else (None, None)
    dkf, dvv, got = _attn_bwd_dkv(sv["qf"], sv["kf"], sv["vv"], do, sv["lset"], deltat, tq=tq, comm=comm_dkv)
    comm_out = (parts, got)
    (d_c, d_kr, dq_raw, dkv_raw, cqnt, ckvnt, g_qa, g_kva, g_qg, g_kg) = _mla_prep_bwd(
        p1, kr, ctab, stab, w["q_a_norm_g"], w["kv_a_norm_g"], w["w_uq"], w["w_ukv"], w["mla_q_norm_g"], w["mla_k_norm_g"],
        dqf, dkf, dvv)
    g["q_a_norm_g"], g["kv_a_norm_g"] = g_qa[0], g_kva[0]
    g["mla_q_norm_g"], g["mla_k_norm_g"] = g_qg[0, :QK_HEAD], g_kg[0, :QK_HEAD]
    g_wuq = _mm(cqnt, dq_raw, name="g_w_uq" + sfx)
    g["w_uq"] = g_wuq.reshape(ql, MLA_HEADS, HEAD_PAD)[:, :, :QK_HEAD].reshape(ql, MLA_HEADS * QK_HEAD)
    g["w_ukv"] = _mm(ckvnt, dkv_raw, name="g_w_ukv" + sfx)
    d_conv, gc0, gc1, gc2 = _conv_bwd(p1, dy4, w["conv_w"])
    g["conv_w"] = jnp.concatenate([gc0, gc1, gc2], axis=0)
    d_x, dkx, dvx, g_xq = _xattn_bwd(p1, dy4, sv["kx"], sv["vx"], w["xattn_q_norm_g"])
    g["xattn_q_norm_g"] = g_xq[0]
    d_memkv, g_xk = _memkv_prep_bwd(sv["mem_kv"], w["xattn_k_norm_g"], dkx, dvx)
    g["xattn_k_norm_g"] = g_xk[0]
    g["w_mem_kv"] = _mm(sv["memnt"], d_memkv, name="g_w_mem_kv" + sfx)
    d_memn = _mm(d_memkv, w["w_mem_kv"], trans_b=True, name="d_memn" + sfx)
    _, g_mn = _norm_bwd(mem, w["mem_norm_g"], d_memn, d_memn, name="mem_norm_bwd" + sfx)
    g["mem_norm_g"] = g_mn[0]
    dp1 = jnp.concatenate([d_pool, d_c, d_mz, d_conv, d_x], axis=1)
    ht = sv["ht"]
    o_kr = 2 * BRANCH_W + ql + kvl
    if own is None:
        g_w1 = _mm(ht, dp1, name="g_w1" + sfx)
        g_wg = _mm(ht, dgp, name="g_wg" + sfx)
        g_wkr = _mm(ht, d_kr, name="g_wkr" + sfx)
        g["w_in"] = jnp.concatenate([g_w1[:, :o_kr], g_wkr[:, :QK_ROPE], g_w1[:, o_kr:], g_wg], axis=1)
        dh = _mm(dp1, w["w1"], trans_b=True, name="dh_main" + sfx)
        dh = _mm(dgp, w["wg"], trans_b=True, add=dh, name="dh_gates" + sfx)
        own_out = None
    else:
        rest_names = SHARDED[1:]
        gl_a = own["layout"](g, rest_names)
        g_w1, ra = _mm(ht, dp1, name="g_w1_with_to_owner" + sfx, comm=_to_owner_comm([t.astype(BF16) for t in gl_a], l))
        parts_a = own["add_owner"](gl_a, ra, rest_names)
        g_wg, r3_a = _mm(ht, dgp, name="g_wg_with_exchange" + sfx, comm=_exchange_comm([pb for _, pb in parts_a], l))
        g_wkr = _mm(ht, d_kr, name="g_wkr" + sfx)
        g["w_in"] = jnp.concatenate([g_w1[:, :o_kr], g_wkr[:, :QK_ROPE], g_w1[:, o_kr:], g_wg], axis=1)
        gl_b = own["layout"](g, SHARDED[:1])
        dh, rb = _mm(dp1, w["w1"], trans_b=True, name="dh_main_with_to_owner" + sfx,
                     comm=_to_owner_comm([t.astype(BF16) for t in gl_b], l))
        parts_b = own["add_owner"](gl_b, rb, SHARDED[:1])
        dh, r3_b = _mm(dgp, w["wg"], trans_b=True, add=dh, name="dh_gates_with_exchange" + sfx,
                       comm=_exchange_comm([pb for _, pb in parts_b], l))
        own_out = (parts_b + parts_a, r3_b + r3_a)
    dh = _mm(d_kr, w["wkr"], trans_b=True, add=dh, name="dh_krope" + sfx)
    dx, g_ng = _norm_bwd(sv["x"], w["norm_g"], dh, dx_out, name="norm_bwd" + sfx)
    g["norm_g"] = g_ng[0]
    return dx, g, comm_out, own_out


def _as4(a):
    rest = a.shape[2:]
    r = 1
    for n in rest[:-1]:
        r *= n
    return a.reshape(2, 4, r, rest[-1])


def kernel(x, mem, positions, norm_g, w_in, gate_b, pool_w, pool_scale, q_a_norm_g, kv_a_norm_g, w_uq, w_ukv, mla_q_norm_g, mla_k_norm_g, conv_w, mem_norm_g, w_mem_kv, xattn_q_norm_g, xattn_k_norm_g, w_branch, w_out, loss_target, m_norm_g, m_w_in, m_gate_b, m_pool_w, m_pool_scale, m_q_a_norm_g, m_kv_a_norm_g, m_w_uq, m_w_ukv, m_mla_q_norm_g, m_mla_k_norm_g, m_conv_w, m_mem_norm_g, m_w_mem_kv, m_xattn_q_norm_g, m_xattn_k_norm_g, m_w_branch, m_w_out, v_norm_g, v_w_in, v_gate_b, v_pool_w, v_pool_scale, v_q_a_norm_g, v_kv_a_norm_g, v_w_uq, v_w_ukv, v_mla_q_norm_g, v_mla_k_norm_g, v_conv_w, v_mem_norm_g, v_w_mem_kv, v_xattn_q_norm_g, v_xattn_k_norm_g, v_w_branch, v_w_out):
    wts = dict(norm_g=norm_g, w_in=w_in, gate_b=gate_b, pool_w=pool_w, pool_scale=pool_scale, q_a_norm_g=q_a_norm_g,
               kv_a_norm_g=kv_a_norm_g, w_uq=w_uq, w_ukv=w_ukv, mla_q_norm_g=mla_q_norm_g, mla_k_norm_g=mla_k_norm_g,
               conv_w=conv_w, mem_norm_g=mem_norm_g, w_mem_kv=w_mem_kv, xattn_q_norm_g=xattn_q_norm_g,
               xattn_k_norm_g=xattn_k_norm_g, w_branch=w_branch, w_out=w_out)
    mom = dict(norm_g=m_norm_g, w_in=m_w_in, gate_b=m_gate_b, pool_w=m_pool_w, pool_scale=m_pool_scale, q_a_norm_g=m_q_a_norm_g,
               kv_a_norm_g=m_kv_a_norm_g, w_uq=m_w_uq, w_ukv=m_w_ukv, mla_q_norm_g=m_mla_q_norm_g, mla_k_norm_g=m_mla_k_norm_g,
               conv_w=m_conv_w, mem_norm_g=m_mem_norm_g, w_mem_kv=m_w_mem_kv, xattn_q_norm_g=m_xattn_q_norm_g,
               xattn_k_norm_g=m_xattn_k_norm_g, w_branch=m_w_branch, w_out=m_w_out)
    vel = dict(norm_g=v_norm_g, w_in=v_w_in, gate_b=v_gate_b, pool_w=v_pool_w, pool_scale=v_pool_scale, q_a_norm_g=v_q_a_norm_g,
               kv_a_norm_g=v_kv_a_norm_g, w_uq=v_w_uq, w_ukv=v_w_ukv, mla_q_norm_g=v_mla_q_norm_g, mla_k_norm_g=v_mla_k_norm_g,
               conv_w=v_conv_w, mem_norm_g=v_mem_norm_g, w_mem_kv=v_w_mem_kv, xattn_q_norm_g=v_xattn_q_norm_g,
               xattn_k_norm_g=v_xattn_k_norm_g, w_branch=v_w_branch, w_out=v_w_out)
    depth = norm_g.shape[0]
    assert depth == 2 and x.shape[0] == 1
    xs, mems, tgt = x[0], mem[0], loss_target[0]
    s = xs.shape[0]
    ql, kvl = q_a_norm_g.shape[1], kv_a_norm_g.shape[1]
    tq = _tile(s, 512, 128)
    ctab, stab = _rope_tables(positions[0])

    chip = 2 * lax.axis_index("x") + lax.axis_index("y")
    k_idx = chip.astype(jnp.int32).reshape(1)
    rep = {n: wts[n] for n in REPLICATED}
    send = [[wts[n][l].astype(F32 if n == "conv_w" else BF16) for n in SHARDED] for l in range(depth)]

    def gathered(got, l, names, first):
        return {n: (g, own, chip) for n, g, own in zip(names, got, send[l][first:first + len(names)])}

    got = _run_comm(_gather_comm(send[0][:1], 0), "weights_gather_w_in_l0")
    lw = [_layer_weights(gathered(got, 0, SHARDED[:1], 0), rep, 0, ql, kvl), None]
    late0 = (_gather_comm(send[0][1:], 0), lambda got: _other_weights(gathered(got, 0, SHARDED[1:], 1), ql))

    act, sv0, got1 = _forward_layer(xs, mems, ctab, stab, lw[0], tq, 0, comm=_gather_comm(send[1], 1), late=late0)
    lw[1] = _layer_weights(gathered(got1, 1, SHARDED, 0), rep, 1, ql, kvl)
    lw[1].update(_other_weights(gathered(got1, 1, SHARDED, 0), ql))
    act, sv1, _ = _forward_layer(act, mems, ctab, stab, lw[1], tq, 1)
    dy, loss_part = _loss_head(act, tgt)

    def shard_layout(g, names=SHARDED):
        shards = [jnp.swapaxes(_to_shards(g[n], n), -1, -2) if n == "w_in" else _to_shards(g[n], n) for n in names]
        return [t.reshape(4, -1, t.shape[-1]) for t in shards]

    def own_flag(l):
        return (lax.axis_index("c") == l).astype(jnp.int32).reshape(1)

    def add_owner(gl, ra, l, names=SHARDED):
        return [_add_owner(a, b, own_flag(l), name=f"add_owner_{n}_l{l}") for a, b, n in zip(gl, ra, names)]

    def finish_layer(parts, r3s, l, bufs):
        bufs = [_add_chips_layer(p, r3, k_idx, own_flag(l), l, None if bufs is None else bufs[i], name=f"add_chips_{n}_l{l}")
                for i, ((p, _), r3, n) in enumerate(zip(parts, r3s, SHARDED))]
        return _run_comm(_share_comm(bufs, l), f"grads_share_l{l}")

    grads = [None] * depth
    dxl, grads[1], _, _ = _backward_layer(dy, sv1, mems, ctab, stab, lw[1], tq, 1, ql, kvl)
    gl1 = shard_layout(grads[1])

    def after_dq(ra):
        parts = add_owner(gl1, ra, 1)
        return parts, _exchange_comm([pb for _, pb in parts], 1)

    own0 = dict(layout=shard_layout, add_owner=lambda gl, ra, names: add_owner(gl, ra, 0, names))
    dxl, grads[0], (parts1, r3_1), (parts0, r3_0) = _backward_layer(
        dxl, sv0, mems, ctab, stab, lw[0], tq, 0, ql, kvl,
        comm=(_to_owner_comm([g.astype(BF16) for g in gl1], 1), after_dq), own=own0)
    grad_x = dxl[None]
    bufs = finish_layer(parts1, r3_1, 1, None)
    reduced = finish_layer(parts0, r3_0, 0, bufs)
    gsum = {n: (jnp.swapaxes(r.reshape(2, wts[n].shape[2], wts[n].shape[1]), 1, 2) if n == "w_in" else r.reshape(wts[n].shape))
            for n, r in zip(SHARDED, reduced)}

    flat = [jnp.stack([grads[l][n] for l in range(depth)], axis=0).reshape(-1) for n in REPLICATED]
    sizes = [f.shape[0] for f in flat]
    total = sum(sizes) + 1
    rows = -(-total // 1024) * 8
    vec = jnp.concatenate(flat + [loss_part[0, :1], jnp.zeros((rows * 128 - total,), F32)]).reshape(rows, 128)
    red = _all_reduce_small(vec).reshape(-1)
    off = 0
    for n, sz in zip(REPLICATED, sizes):
        gsum[n] = red[off:off + sz].reshape(wts[n].shape)
        off += sz
    loss = red[off]

    delta, new_m, new_v = {}, {}, {}
    for n in WEIGHTS:
        if n == "w_in":
            tr_ = lambda a: jnp.swapaxes(a, 1, 2)
            delta[n], new_m[n], new_v[n] = (tr_(o) for o in _adamw(tr_(wts[n]), tr_(gsum[n]), tr_(mom[n]), tr_(vel[n]),
                                                                   name=f"adamw_{n}"))
        else:
            delta[n], new_m[n], new_v[n] = _adamw(wts[n], gsum[n], mom[n], vel[n], name=f"adamw_{n}")
    return (loss, grad_x, *[gsum[n] for n in WEIGHTS], *[delta[n] for n in WEIGHTS],
            *[new_m[n] for n in WEIGHTS], *[new_v[n] for n in WEIGHTS])
```

```python
import functools

import jax
import jax.numpy as jnp
from jax import lax
from jax.experimental import pallas as pl
from jax.experimental.pallas import tpu as pltpu

F32 = jnp.float32
BF16 = jnp.bfloat16
MESH = pl.DeviceIdType.MESH

EPS = 1e-6
N_BRANCH = 4
BRANCH_W = 1024
POOL_GROUPS = 4
POOL_GW = BRANCH_W // POOL_GROUPS
POOL_HALO = 16
CONV_HALO = 8
MLA_HEADS = 8
QK_NOPE = 128
QK_ROPE = 64
QK_HEAD = QK_NOPE + QK_ROPE
HEAD_PAD = 256
V_HEAD = 128
ROPE_THETA = 10000.0
XATTN_HEADS = 4
XATTN_HEAD_DIM = BRANCH_W // XATTN_HEADS
ADAM_LR, ADAM_B1, ADAM_B2, ADAM_EPS, ADAM_WD, ADAM_STEP = 0.001, 0.9, 0.999, 1e-08, 0.01, 10
NEG = -1e30
VMEM_LIMIT = 48 * 1024 * 1024

SHARDED = ("w_in", "pool_w", "w_uq", "w_ukv", "conv_w", "w_mem_kv", "w_branch", "w_out")
REPLICATED = ("norm_g", "gate_b", "pool_scale", "q_a_norm_g", "kv_a_norm_g", "mla_q_norm_g", "mla_k_norm_g",
              "mem_norm_g", "xattn_q_norm_g", "xattn_k_norm_g")
WEIGHTS = ("norm_g", "w_in", "gate_b", "pool_w", "pool_scale", "q_a_norm_g", "kv_a_norm_g", "w_uq", "w_ukv",
           "mla_q_norm_g", "mla_k_norm_g", "conv_w", "mem_norm_g", "w_mem_kv", "xattn_q_norm_g", "xattn_k_norm_g",
           "w_branch", "w_out")
SHARD_AXIS = {"w_in": 1, "pool_w": 1, "w_uq": 1, "w_ukv": 1, "conv_w": 1, "w_mem_kv": 0, "w_branch": 2, "w_out": 0}


def _cparams(sem=None):
    return pltpu.CompilerParams(dimension_semantics=sem, vmem_limit_bytes=VMEM_LIMIT)


def _tile(n, pref, unit=128):
    if n <= pref:
        return n
    t = (pref // unit) * unit
    while t >= unit:
        if n % t == 0:
            return t
        t -= unit
    return n


def _silu(z):
    return z * jax.nn.sigmoid(z)


def _dsilu(z):
    s = jax.nn.sigmoid(z)
    return s * (1.0 + z * (1.0 - s))


def _dot(a, b):
    return jnp.dot(a, b, preferred_element_type=F32)


def _dot_nt(a, b):
    return lax.dot_general(a, b, (((1,), (1,)), ((), ())), preferred_element_type=F32)


def _dot_tn(a, b):
    return lax.dot_general(a, b, (((0,), (0,)), ((), ())), preferred_element_type=F32)


def _rms(x, g, n):
    r = lax.rsqrt(jnp.sum(x * x, axis=-1, keepdims=True) * (1.0 / n) + EPS)
    return x * r * g


def _rms_bwd(x, g, n, dout):
    r = lax.rsqrt(jnp.sum(x * x, axis=-1, keepdims=True) * (1.0 / n) + EPS)
    y = x * r
    dy = dout * g
    dx = r * (dy - y * (jnp.sum(dy * y, axis=-1, keepdims=True) * (1.0 / n)))
    return dx, dout * y


def _rope(x, ctab, stab):
    lane = lax.broadcasted_iota(jnp.int32, x.shape, 1)
    partner = jnp.where(lane < QK_NOPE + QK_ROPE // 2, pltpu.roll(x, HEAD_PAD - QK_ROPE // 2, 1),
                        pltpu.roll(x, QK_ROPE // 2, 1))
    return x * ctab + partner * stab


def _rope_bwd(d, ctab, stab):
    lane = lax.broadcasted_iota(jnp.int32, d.shape, 1)
    ds = d * stab
    partner = jnp.where(lane < QK_NOPE + QK_ROPE // 2, pltpu.roll(ds, HEAD_PAD - QK_ROPE // 2, 1),
                        pltpu.roll(ds, QK_ROPE // 2, 1))
    return d * ctab + jnp.where((lane >= QK_NOPE) & (lane < QK_HEAD), partner, 0.0)


def _mm(a, b, *, name, trans_b=False, add=None, out_dtype=F32, tm=1024, tn=1024, tk=2048, comm=None):
    batched = a.ndim == 3
    if batched:
        nb, m, k = a.shape
    else:
        m, k = a.shape
    n = b.shape[-2] if trans_b else b.shape[-1]
    tm, tn, tk = _tile(m, tm, 8), _tile(n, tn), _tile(k, tk)
    nk = k // tk
    c_in, c_out, c_sem, c_shapes = _comm_extras(comm)
    nci, nco = len(c_in), len(c_out)
    assert comm is None or not batched
    n_in = 2 + (add is not None)

    def body(*refs):
        a_ref, b_ref = refs[:2]
        add_ref = refs[2] if add is not None else None
        o_ref = refs[n_in + nci]
        rest = refs[n_in + nci + 1 + nco:]
        if comm is not None:
            i, j, k3 = pl.program_id(0), pl.program_id(1), pl.program_id(2)
            ni, nj = m // tm, n // tn
            origin = (j == 0) & (k3 == 0)
            c_first, c_mid, c_last = _comm_hooks(comm, refs[n_in:n_in + nci], refs[n_in + nci + 1:n_in + nci + 1 + nco],
                                                 rest[(1 if nk > 1 else 0):], (i == 0) & origin, (i == (3 * ni) // 4) & origin,
                                                 (i == ni - 1) & (j == nj - 1) & (k3 == nk - 1))
            c_first()
        av = a_ref[...].astype(BF16)
        bv = b_ref[...].astype(BF16)
        part = _dot_nt(av, bv) if trans_b else _dot(av, bv)

        def finish(acc):
            if add_ref is not None:
                acc = acc + add_ref[...]
            o_ref[...] = acc.astype(o_ref.dtype)

        if nk == 1:
            finish(part)
        else:
            acc_ref = rest[0]
            kk = pl.program_id(3 if batched else 2)

            @pl.when(kk == 0)
            def _():
                acc_ref[...] = part

            @pl.when(kk > 0)
            def _():
                acc_ref[...] += part

            @pl.when(kk == nk - 1)
            def _():
                finish(acc_ref[...])

        if comm is not None:
            c_mid()
            c_last()

    if batched:
        a_spec = pl.BlockSpec((None, tm, tk), lambda bb, i, j, kk: (bb, i, kk))
        b_spec = (pl.BlockSpec((None, tn, tk), lambda bb, i, j, kk: (bb, j, kk)) if trans_b
                  else pl.BlockSpec((None, tk, tn), lambda bb, i, j, kk: (bb, kk, j)))
        o_spec = pl.BlockSpec((None, tm, tn), lambda bb, i, j, kk: (bb, i, j))
        grid = (nb, m // tm, n // tn, nk)
        out_shape = jax.ShapeDtypeStruct((nb, m, n), out_dtype)
        sem = ("parallel", "parallel", "parallel", "arbitrary")
    else:
        a_spec = pl.BlockSpec((tm, tk), lambda i, j, kk: (i, kk))
        b_spec = (pl.BlockSpec((tn, tk), lambda i, j, kk: (j, kk)) if trans_b
                  else pl.BlockSpec((tk, tn), lambda i, j, kk: (kk, j)))
        o_spec = pl.BlockSpec((tm, tn), lambda i, j, kk: (i, j))
        grid = (m // tm, n // tn, nk)
        out_shape = jax.ShapeDtypeStruct((m, n), out_dtype)
        sem = ("parallel", "parallel", "arbitrary")
    in_specs = [a_spec, b_spec] + ([o_spec] if add is not None else [])
    args = (a, b) + ((add,) if add is not None else ())
    scratch = [pltpu.VMEM((tm, tn), F32)] if nk > 1 else []
    if comm is None:
        return pl.pallas_call(body, out_shape=out_shape, grid=grid, in_specs=in_specs, out_specs=o_spec,
                              scratch_shapes=scratch, compiler_params=_cparams(sem), name=name)(*args)
    res = pl.pallas_call(body, out_shape=(out_shape,) + c_shapes, grid=grid, in_specs=in_specs + c_in,
                         out_specs=(o_spec,) + tuple(c_out), scratch_shapes=scratch + c_sem,
                         compiler_params=_cparams(("arbitrary",) * 3), name=name)(*args, *comm["ins"])
    return res[0], list(res[1:])


def _norm_fwd(x, g, *, name, t=256):
    s, d = x.shape
    t = _tile(s, t, 128)

    def body(x_ref, g_ref, h_ref, ht_ref):
        h = _rms(x_ref[...], g_ref[...], d)
        h_ref[...] = h.astype(BF16)
        ht_ref[...] = h.T.astype(BF16)

    return pl.pallas_call(
        body, out_shape=(jax.ShapeDtypeStruct((s, d), BF16), jax.ShapeDtypeStruct((d, s), BF16)),
        grid=(s // t,),
        in_specs=[pl.BlockSpec((t, d), lambda i: (i, 0)), pl.BlockSpec((1, d), lambda i: (0, 0))],
        out_specs=(pl.BlockSpec((t, d), lambda i: (i, 0)), pl.BlockSpec((d, t), lambda i: (0, i))),
        compiler_params=_cparams(("parallel",)), name=name)(x, g)


def _norm_bwd(x, g, dh, dres, *, name, t=256):
    s, d = x.shape
    t = _tile(s, t, 8)

    def body(x_ref, g_ref, dh_ref, dres_ref, dx_ref, dg_ref):
        dx, dgt = _rms_bwd(x_ref[...], g_ref[...], d, dh_ref[...])
        dx_ref[...] = dx + dres_ref[...]
        part = jnp.sum(dgt, axis=0, keepdims=True)

        @pl.when(pl.program_id(0) == 0)
        def _():
            dg_ref[...] = part

        @pl.when(pl.program_id(0) > 0)
        def _():
            dg_ref[...] += part

    row = pl.BlockSpec((t, d), lambda i: (i, 0))
    vec = pl.BlockSpec((1, d), lambda i: (0, 0))
    return pl.pallas_call(
        body, out_shape=(jax.ShapeDtypeStruct((s, d), F32), jax.ShapeDtypeStruct((1, d), F32)),
        grid=(s // t,), in_specs=[row, vec, row, row], out_specs=(row, vec),
        compiler_params=_cparams(("arbitrary",)), name=name)(x, g, dh, dres)


def _loss_head(y, tgt, *, t=256):
    s, d = y.shape
    t = _tile(s, t, 8)

    def body(y_ref, t_ref, dy_ref, l_ref):
        e = y_ref[...] - t_ref[...]
        dy_ref[...] = e * (1.0 / d)
        part = jnp.zeros((1, 128), F32) + jnp.sum(e * e) * (0.5 / d)

        @pl.when(pl.program_id(0) == 0)
        def _():
            l_ref[...] = part

        @pl.when(pl.program_id(0) > 0)
        def _():
            l_ref[...] += part

    row = pl.BlockSpec((t, d), lambda i: (i, 0))
    return pl.pallas_call(
        body, out_shape=(jax.ShapeDtypeStruct((s, d), F32), jax.ShapeDtypeStruct((1, 128), F32)),
        grid=(s // t,), in_specs=[row, row], out_specs=(row, pl.BlockSpec((1, 128), lambda i: (0, 0))),
        compiler_params=_cparams(("arbitrary",)), name="loss_head")(y, tgt)


def _pool_mixed(scr, v, halo, first, row0, t):
    scr[0:POOL_HALO, :] = jnp.where(first, 0.0, halo)
    scr[POOL_HALO:POOL_HALO + t, :] = v
    row = row0 + lax.broadcasted_iota(jnp.int32, (t, 1), 0)
    mixed = []
    for g in range(POOL_GROUPS):
        w = 2 ** (g + 1)
        acc = scr[:, g * POOL_GW:(g + 1) * POOL_GW]
        sh = 1
        while sh < w:
            acc = acc + pltpu.roll(acc, sh, 0)
            sh *= 2
        cnt = jnp.minimum(row + 1, w).astype(F32)
        mixed.append(acc[POOL_HALO:POOL_HALO + t, :] / cnt - v[:, g * POOL_GW:(g + 1) * POOL_GW])
    return mixed


def _pool_fwd(p1, pool_w, pool_scale, *, t=256):
    s = p1.shape[0]
    t = _tile(s, t, 128)
    hb = t // POOL_HALO

    def body(pv_ref, halo_ref, pz_ref, pw_ref, sc_ref, y_ref, yt_ref, scr):
        i = pl.program_id(0)
        mixed = _pool_mixed(scr, pv_ref[...], halo_ref[...], i == 0, i * t, t)
        outs = [_dot(mixed[g].astype(BF16), pw_ref[g]) for g in range(POOL_GROUPS)]
        y = jnp.concatenate(outs, axis=1) * sc_ref[...] * _silu(pz_ref[...])
        y_ref[...] = y.astype(BF16)
        yt_ref[...] = y.T.astype(BF16)

    return pl.pallas_call(
        body, out_shape=(jax.ShapeDtypeStruct((s, BRANCH_W), BF16), jax.ShapeDtypeStruct((BRANCH_W, s), BF16)),
        grid=(s // t,),
        in_specs=[pl.BlockSpec((t, BRANCH_W), lambda i: (i, 0)),
                  pl.BlockSpec((POOL_HALO, BRANCH_W), lambda i: (jnp.maximum(i * hb - 1, 0), 0)),
                  pl.BlockSpec((t, BRANCH_W), lambda i: (i, 1)),
                  pl.BlockSpec((POOL_GROUPS, POOL_GW, POOL_GW), lambda i: (0, 0, 0)),
                  pl.BlockSpec((1, BRANCH_W), lambda i: (0, 0))],
        out_specs=(pl.BlockSpec((t, BRANCH_W), lambda i: (i, 0)), pl.BlockSpec((BRANCH_W, t), lambda i: (0, i))),
        scratch_shapes=[pltpu.VMEM((t + POOL_HALO, BRANCH_W), F32)],
        compiler_params=_cparams(("parallel",)), name="pool_fwd")(p1, p1, p1, pool_w, pool_scale)


def _pool_bwd(p1, dy, pool_w, pool_scale, *, t=256):
    s = p1.shape[0]
    t = _tile(s, t, 128)
    hb = t // POOL_HALO
    nt = s // t
    last_hb = s // POOL_HALO - 1

    def body(pv_ref, halo_ref, pz_ref, pzn_ref, dy_ref, dyn_ref, pw_ref, sc_ref, d_ref, gw_ref, gs_ref, scr, scr2, scr3):
        i = pl.program_id(0)
        mixed = _pool_mixed(scr, pv_ref[...], halo_ref[...], i == 0, i * t, t)
        scale = sc_ref[...]
        pz = pz_ref[...]
        dy = dy_ref[...]
        raw = jnp.concatenate([_dot(mixed[g].astype(BF16), pw_ref[g]) for g in range(POOL_GROUPS)], axis=1)
        d_pool = dy * _silu(pz)
        d_ref[:, BRANCH_W:2 * BRANCH_W] = (dy * raw * scale * _dsilu(pz)).astype(BF16)
        gs_part = jnp.sum(d_pool * raw, axis=0, keepdims=True)
        scr2[0:t, :] = d_pool * scale
        scr2[t:t + POOL_HALO, :] = jnp.where(i == nt - 1, 0.0, dyn_ref[...] * _silu(pzn_ref[...]) * scale)
        row = i * t + lax.broadcasted_iota(jnp.int32, (t + POOL_HALO, 1), 0)
        gw_parts = []
        for g in range(POOL_GROUPS):
            w = 2 ** (g + 1)
            sl = slice(g * POOL_GW, (g + 1) * POOL_GW)
            do_g = scr2[:, sl].astype(BF16)
            dm = _dot_nt(do_g, pw_ref[g])
            gw_parts.append(_dot_tn(mixed[g].astype(BF16), do_g[0:t, :]))
            cnt = jnp.minimum(row + 1, w).astype(F32)
            acc = dm / cnt
            sh = 1
            while sh < w:
                acc = acc + pltpu.roll(acc, t + POOL_HALO - sh, 0)
                sh *= 2
            scr3[:, sl] = acc - dm
        d_ref[:, 0:BRANCH_W] = scr3[0:t, :].astype(BF16)

        @pl.when(i == 0)
        def _():
            for g in range(POOL_GROUPS):
                gw_ref[g] = gw_parts[g]
            gs_ref[...] = gs_part

        @pl.when(i > 0)
        def _():
            for g in range(POOL_GROUPS):
                gw_ref[g] += gw_parts[g]
            gs_ref[...] += gs_part

    tile = lambda col: pl.BlockSpec((t, BRANCH_W), lambda i: (i, col))
    nxt = lambda col: pl.BlockSpec((POOL_HALO, BRANCH_W), lambda i: (jnp.minimum((i + 1) * hb, last_hb), col))
    return pl.pallas_call(
        body,
        out_shape=(jax.ShapeDtypeStruct((s, 2 * BRANCH_W), BF16),
                   jax.ShapeDtypeStruct((POOL_GROUPS, POOL_GW, POOL_GW), F32),
                   jax.ShapeDtypeStruct((1, BRANCH_W), F32)),
        grid=(nt,),
        in_specs=[tile(0), pl.BlockSpec((POOL_HALO, BRANCH_W), lambda i: (jnp.maximum(i * hb - 1, 0), 0)),
                  tile(1), nxt(1),
                  pl.BlockSpec((None, t, BRANCH_W), lambda i: (0, i, 0)),
                  pl.BlockSpec((None, POOL_HALO, BRANCH_W), lambda i: (0, jnp.minimum((i + 1) * hb, last_hb), 0)),
                  pl.BlockSpec((POOL_GROUPS, POOL_GW, POOL_GW), lambda i: (0, 0, 0)),
                  pl.BlockSpec((1, BRANCH_W), lambda i: (0, 0))],
        out_specs=(pl.BlockSpec((t, 2 * BRANCH_W), lambda i: (i, 0)),
                   pl.BlockSpec((POOL_GROUPS, POOL_GW, POOL_GW), lambda i: (0, 0, 0)),
                   pl.BlockSpec((1, BRANCH_W), lambda i: (0, 0))),
        scratch_shapes=[pltpu.VMEM((t + POOL_HALO, BRANCH_W), F32)] * 3,
        compiler_params=_cparams(("arbitrary",)), name="pool_bwd")(p1, p1, p1, p1, dy, dy, pool_w, pool_scale)


CONV_COL = 4


def _conv_taps(scr, u, uh, first, t):
    scr[0:CONV_HALO, :] = jnp.where(first, 0.0, uh)
    scr[CONV_HALO:CONV_HALO + t, :] = u
    e = scr[...]
    u1 = pltpu.roll(e, 1, 0)[CONV_HALO:CONV_HALO + t, :]
    u2 = pltpu.roll(e, 2, 0)[CONV_HALO:CONV_HALO + t, :]
    return u2, u1, u


def _conv_fwd(p1, conv_w, *, t=256):
    s = p1.shape[0]
    t = _tile(s, t, 128)
    hb = t // CONV_HALO

    def body(cb_ref, cc_ref, cx_ref, cz_ref, cch_ref, cxh_ref, w_ref, y_ref, yt_ref, scr):
        i = pl.program_id(0)
        u0, u1, u2 = _conv_taps(scr, cc_ref[...] * cx_ref[...], cch_ref[...] * cxh_ref[...], i == 0, t)
        w = w_ref[...]
        y = (w[0:1, :] * u0 + w[1:2, :] * u1 + w[2:3, :] * u2) * cb_ref[...] * _silu(cz_ref[...])
        y_ref[...] = y.astype(BF16)
        yt_ref[...] = y.T.astype(BF16)

    tile = lambda col: pl.BlockSpec((t, BRANCH_W), lambda i: (i, CONV_COL + col))
    prev = lambda col: pl.BlockSpec((CONV_HALO, BRANCH_W), lambda i: (jnp.maximum(i * hb - 1, 0), CONV_COL + col))
    return pl.pallas_call(
        body, out_shape=(jax.ShapeDtypeStruct((s, BRANCH_W), BF16), jax.ShapeDtypeStruct((BRANCH_W, s), BF16)),
        grid=(s // t,),
        in_specs=[tile(0), tile(1), tile(2), tile(3), prev(1), prev(2), pl.BlockSpec((3, BRANCH_W), lambda i: (0, 0))],
        out_specs=(pl.BlockSpec((t, BRANCH_W), lambda i: (i, 0)), pl.BlockSpec((BRANCH_W, t), lambda i: (0, i))),
        scratch_shapes=[pltpu.VMEM((t + CONV_HALO, BRANCH_W), F32)],
        compiler_params=_cparams(("parallel",)), name="conv_fwd")(p1, p1, p1, p1, p1, p1, conv_w)


def _conv_bwd(p1, dy, conv_w, *, t=256):
    s = p1.shape[0]
    t = _tile(s, t, 128)
    hb = t // CONV_HALO
    nt = s // t
    last_hb = s // CONV_HALO - 1

    def body(cb_ref, cc_ref, cx_ref, cz_ref, cch_ref, cxh_ref, cbn_ref, czn_ref, dy_ref, dyn_ref, w_ref,
             d_ref, g0_ref, g1_ref, g2_ref, scr, scr2):
        i = pl.program_id(0)
        cb, cc, cx, cz = cb_ref[...], cc_ref[...], cx_ref[...], cz_ref[...]
        u0, u1, u2 = _conv_taps(scr, cc * cx, cch_ref[...] * cxh_ref[...], i == 0, t)
        w = w_ref[...]
        w0, w1, w2 = w[0:1, :], w[1:2, :], w[2:3, :]
        y = w0 * u0 + w1 * u1 + w2 * u2
        dy = dy_ref[...]
        sz = _silu(cz)
        d_ref[:, 0:BRANCH_W] = (dy * sz * y).astype(BF16)
        d_ref[:, 3 * BRANCH_W:4 * BRANCH_W] = (dy * cb * y * _dsilu(cz)).astype(BF16)
        d_y = dy * sz * cb
        parts = [jnp.sum(d_y * u, axis=0, keepdims=True) for u in (u0, u1, u2)]
        scr2[0:t, :] = d_y
        scr2[t:t + CONV_HALO, :] = jnp.where(i == nt - 1, 0.0, dyn_ref[...] * _silu(czn_ref[...]) * cbn_ref[...])
        e = scr2[...]
        n = t + CONV_HALO
        du = (w2 * e + w1 * pltpu.roll(e, n - 1, 0) + w0 * pltpu.roll(e, n - 2, 0))[0:t, :]
        d_ref[:, BRANCH_W:2 * BRANCH_W] = (du * cx).astype(BF16)
        d_ref[:, 2 * BRANCH_W:3 * BRANCH_W] = (du * cc).astype(BF16)

        @pl.when(i == 0)
        def _():
            g0_ref[...] = parts[0]
            g1_ref[...] = parts[1]
            g2_ref[...] = parts[2]

        @pl.when(i > 0)
        def _():
            g0_ref[...] += parts[0]
            g1_ref[...] += parts[1]
            g2_ref[...] += parts[2]

    tile = lambda col: pl.BlockSpec((t, BRANCH_W), lambda i: (i, CONV_COL + col))
    prev = lambda col: pl.BlockSpec((CONV_HALO, BRANCH_W), lambda i: (jnp.maximum(i * hb - 1, 0), CONV_COL + col))
    nxt = lambda col: pl.BlockSpec((CONV_HALO, BRANCH_W), lambda i: (jnp.minimum((i + 1) * hb, last_hb), CONV_COL + col))
    vec = pl.BlockSpec((1, BRANCH_W), lambda i: (0, 0))
    gshape = jax.ShapeDtypeStruct((1, BRANCH_W), F32)
    return pl.pallas_call(
        body, out_shape=(jax.ShapeDtypeStruct((s, 4 * BRANCH_W), BF16), gshape, gshape, gshape),
        grid=(nt,),
        in_specs=[tile(0), tile(1), tile(2), tile(3), prev(1), prev(2), nxt(0), nxt(3),
                  pl.BlockSpec((None, t, BRANCH_W), lambda i: (2, i, 0)),
                  pl.BlockSpec((None, CONV_HALO, BRANCH_W), lambda i: (2, jnp.minimum((i + 1) * hb, last_hb), 0)),
                  pl.BlockSpec((3, BRANCH_W), lambda i: (0, 0))],
        out_specs=(pl.BlockSpec((t, 4 * BRANCH_W), lambda i: (i, 0)), vec, vec, vec),
        scratch_shapes=[pltpu.VMEM((t + CONV_HALO, BRANCH_W), F32)] * 2,
        compiler_params=_cparams(("arbitrary",)), name="conv_bwd")(p1, p1, p1, p1, p1, p1, p1, p1, dy, dy, conv_w)


def _mla_prep_fwd(p1, kr, ctab, stab, qa_g, kva_g, w_uq, w_ukv, q_g, k_g, *, t=256):
    s = p1.shape[0]
    t = _tile(s, t, 128)
    ql, kvl = qa_g.shape[1], kva_g.shape[1]
    assert ql == kvl and 2048 % ql == 0
    cq_blk = 2048 // ql

    def body(cq_ref, ckv_ref, kr_ref, c_ref, s_ref, qag_ref, kvag_ref, wuq_ref, wukv_ref, qg_ref, kg_ref,
             qf_ref, kf_ref, v_ref, vt_ref):
        q_raw = _dot(_rms(cq_ref[...], qag_ref[...], ql).astype(BF16), wuq_ref[...])
        kv_raw = _dot(_rms(ckv_ref[...], kvag_ref[...], kvl).astype(BF16), wukv_ref[...])
        krp = kr_ref[...]
        ct, st = c_ref[...], s_ref[...]
        for h in range(MLA_HEADS):
            qh = q_raw[:, h * HEAD_PAD:(h + 1) * HEAD_PAD]
            qf_ref[h] = _rope(_rms(qh, qg_ref[...], QK_HEAD), ct, st).astype(BF16)
            kh = jnp.concatenate([kv_raw[:, h * HEAD_PAD:h * HEAD_PAD + QK_NOPE], krp], axis=1)
            kf_ref[h] = _rope(_rms(kh, kg_ref[...], QK_HEAD), ct, st).astype(BF16)
            vh = kv_raw[:, h * HEAD_PAD + QK_NOPE:(h + 1) * HEAD_PAD]
            v_ref[h] = vh.astype(BF16)
            vt_ref[h] = vh.T.astype(BF16)

    full = lambda shp: pl.BlockSpec(shp, lambda i: tuple(0 for _ in shp))
    return pl.pallas_call(
        body,
        out_shape=(jax.ShapeDtypeStruct((MLA_HEADS, s, HEAD_PAD), BF16), jax.ShapeDtypeStruct((MLA_HEADS, s, HEAD_PAD), BF16),
                   jax.ShapeDtypeStruct((MLA_HEADS, s, V_HEAD), BF16), jax.ShapeDtypeStruct((MLA_HEADS, V_HEAD, s), BF16)),
        grid=(s // t,),
        in_specs=[pl.BlockSpec((t, ql), lambda i: (i, cq_blk)), pl.BlockSpec((t, kvl), lambda i: (i, cq_blk + 1)),
                  pl.BlockSpec((t, 128), lambda i: (i, 0)),
                  pl.BlockSpec((t, HEAD_PAD), lambda i: (i, 0)), pl.BlockSpec((t, HEAD_PAD), lambda i: (i, 0)),
                  full((1, ql)), full((1, kvl)), full(w_uq.shape), full(w_ukv.shape), full((1, HEAD_PAD)), full((1, HEAD_PAD))],
        out_specs=(pl.BlockSpec((MLA_HEADS, t, HEAD_PAD), lambda i: (0, i, 0)),
                   pl.BlockSpec((MLA_HEADS, t, HEAD_PAD), lambda i: (0, i, 0)),
                   pl.BlockSpec((MLA_HEADS, t, V_HEAD), lambda i: (0, i, 0)),
                   pl.BlockSpec((MLA_HEADS, V_HEAD, t), lambda i: (0, 0, i))),
        compiler_params=_cparams(("parallel",)), name="mla_prep_fwd")(
            p1, p1, kr, ctab, stab, qa_g, kva_g, w_uq, w_ukv, q_g, k_g)


def _mla_prep_bwd(p1, kr, ctab, stab, qa_g, kva_g, w_uq, w_ukv, q_g, k_g, dqf, dkf, dv, *, t=256):
    s = p1.shape[0]
    t = _tile(s, t, 128)
    ql, kvl = qa_g.shape[1], kva_g.shape[1]
    cq_blk = 2048 // ql
    nq = MLA_HEADS * HEAD_PAD

    def body(cq_ref, ckv_ref, kr_ref, c_ref, s_ref, qag_ref, kvag_ref, wuq_ref, wukv_ref, qg_ref, kg_ref,
             dqf_ref, dkf_ref, dv_ref,
             dc_ref, dkr_ref, dqraw_ref, dkvraw_ref, cqnt_ref, ckvnt_ref, gqa_ref, gkva_ref, gqg_ref, gkg_ref):
        i = pl.program_id(0)
        cq, ckv = cq_ref[...], ckv_ref[...]
        cqn = _rms(cq, qag_ref[...], ql)
        ckvn = _rms(ckv, kvag_ref[...], kvl)
        cqnt_ref[...] = cqn.T.astype(BF16)
        ckvnt_ref[...] = ckvn.T.astype(BF16)
        q_raw = _dot(cqn.astype(BF16), wuq_ref[...])
        kv_raw = _dot(ckvn.astype(BF16), wukv_ref[...])
        krp = kr_ref[...]
        ct, st = c_ref[...], s_ref[...]
        gqg = jnp.zeros((1, HEAD_PAD), F32)
        gkg = jnp.zeros((1, HEAD_PAD), F32)
        dkr = jnp.zeros((t, HEAD_PAD - QK_NOPE), F32)
        dq_parts, dkv_parts = [], []
        for h in range(MLA_HEADS):
            qh = q_raw[:, h * HEAD_PAD:(h + 1) * HEAD_PAD]
            dx, dg = _rms_bwd(qh, qg_ref[...], QK_HEAD, _rope_bwd(dqf_ref[h], ct, st))
            gqg = gqg + jnp.sum(dg, axis=0, keepdims=True)
            dq_parts.append(dx)
            kh = jnp.concatenate([kv_raw[:, h * HEAD_PAD:h * HEAD_PAD + QK_NOPE], krp], axis=1)
            dx, dg = _rms_bwd(kh, kg_ref[...], QK_HEAD, _rope_bwd(dkf_ref[h], ct, st))
            gkg = gkg + jnp.sum(dg, axis=0, keepdims=True)
            dkr = dkr + dx[:, QK_NOPE:HEAD_PAD]
            dkv_parts += [dx[:, 0:QK_NOPE], dv_ref[h]]
        dq_raw = jnp.concatenate(dq_parts, axis=1).astype(BF16)
        dkv_raw = jnp.concatenate(dkv_parts, axis=1).astype(BF16)
        dqraw_ref[...] = dq_raw
        dkvraw_ref[...] = dkv_raw
        dkr_ref[...] = dkr.astype(BF16)
        dcq, gqa = _rms_bwd(cq, qag_ref[...], ql, _dot_nt(dq_raw, wuq_ref[...]))
        dckv, gkva = _rms_bwd(ckv, kvag_ref[...], kvl, _dot_nt(dkv_raw, wukv_ref[...]))
        dc_ref[:, 0:ql] = dcq.astype(BF16)
        dc_ref[:, ql:ql + kvl] = dckv.astype(BF16)
        gqa = jnp.sum(gqa, axis=0, keepdims=True)
        gkva = jnp.sum(gkva, axis=0, keepdims=True)

        @pl.when(i == 0)
        def _():
            gqa_ref[...] = gqa
            gkva_ref[...] = gkva
            gqg_ref[...] = gqg
            gkg_ref[...] = gkg

        @pl.when(i > 0)
        def _():
            gqa_ref[...] += gqa
            gkva_ref[...] += gkva
            gqg_ref[...] += gqg
            gkg_ref[...] += gkg

    full = lambda shp: pl.BlockSpec(shp, lambda i: tuple(0 for _ in shp))
    hblk = lambda w: pl.BlockSpec((MLA_HEADS, t, w), lambda i: (0, i, 0))
    sds = jax.ShapeDtypeStruct
    return pl.pallas_call(
        body,
        out_shape=(sds((s, ql + kvl), BF16), sds((s, 128), BF16), sds((s, nq), BF16), sds((s, nq), BF16),
                   sds((ql, s), BF16), sds((kvl, s), BF16),
                   sds((1, ql), F32), sds((1, kvl), F32), sds((1, HEAD_PAD), F32), sds((1, HEAD_PAD), F32)),
        grid=(s // t,),
        in_specs=[pl.BlockSpec((t, ql), lambda i: (i, cq_blk)), pl.BlockSpec((t, kvl), lambda i: (i, cq_blk + 1)),
                  pl.BlockSpec((t, 128), lambda i: (i, 0)),
                  pl.BlockSpec((t, HEAD_PAD), lambda i: (i, 0)), pl.BlockSpec((t, HEAD_PAD), lambda i: (i, 0)),
                  full((1, ql)), full((1, kvl)), full(w_uq.shape), full(w_ukv.shape), full((1, HEAD_PAD)), full((1, HEAD_PAD)),
                  hblk(HEAD_PAD), hblk(HEAD_PAD), hblk(V_HEAD)],
        out_specs=(pl.BlockSpec((t, ql + kvl), lambda i: (i, 0)), pl.BlockSpec((t, 128), lambda i: (i, 0)),
                   pl.BlockSpec((t, nq), lambda i: (i, 0)), pl.BlockSpec((t, nq), lambda i: (i, 0)),
                   pl.BlockSpec((ql, t), lambda i: (0, i)), pl.BlockSpec((kvl, t), lambda i: (0, i)),
                   full((1, ql)), full((1, kvl)), full((1, HEAD_PAD)), full((1, HEAD_PAD))),
        compiler_params=_cparams(("arbitrary",)), name="mla_prep_bwd")(
            p1, p1, kr, ctab, stab, qa_g, kva_g, w_uq, w_ukv, q_g, k_g, dqf, dkf, dv)


MZ_BLK128 = 3072 // 128
ATT_SCALE = QK_HEAD ** -0.5


HPS = 2


def _causal_pairs(nq, by_query):
    if by_query:
        prs = [(qi, ki) for qi in range(nq) for ki in range(qi + 1)]
    else:
        prs = [(qi, ki) for ki in range(nq) for qi in range(ki, nq)]
    return (jnp.asarray([p[0] for p in prs], jnp.int32), jnp.asarray([p[1] for p in prs], jnp.int32), len(prs))


def _comm_hooks(comm, cins, couts, csems, first, middle, last):
    ph = comm["phases"]
    assert len(ph) in (2, 3)

    def at(pred, phase):
        @pl.when(pred)
        def _():
            phase(cins, couts, *csems)

    return (lambda: at(first, ph[0])), (lambda: [at(middle, ph[1])] if len(ph) == 3 else None), (lambda: at(last, ph[-1]))


def _comm_extras(comm):
    if comm is None:
        return [], [], [], ()
    sems = [pltpu.SemaphoreType.DMA((comm["nsem"],)), pltpu.SemaphoreType.DMA((comm["nsem"],))]
    return [ANY] * len(comm["ins"]), [ANY] * len(comm["outs"]), sems, tuple(comm["outs"])


def _attn_fwd(qf, kf, vt, p1, *, tq, comm=None):
    nh, s, _ = qf.shape
    nq = s // tq
    qtab, ktab, npairs = _causal_pairs(nq, True)
    wv = HPS * V_HEAD
    c_in, c_out, c_sem, c_shapes = _comm_extras(comm)
    nci, nco = len(c_in), len(c_out)

    def body(*refs):
        qt_ref, kt_ref, q_ref, k_ref, vt_ref, mz_ref = refs[:6]
        y_ref, yt_ref, o_ref, lse_ref, lset_ref = refs[6 + nci:11 + nci]
        m_scr, l_scr, acc_scr = refs[11 + nci + nco:14 + nci + nco]
        pr = pl.program_id(1)
        qi, ki = qt_ref[pr], kt_ref[pr]
        if comm is not None:
            hg = pl.program_id(0)
            last_hg = nh // HPS - 1
            c_first, c_mid, c_last = _comm_hooks(comm, refs[6:6 + nci], refs[11 + nci:11 + nci + nco], refs[14 + nci + nco:],
                                                 (hg == 0) & (pr == 0), (hg == last_hg) & (pr == 0),
                                                 (hg == last_hg) & (pr == npairs - 1))
            c_first()

        @pl.when(ki == 0)
        def _():
            m_scr[...] = jnp.full((HPS, 1, tq), NEG, F32)
            l_scr[...] = jnp.zeros((HPS, 1, tq), F32)
            acc_scr[...] = jnp.zeros((HPS, V_HEAD, tq), F32)

        def step(diagonal):
            for u in range(HPS):
                st = _dot_nt(k_ref[u], q_ref[u]) * ATT_SCALE
                if diagonal:
                    r = lax.broadcasted_iota(jnp.int32, (tq, tq), 0)
                    c = lax.broadcasted_iota(jnp.int32, (tq, tq), 1)
                    st = jnp.where(r <= c, st, NEG)
                m_old = m_scr[u]
                m_new = jnp.maximum(m_old, jnp.max(st, axis=0, keepdims=True))
                alpha = jnp.exp(m_old - m_new)
                pt = jnp.exp(st - m_new)
                l_scr[u] = alpha * l_scr[u] + jnp.sum(pt, axis=0, keepdims=True)
                acc_scr[u] = alpha * acc_scr[u] + _dot(vt_ref[u], pt.astype(BF16))
                m_scr[u] = m_new

        @pl.when(ki < qi)
        def _():
            step(False)

        @pl.when(ki == qi)
        def _():
            step(True)
            outs = []
            for u in range(HPS):
                l = l_scr[u]
                outs.append((acc_scr[u] / l).T)
                lset = m_scr[u] + jnp.log(l)
                lset_ref[u] = lset
                lse_ref[u] = jnp.broadcast_to(lset, (128, tq)).T[:, 0:1]
            o = jnp.concatenate(outs, axis=1)
            o_ref[...] = o
            y = o * _silu(mz_ref[...])
            y_ref[...] = y.astype(BF16)
            yt_ref[...] = y.T.astype(BF16)

        if comm is not None:
            c_mid()
            c_last()

    sds = jax.ShapeDtypeStruct
    gs = pltpu.PrefetchScalarGridSpec(
        num_scalar_prefetch=2, grid=(nh // HPS, npairs),
        in_specs=[pl.BlockSpec((HPS, tq, HEAD_PAD), lambda h, p, qt, kt: (h, qt[p], 0)),
                  pl.BlockSpec((HPS, tq, HEAD_PAD), lambda h, p, qt, kt: (h, kt[p], 0)),
                  pl.BlockSpec((HPS, V_HEAD, tq), lambda h, p, qt, kt: (h, 0, kt[p])),
                  pl.BlockSpec((tq, wv), lambda h, p, qt, kt: (qt[p], MZ_BLK128 // HPS + h))] + c_in,
        out_specs=(pl.BlockSpec((tq, wv), lambda h, p, qt, kt: (qt[p], h)),
                   pl.BlockSpec((wv, tq), lambda h, p, qt, kt: (h, qt[p])),
                   pl.BlockSpec((tq, wv), lambda h, p, qt, kt: (qt[p], h)),
                   pl.BlockSpec((HPS, tq, 1), lambda h, p, qt, kt: (h, qt[p], 0)),
                   pl.BlockSpec((HPS, 1, tq), lambda h, p, qt, kt: (h, 0, qt[p]))) + tuple(c_out),
        scratch_shapes=[pltpu.VMEM((HPS, 1, tq), F32), pltpu.VMEM((HPS, 1, tq), F32), pltpu.VMEM((HPS, V_HEAD, tq), F32)] + c_sem)
    sem = ("parallel", "arbitrary") if comm is None else ("arbitrary", "arbitrary")
    res = pl.pallas_call(
        body,
        out_shape=(sds((s, BRANCH_W), BF16), sds((BRANCH_W, s), BF16), sds((s, BRANCH_W), F32),
                   sds((nh, s, 1), F32), sds((nh, 1, s), F32)) + c_shapes,
        grid_spec=gs, compiler_params=_cparams(sem),
        name="attn_fwd" if comm is None else "attn_fwd_with_gather")(qtab, ktab, qf, kf, vt, p1, *(comm["ins"] if comm else ()))
    return res[:5], list(res[5:])


def _attn_bwd_pre(dy4, o, p1, *, t=256):
    s = o.shape[0]
    t = _tile(s, t, 128)

    def body(dy_ref, o_ref, mz_ref, do_ref, dmz_ref, dl_ref, dlt_ref):
        dy, o_, mz = dy_ref[...], o_ref[...], mz_ref[...]
        do = dy * _silu(mz)
        do_ref[...] = do.astype(BF16)
        dmz_ref[...] = (dy * o_ * _dsilu(mz)).astype(BF16)
        prod = do * o_
        for h in range(MLA_HEADS):
            dl = jnp.sum(prod[:, h * V_HEAD:(h + 1) * V_HEAD], axis=-1, keepdims=True)
            dl_ref[h] = dl
            dlt_ref[h] = jnp.broadcast_to(dl, (t, 128)).T[0:1, :]

    sds = jax.ShapeDtypeStruct
    return pl.pallas_call(
        body,
        out_shape=(sds((s, BRANCH_W), BF16), sds((s, BRANCH_W), BF16), sds((MLA_HEADS, s, 1), F32), sds((MLA_HEADS, 1, s), F32)),
        grid=(s // t,),
        in_specs=[pl.BlockSpec((None, t, BRANCH_W), lambda i: (1, i, 0)), pl.BlockSpec((t, BRANCH_W), lambda i: (i, 0)),
                  pl.BlockSpec((t, BRANCH_W), lambda i: (i, 3))],
        out_specs=(pl.BlockSpec((t, BRANCH_W), lambda i: (i, 0)), pl.BlockSpec((t, BRANCH_W), lambda i: (i, 0)),
                   pl.BlockSpec((MLA_HEADS, t, 1), lambda i: (0, i, 0)), pl.BlockSpec((MLA_HEADS, 1, t), lambda i: (0, 0, i))),
        compiler_params=_cparams(("parallel",)), name="attn_bwd_pre")(dy4, o, p1)


def _attn_bwd_dq(qf, kf, vv, do, lse, delta, *, tq, comm=None):
    nh, s, _ = qf.shape
    nq = s // tq
    qtab, ktab, npairs = _causal_pairs(nq, True)
    wv = HPS * V_HEAD
    c_in, c_out, c_sem, c_shapes = _comm_extras(comm)
    nci, nco = len(c_in), len(c_out)

    def body(*refs):
        qt_ref, kt_ref, q_ref, k_ref, v_ref, do_ref, lse_ref, dl_ref = refs[:8]
        dq_ref = refs[8 + nci]
        acc_scr = refs[9 + nci + nco]
        pr = pl.program_id(1)
        qi, ki = qt_ref[pr], kt_ref[pr]
        if comm is not None:
            hg = pl.program_id(0)
            last_hg = nh // HPS - 1
            c_first, c_mid, c_last = _comm_hooks(comm, refs[8:8 + nci], refs[9 + nci:9 + nci + nco], refs[10 + nci + nco:],
                                                 (hg == 0) & (pr == 0), (hg == last_hg) & (pr == 0),
                                                 (hg == last_hg) & (pr == npairs - 1))
            c_first()

        @pl.when(ki == 0)
        def _():
            acc_scr[...] = jnp.zeros((HPS, tq, HEAD_PAD), F32)

        def step(diagonal):
            for u in range(HPS):
                k = k_ref[u]
                sc = _dot_nt(q_ref[u], k) * ATT_SCALE
                p = jnp.exp(sc - lse_ref[u])
                if diagonal:
                    r = lax.broadcasted_iota(jnp.int32, (tq, tq), 0)
                    c = lax.broadcasted_iota(jnp.int32, (tq, tq), 1)
                    p = jnp.where(c <= r, p, 0.0)
                dp = _dot_nt(do_ref[:, u * V_HEAD:(u + 1) * V_HEAD], v_ref[u])
                ds = p * (dp - dl_ref[u]) * ATT_SCALE
                acc_scr[u] += _dot(ds.astype(BF16), k)

        @pl.when(ki < qi)
        def _():
            step(False)

        @pl.when(ki == qi)
        def _():
            step(True)
            dq_ref[...] = acc_scr[...]

        if comm is not None:
            c_mid()
            c_last()

    gs = pltpu.PrefetchScalarGridSpec(
        num_scalar_prefetch=2, grid=(nh // HPS, npairs),
        in_specs=[pl.BlockSpec((HPS, tq, HEAD_PAD), lambda h, p, qt, kt: (h, qt[p], 0)),
                  pl.BlockSpec((HPS, tq, HEAD_PAD), lambda h, p, qt, kt: (h, kt[p], 0)),
                  pl.BlockSpec((HPS, tq, V_HEAD), lambda h, p, qt, kt: (h, kt[p], 0)),
                  pl.BlockSpec((tq, wv), lambda h, p, qt, kt: (qt[p], h)),
                  pl.BlockSpec((HPS, tq, 1), lambda h, p, qt, kt: (h, qt[p], 0)),
                  pl.BlockSpec((HPS, tq, 1), lambda h, p, qt, kt: (h, qt[p], 0))] + c_in,
        out_specs=(pl.BlockSpec((HPS, tq, HEAD_PAD), lambda h, p, qt, kt: (h, qt[p], 0)),) + tuple(c_out),
        scratch_shapes=[pltpu.VMEM((HPS, tq, HEAD_PAD), F32)] + c_sem)
    sem = ("parallel", "arbitrary") if comm is None else ("arbitrary", "arbitrary")
    res = pl.pallas_call(
        body, out_shape=(jax.ShapeDtypeStruct((nh, s, HEAD_PAD), F32),) + c_shapes, grid_spec=gs,
        compiler_params=_cparams(sem), name="attn_bwd_dq" if comm is None else "attn_bwd_dq_with_to_owner")(
            qtab, ktab, qf, kf, vv, do, lse, delta, *(comm["ins"] if comm else ()))
    return res[0], list(res[1:])


def _attn_bwd_dkv(qf, kf, vv, do, lset, deltat, *, tq, comm=None):
    nh, s, _ = qf.shape
    nq = s // tq
    qtab, ktab, npairs = _causal_pairs(nq, False)
    wv = HPS * V_HEAD
    c_in, c_out, c_sem, c_shapes = _comm_extras(comm)
    nci, nco = len(c_in), len(c_out)

    def body(*refs):
        qt_ref, kt_ref, k_ref, v_ref, q_ref, do_ref, lse_ref, dl_ref = refs[:8]
        dk_ref, dv_ref = refs[8 + nci:10 + nci]
        dk_scr, dv_scr = refs[10 + nci + nco:12 + nci + nco]
        pr = pl.program_id(1)
        qi, ki = qt_ref[pr], kt_ref[pr]
        if comm is not None:
            hg = pl.program_id(0)
            last_hg = nh // HPS - 1
            c_first, c_mid, c_last = _comm_hooks(comm, refs[8:8 + nci], refs[10 + nci:10 + nci + nco], refs[12 + nci + nco:],
                                                 (hg == 0) & (pr == 0), (hg == last_hg) & (pr == 0),
                                                 (hg == last_hg) & (pr == npairs - 1))
            c_first()

        def step(diagonal):
            for u in range(HPS):
                q = q_ref[u]
                do_ = do_ref[:, u * V_HEAD:(u + 1) * V_HEAD]
                st = _dot_nt(k_ref[u], q) * ATT_SCALE
                pt = jnp.exp(st - lse_ref[u])
                if diagonal:
                    r = lax.broadcasted_iota(jnp.int32, (tq, tq), 0)
                    c = lax.broadcasted_iota(jnp.int32, (tq, tq), 1)
                    pt = jnp.where(r <= c, pt, 0.0)
                dpt = _dot_nt(v_ref[u], do_)
                dst = pt * (dpt - dl_ref[u]) * ATT_SCALE
                if diagonal:
                    dv_scr[u] = _dot(pt.astype(BF16), do_)
                    dk_scr[u] = _dot(dst.astype(BF16), q)
                else:
                    dv_scr[u] += _dot(pt.astype(BF16), do_)
                    dk_scr[u] += _dot(dst.astype(BF16), q)

        @pl.when(qi == ki)
        def _():
            step(True)

        @pl.when(qi > ki)
        def _():
            step(False)

        @pl.when(qi == nq - 1)
        def _():
            dk_ref[...] = dk_scr[...]
            dv_ref[...] = dv_scr[...]

        if comm is not None:
            c_mid()
            c_last()

    sds = jax.ShapeDtypeStruct
    gs = pltpu.PrefetchScalarGridSpec(
        num_scalar_prefetch=2, grid=(nh // HPS, npairs),
        in_specs=[pl.BlockSpec((HPS, tq, HEAD_PAD), lambda h, p, qt, kt: (h, kt[p], 0)),
                  pl.BlockSpec((HPS, tq, V_HEAD), lambda h, p, qt, kt: (h, kt[p], 0)),
                  pl.BlockSpec((HPS, tq, HEAD_PAD), lambda h, p, qt, kt: (h, qt[p], 0)),
                  pl.BlockSpec((tq, wv), lambda h, p, qt, kt: (qt[p], h)),
                  pl.BlockSpec((HPS, 1, tq), lambda h, p, qt, kt: (h, 0, qt[p])),
                  pl.BlockSpec((HPS, 1, tq), lambda h, p, qt, kt: (h, 0, qt[p]))] + c_in,
        out_specs=(pl.BlockSpec((HPS, tq, HEAD_PAD), lambda h, p, qt, kt: (h, kt[p], 0)),
                   pl.BlockSpec((HPS, tq, V_HEAD), lambda h, p, qt, kt: (h, kt[p], 0))) + tuple(c_out),
        scratch_shapes=[pltpu.VMEM((HPS, tq, HEAD_PAD), F32), pltpu.VMEM((HPS, tq, V_HEAD), F32)] + c_sem)
    sem = ("parallel", "arbitrary") if comm is None else ("arbitrary", "arbitrary")
    res = pl.pallas_call(
        body, out_shape=(sds((nh, s, HEAD_PAD), F32), sds((nh, s, V_HEAD), F32)) + c_shapes, grid_spec=gs,
        compiler_params=_cparams(sem), name="attn_bwd_dkv" if comm is None else "attn_bwd_dkv_with_exchange")(
            qtab, ktab, kf, vv, qf, do, lset, deltat, *(comm["ins"] if comm else ()))
    return res[0], res[1], list(res[2:])


XATT_SCALE = XATTN_HEAD_DIM ** -0.5
XQ_COL = 8


def _memkv_prep(mem_kv, k_g):
    m = mem_kv.shape[0]

    def body(kv_ref, g_ref, k_ref, v_ref):
        for h in range(XATTN_HEADS):
            sl = slice(h * XATTN_HEAD_DIM, (h + 1) * XATTN_HEAD_DIM)
            k_ref[:, sl] = _rms(kv_ref[:, sl], g_ref[...], XATTN_HEAD_DIM).astype(BF16)
        v_ref[...] = kv_ref[:, BRANCH_W:2 * BRANCH_W].astype(BF16)

    sds = jax.ShapeDtypeStruct
    return pl.pallas_call(body, out_shape=(sds((m, BRANCH_W), BF16), sds((m, BRANCH_W), BF16)),
                          compiler_params=_cparams(), name="memkv_prep")(mem_kv, k_g)


def _memkv_prep_bwd(mem_kv, k_g, dk, dv):
    m = mem_kv.shape[0]

    def body(kv_ref, g_ref, dk_ref, dv_ref, d_ref, gk_ref):
        gk = jnp.zeros((1, XATTN_HEAD_DIM), F32)
        for h in range(XATTN_HEADS):
            sl = slice(h * XATTN_HEAD_DIM, (h + 1) * XATTN_HEAD_DIM)
            dx, dg = _rms_bwd(kv_ref[:, sl], g_ref[...], XATTN_HEAD_DIM, dk_ref[:, sl])
            d_ref[:, sl] = dx.astype(BF16)
            gk = gk + jnp.sum(dg, axis=0, keepdims=True)
        d_ref[:, BRANCH_W:2 * BRANCH_W] = dv_ref[...].astype(BF16)
        gk_ref[...] = gk

    sds = jax.ShapeDtypeStruct
    return pl.pallas_call(body, out_shape=(sds((m, 2 * BRANCH_W), BF16), sds((1, XATTN_HEAD_DIM), F32)),
                          compiler_params=_cparams(), name="memkv_prep_bwd")(mem_kv, k_g, dk, dv)


def _xattn_probs(xq, k_ref, qg, h):
    sl = slice(h * XATTN_HEAD_DIM, (h + 1) * XATTN_HEAD_DIM)
    q = _rms(xq[:, sl], qg, XATTN_HEAD_DIM).astype(BF16)
    sc = _dot_nt(q, k_ref[:, sl]) * XATT_SCALE
    e = jnp.exp(sc - jnp.max(sc, axis=-1, keepdims=True))
    return q, e / jnp.sum(e, axis=-1, keepdims=True)


def _xattn_fwd(p1, kx, vx, q_g, *, t=256):
    s = p1.shape[0]
    m = kx.shape[0]
    t = _tile(s, t, 128)

    def body(xq_ref, xz_ref, k_ref, v_ref, g_ref, y_ref, yt_ref):
        xq = xq_ref[...]
        outs = []
        for h in range(XATTN_HEADS):
            _, p = _xattn_probs(xq, k_ref, g_ref[...], h)
            outs.append(_dot(p.astype(BF16), v_ref[:, h * XATTN_HEAD_DIM:(h + 1) * XATTN_HEAD_DIM]))
        y = jnp.concatenate(outs, axis=1) * _silu(xz_ref[...])
        y_ref[...] = y.astype(BF16)
        yt_ref[...] = y.T.astype(BF16)

    full = lambda shp: pl.BlockSpec(shp, lambda i: tuple(0 for _ in shp))
    sds = jax.ShapeDtypeStruct
    return pl.pallas_call(
        body, out_shape=(sds((s, BRANCH_W), BF16), sds((BRANCH_W, s), BF16)), grid=(s // t,),
        in_specs=[pl.BlockSpec((t, BRANCH_W), lambda i: (i, XQ_COL)), pl.BlockSpec((t, BRANCH_W), lambda i: (i, XQ_COL + 1)),
                  full((m, BRANCH_W)), full((m, BRANCH_W)), full((1, XATTN_HEAD_DIM))],
        out_specs=(pl.BlockSpec((t, BRANCH_W), lambda i: (i, 0)), pl.BlockSpec((BRANCH_W, t), lambda i: (0, i))),
        compiler_params=_cparams(("parallel",)), name="xattn_fwd")(p1, p1, kx, vx, q_g)


def _xattn_bwd(p1, dy4, kx, vx, q_g, *, t=256):
    s = p1.shape[0]
    m = kx.shape[0]
    t = _tile(s, t, 128)

    def body(xq_ref, xz_ref, dy_ref, k_ref, v_ref, g_ref, d_ref, dk_ref, dv_ref, gq_ref):
        i = pl.program_id(0)
        xq, xz, dy = xq_ref[...], xz_ref[...], dy_ref[...]
        do = dy * _silu(xz)
        gq = jnp.zeros((1, XATTN_HEAD_DIM), F32)
        outs, dks, dvs = [], [], []
        for h in range(XATTN_HEADS):
            sl = slice(h * XATTN_HEAD_DIM, (h + 1) * XATTN_HEAD_DIM)
            q, p = _xattn_probs(xq, k_ref, g_ref[...], h)
            pb = p.astype(BF16)
            outs.append(_dot(pb, v_ref[:, sl]))
            do_h = do[:, sl].astype(BF16)
            dvs.append(_dot_tn(pb, do_h))
            dp = _dot_nt(do_h, v_ref[:, sl])
            ds = (p * (dp - jnp.sum(p * dp, axis=-1, keepdims=True)) * XATT_SCALE).astype(BF16)
            dks.append(_dot_tn(ds, q))
            dx, dg = _rms_bwd(xq[:, sl], g_ref[...], XATTN_HEAD_DIM, _dot(ds, k_ref[:, sl]))
            d_ref[:, sl] = dx.astype(BF16)
            gq = gq + jnp.sum(dg, axis=0, keepdims=True)
        o = jnp.concatenate(outs, axis=1)
        d_ref[:, BRANCH_W:2 * BRANCH_W] = (dy * o * _dsilu(xz)).astype(BF16)
        dk = jnp.concatenate(dks, axis=1)
        dv = jnp.concatenate(dvs, axis=1)

        @pl.when(i == 0)
        def _():
            dk_ref[...] = dk
            dv_ref[...] = dv
            gq_ref[...] = gq

        @pl.when(i > 0)
        def _():
            dk_ref[...] += dk
            dv_ref[...] += dv
            gq_ref[...] += gq

    full = lambda shp: pl.BlockSpec(shp, lambda i: tuple(0 for _ in shp))
    sds = jax.ShapeDtypeStruct
    return pl.pallas_call(
        body, out_shape=(sds((s, 2 * BRANCH_W), BF16), sds((m, BRANCH_W), F32), sds((m, BRANCH_W), F32), sds((1, XATTN_HEAD_DIM), F32)),
        grid=(s // t,),
        in_specs=[pl.BlockSpec((t, BRANCH_W), lambda i: (i, XQ_COL)), pl.BlockSpec((t, BRANCH_W), lambda i: (i, XQ_COL + 1)),
                  pl.BlockSpec((None, t, BRANCH_W), lambda i: (3, i, 0)),
                  full((m, BRANCH_W)), full((m, BRANCH_W)), full((1, XATTN_HEAD_DIM))],
        out_specs=(pl.BlockSpec((t, 2 * BRANCH_W), lambda i: (i, 0)), full((m, BRANCH_W)), full((m, BRANCH_W)),
                   full((1, XATTN_HEAD_DIM))),
        compiler_params=_cparams(("arbitrary",)), name="xattn_bwd")(p1, p1, dy4, kx, vx, q_g)


def _gate_fwd(ystack, w_branch, gp, gate_b, *, tm=1024, tn=1024):
    _, s, _ = ystack.shape
    d = w_branch.shape[2]
    tm, tn = _tile(s, tm, 128), _tile(d, tn)
    nj = d // tn

    def body(y_ref, w_ref, gp_ref, gb_ref, o_ref, ot_ref, acc_scr):
        b = pl.program_id(2)
        part = jax.nn.sigmoid(gp_ref[...] + gb_ref[...]) * _dot(y_ref[...], w_ref[...])

        @pl.when(b == 0)
        def _():
            acc_scr[...] = part

        @pl.when(b > 0)
        def _():
            acc_scr[...] += part

        @pl.when(b == N_BRANCH - 1)
        def _():
            acc = acc_scr[...]
            o_ref[...] = acc.astype(BF16)
            ot_ref[...] = acc.T.astype(BF16)

    sds = jax.ShapeDtypeStruct
    return pl.pallas_call(
        body, out_shape=(sds((s, d), BF16), sds((d, s), BF16)), grid=(s // tm, nj, N_BRANCH),
        in_specs=[pl.BlockSpec((None, tm, BRANCH_W), lambda i, j, b: (b, i, 0)),
                  pl.BlockSpec((None, BRANCH_W, tn), lambda i, j, b: (b, 0, j)),
                  pl.BlockSpec((tm, tn), lambda i, j, b: (i, b * nj + j)),
                  pl.BlockSpec((1, tn), lambda i, j, b: (0, b * nj + j))],
        out_specs=(pl.BlockSpec((tm, tn), lambda i, j, b: (i, j)), pl.BlockSpec((tn, tm), lambda i, j, b: (j, i))),
        scratch_shapes=[pltpu.VMEM((tm, tn), F32)],
        compiler_params=_cparams(("parallel", "parallel", "arbitrary")), name="gate_fwd")(ystack, w_branch, gp, gate_b)


def _gate_bwd(ystack, w_branch, gp, gate_b, dm, *, tm=1024, tn=1024):
    _, s, _ = ystack.shape
    d = w_branch.shape[2]
    tm, tn = _tile(s, tm, 128), _tile(d, tn)
    nj = d // tn

    def body(y_ref, w_ref, gp_ref, gb_ref, dm_ref, dp_ref, dg_ref, gb_out_ref):
        i = pl.program_id(2)
        proj = _dot(y_ref[...], w_ref[...])
        gate = jax.nn.sigmoid(gp_ref[...] + gb_ref[...])
        dmv = dm_ref[...]
        dp_ref[...] = (dmv * gate).astype(BF16)
        dpre = dmv * proj * gate * (1.0 - gate)
        dg_ref[...] = dpre.astype(BF16)
        part = jnp.sum(dpre, axis=0, keepdims=True)

        @pl.when(i == 0)
        def _():
            gb_out_ref[...] = part

        @pl.when(i > 0)
        def _():
            gb_out_ref[...] += part

    sds = jax.ShapeDtypeStruct
    return pl.pallas_call(
        body, out_shape=(sds((N_BRANCH, s, d), BF16), sds((s, N_BRANCH * d), BF16), sds((1, N_BRANCH * d), F32)),
        grid=(N_BRANCH, nj, s // tm),
        in_specs=[pl.BlockSpec((None, tm, BRANCH_W), lambda b, j, i: (b, i, 0)),
                  pl.BlockSpec((None, BRANCH_W, tn), lambda b, j, i: (b, 0, j)),
                  pl.BlockSpec((tm, tn), lambda b, j, i: (i, b * nj + j)),
                  pl.BlockSpec((1, tn), lambda b, j, i: (0, b * nj + j)),
                  pl.BlockSpec((tm, tn), lambda b, j, i: (i, j))],
        out_specs=(pl.BlockSpec((None, tm, tn), lambda b, j, i: (b, i, j)),
                   pl.BlockSpec((tm, tn), lambda b, j, i: (i, b * nj + j)),
                   pl.BlockSpec((1, tn), lambda b, j, i: (0, b * nj + j))),
        compiler_params=_cparams(("parallel", "parallel", "arbitrary")), name="gate_bwd")(ystack, w_branch, gp, gate_b, dm)


def _adamw(w, g, m, v, *, name):
    shape = w.shape
    c = shape[-1]
    r = 1
    for n in shape[:-1]:
        r *= n
    w2, g2, m2, v2 = (a.reshape(r, c) for a in (w, g, m, v))
    tr = _tile(r, max(8, (1 << 19) // c // 8 * 8), 8)
    c1 = 1.0 / (1.0 - ADAM_B1 ** ADAM_STEP)
    c2 = 1.0 / (1.0 - ADAM_B2 ** ADAM_STEP)

    def body(w_ref, g_ref, m_ref, v_ref, d_ref, nm_ref, nv_ref):
        gv = g_ref[...]
        nm = ADAM_B1 * m_ref[...] + (1.0 - ADAM_B1) * gv
        nv = ADAM_B2 * v_ref[...] + (1.0 - ADAM_B2) * (gv * gv)
        nm_ref[...] = nm
        nv_ref[...] = nv
        d_ref[...] = -ADAM_LR * ((nm * c1) / (jnp.sqrt(nv * c2) + ADAM_EPS) + ADAM_WD * w_ref[...])

    blk = pl.BlockSpec((tr, c), lambda i: (i, 0))
    sd = jax.ShapeDtypeStruct((r, c), F32)
    d2, nm2, nv2 = pl.pallas_call(body, out_shape=(sd, sd, sd), grid=(r // tr,), in_specs=[blk] * 4, out_specs=(blk,) * 3,
                                  compiler_params=_cparams(("parallel",)), name=name)(w2, g2, m2, v2)
    return d2.reshape(shape), nm2.reshape(shape), nv2.reshape(shape)


def _place():
    x, y, c = lax.axis_index("x"), lax.axis_index("y"), lax.axis_index("c")
    chips = [(1 - x, y), (x, 1 - y), (1 - x, 1 - y)]
    return x, y, c, 2 * x + y, chips, [2 * cx + cy for cx, cy in chips]


ANY = pl.BlockSpec(memory_space=pl.ANY)


def _all_gather(shards):
    n = len(shards)

    def body(*refs):
        ins, outs = refs[:n], refs[n:2 * n]
        send, recv = refs[2 * n:]
        x, y, c, k, chips, ks = _place()
        sib = (x, y, 1 - c)
        sends = []
        for a in range(n):
            for j in range(3):
                cp = pltpu.make_async_remote_copy(src_ref=ins[a].at[c], dst_ref=outs[a].at[c, k], send_sem=send.at[6 * a + j],
                                                  recv_sem=recv.at[6 * a + j], device_id=(*chips[j], c), device_id_type=MESH)
                cp.start()
                sends.append(cp)
        for a in range(n):
            for j in range(3):
                slab = outs[a].at[c, ks[j]]
                pltpu.make_async_remote_copy(src_ref=slab, dst_ref=slab, send_sem=send.at[6 * a + j], recv_sem=recv.at[6 * a + j],
                                             device_id=(*chips[j], c), device_id_type=MESH).wait_recv()
                cp = pltpu.make_async_remote_copy(src_ref=slab, dst_ref=slab, send_sem=send.at[6 * a + 3 + j],
                                                  recv_sem=recv.at[6 * a + 3 + j], device_id=sib, device_id_type=MESH)
                cp.start()
                sends.append(cp)
        for a in range(n):
            for j in range(3):
                slab = outs[a].at[1 - c, ks[j]]
                pltpu.make_async_remote_copy(src_ref=slab, dst_ref=slab, send_sem=send.at[6 * a + 3 + j],
                                             recv_sem=recv.at[6 * a + 3 + j], device_id=sib, device_id_type=MESH).wait_recv()
        for cp in sends:
            cp.wait_send()

    out_shape = tuple(jax.ShapeDtypeStruct((2, 4) + s.shape[1:], s.dtype) for s in shards)
    return pl.pallas_call(
        body, out_shape=out_shape, in_specs=[ANY] * n, out_specs=(ANY,) * n,
        scratch_shapes=[pltpu.SemaphoreType.DMA((6 * n,)), pltpu.SemaphoreType.DMA((6 * n,))],
        name="weights_all_gather")(*shards)


def _rs_exchange_cores(grads):
    n = len(grads)

    def body(*refs):
        ins, outs = refs[:n], refs[n:2 * n]
        send, recv = refs[2 * n:]
        x, y, c, _, _, _ = _place()
        sib = (x, y, 1 - c)
        cps = []
        for a in range(n):
            cp = pltpu.make_async_remote_copy(src_ref=ins[a].at[1 - c], dst_ref=outs[a], send_sem=send.at[a], recv_sem=recv.at[a],
                                              device_id=sib, device_id_type=MESH)
            cp.start()
            cps.append(cp)
        for cp in cps:
            cp.wait()

    out_shape = tuple(jax.ShapeDtypeStruct(g.shape[1:], g.dtype) for g in grads)
    return pl.pallas_call(body, out_shape=out_shape, in_specs=[ANY] * n, out_specs=(ANY,) * n,
                          scratch_shapes=[pltpu.SemaphoreType.DMA((n,)), pltpu.SemaphoreType.DMA((n,))],
                          name="grads_exchange_cores")(*grads)


def _rs_exchange_chips(parts):
    n = len(parts)

    def body(*refs):
        ins, outs = refs[:n], refs[n:2 * n]
        send, recv = refs[2 * n:]
        x, y, c, k, chips, ks = _place()
        sends = []
        for a in range(n):
            for j in range(3):
                cp = pltpu.make_async_remote_copy(src_ref=ins[a].at[ks[j]], dst_ref=outs[a].at[j], send_sem=send.at[3 * a + j],
                                                  recv_sem=recv.at[3 * a + j], device_id=(*chips[j], c), device_id_type=MESH)
                cp.start()
                sends.append(cp)
        for cp in sends:
            cp.wait()

    out_shape = tuple(jax.ShapeDtypeStruct((3,) + p.shape[1:], p.dtype) for p in parts)
    return pl.pallas_call(
        body, out_shape=out_shape, in_specs=[ANY] * n, out_specs=(ANY,) * n,
        scratch_shapes=[pltpu.SemaphoreType.DMA((3 * n,)), pltpu.SemaphoreType.DMA((3 * n,))],
        name="grads_exchange_chips")(*parts)


def _rs_share_cores(bufs):
    n = len(bufs)

    def body(*refs):
        outs = refs[n:2 * n]
        send, recv = refs[2 * n:]
        x, y, c, _, _, _ = _place()
        sib = (x, y, 1 - c)
        cps = []
        for a in range(n):
            cp = pltpu.make_async_remote_copy(src_ref=outs[a].at[c], dst_ref=outs[a].at[c], send_sem=send.at[a], recv_sem=recv.at[a],
                                              device_id=sib, device_id_type=MESH)
            cp.start()
            cps.append(cp)
        for a in range(n):
            slab = outs[a].at[1 - c]
            pltpu.make_async_remote_copy(src_ref=slab, dst_ref=slab, send_sem=send.at[a], recv_sem=recv.at[a],
                                         device_id=sib, device_id_type=MESH).wait_recv()
        for cp in cps:
            cp.wait_send()

    out_shape = tuple(jax.ShapeDtypeStruct(b.shape, b.dtype) for b in bufs)
    return pl.pallas_call(
        body, out_shape=out_shape, in_specs=[ANY] * n, out_specs=(ANY,) * n,
        input_output_aliases={a: a for a in range(n)},
        scratch_shapes=[pltpu.SemaphoreType.DMA((n,)), pltpu.SemaphoreType.DMA((n,))],
        name="grads_share_cores")(*bufs)


def _add_core_halves(g, ra, c_idx, *, name):
    _, _, r, c = g.shape
    tr = _tile(r, max(16, (1 << 19) // c // 16 * 16), 16)

    def body(c_ref, g_ref, ra_ref, o_ref, ob_ref):
        tot = g_ref[...] + ra_ref[...]
        o_ref[...] = tot
        ob_ref[...] = tot.astype(BF16)

    blk = pl.BlockSpec((None, tr, c), lambda j, i, cr: (j, i, 0))
    gs = pltpu.PrefetchScalarGridSpec(
        num_scalar_prefetch=1, grid=(4, r // tr),
        in_specs=[pl.BlockSpec((None, None, tr, c), lambda j, i, cr: (cr[0], j, i, 0)), blk],
        out_specs=(blk, blk))
    return pl.pallas_call(body, out_shape=(jax.ShapeDtypeStruct((4, r, c), F32), jax.ShapeDtypeStruct((4, r, c), BF16)), grid_spec=gs,
                          compiler_params=_cparams(("parallel", "parallel")), name=name)(c_idx, g, ra)


def _add_chips(p, r3, k_idx, c_idx, *, name):
    _, r, c = p.shape
    tr = _tile(r, max(16, (1 << 18) // c // 16 * 16), 16)

    def body(k_ref, c_ref, p_ref, r_ref, o_ref):
        o_ref[...] = ((p_ref[...] + r_ref[0].astype(F32)) + r_ref[1].astype(F32)) + r_ref[2].astype(F32)

    gs = pltpu.PrefetchScalarGridSpec(
        num_scalar_prefetch=2, grid=(r // tr,),
        in_specs=[pl.BlockSpec((None, tr, c), lambda i, kr, cr: (kr[0], i, 0)), pl.BlockSpec((3, tr, c), lambda i, kr, cr: (0, i, 0))],
        out_specs=pl.BlockSpec((None, tr, c), lambda i, kr, cr: (cr[0], i, 0)))
    return pl.pallas_call(body, out_shape=jax.ShapeDtypeStruct((2, r, c), F32), grid_spec=gs,
                          compiler_params=_cparams(("parallel",)), name=name)(k_idx, c_idx, p, r3)


def _rdma(src, dst, send, recv, idx, dev):
    return pltpu.make_async_remote_copy(src_ref=src, dst_ref=dst, send_sem=send.at[idx], recv_sem=recv.at[idx],
                                        device_id=dev, device_id_type=MESH)


def _run_comm(comm, name):
    n_in, n_out = len(comm["ins"]), len(comm["outs"])

    def body(*refs):
        ins, outs = refs[:n_in], refs[n_in:n_in + n_out]
        send, recv = refs[n_in + n_out:]
        for phase in comm["phases"]:
            phase(ins, outs, send, recv)

    return pl.pallas_call(
        body, out_shape=tuple(comm["outs"]), in_specs=[ANY] * n_in, out_specs=(ANY,) * n_out,
        input_output_aliases=comm.get("aliases", {}),
        scratch_shapes=[pltpu.SemaphoreType.DMA((comm["nsem"],)), pltpu.SemaphoreType.DMA((comm["nsem"],))],
        name=name)(*comm["ins"])


def _gather_comm(shards, layer):
    n = len(shards)

    def start(ins, outs, send, recv):
        x, y, c, k, chips, ks = _place()

        @pl.when(c == layer)
        def _():
            for a in range(n):
                for j in range(3):
                    _rdma(ins[a], outs[a].at[k], send, recv, 6 * a + j, (*chips[j], c)).start()

    def forward(ins, outs, send, recv):
        x, y, c, k, chips, ks = _place()

        @pl.when(c == layer)
        def _():
            for a in range(n):
                for j in range(3):
                    slab = outs[a].at[ks[j]]
                    _rdma(slab, slab, send, recv, 6 * a + j, (*chips[j], c)).wait_recv()
                    _rdma(slab, slab, send, recv, 6 * a + 3 + j, (x, y, 1 - c)).start()

    def finish(ins, outs, send, recv):
        x, y, c, k, chips, ks = _place()

        @pl.when(c == layer)
        def _():
            for a in range(n):
                for j in range(3):
                    slab = outs[a].at[ks[j]]
                    _rdma(ins[a], outs[a].at[k], send, recv, 6 * a + j, (*chips[j], c)).wait_send()
                    _rdma(slab, slab, send, recv, 6 * a + 3 + j, (x, y, 1 - c)).wait_send()

        @pl.when(c != layer)
        def _():
            for a in range(n):
                for j in range(3):
                    slab = outs[a].at[ks[j]]
                    _rdma(slab, slab, send, recv, 6 * a + 3 + j, (x, y, 1 - c)).wait_recv()

    return dict(ins=list(shards), outs=[jax.ShapeDtypeStruct((4,) + s.shape, s.dtype) for s in shards], nsem=6 * n,
                phases=[start, forward, finish])


def _to_owner_comm(grads, layer):
    n = len(grads)

    def start(ins, outs, send, recv):
        x, y, c, _, _, _ = _place()

        @pl.when(c != layer)
        def _():
            for a in range(n):
                _rdma(ins[a], outs[a], send, recv, a, (x, y, 1 - c)).start()

    def finish(ins, outs, send, recv):
        x, y, c, _, _, _ = _place()

        @pl.when(c != layer)
        def _():
            for a in range(n):
                _rdma(ins[a], outs[a], send, recv, a, (x, y, 1 - c)).wait_send()

        @pl.when(c == layer)
        def _():
            for a in range(n):
                _rdma(ins[a], outs[a], send, recv, a, (x, y, 1 - c)).wait_recv()

    return dict(ins=list(grads), outs=[jax.ShapeDtypeStruct(g.shape, g.dtype) for g in grads], nsem=n, phases=[start, finish])


def _exchange_comm(parts, layer):
    n = len(parts)

    def start(ins, outs, send, recv):
        x, y, c, k, chips, ks = _place()

        @pl.when(c == layer)
        def _():
            for a in range(n):
                for j in range(3):
                    _rdma(ins[a].at[ks[j]], outs[a].at[j], send, recv, 3 * a + j, (*chips[j], c)).start()

    def finish(ins, outs, send, recv):
        x, y, c, k, chips, ks = _place()

        @pl.when(c == layer)
        def _():
            for a in range(n):
                for j in range(3):
                    _rdma(ins[a].at[ks[j]], outs[a].at[j], send, recv, 3 * a + j, (*chips[j], c)).wait()

    return dict(ins=list(parts), outs=[jax.ShapeDtypeStruct((3,) + p.shape[1:], p.dtype) for p in parts], nsem=3 * n,
                phases=[start, finish])


def _share_comm(bufs, layer):
    n = len(bufs)

    def go(ins, outs, send, recv):
        x, y, c, _, _, _ = _place()

        @pl.when(c == layer)
        def _():
            for a in range(n):
                _rdma(outs[a].at[layer], outs[a].at[layer], send, recv, a, (x, y, 1 - c)).start()
            for a in range(n):
                _rdma(outs[a].at[layer], outs[a].at[layer], send, recv, a, (x, y, 1 - c)).wait_send()

        @pl.when(c != layer)
        def _():
            for a in range(n):
                _rdma(outs[a].at[layer], outs[a].at[layer], send, recv, a, (x, y, 1 - c)).wait_recv()

    return dict(ins=list(bufs), outs=[jax.ShapeDtypeStruct(b.shape, b.dtype) for b in bufs], nsem=n, phases=[go],
                aliases={a: a for a in range(n)})


def _add_owner(g, ra, own, *, name):
    _, r, c = g.shape
    tr = _tile(r, max(16, (1 << 20) // c // 16 * 16), 16)

    def body(own_ref, g_ref, ra_ref, o_ref, ob_ref):
        tot = g_ref[...] + ra_ref[...].astype(F32)
        o_ref[...] = tot
        ob_ref[...] = tot.astype(BF16)

    blk = pl.BlockSpec((None, tr, c), lambda j, i, o: (j * o[0], i * o[0], 0))
    gs = pltpu.PrefetchScalarGridSpec(num_scalar_prefetch=1, grid=(4, r // tr), in_specs=[blk, blk], out_specs=(blk, blk))
    return pl.pallas_call(body, out_shape=(jax.ShapeDtypeStruct((4, r, c), F32), jax.ShapeDtypeStruct((4, r, c), BF16)),
                          grid_spec=gs, compiler_params=_cparams(("arbitrary", "arbitrary")), name=name)(own, g, ra)


def _add_chips_layer(p, r3, k_idx, own, layer, buf, *, name):
    _, r, c = p.shape
    tr = _tile(r, max(16, (1 << 20) // c // 16 * 16), 16)

    def body(k_ref, own_ref, p_ref, r_ref, *rest):
        o_ref = rest[-1]
        o_ref[...] = ((p_ref[...] + r_ref[0].astype(F32)) + r_ref[1].astype(F32)) + r_ref[2].astype(F32)

    in_specs = [pl.BlockSpec((None, tr, c), lambda i, kr, o: (kr[0] * o[0], i * o[0], 0)),
                pl.BlockSpec((3, tr, c), lambda i, kr, o: (0, i * o[0], 0))]
    args = [k_idx, own, p, r3]
    aliases = {}
    if buf is not None:
        in_specs.append(ANY)
        args.append(buf)
        aliases = {4: 0}
    gs = pltpu.PrefetchScalarGridSpec(num_scalar_prefetch=2, grid=(r // tr,), in_specs=in_specs,
                                      out_specs=pl.BlockSpec((None, tr, c), lambda i, kr, o: (layer, i * o[0], 0)))
    return pl.pallas_call(body, out_shape=jax.ShapeDtypeStruct((2, r, c), F32), grid_spec=gs, input_output_aliases=aliases,
                          compiler_params=_cparams(("arbitrary",)), name=name)(*args)


def _all_reduce_small(vec):
    r = vec.shape[0]

    def body(v_ref, gath_ref, sum_ref, send, recv):
        x, y, c = lax.axis_index("x"), lax.axis_index("y"), lax.axis_index("c")
        me = 4 * x + 2 * y + c
        gath_ref[me] = v_ref[...]
        cps = []
        for f in range(1, 8):
            fx, fy, fc = (f >> 2) & 1, (f >> 1) & 1, f & 1
            peer = (x ^ fx, y ^ fy, c ^ fc)
            cp = pltpu.make_async_remote_copy(src_ref=v_ref, dst_ref=gath_ref.at[me], send_sem=send.at[f - 1], recv_sem=recv.at[f - 1],
                                              device_id=peer, device_id_type=MESH)
            cp.start()
            cps.append(cp)
        for f in range(1, 8):
            fx, fy, fc = (f >> 2) & 1, (f >> 1) & 1, f & 1
            src = 4 * (x ^ fx) + 2 * (y ^ fy) + (c ^ fc)
            pltpu.make_async_remote_copy(src_ref=v_ref, dst_ref=gath_ref.at[src], send_sem=send.at[f - 1], recv_sem=recv.at[f - 1],
                                         device_id=(x ^ fx, y ^ fy, c ^ fc), device_id_type=MESH).wait_recv()
        for cp in cps:
            cp.wait_send()
        acc = gath_ref[0]
        for i in range(1, 8):
            acc = acc + gath_ref[i]
        sum_ref[...] = acc

    vm = pl.BlockSpec(memory_space=pltpu.VMEM)
    _, total = pl.pallas_call(
        body, out_shape=(jax.ShapeDtypeStruct((8, r, 128), F32), jax.ShapeDtypeStruct((r, 128), F32)),
        in_specs=[vm], out_specs=(vm, vm),
        scratch_shapes=[pltpu.SemaphoreType.DMA((7,)), pltpu.SemaphoreType.DMA((7,))],
        name="small_all_reduce")(vec)
    return total


def _full_weight(gw, name):
    gathered, own, chip = gw[name]
    return jnp.concatenate([jnp.where(chip == k, own, gathered[k]) for k in range(4)], axis=SHARD_AXIS[name])


def _to_shards(full, name):
    return jnp.stack(jnp.split(full, 4, axis=SHARD_AXIS[name]), axis=0)


def _rope_tables(positions):
    inv = ROPE_THETA ** (-jnp.arange(0, QK_ROPE, 2, dtype=F32) / QK_ROPE)
    ang = positions.astype(F32)[:, None] * inv
    cos, sin = jnp.cos(ang), jnp.sin(ang)
    s = positions.shape[0]
    pad = jnp.zeros((s, HEAD_PAD - QK_HEAD), F32)
    ctab = jnp.concatenate([jnp.ones((s, QK_NOPE), F32), cos, cos, pad], axis=1)
    stab = jnp.concatenate([jnp.zeros((s, QK_NOPE), F32), -sin, sin, pad], axis=1)
    return ctab, stab


def _pad_gain(g):
    return jnp.concatenate([g, jnp.zeros((HEAD_PAD - QK_HEAD,), F32)])[None, :]


def _layer_weights(gw, rep, l, ql, kvl):
    d = rep["norm_g"].shape[1]
    w_in = _full_weight(gw, "w_in")
    o_kr = 2 * BRANCH_W + ql + kvl
    o_g = o_kr + QK_ROPE + 7 * BRANCH_W
    w = {}
    w["w1"] = jnp.concatenate([w_in[:, :o_kr], w_in[:, o_kr + QK_ROPE:o_g]], axis=1)
    w["wg"] = w_in[:, o_g:]
    w["wkr"] = jnp.concatenate([w_in[:, o_kr:o_kr + QK_ROPE], jnp.zeros((d, 128 - QK_ROPE), BF16)], axis=1)
    for nme in ("norm_g", "gate_b", "pool_scale", "q_a_norm_g", "kv_a_norm_g", "mem_norm_g", "xattn_q_norm_g", "xattn_k_norm_g"):
        w[nme] = rep[nme][l][None, :]
    w["mla_q_norm_g"] = _pad_gain(rep["mla_q_norm_g"][l])
    w["mla_k_norm_g"] = _pad_gain(rep["mla_k_norm_g"][l])
    return w


def _other_weights(gw, ql):
    w = {}
    wuq = _full_weight(gw, "w_uq").reshape(ql, MLA_HEADS, QK_HEAD)
    w["w_uq"] = jnp.pad(wuq, ((0, 0), (0, 0), (0, HEAD_PAD - QK_HEAD))).reshape(ql, MLA_HEADS * HEAD_PAD)
    for nme in ("w_ukv", "pool_w", "conv_w", "w_mem_kv", "w_branch", "w_out"):
        w[nme] = _full_weight(gw, nme)
    return w


def _forward_layer(x, mem, ctab, stab, w, tq, l, comm=None, late=None):
    sfx = f"_l{l}"
    h, ht = _norm_fwd(x, w["norm_g"], name="norm_fwd" + sfx)
    if late is None:
        p1 = _mm(h, w["w1"], name="proj_main" + sfx)
    else:
        p1, got = _mm(h, w["w1"], name="proj_main_with_gather" + sfx, comm=late[0])
        w.update(late[1](got))
    gp = _mm(h, w["wg"], name="proj_gates" + sfx)
    kr = _mm(h, w["wkr"], name="proj_krope" + sfx)
    y_pool, yt_pool = _pool_fwd(p1, w["pool_w"], w["pool_scale"])
    qf, kf, vv, vt = _mla_prep_fwd(p1, kr, ctab, stab, w["q_a_norm_g"], w["kv_a_norm_g"], w["w_uq"], w["w_ukv"],
                               w["mla_q_norm_g"], w["mla_k_norm_g"])
    (y_mla, yt_mla, o_att, lse, lset), comm_out = _attn_fwd(qf, kf, vt, p1, tq=tq, comm=comm)
    y_conv, yt_conv = _conv_fwd(p1, w["conv_w"])
    memn, memnt = _norm_fwd(mem, w["mem_norm_g"], name="mem_norm" + sfx)
    mem_kv = _mm(memn, w["w_mem_kv"], name="mem_kv" + sfx)
    kx, vx = _memkv_prep(mem_kv, w["xattn_k_norm_g"])
    y_mem, yt_mem = _xattn_fwd(p1, kx, vx, w["xattn_q_norm_g"])
    ystack = jnp.stack([y_pool, y_mla, y_conv, y_mem])
    ytstack = jnp.stack([yt_pool, yt_mla, yt_conv, yt_mem])
    merged, mergedt = _gate_fwd(ystack, w["w_branch"], gp, w["gate_b"])
    x_out = _mm(merged, w["w_out"], add=x, name="out_proj" + sfx)
    saved = dict(x=x, ht=ht, p1=p1, gp=gp, kr=kr, qf=qf, kf=kf, vv=vv, o_att=o_att, lse=lse, lset=lset, memnt=memnt,
                 mem_kv=mem_kv, kx=kx, vx=vx, ystack=ystack, ytstack=ytstack, mergedt=mergedt)
    return x_out, saved, comm_out


def _backward_layer(dx_out, sv, mem, ctab, stab, w, tq, l, ql, kvl, comm=None, own=None):
    sfx = f"_l{l}"
    g = {}
    g["w_out"] = _mm(sv["mergedt"], dx_out, name="g_w_out" + sfx)
    dm = _mm(dx_out, w["w_out"], trans_b=True, name="d_merged" + sfx)
    dproj, dgp, g_gate_b = _gate_bwd(sv["ystack"], w["w_branch"], sv["gp"], w["gate_b"], dm)
    g["gate_b"] = g_gate_b[0]
    g["w_branch"] = _mm(sv["ytstack"], dproj, name="g_w_branch" + sfx)
    dy4 = _mm(dproj, w["w_branch"], trans_b=True, name="d_branches" + sfx)
    p1, kr = sv["p1"], sv["kr"]
    d_pool, g_pw, g_ps = _pool_bwd(p1, dy4, w["pool_w"], w["pool_scale"])
    g["pool_w"], g["pool_scale"] = g_pw, g_ps[0]
    do, d_mz, delta, deltat = _attn_bwd_pre(dy4, sv["o_att"], p1)
    dqf, got = _attn_bwd_dq(sv["qf"], sv["kf"], sv["vv"], do, sv["lse"], delta, tq=tq, comm=comm[0] if comm else None)
    parts, comm_dkv = comm[1](got) if comm else (None, None)
    dkf, dvv, got = _attn_bwd_dkv(sv["qf"], sv["kf"], sv["vv"], do, sv["lset"], deltat, tq=tq, comm=comm_dkv)
    comm_out = (parts, got)
    (d_c, d_kr, dq_raw, dkv_raw, cqnt, ckvnt, g_qa, g_kva, g_qg, g_kg) = _mla_prep_bwd(
        p1, kr, ctab, stab, w["q_a_norm_g"], w["kv_a_norm_g"], w["w_uq"], w["w_ukv"], w["mla_q_norm_g"], w["mla_k_norm_g"],
        dqf, dkf, dvv)
    g["q_a_norm_g"], g["kv_a_norm_g"] = g_qa[0], g_kva[0]
    g["mla_q_norm_g"], g["mla_k_norm_g"] = g_qg[0, :QK_HEAD], g_kg[0, :QK_HEAD]
    g_wuq = _mm(cqnt, dq_raw, name="g_w_uq" + sfx)
    g["w_uq"] = g_wuq.reshape(ql, MLA_HEADS, HEAD_PAD)[:, :, :QK_HEAD].reshape(ql, MLA_HEADS * QK_HEAD)
    g["w_ukv"] = _mm(ckvnt, dkv_raw, name="g_w_ukv" + sfx)
    d_conv, gc0, gc1, gc2 = _conv_bwd(p1, dy4, w["conv_w"])
    g["conv_w"] = jnp.concatenate([gc0, gc1, gc2], axis=0)
    d_x, dkx, dvx, g_xq = _xattn_bwd(p1, dy4, sv["kx"], sv["vx"], w["xattn_q_norm_g"])
    g["xattn_q_norm_g"] = g_xq[0]
    d_memkv, g_xk = _memkv_prep_bwd(sv["mem_kv"], w["xattn_k_norm_g"], dkx, dvx)
    g["xattn_k_norm_g"] = g_xk[0]
    g["w_mem_kv"] = _mm(sv["memnt"], d_memkv, name="g_w_mem_kv" + sfx)
    d_memn = _mm(d_memkv, w["w_mem_kv"], trans_b=True, name="d_memn" + sfx)
    _, g_mn = _norm_bwd(mem, w["mem_norm_g"], d_memn, d_memn, name="mem_norm_bwd" + sfx)
    g["mem_norm_g"] = g_mn[0]
    dp1 = jnp.concatenate([d_pool, d_c, d_mz, d_conv, d_x], axis=1)
    ht = sv["ht"]
    o_kr = 2 * BRANCH_W + ql + kvl
    if own is None:
        g_w1 = _mm(ht, dp1, name="g_w1" + sfx)
        g_wg = _mm(ht, dgp, name="g_wg" + sfx)
        g_wkr = _mm(ht, d_kr, name="g_wkr" + sfx)
        g["w_in"] = jnp.concatenate([g_w1[:, :o_kr], g_wkr[:, :QK_ROPE], g_w1[:, o_kr:], g_wg], axis=1)
        dh = _mm(dp1, w["w1"], trans_b=True, name="dh_main" + sfx)
        dh = _mm(dgp, w["wg"], trans_b=True, add=dh, name="dh_gates" + sfx)
        own_out = None
    else:
        rest_names = SHARDED[1:]
        gl_a = own["layout"](g, rest_names)
        g_w1, ra = _mm(ht, dp1, name="g_w1_with_to_owner" + sfx, comm=_to_owner_comm([t.astype(BF16) for t in gl_a], l))
        parts_a = own["add_owner"](gl_a, ra, rest_names)
        g_wg, r3_a = _mm(ht, dgp, name="g_wg_with_exchange" + sfx, comm=_exchange_comm([pb for _, pb in parts_a], l))
        g_wkr = _mm(ht, d_kr, name="g_wkr" + sfx)
        g["w_in"] = jnp.concatenate([g_w1[:, :o_kr], g_wkr[:, :QK_ROPE], g_w1[:, o_kr:], g_wg], axis=1)
        gl_b = own["layout"](g, SHARDED[:1])
        dh, rb = _mm(dp1, w["w1"], trans_b=True, name="dh_main_with_to_owner" + sfx,
                     comm=_to_owner_comm([t.astype(BF16) for t in gl_b], l))
        parts_b = own["add_owner"](gl_b, rb, SHARDED[:1])
        dh, r3_b = _mm(dgp, w["wg"], trans_b=True, add=dh, name="dh_gates_with_exchange" + sfx,
                       comm=_exchange_comm([pb for _, pb in parts_b], l))
        own_out = (parts_b + parts_a, r3_b + r3_a)
    dh = _mm(d_kr, w["wkr"], trans_b=True, add=dh, name="dh_krope" + sfx)
    dx, g_ng = _norm_bwd(sv["x"], w["norm_g"], dh, dx_out, name="norm_bwd" + sfx)
    g["norm_g"] = g_ng[0]
    return dx, g, comm_out, own_out


def _as4(a):
    rest = a.shape[2:]
    r = 1
    for n in rest[:-1]:
        r *= n
    return a.reshape(2, 4, r, rest[-1])


def kernel(x, mem, positions, norm_g, w_in, gate_b, pool_w, pool_scale, q_a_norm_g, kv_a_norm_g, w_uq, w_ukv, mla_q_norm_g, mla_k_norm_g, conv_w, mem_norm_g, w_mem_kv, xattn_q_norm_g, xattn_k_norm_g, w_branch, w_out, loss_target, m_norm_g, m_w_in, m_gate_b, m_pool_w, m_pool_scale, m_q_a_norm_g, m_kv_a_norm_g, m_w_uq, m_w_ukv, m_mla_q_norm_g, m_mla_k_norm_g, m_conv_w, m_mem_norm_g, m_w_mem_kv, m_xattn_q_norm_g, m_xattn_k_norm_g, m_w_branch, m_w_out, v_norm_g, v_w_in, v_gate_b, v_pool_w, v_pool_scale, v_q_a_norm_g, v_kv_a_norm_g, v_w_uq, v_w_ukv, v_mla_q_norm_g, v_mla_k_norm_g, v_conv_w, v_mem_norm_g, v_w_mem_kv, v_xattn_q_norm_g, v_xattn_k_norm_g, v_w_branch, v_w_out):
    wts = dict(norm_g=norm_g, w_in=w_in, gate_b=gate_b, pool_w=pool_w, pool_scale=pool_scale, q_a_norm_g=q_a_norm_g,
               kv_a_norm_g=kv_a_norm_g, w_uq=w_uq, w_ukv=w_ukv, mla_q_norm_g=mla_q_norm_g, mla_k_norm_g=mla_k_norm_g,
               conv_w=conv_w, mem_norm_g=mem_norm_g, w_mem_kv=w_mem_kv, xattn_q_norm_g=xattn_q_norm_g,
               xattn_k_norm_g=xattn_k_norm_g, w_branch=w_branch, w_out=w_out)
    mom = dict(norm_g=m_norm_g, w_in=m_w_in, gate_b=m_gate_b, pool_w=m_pool_w, pool_scale=m_pool_scale, q_a_norm_g=m_q_a_norm_g,
               kv_a_norm_g=m_kv_a_norm_g, w_uq=m_w_uq, w_ukv=m_w_ukv, mla_q_norm_g=m_mla_q_norm_g, mla_k_norm_g=m_mla_k_norm_g,
               conv_w=m_conv_w, mem_norm_g=m_mem_norm_g, w_mem_kv=m_w_mem_kv, xattn_q_norm_g=m_xattn_q_norm_g,
               xattn_k_norm_g=m_xattn_k_norm_g, w_branch=m_w_branch, w_out=m_w_out)
    vel = dict(norm_g=v_norm_g, w_in=v_w_in, gate_b=v_gate_b, pool_w=v_pool_w, pool_scale=v_pool_scale, q_a_norm_g=v_q_a_norm_g,
               kv_a_norm_g=v_kv_a_norm_g, w_uq=v_w_uq, w_ukv=v_w_ukv, mla_q_norm_g=v_mla_q_norm_g, mla_k_norm_g=v_mla_k_norm_g,
               conv_w=v_conv_w, mem_norm_g=v_mem_norm_g, w_mem_kv=v_w_mem_kv, xattn_q_norm_g=v_xattn_q_norm_g,
               xattn_k_norm_g=v_xattn_k_norm_g, w_branch=v_w_branch, w_out=v_w_out)
    depth = norm_g.shape[0]
    assert depth == 2 and x.shape[0] == 1
    xs, mems, tgt = x[0], mem[0], loss_target[0]
    s = xs.shape[0]
    ql, kvl = q_a_norm_g.shape[1], kv_a_norm_g.shape[1]
    tq = _tile(s, 1024, 128)
    ctab, stab = _rope_tables(positions[0])

    chip = 2 * lax.axis_index("x") + lax.axis_index("y")
    k_idx = chip.astype(jnp.int32).reshape(1)
    rep = {n: wts[n] for n in REPLICATED}
    send = [[wts[n][l].astype(F32 if n == "conv_w" else BF16) for n in SHARDED] for l in range(depth)]

    def gathered(got, l, names, first):
        return {n: (g, own, chip) for n, g, own in zip(names, got, send[l][first:first + len(names)])}

    got = _run_comm(_gather_comm(send[0][:1], 0), "weights_gather_w_in_l0")
    lw = [_layer_weights(gathered(got, 0, SHARDED[:1], 0), rep, 0, ql, kvl), None]
    late0 = (_gather_comm(send[0][1:], 0), lambda got: _other_weights(gathered(got, 0, SHARDED[1:], 1), ql))

    act, sv0, got1 = _forward_layer(xs, mems, ctab, stab, lw[0], tq, 0, comm=_gather_comm(send[1], 1), late=late0)
    lw[1] = _layer_weights(gathered(got1, 1, SHARDED, 0), rep, 1, ql, kvl)
    lw[1].update(_other_weights(gathered(got1, 1, SHARDED, 0), ql))
    act, sv1, _ = _forward_layer(act, mems, ctab, stab, lw[1], tq, 1)
    dy, loss_part = _loss_head(act, tgt)

    def shard_layout(g, names=SHARDED):
        shards = [jnp.swapaxes(_to_shards(g[n], n), -1, -2) if n == "w_in" else _to_shards(g[n], n) for n in names]
        return [t.reshape(4, -1, t.shape[-1]) for t in shards]

    def own_flag(l):
        return (lax.axis_index("c") == l).astype(jnp.int32).reshape(1)

    def add_owner(gl, ra, l, names=SHARDED):
        return [_add_owner(a, b, own_flag(l), name=f"add_owner_{n}_l{l}") for a, b, n in zip(gl, ra, names)]

    def finish_layer(parts, r3s, l, bufs):
        bufs = [_add_chips_layer(p, r3, k_idx, own_flag(l), l, None if bufs is None else bufs[i], name=f"add_chips_{n}_l{l}")
                for i, ((p, _), r3, n) in enumerate(zip(parts, r3s, SHARDED))]
        return _run_comm(_share_comm(bufs, l), f"grads_share_l{l}")

    grads = [None] * depth
    dxl, grads[1], _, _ = _backward_layer(dy, sv1, mems, ctab, stab, lw[1], tq, 1, ql, kvl)
    gl1 = shard_layout(grads[1])

    def after_dq(ra):
        parts = add_owner(gl1, ra, 1)
        return parts, _exchange_comm([pb for _, pb in parts], 1)

    own0 = dict(layout=shard_layout, add_owner=lambda gl, ra, names: add_owner(gl, ra, 0, names))
    dxl, grads[0], (parts1, r3_1), (parts0, r3_0) = _backward_layer(
        dxl, sv0, mems, ctab, stab, lw[0], tq, 0, ql, kvl,
        comm=(_to_owner_comm([g.astype(BF16) for g in gl1], 1), after_dq), own=own0)
    grad_x = dxl[None]
    bufs = finish_layer(parts1, r3_1, 1, None)
    reduced = finish_layer(parts0, r3_0, 0, bufs)
    gsum = {n: (jnp.swapaxes(r.reshape(2, wts[n].shape[2], wts[n].shape[1]), 1, 2) if n == "w_in" else r.reshape(wts[n].shape))
            for n, r in zip(SHARDED, reduced)}

    flat = [jnp.stack([grads[l][n] for l in range(depth)], axis=0).reshape(-1) for n in REPLICATED]
    sizes = [f.shape[0] for f in flat]
    total = sum(sizes) + 1
    rows = -(-total // 1024) * 8
    vec = jnp.concatenate(flat + [loss_part[0, :1], jnp.zeros((rows * 128 - total,), F32)]).reshape(rows, 128)
    red = _all_reduce_small(vec).reshape(-1)
    off = 0
    for n, sz in zip(REPLICATED, sizes):
        gsum[n] = red[off:off + sz].reshape(wts[n].shape)
        off += sz
    loss = red[off]

    delta, new_m, new_v = {}, {}, {}
    for n in WEIGHTS:
        if n == "w_in":
            tr_ = lambda a: jnp.swapaxes(a, 1, 2)
            delta[n], new_m[n], new_v[n] = (tr_(o) for o in _adamw(tr_(wts[n]), tr_(gsum[n]), tr_(mom[n]), tr_(vel[n]),
                                                                   name=f"adamw_{n}"))
        else:
            delta[n], new_m[n], new_v[n] = _adamw(wts[n], gsum[n], mom[n], vel[n], name=f"adamw_{n}")
    return (loss, grad_x, *[gsum[n] for n in WEIGHTS], *[delta[n] for n in WEIGHTS],
            *[new_m[n] for n in WEIGHTS], *[new_v[n] for n in WEIGHTS])
```

```python
import functools

import jax
import jax.numpy as jnp
from jax import lax
from jax.experimental import pallas as pl
from jax.experimental.pallas import tpu as pltpu

F32 = jnp.float32
BF16 = jnp.bfloat16
MESH = pl.DeviceIdType.MESH

EPS = 1e-6
N_BRANCH = 4
BRANCH_W = 1024
POOL_GROUPS = 4
POOL_GW = BRANCH_W // POOL_GROUPS
POOL_HALO = 16
CONV_HALO = 8
MLA_HEADS = 8
QK_NOPE = 128
QK_ROPE = 64
QK_HEAD = QK_NOPE + QK_ROPE
HEAD_PAD = 256
V_HEAD = 128
ROPE_THETA = 10000.0
XATTN_HEADS = 4
XATTN_HEAD_DIM = BRANCH_W // XATTN_HEADS
ADAM_LR, ADAM_B1, ADAM_B2, ADAM_EPS, ADAM_WD, ADAM_STEP = 0.001, 0.9, 0.999, 1e-08, 0.01, 10
NEG = -1e30
VMEM_LIMIT = 48 * 1024 * 1024

SHARDED = ("w_in", "pool_w", "w_uq", "w_ukv", "conv_w", "w_mem_kv", "w_branch", "w_out")
REPLICATED = ("norm_g", "gate_b", "pool_scale", "q_a_norm_g", "kv_a_norm_g", "mla_q_norm_g", "mla_k_norm_g",
              "mem_norm_g", "xattn_q_norm_g", "xattn_k_norm_g")
WEIGHTS = ("norm_g", "w_in", "gate_b", "pool_w", "pool_scale", "q_a_norm_g", "kv_a_norm_g", "w_uq", "w_ukv",
           "mla_q_norm_g", "mla_k_norm_g", "conv_w", "mem_norm_g", "w_mem_kv", "xattn_q_norm_g", "xattn_k_norm_g",
           "w_branch", "w_out")
SHARD_AXIS = {"w_in": 1, "pool_w": 1, "w_uq": 1, "w_ukv": 1, "conv_w": 1, "w_mem_kv": 0, "w_branch": 2, "w_out": 0}


def _cparams(sem=None):
    return pltpu.CompilerParams(dimension_semantics=sem, vmem_limit_bytes=VMEM_LIMIT)


def _tile(n, pref, unit=128):
    if n <= pref:
        return n
    t = (pref // unit) * unit
    while t >= unit:
        if n % t == 0:
            return t
        t -= unit
    return n


def _silu(z):
    return z * jax.nn.sigmoid(z)


def _dsilu(z):
    s = jax.nn.sigmoid(z)
    return s * (1.0 + z * (1.0 - s))


def _dot(a, b):
    return jnp.dot(a, b, preferred_element_type=F32)


def _dot_nt(a, b):
    return lax.dot_general(a, b, (((1,), (1,)), ((), ())), preferred_element_type=F32)


def _dot_tn(a, b):
    return lax.dot_general(a, b, (((0,), (0,)), ((), ())), preferred_element_type=F32)


def _rms(x, g, n):
    r = lax.rsqrt(jnp.sum(x * x, axis=-1, keepdims=True) * (1.0 / n) + EPS)
    return x * r * g


def _rms_bwd(x, g, n, dout):
    r = lax.rsqrt(jnp.sum(x * x, axis=-1, keepdims=True) * (1.0 / n) + EPS)
    y = x * r
    dy = dout * g
    dx = r * (dy - y * (jnp.sum(dy * y, axis=-1, keepdims=True) * (1.0 / n)))
    return dx, dout * y


def _rope(x, ctab, stab):
    lane = lax.broadcasted_iota(jnp.int32, x.shape, 1)
    partner = jnp.where(lane < QK_NOPE + QK_ROPE // 2, pltpu.roll(x, HEAD_PAD - QK_ROPE // 2, 1),
                        pltpu.roll(x, QK_ROPE // 2, 1))
    return x * ctab + partner * stab


def _rope_bwd(d, ctab, stab):
    lane = lax.broadcasted_iota(jnp.int32, d.shape, 1)
    ds = d * stab
    partner = jnp.where(lane < QK_NOPE + QK_ROPE // 2, pltpu.roll(ds, HEAD_PAD - QK_ROPE // 2, 1),
                        pltpu.roll(ds, QK_ROPE // 2, 1))
    return d * ctab + jnp.where((lane >= QK_NOPE) & (lane < QK_HEAD), partner, 0.0)


def _mm(a, b, *, name, trans_b=False, add=None, out_dtype=F32, tm=1024, tn=1024, tk=2048, comm=None):
    batched = a.ndim == 3
    if batched:
        nb, m, k = a.shape
    else:
        m, k = a.shape
    n = b.shape[-2] if trans_b else b.shape[-1]
    tm, tn, tk = _tile(m, tm, 8), _tile(n, tn), _tile(k, tk)
    nk = k // tk
    c_in, c_out, c_sem, c_shapes = _comm_extras(comm)
    nci, nco = len(c_in), len(c_out)
    assert comm is None or not batched
    n_in = 2 + (add is not None)

    def body(*refs):
        a_ref, b_ref = refs[:2]
        add_ref = refs[2] if add is not None else None
        o_ref = refs[n_in + nci]
        rest = refs[n_in + nci + 1 + nco:]
        if comm is not None:
            i, j, k3 = pl.program_id(0), pl.program_id(1), pl.program_id(2)
            ni, nj = m // tm, n // tn
            origin = (j == 0) & (k3 == 0)
            c_first, c_mid, c_last = _comm_hooks(comm, refs[n_in:n_in + nci], refs[n_in + nci + 1:n_in + nci + 1 + nco],
                                                 rest[(1 if nk > 1 else 0):], (i == 0) & origin, (i == (3 * ni) // 4) & origin,
                                                 (i == ni - 1) & (j == nj - 1) & (k3 == nk - 1))
            c_first()
        av = a_ref[...].astype(BF16)
        bv = b_ref[...].astype(BF16)
        part = _dot_nt(av, bv) if trans_b else _dot(av, bv)

        def finish(acc):
            if add_ref is not None:
                acc = acc + add_ref[...]
            o_ref[...] = acc.astype(o_ref.dtype)

        if nk == 1:
            finish(part)
        else:
            acc_ref = rest[0]
            kk = pl.program_id(3 if batched else 2)

            @pl.when(kk == 0)
            def _():
                acc_ref[...] = part

            @pl.when(kk > 0)
            def _():
                acc_ref[...] += part

            @pl.when(kk == nk - 1)
            def _():
                finish(acc_ref[...])

        if comm is not None:
            c_mid()
            c_last()

    if batched:
        a_spec = pl.BlockSpec((None, tm, tk), lambda bb, i, j, kk: (bb, i, kk))
        b_spec = (pl.BlockSpec((None, tn, tk), lambda bb, i, j, kk: (bb, j, kk)) if trans_b
                  else pl.BlockSpec((None, tk, tn), lambda bb, i, j, kk: (bb, kk, j)))
        o_spec = pl.BlockSpec((None, tm, tn), lambda bb, i, j, kk: (bb, i, j))
        grid = (nb, m // tm, n // tn, nk)
        out_shape = jax.ShapeDtypeStruct((nb, m, n), out_dtype)
        sem = ("parallel", "parallel", "parallel", "arbitrary")
    else:
        a_spec = pl.BlockSpec((tm, tk), lambda i, j, kk: (i, kk))
        b_spec = (pl.BlockSpec((tn, tk), lambda i, j, kk: (j, kk)) if trans_b
                  else pl.BlockSpec((tk, tn), lambda i, j, kk: (kk, j)))
        o_spec = pl.BlockSpec((tm, tn), lambda i, j, kk: (i, j))
        grid = (m // tm, n // tn, nk)
        out_shape = jax.ShapeDtypeStruct((m, n), out_dtype)
        sem = ("parallel", "parallel", "arbitrary")
    in_specs = [a_spec, b_spec] + ([o_spec] if add is not None else [])
    args = (a, b) + ((add,) if add is not None else ())
    scratch = [pltpu.VMEM((tm, tn), F32)] if nk > 1 else []
    if comm is None:
        return pl.pallas_call(body, out_shape=out_shape, grid=grid, in_specs=in_specs, out_specs=o_spec,
                              scratch_shapes=scratch, compiler_params=_cparams(sem), name=name)(*args)
    res = pl.pallas_call(body, out_shape=(out_shape,) + c_shapes, grid=grid, in_specs=in_specs + c_in,
                         out_specs=(o_spec,) + tuple(c_out), scratch_shapes=scratch + c_sem,
                         compiler_params=_cparams(("arbitrary",) * 3), name=name)(*args, *comm["ins"])
    return res[0], list(res[1:])


def _norm_fwd(x, g, *, name, t=256):
    s, d = x.shape
    t = _tile(s, t, 128)

    def body(x_ref, g_ref, h_ref, ht_ref):
        h = _rms(x_ref[...], g_ref[...], d)
        h_ref[...] = h.astype(BF16)
        ht_ref[...] = h.T.astype(BF16)

    return pl.pallas_call(
        body, out_shape=(jax.ShapeDtypeStruct((s, d), BF16), jax.ShapeDtypeStruct((d, s), BF16)),
        grid=(s // t,),
        in_specs=[pl.BlockSpec((t, d), lambda i: (i, 0)), pl.BlockSpec((1, d), lambda i: (0, 0))],
        out_specs=(pl.BlockSpec((t, d), lambda i: (i, 0)), pl.BlockSpec((d, t), lambda i: (0, i))),
        compiler_params=_cparams(("parallel",)), name=name)(x, g)


def _norm_bwd(x, g, dh, dres, *, name, t=256):
    s, d = x.shape
    t = _tile(s, t, 8)

    def body(x_ref, g_ref, dh_ref, dres_ref, dx_ref, dg_ref):
        dx, dgt = _rms_bwd(x_ref[...], g_ref[...], d, dh_ref[...])
        dx_ref[...] = dx + dres_ref[...]
        part = jnp.sum(dgt, axis=0, keepdims=True)

        @pl.when(pl.program_id(0) == 0)
        def _():
            dg_ref[...] = part

        @pl.when(pl.program_id(0) > 0)
        def _():
            dg_ref[...] += part

    row = pl.BlockSpec((t, d), lambda i: (i, 0))
    vec = pl.BlockSpec((1, d), lambda i: (0, 0))
    return pl.pallas_call(
        body, out_shape=(jax.ShapeDtypeStruct((s, d), F32), jax.ShapeDtypeStruct((1, d), F32)),
        grid=(s // t,), in_specs=[row, vec, row, row], out_specs=(row, vec),
        compiler_params=_cparams(("arbitrary",)), name=name)(x, g, dh, dres)


def _loss_head(y, tgt, *, t=256):
    s, d = y.shape
    t = _tile(s, t, 8)

    def body(y_ref, t_ref, dy_ref, l_ref):
        e = y_ref[...] - t_ref[...]
        dy_ref[...] = e * (1.0 / d)
        part = jnp.zeros((1, 128), F32) + jnp.sum(e * e) * (0.5 / d)

        @pl.when(pl.program_id(0) == 0)
        def _():
            l_ref[...] = part

        @pl.when(pl.program_id(0) > 0)
        def _():
            l_ref[...] += part

    row = pl.BlockSpec((t, d), lambda i: (i, 0))
    return pl.pallas_call(
        body, out_shape=(jax.ShapeDtypeStruct((s, d), F32), jax.ShapeDtypeStruct((1, 128), F32)),
        grid=(s // t,), in_specs=[row, row], out_specs=(row, pl.BlockSpec((1, 128), lambda i: (0, 0))),
        compiler_params=_cparams(("arbitrary",)), name="loss_head")(y, tgt)


def _pool_mixed(scr, v, halo, first, row0, t):
    scr[0:POOL_HALO, :] = jnp.where(first, 0.0, halo)
    scr[POOL_HALO:POOL_HALO + t, :] = v
    row = row0 + lax.broadcasted_iota(jnp.int32, (t, 1), 0)
    mixed = []
    for g in range(POOL_GROUPS):
        w = 2 ** (g + 1)
        acc = scr[:, g * POOL_GW:(g + 1) * POOL_GW]
        sh = 1
        while sh < w:
            acc = acc + pltpu.roll(acc, sh, 0)
            sh *= 2
        cnt = jnp.minimum(row + 1, w).astype(F32)
        mixed.append(acc[POOL_HALO:POOL_HALO + t, :] / cnt - v[:, g * POOL_GW:(g + 1) * POOL_GW])
    return mixed


def _pool_fwd(p1, pool_w, pool_scale, *, t=256):
    s = p1.shape[0]
    t = _tile(s, t, 128)
    hb = t // POOL_HALO

    def body(pv_ref, halo_ref, pz_ref, pw_ref, sc_ref, y_ref, yt_ref, scr):
        i = pl.program_id(0)
        mixed = _pool_mixed(scr, pv_ref[...], halo_ref[...], i == 0, i * t, t)
        outs = [_dot(mixed[g].astype(BF16), pw_ref[g]) for g in range(POOL_GROUPS)]
        y = jnp.concatenate(outs, axis=1) * sc_ref[...] * _silu(pz_ref[...])
        y_ref[...] = y.astype(BF16)
        yt_ref[...] = y.T.astype(BF16)

    return pl.pallas_call(
        body, out_shape=(jax.ShapeDtypeStruct((s, BRANCH_W), BF16), jax.ShapeDtypeStruct((BRANCH_W, s), BF16)),
        grid=(s // t,),
        in_specs=[pl.BlockSpec((t, BRANCH_W), lambda i: (i, 0)),
                  pl.BlockSpec((POOL_HALO, BRANCH_W), lambda i: (jnp.maximum(i * hb - 1, 0), 0)),
                  pl.BlockSpec((t, BRANCH_W), lambda i: (i, 1)),
                  pl.BlockSpec((POOL_GROUPS, POOL_GW, POOL_GW), lambda i: (0, 0, 0)),
                  pl.BlockSpec((1, BRANCH_W), lambda i: (0, 0))],
        out_specs=(pl.BlockSpec((t, BRANCH_W), lambda i: (i, 0)), pl.BlockSpec((BRANCH_W, t), lambda i: (0, i))),
        scratch_shapes=[pltpu.VMEM((t + POOL_HALO, BRANCH_W), F32)],
        compiler_params=_cparams(("parallel",)), name="pool_fwd")(p1, p1, p1, pool_w, pool_scale)


def _pool_bwd(p1, dy, pool_w, pool_scale, *, t=256):
    s = p1.shape[0]
    t = _tile(s, t, 128)
    hb = t // POOL_HALO
    nt = s // t
    last_hb = s // POOL_HALO - 1

    def body(pv_ref, halo_ref, pz_ref, pzn_ref, dy_ref, dyn_ref, pw_ref, sc_ref, d_ref, gw_ref, gs_ref, scr, scr2, scr3):
        i = pl.program_id(0)
        mixed = _pool_mixed(scr, pv_ref[...], halo_ref[...], i == 0, i * t, t)
        scale = sc_ref[...]
        pz = pz_ref[...]
        dy = dy_ref[...]
        raw = jnp.concatenate([_dot(mixed[g].astype(BF16), pw_ref[g]) for g in range(POOL_GROUPS)], axis=1)
        d_pool = dy * _silu(pz)
        d_ref[:, BRANCH_W:2 * BRANCH_W] = (dy * raw * scale * _dsilu(pz)).astype(BF16)
        gs_part = jnp.sum(d_pool * raw, axis=0, keepdims=True)
        scr2[0:t, :] = d_pool * scale
        scr2[t:t + POOL_HALO, :] = jnp.where(i == nt - 1, 0.0, dyn_ref[...] * _silu(pzn_ref[...]) * scale)
        row = i * t + lax.broadcasted_iota(jnp.int32, (t + POOL_HALO, 1), 0)
        gw_parts = []
        for g in range(POOL_GROUPS):
            w = 2 ** (g + 1)
            sl = slice(g * POOL_GW, (g + 1) * POOL_GW)
            do_g = scr2[:, sl].astype(BF16)
            dm = _dot_nt(do_g, pw_ref[g])
            gw_parts.append(_dot_tn(mixed[g].astype(BF16), do_g[0:t, :]))
            cnt = jnp.minimum(row + 1, w).astype(F32)
            acc = dm / cnt
            sh = 1
            while sh < w:
                acc = acc + pltpu.roll(acc, t + POOL_HALO - sh, 0)
                sh *= 2
            scr3[:, sl] = acc - dm
        d_ref[:, 0:BRANCH_W] = scr3[0:t, :].astype(BF16)

        @pl.when(i == 0)
        def _():
            for g in range(POOL_GROUPS):
                gw_ref[g] = gw_parts[g]
            gs_ref[...] = gs_part

        @pl.when(i > 0)
        def _():
            for g in range(POOL_GROUPS):
                gw_ref[g] += gw_parts[g]
            gs_ref[...] += gs_part

    tile = lambda col: pl.BlockSpec((t, BRANCH_W), lambda i: (i, col))
    nxt = lambda col: pl.BlockSpec((POOL_HALO, BRANCH_W), lambda i: (jnp.minimum((i + 1) * hb, last_hb), col))
    return pl.pallas_call(
        body,
        out_shape=(jax.ShapeDtypeStruct((s, 2 * BRANCH_W), BF16),
                   jax.ShapeDtypeStruct((POOL_GROUPS, POOL_GW, POOL_GW), F32),
                   jax.ShapeDtypeStruct((1, BRANCH_W), F32)),
        grid=(nt,),
        in_specs=[tile(0), pl.BlockSpec((POOL_HALO, BRANCH_W), lambda i: (jnp.maximum(i * hb - 1, 0), 0)),
                  tile(1), nxt(1),
                  pl.BlockSpec((None, t, BRANCH_W), lambda i: (0, i, 0)),
                  pl.BlockSpec((None, POOL_HALO, BRANCH_W), lambda i: (0, jnp.minimum((i + 1) * hb, last_hb), 0)),
                  pl.BlockSpec((POOL_GROUPS, POOL_GW, POOL_GW), lambda i: (0, 0, 0)),
                  pl.BlockSpec((1, BRANCH_W), lambda i: (0, 0))],
        out_specs=(pl.BlockSpec((t, 2 * BRANCH_W), lambda i: (i, 0)),
                   pl.BlockSpec((POOL_GROUPS, POOL_GW, POOL_GW), lambda i: (0, 0, 0)),
                   pl.BlockSpec((1, BRANCH_W), lambda i: (0, 0))),
        scratch_shapes=[pltpu.VMEM((t + POOL_HALO, BRANCH_W), F32)] * 3,
        compiler_params=_cparams(("arbitrary",)), name="pool_bwd")(p1, p1, p1, p1, dy, dy, pool_w, pool_scale)


CONV_COL = 4


def _conv_taps(scr, u, uh, first, t):
    scr[0:CONV_HALO, :] = jnp.where(first, 0.0, uh)
    scr[CONV_HALO:CONV_HALO + t, :] = u
    e = scr[...]
    u1 = pltpu.roll(e, 1, 0)[CONV_HALO:CONV_HALO + t, :]
    u2 = pltpu.roll(e, 2, 0)[CONV_HALO:CONV_HALO + t, :]
    return u2, u1, u


def _conv_fwd(p1, conv_w, *, t=256):
    s = p1.shape[0]
    t = _tile(s, t, 128)
    hb = t // CONV_HALO

    def body(cb_ref, cc_ref, cx_ref, cz_ref, cch_ref, cxh_ref, w_ref, y_ref, yt_ref, scr):
        i = pl.program_id(0)
        u0, u1, u2 = _conv_taps(scr, cc_ref[...] * cx_ref[...], cch_ref[...] * cxh_ref[...], i == 0, t)
        w = w_ref[...]
        y = (w[0:1, :] * u0 + w[1:2, :] * u1 + w[2:3, :] * u2) * cb_ref[...] * _silu(cz_ref[...])
        y_ref[...] = y.astype(BF16)
        yt_ref[...] = y.T.astype(BF16)

    tile = lambda col: pl.BlockSpec((t, BRANCH_W), lambda i: (i, CONV_COL + col))
    prev = lambda col: pl.BlockSpec((CONV_HALO, BRANCH_W), lambda i: (jnp.maximum(i * hb - 1, 0), CONV_COL + col))
    return pl.pallas_call(
        body, out_shape=(jax.ShapeDtypeStruct((s, BRANCH_W), BF16), jax.ShapeDtypeStruct((BRANCH_W, s), BF16)),
        grid=(s // t,),
        in_specs=[tile(0), tile(1), tile(2), tile(3), prev(1), prev(2), pl.BlockSpec((3, BRANCH_W), lambda i: (0, 0))],
        out_specs=(pl.BlockSpec((t, BRANCH_W), lambda i: (i, 0)), pl.BlockSpec((BRANCH_W, t), lambda i: (0, i))),
        scratch_shapes=[pltpu.VMEM((t + CONV_HALO, BRANCH_W), F32)],
        compiler_params=_cparams(("parallel",)), name="conv_fwd")(p1, p1, p1, p1, p1, p1, conv_w)


def _conv_bwd(p1, dy, conv_w, *, t=256):
    s = p1.shape[0]
    t = _tile(s, t, 128)
    hb = t // CONV_HALO
    nt = s // t
    last_hb = s // CONV_HALO - 1

    def body(cb_ref, cc_ref, cx_ref, cz_ref, cch_ref, cxh_ref, cbn_ref, czn_ref, dy_ref, dyn_ref, w_ref,
             d_ref, g0_ref, g1_ref, g2_ref, scr, scr2):
        i = pl.program_id(0)
        cb, cc, cx, cz = cb_ref[...], cc_ref[...], cx_ref[...], cz_ref[...]
        u0, u1, u2 = _conv_taps(scr, cc * cx, cch_ref[...] * cxh_ref[...], i == 0, t)
        w = w_ref[...]
        w0, w1, w2 = w[0:1, :], w[1:2, :], w[2:3, :]
        y = w0 * u0 + w1 * u1 + w2 * u2
        dy = dy_ref[...]
        sz = _silu(cz)
        d_ref[:, 0:BRANCH_W] = (dy * sz * y).astype(BF16)
        d_ref[:, 3 * BRANCH_W:4 * BRANCH_W] = (dy * cb * y * _dsilu(cz)).astype(BF16)
        d_y = dy * sz * cb
        parts = [jnp.sum(d_y * u, axis=0, keepdims=True) for u in (u0, u1, u2)]
        scr2[0:t, :] = d_y
        scr2[t:t + CONV_HALO, :] = jnp.where(i == nt - 1, 0.0, dyn_ref[...] * _silu(czn_ref[...]) * cbn_ref[...])
        e = scr2[...]
        n = t + CONV_HALO
        du = (w2 * e + w1 * pltpu.roll(e, n - 1, 0) + w0 * pltpu.roll(e, n - 2, 0))[0:t, :]
        d_ref[:, BRANCH_W:2 * BRANCH_W] = (du * cx).astype(BF16)
        d_ref[:, 2 * BRANCH_W:3 * BRANCH_W] = (du * cc).astype(BF16)

        @pl.when(i == 0)
        def _():
            g0_ref[...] = parts[0]
            g1_ref[...] = parts[1]
            g2_ref[...] = parts[2]

        @pl.when(i > 0)
        def _():
            g0_ref[...] += parts[0]
            g1_ref[...] += parts[1]
            g2_ref[...] += parts[2]

    tile = lambda col: pl.BlockSpec((t, BRANCH_W), lambda i: (i, CONV_COL + col))
    prev = lambda col: pl.BlockSpec((CONV_HALO, BRANCH_W), lambda i: (jnp.maximum(i * hb - 1, 0), CONV_COL + col))
    nxt = lambda col: pl.BlockSpec((CONV_HALO, BRANCH_W), lambda i: (jnp.minimum((i + 1) * hb, last_hb), CONV_COL + col))
    vec = pl.BlockSpec((1, BRANCH_W), lambda i: (0, 0))
    gshape = jax.ShapeDtypeStruct((1, BRANCH_W), F32)
    return pl.pallas_call(
        body, out_shape=(jax.ShapeDtypeStruct((s, 4 * BRANCH_W), BF16), gshape, gshape, gshape),
        grid=(nt,),
        in_specs=[tile(0), tile(1), tile(2), tile(3), prev(1), prev(2), nxt(0), nxt(3),
                  pl.BlockSpec((None, t, BRANCH_W), lambda i: (2, i, 0)),
                  pl.BlockSpec((None, CONV_HALO, BRANCH_W), lambda i: (2, jnp.minimum((i + 1) * hb, last_hb), 0)),
                  pl.BlockSpec((3, BRANCH_W), lambda i: (0, 0))],
        out_specs=(pl.BlockSpec((t, 4 * BRANCH_W), lambda i: (i, 0)), vec, vec, vec),
        scratch_shapes=[pltpu.VMEM((t + CONV_HALO, BRANCH_W), F32)] * 2,
        compiler_params=_cparams(("arbitrary",)), name="conv_bwd")(p1, p1, p1, p1, p1, p1, p1, p1, dy, dy, conv_w)


def _mla_prep_fwd(p1, kr, ctab, stab, qa_g, kva_g, w_uq, w_ukv, q_g, k_g, *, t=256):
    s = p1.shape[0]
    t = _tile(s, t, 128)
    ql, kvl = qa_g.shape[1], kva_g.shape[1]
    assert ql == kvl and 2048 % ql == 0
    cq_blk = 2048 // ql

    def body(cq_ref, ckv_ref, kr_ref, c_ref, s_ref, qag_ref, kvag_ref, wuq_ref, wukv_ref, qg_ref, kg_ref,
             qf_ref, kf_ref, v_ref, vt_ref):
        q_raw = _dot(_rms(cq_ref[...], qag_ref[...], ql).astype(BF16), wuq_ref[...])
        kv_raw = _dot(_rms(ckv_ref[...], kvag_ref[...], kvl).astype(BF16), wukv_ref[...])
        krp = kr_ref[...]
        ct, st = c_ref[...], s_ref[...]
        for h in range(MLA_HEADS):
            qh = q_raw[:, h * HEAD_PAD:(h + 1) * HEAD_PAD]
            qf_ref[h] = _rope(_rms(qh, qg_ref[...], QK_HEAD), ct, st).astype(BF16)
            kh = jnp.concatenate([kv_raw[:, h * HEAD_PAD:h * HEAD_PAD + QK_NOPE], krp], axis=1)
            kf_ref[h] = _rope(_rms(kh, kg_ref[...], QK_HEAD), ct, st).astype(BF16)
            vh = kv_raw[:, h * HEAD_PAD + QK_NOPE:(h + 1) * HEAD_PAD]
            v_ref[h] = vh.astype(BF16)
            vt_ref[h] = vh.T.astype(BF16)

    full = lambda shp: pl.BlockSpec(shp, lambda i: tuple(0 for _ in shp))
    return pl.pallas_call(
        body,
        out_shape=(jax.ShapeDtypeStruct((MLA_HEADS, s, HEAD_PAD), BF16), jax.ShapeDtypeStruct((MLA_HEADS, s, HEAD_PAD), BF16),
                   jax.ShapeDtypeStruct((MLA_HEADS, s, V_HEAD), BF16), jax.ShapeDtypeStruct((MLA_HEADS, V_HEAD, s), BF16)),
        grid=(s // t,),
        in_specs=[pl.BlockSpec((t, ql), lambda i: (i, cq_blk)), pl.BlockSpec((t, kvl), lambda i: (i, cq_blk + 1)),
                  pl.BlockSpec((t, 128), lambda i: (i, 0)),
                  pl.BlockSpec((t, HEAD_PAD), lambda i: (i, 0)), pl.BlockSpec((t, HEAD_PAD), lambda i: (i, 0)),
                  full((1, ql)), full((1, kvl)), full(w_uq.shape), full(w_ukv.shape), full((1, HEAD_PAD)), full((1, HEAD_PAD))],
        out_specs=(pl.BlockSpec((MLA_HEADS, t, HEAD_PAD), lambda i: (0, i, 0)),
                   pl.BlockSpec((MLA_HEADS, t, HEAD_PAD), lambda i: (0, i, 0)),
                   pl.BlockSpec((MLA_HEADS, t, V_HEAD), lambda i: (0, i, 0)),
                   pl.BlockSpec((MLA_HEADS, V_HEAD, t), lambda i: (0, 0, i))),
        compiler_params=_cparams(("parallel",)), name="mla_prep_fwd")(
            p1, p1, kr, ctab, stab, qa_g, kva_g, w_uq, w_ukv, q_g, k_g)


def _mla_prep_bwd(p1, kr, ctab, stab, qa_g, kva_g, w_uq, w_ukv, q_g, k_g, dqf, dkf, dv, *, t=256):
    s = p1.shape[0]
    t = _tile(s, t, 128)
    ql, kvl = qa_g.shape[1], kva_g.shape[1]
    cq_blk = 2048 // ql
    nq = MLA_HEADS * HEAD_PAD

    def body(cq_ref, ckv_ref, kr_ref, c_ref, s_ref, qag_ref, kvag_ref, wuq_ref, wukv_ref, qg_ref, kg_ref,
             dqf_ref, dkf_ref, dv_ref,
             dc_ref, dkr_ref, dqraw_ref, dkvraw_ref, cqnt_ref, ckvnt_ref, gqa_ref, gkva_ref, gqg_ref, gkg_ref):
        i = pl.program_id(0)
        cq, ckv = cq_ref[...], ckv_ref[...]
        cqn = _rms(cq, qag_ref[...], ql)
        ckvn = _rms(ckv, kvag_ref[...], kvl)
        cqnt_ref[...] = cqn.T.astype(BF16)
        ckvnt_ref[...] = ckvn.T.astype(BF16)
        q_raw = _dot(cqn.astype(BF16), wuq_ref[...])
        kv_raw = _dot(ckvn.astype(BF16), wukv_ref[...])
        krp = kr_ref[...]
        ct, st = c_ref[...], s_ref[...]
        gqg = jnp.zeros((1, HEAD_PAD), F32)
        gkg = jnp.zeros((1, HEAD_PAD), F32)
        dkr = jnp.zeros((t, HEAD_PAD - QK_NOPE), F32)
        dq_parts, dkv_parts = [], []
        for h in range(MLA_HEADS):
            qh = q_raw[:, h * HEAD_PAD:(h + 1) * HEAD_PAD]
            dx, dg = _rms_bwd(qh, qg_ref[...], QK_HEAD, _rope_bwd(dqf_ref[h], ct, st))
            gqg = gqg + jnp.sum(dg, axis=0, keepdims=True)
            dq_parts.append(dx)
            kh = jnp.concatenate([kv_raw[:, h * HEAD_PAD:h * HEAD_PAD + QK_NOPE], krp], axis=1)
            dx, dg = _rms_bwd(kh, kg_ref[...], QK_HEAD, _rope_bwd(dkf_ref[h], ct, st))
            gkg = gkg + jnp.sum(dg, axis=0, keepdims=True)
            dkr = dkr + dx[:, QK_NOPE:HEAD_PAD]
            dkv_parts += [dx[:, 0:QK_NOPE], dv_ref[h]]
        dq_raw = jnp.concatenate(dq_parts, axis=1).astype(BF16)
        dkv_raw = jnp.concatenate(dkv_parts, axis=1).astype(BF16)
        dqraw_ref[...] = dq_raw
        dkvraw_ref[...] = dkv_raw
        dkr_ref[...] = dkr.astype(BF16)
        dcq, gqa = _rms_bwd(cq, qag_ref[...], ql, _dot_nt(dq_raw, wuq_ref[...]))
        dckv, gkva = _rms_bwd(ckv, kvag_ref[...], kvl, _dot_nt(dkv_raw, wukv_ref[...]))
        dc_ref[:, 0:ql] = dcq.astype(BF16)
        dc_ref[:, ql:ql + kvl] = dckv.astype(BF16)
        gqa = jnp.sum(gqa, axis=0, keepdims=True)
        gkva = jnp.sum(gkva, axis=0, keepdims=True)

        @pl.when(i == 0)
        def _():
            gqa_ref[...] = gqa
            gkva_ref[...] = gkva
            gqg_ref[...] = gqg
            gkg_ref[...] = gkg

        @pl.when(i > 0)
        def _():
            gqa_ref[...] += gqa
            gkva_ref[...] += gkva
            gqg_ref[...] += gqg
            gkg_ref[...] += gkg

    full = lambda shp: pl.BlockSpec(shp, lambda i: tuple(0 for _ in shp))
    hblk = lambda w: pl.BlockSpec((MLA_HEADS, t, w), lambda i: (0, i, 0))
    sds = jax.ShapeDtypeStruct
    return pl.pallas_call(
        body,
        out_shape=(sds((s, ql + kvl), BF16), sds((s, 128), BF16), sds((s, nq), BF16), sds((s, nq), BF16),
                   sds((ql, s), BF16), sds((kvl, s), BF16),
                   sds((1, ql), F32), sds((1, kvl), F32), sds((1, HEAD_PAD), F32), sds((1, HEAD_PAD), F32)),
        grid=(s // t,),
        in_specs=[pl.BlockSpec((t, ql), lambda i: (i, cq_blk)), pl.BlockSpec((t, kvl), lambda i: (i, cq_blk + 1)),
                  pl.BlockSpec((t, 128), lambda i: (i, 0)),
                  pl.BlockSpec((t, HEAD_PAD), lambda i: (i, 0)), pl.BlockSpec((t, HEAD_PAD), lambda i: (i, 0)),
                  full((1, ql)), full((1, kvl)), full(w_uq.shape), full(w_ukv.shape), full((1, HEAD_PAD)), full((1, HEAD_PAD)),
                  hblk(HEAD_PAD), hblk(HEAD_PAD), hblk(V_HEAD)],
        out_specs=(pl.BlockSpec((t, ql + kvl), lambda i: (i, 0)), pl.BlockSpec((t, 128), lambda i: (i, 0)),
                   pl.BlockSpec((t, nq), lambda i: (i, 0)), pl.BlockSpec((t, nq), lambda i: (i, 0)),
                   pl.BlockSpec((ql, t), lambda i: (0, i)), pl.BlockSpec((kvl, t), lambda i: (0, i)),
                   full((1, ql)), full((1, kvl)), full((1, HEAD_PAD)), full((1, HEAD_PAD))),
        compiler_params=_cparams(("arbitrary",)), name="mla_prep_bwd")(
            p1, p1, kr, ctab, stab, qa_g, kva_g, w_uq, w_ukv, q_g, k_g, dqf, dkf, dv)


MZ_BLK128 = 3072 // 128
ATT_SCALE = QK_HEAD ** -0.5


HPS = 2


def _causal_pairs(nq, by_query):
    if by_query:
        prs = [(qi, ki) for qi in range(nq) for ki in range(qi + 1)]
    else:
        prs = [(qi, ki) for ki in range(nq) for qi in range(ki, nq)]
    return (jnp.asarray([p[0] for p in prs], jnp.int32), jnp.asarray([p[1] for p in prs], jnp.int32), len(prs))


def _comm_hooks(comm, cins, couts, csems, first, middle, last):
    ph = comm["phases"]
    assert len(ph) in (2, 3)

    def at(pred, phase):
        @pl.when(pred)
        def _():
            phase(cins, couts, *csems)

    return (lambda: at(first, ph[0])), (lambda: [at(middle, ph[1])] if len(ph) == 3 else None), (lambda: at(last, ph[-1]))


def _comm_extras(comm):
    if comm is None:
        return [], [], [], ()
    sems = [pltpu.SemaphoreType.DMA((comm["nsem"],)), pltpu.SemaphoreType.DMA((comm["nsem"],))]
    return [ANY] * len(comm["ins"]), [ANY] * len(comm["outs"]), sems, tuple(comm["outs"])


def _attn_fwd(qf, kf, vt, p1, *, tq, comm=None):
    nh, s, _ = qf.shape
    nq = s // tq
    qtab, ktab, npairs = _causal_pairs(nq, True)
    wv = HPS * V_HEAD
    c_in, c_out, c_sem, c_shapes = _comm_extras(comm)
    nci, nco = len(c_in), len(c_out)

    def body(*refs):
        qt_ref, kt_ref, q_ref, k_ref, vt_ref, mz_ref = refs[:6]
        y_ref, yt_ref, o_ref, lse_ref, lset_ref = refs[6 + nci:11 + nci]
        m_scr, l_scr, acc_scr = refs[11 + nci + nco:14 + nci + nco]
        pr = pl.program_id(1)
        qi, ki = qt_ref[pr], kt_ref[pr]
        if comm is not None:
            hg = pl.program_id(0)
            last_hg = nh // HPS - 1
            c_first, c_mid, c_last = _comm_hooks(comm, refs[6:6 + nci], refs[11 + nci:11 + nci + nco], refs[14 + nci + nco:],
                                                 (hg == 0) & (pr == 0), (hg == last_hg) & (pr == 0),
                                                 (hg == last_hg) & (pr == npairs - 1))
            c_first()

        @pl.when(ki == 0)
        def _():
            m_scr[...] = jnp.full((HPS, 1, tq), NEG, F32)
            l_scr[...] = jnp.zeros((HPS, 1, tq), F32)
            acc_scr[...] = jnp.zeros((HPS, V_HEAD, tq), F32)

        def step(diagonal):
            for u in range(HPS):
                st = _dot_nt(k_ref[u], q_ref[u]) * ATT_SCALE
                if diagonal:
                    r = lax.broadcasted_iota(jnp.int32, (tq, tq), 0)
                    c = lax.broadcasted_iota(jnp.int32, (tq, tq), 1)
                    st = jnp.where(r <= c, st, NEG)
                m_old = m_scr[u]
                m_new = jnp.maximum(m_old, jnp.max(st, axis=0, keepdims=True))
                alpha = jnp.exp(m_old - m_new)
                pt = jnp.exp(st - m_new)
                l_scr[u] = alpha * l_scr[u] + jnp.sum(pt, axis=0, keepdims=True)
                acc_scr[u] = alpha * acc_scr[u] + _dot(vt_ref[u], pt.astype(BF16))
                m_scr[u] = m_new

        @pl.when(ki < qi)
        def _():
            step(False)

        @pl.when(ki == qi)
        def _():
            step(True)
            outs = []
            for u in range(HPS):
                l = l_scr[u]
                outs.append((acc_scr[u] / l).T)
                lset = m_scr[u] + jnp.log(l)
                lset_ref[u] = lset
                lse_ref[u] = jnp.broadcast_to(lset, (128, tq)).T[:, 0:1]
            o = jnp.concatenate(outs, axis=1)
            o_ref[...] = o
            y = o * _silu(mz_ref[...])
            y_ref[...] = y.astype(BF16)
            yt_ref[...] = y.T.astype(BF16)

        if comm is not None:
            c_mid()
            c_last()

    sds = jax.ShapeDtypeStruct
    gs = pltpu.PrefetchScalarGridSpec(
        num_scalar_prefetch=2, grid=(nh // HPS, npairs),
        in_specs=[pl.BlockSpec((HPS, tq, HEAD_PAD), lambda h, p, qt, kt: (h, qt[p], 0)),
                  pl.BlockSpec((HPS, tq, HEAD_PAD), lambda h, p, qt, kt: (h, kt[p], 0)),
                  pl.BlockSpec((HPS, V_HEAD, tq), lambda h, p, qt, kt: (h, 0, kt[p])),
                  pl.BlockSpec((tq, wv), lambda h, p, qt, kt: (qt[p], MZ_BLK128 // HPS + h))] + c_in,
        out_specs=(pl.BlockSpec((tq, wv), lambda h, p, qt, kt: (qt[p], h)),
                   pl.BlockSpec((wv, tq), lambda h, p, qt, kt: (h, qt[p])),
                   pl.BlockSpec((tq, wv), lambda h, p, qt, kt: (qt[p], h)),
                   pl.BlockSpec((HPS, tq, 1), lambda h, p, qt, kt: (h, qt[p], 0)),
                   pl.BlockSpec((HPS, 1, tq), lambda h, p, qt, kt: (h, 0, qt[p]))) + tuple(c_out),
        scratch_shapes=[pltpu.VMEM((HPS, 1, tq), F32), pltpu.VMEM((HPS, 1, tq), F32), pltpu.VMEM((HPS, V_HEAD, tq), F32)] + c_sem)
    sem = ("parallel", "arbitrary") if comm is None else ("arbitrary", "arbitrary")
    res = pl.pallas_call(
        body,
        out_shape=(sds((s, BRANCH_W), BF16), sds((BRANCH_W, s), BF16), sds((s, BRANCH_W), F32),
                   sds((nh, s, 1), F32), sds((nh, 1, s), F32)) + c_shapes,
        grid_spec=gs, compiler_params=_cparams(sem),
        name="attn_fwd" if comm is None else "attn_fwd_with_gather")(qtab, ktab, qf, kf, vt, p1, *(comm["ins"] if comm else ()))
    return res[:5], list(res[5:])


def _attn_bwd_pre(dy4, o, p1, *, t=256):
    s = o.shape[0]
    t = _tile(s, t, 128)

    def body(dy_ref, o_ref, mz_ref, do_ref, dmz_ref, dl_ref, dlt_ref):
        dy, o_, mz = dy_ref[...], o_ref[...], mz_ref[...]
        do = dy * _silu(mz)
        do_ref[...] = do.astype(BF16)
        dmz_ref[...] = (dy * o_ * _dsilu(mz)).astype(BF16)
        prod = do * o_
        for h in range(MLA_HEADS):
            dl = jnp.sum(prod[:, h * V_HEAD:(h + 1) * V_HEAD], axis=-1, keepdims=True)
            dl_ref[h] = dl
            dlt_ref[h] = jnp.broadcast_to(dl, (t, 128)).T[0:1, :]

    sds = jax.ShapeDtypeStruct
    return pl.pallas_call(
        body,
        out_shape=(sds((s, BRANCH_W), BF16), sds((s, BRANCH_W), BF16), sds((MLA_HEADS, s, 1), F32), sds((MLA_HEADS, 1, s), F32)),
        grid=(s // t,),
        in_specs=[pl.BlockSpec((None, t, BRANCH_W), lambda i: (1, i, 0)), pl.BlockSpec((t, BRANCH_W), lambda i: (i, 0)),
                  pl.BlockSpec((t, BRANCH_W), lambda i: (i, 3))],
        out_specs=(pl.BlockSpec((t, BRANCH_W), lambda i: (i, 0)), pl.BlockSpec((t, BRANCH_W), lambda i: (i, 0)),
                   pl.BlockSpec((MLA_HEADS, t, 1), lambda i: (0, i, 0)), pl.BlockSpec((MLA_HEADS, 1, t), lambda i: (0, 0, i))),
        compiler_params=_cparams(("parallel",)), name="attn_bwd_pre")(dy4, o, p1)


def _attn_bwd_dq(qf, kf, vv, do, lse, delta, *, tq, comm=None):
    nh, s, _ = qf.shape
    nq = s // tq
    qtab, ktab, npairs = _causal_pairs(nq, True)
    wv = HPS * V_HEAD
    c_in, c_out, c_sem, c_shapes = _comm_extras(comm)
    nci, nco = len(c_in), len(c_out)

    def body(*refs):
        qt_ref, kt_ref, q_ref, k_ref, v_ref, do_ref, lse_ref, dl_ref = refs[:8]
        dq_ref = refs[8 + nci]
        acc_scr = refs[9 + nci + nco]
        pr = pl.program_id(1)
        qi, ki = qt_ref[pr], kt_ref[pr]
        if comm is not None:
            hg = pl.program_id(0)
            last_hg = nh // HPS - 1
            c_first, c_mid, c_last = _comm_hooks(comm, refs[8:8 + nci], refs[9 + nci:9 + nci + nco], refs[10 + nci + nco:],
                                                 (hg == 0) & (pr == 0), (hg == last_hg) & (pr == 0),
                                                 (hg == last_hg) & (pr == npairs - 1))
            c_first()

        @pl.when(ki == 0)
        def _():
            acc_scr[...] = jnp.zeros((HPS, tq, HEAD_PAD), F32)

        def step(diagonal):
            for u in range(HPS):
                k = k_ref[u]
                sc = _dot_nt(q_ref[u], k) * ATT_SCALE
                p = jnp.exp(sc - lse_ref[u])
                if diagonal:
                    r = lax.broadcasted_iota(jnp.int32, (tq, tq), 0)
                    c = lax.broadcasted_iota(jnp.int32, (tq, tq), 1)
                    p = jnp.where(c <= r, p, 0.0)
                dp = _dot_nt(do_ref[:, u * V_HEAD:(u + 1) * V_HEAD], v_ref[u])
                ds = p * (dp - dl_ref[u]) * ATT_SCALE
                acc_scr[u] += _dot(ds.astype(BF16), k)

        @pl.when(ki < qi)
        def _():
            step(False)

        @pl.when(ki == qi)
        def _():
            step(True)
            dq_ref[...] = acc_scr[...]

        if comm is not None:
            c_mid()
            c_last()

    gs = pltpu.PrefetchScalarGridSpec(
        num_scalar_prefetch=2, grid=(nh // HPS, npairs),
        in_specs=[pl.BlockSpec((HPS, tq, HEAD_PAD), lambda h, p, qt, kt: (h, qt[p], 0)),
                  pl.BlockSpec((HPS, tq, HEAD_PAD), lambda h, p, qt, kt: (h, kt[p], 0)),
                  pl.BlockSpec((HPS, tq, V_HEAD), lambda h, p, qt, kt: (h, kt[p], 0)),
                  pl.BlockSpec((tq, wv), lambda h, p, qt, kt: (qt[p], h)),
                  pl.BlockSpec((HPS, tq, 1), lambda h, p, qt, kt: (h, qt[p], 0)),
                  pl.BlockSpec((HPS, tq, 1), lambda h, p, qt, kt: (h, qt[p], 0))] + c_in,
        out_specs=(pl.BlockSpec((HPS, tq, HEAD_PAD), lambda h, p, qt, kt: (h, qt[p], 0)),) + tuple(c_out),
        scratch_shapes=[pltpu.VMEM((HPS, tq, HEAD_PAD), F32)] + c_sem)
    sem = ("parallel", "arbitrary") if comm is None else ("arbitrary", "arbitrary")
    res = pl.pallas_call(
        body, out_shape=(jax.ShapeDtypeStruct((nh, s, HEAD_PAD), F32),) + c_shapes, grid_spec=gs,
        compiler_params=_cparams(sem), name="attn_bwd_dq" if comm is None else "attn_bwd_dq_with_to_owner")(
            qtab, ktab, qf, kf, vv, do, lse, delta, *(comm["ins"] if comm else ()))
    return res[0], list(res[1:])


def _attn_bwd_dkv(qf, kf, vv, do, lset, deltat, *, tq, comm=None):
    nh, s, _ = qf.shape
    nq = s // tq
    qtab, ktab, npairs = _causal_pairs(nq, False)
    wv = HPS * V_HEAD
    c_in, c_out, c_sem, c_shapes = _comm_extras(comm)
    nci, nco = len(c_in), len(c_out)

    def body(*refs):
        qt_ref, kt_ref, k_ref, v_ref, q_ref, do_ref, lse_ref, dl_ref = refs[:8]
        dk_ref, dv_ref = refs[8 + nci:10 + nci]
        dk_scr, dv_scr = refs[10 + nci + nco:12 + nci + nco]
        pr = pl.program_id(1)
        qi, ki = qt_ref[pr], kt_ref[pr]
        if comm is not None:
            hg = pl.program_id(0)
            last_hg = nh // HPS - 1
            c_first, c_mid, c_last = _comm_hooks(comm, refs[8:8 + nci], refs[10 + nci:10 + nci + nco], refs[12 + nci + nco:],
                                                 (hg == 0) & (pr == 0), (hg == last_hg) & (pr == 0),
                                                 (hg == last_hg) & (pr == npairs - 1))
            c_first()

        def step(diagonal):
            for u in range(HPS):
                q = q_ref[u]
                do_ = do_ref[:, u * V_HEAD:(u + 1) * V_HEAD]
                st = _dot_nt(k_ref[u], q) * ATT_SCALE
                pt = jnp.exp(st - lse_ref[u])
                if diagonal:
                    r = lax.broadcasted_iota(jnp.int32, (tq, tq), 0)
                    c = lax.broadcasted_iota(jnp.int32, (tq, tq), 1)
                    pt = jnp.where(r <= c, pt, 0.0)
                dpt = _dot_nt(v_ref[u], do_)
                dst = pt * (dpt - dl_ref[u]) * ATT_SCALE
                if diagonal:
                    dv_scr[u] = _dot(pt.astype(BF16), do_)
                    dk_scr[u] = _dot(dst.astype(BF16), q)
                else:
                    dv_scr[u] += _dot(pt.astype(BF16), do_)
                    dk_scr[u] += _dot(dst.astype(BF16), q)

        @pl.when(qi == ki)
        def _():
            step(True)

        @pl.when(qi > ki)
        def _():
            step(False)

        @pl.when(qi == nq - 1)
        def _():
            dk_ref[...] = dk_scr[...]
            dv_ref[...] = dv_scr[...]

        if comm is not None:
            c_mid()
            c_last()

    sds = jax.ShapeDtypeStruct
    gs = pltpu.PrefetchScalarGridSpec(
        num_scalar_prefetch=2, grid=(nh // HPS, npairs),
        in_specs=[pl.BlockSpec((HPS, tq, HEAD_PAD), lambda h, p, qt, kt: (h, kt[p], 0)),
                  pl.BlockSpec((HPS, tq, V_HEAD), lambda h, p, qt, kt: (h, kt[p], 0)),
                  pl.BlockSpec((HPS, tq, HEAD_PAD), lambda h, p, qt, kt: (h, qt[p], 0)),
                  pl.BlockSpec((tq, wv), lambda h, p, qt, kt: (qt[p], h)),
                  pl.BlockSpec((HPS, 1, tq), lambda h, p, qt, kt: (h, 0, qt[p])),
                  pl.BlockSpec((HPS, 1, tq), lambda h, p, qt, kt: (h, 0, qt[p]))] + c_in,
        out_specs=(pl.BlockSpec((HPS, tq, HEAD_PAD), lambda h, p, qt, kt: (h, kt[p], 0)),
                   pl.BlockSpec((HPS, tq, V_HEAD), lambda h, p, qt, kt: (h, kt[p], 0))) + tuple(c_out),
        scratch_shapes=[pltpu.VMEM((HPS, tq, HEAD_PAD), F32), pltpu.VMEM((HPS, tq, V_HEAD), F32)] + c_sem)
    sem = ("parallel", "arbitrary") if comm is None else ("arbitrary", "arbitrary")
    res = pl.pallas_call(
        body, out_shape=(sds((nh, s, HEAD_PAD), F32), sds((nh, s, V_HEAD), F32)) + c_shapes, grid_spec=gs,
        compiler_params=_cparams(sem), name="attn_bwd_dkv" if comm is None else "attn_bwd_dkv_with_exchange")(
            qtab, ktab, kf, vv, qf, do, lset, deltat, *(comm["ins"] if comm else ()))
    return res[0], res[1], list(res[2:])


XATT_SCALE = XATTN_HEAD_DIM ** -0.5
XQ_COL = 8


def _memkv_prep(mem_kv, k_g):
    m = mem_kv.shape[0]

    def body(kv_ref, g_ref, k_ref, v_ref):
        for h in range(XATTN_HEADS):
            sl = slice(h * XATTN_HEAD_DIM, (h + 1) * XATTN_HEAD_DIM)
            k_ref[:, sl] = _rms(kv_ref[:, sl], g_ref[...], XATTN_HEAD_DIM).astype(BF16)
        v_ref[...] = kv_ref[:, BRANCH_W:2 * BRANCH_W].astype(BF16)

    sds = jax.ShapeDtypeStruct
    return pl.pallas_call(body, out_shape=(sds((m, BRANCH_W), BF16), sds((m, BRANCH_W), BF16)),
                          compiler_params=_cparams(), name="memkv_prep")(mem_kv, k_g)


def _memkv_prep_bwd(mem_kv, k_g, dk, dv):
    m = mem_kv.shape[0]

    def body(kv_ref, g_ref, dk_ref, dv_ref, d_ref, gk_ref):
        gk = jnp.zeros((1, XATTN_HEAD_DIM), F32)
        for h in range(XATTN_HEADS):
            sl = slice(h * XATTN_HEAD_DIM, (h + 1) * XATTN_HEAD_DIM)
            dx, dg = _rms_bwd(kv_ref[:, sl], g_ref[...], XATTN_HEAD_DIM, dk_ref[:, sl])
            d_ref[:, sl] = dx.astype(BF16)
            gk = gk + jnp.sum(dg, axis=0, keepdims=True)
        d_ref[:, BRANCH_W:2 * BRANCH_W] = dv_ref[...].astype(BF16)
        gk_ref[...] = gk

    sds = jax.ShapeDtypeStruct
    return pl.pallas_call(body, out_shape=(sds((m, 2 * BRANCH_W), BF16), sds((1, XATTN_HEAD_DIM), F32)),
                          compiler_params=_cparams(), name="memkv_prep_bwd")(mem_kv, k_g, dk, dv)


def _xattn_probs(xq, k_ref, qg, h):
    sl = slice(h * XATTN_HEAD_DIM, (h + 1) * XATTN_HEAD_DIM)
    q = _rms(xq[:, sl], qg, XATTN_HEAD_DIM).astype(BF16)
    sc = _dot_nt(q, k_ref[:, sl]) * XATT_SCALE
    e = jnp.exp(sc - jnp.max(sc, axis=-1, keepdims=True))
    return q, e / jnp.sum(e, axis=-1, keepdims=True)


def _xattn_fwd(p1, kx, vx, q_g, *, t=256):
    s = p1.shape[0]
    m = kx.shape[0]
    t = _tile(s, t, 128)

    def body(xq_ref, xz_ref, k_ref, v_ref, g_ref, y_ref, yt_ref):
        xq = xq_ref[...]
        outs = []
        for h in range(XATTN_HEADS):
            _, p = _xattn_probs(xq, k_ref, g_ref[...], h)
            outs.append(_dot(p.astype(BF16), v_ref[:, h * XATTN_HEAD_DIM:(h + 1) * XATTN_HEAD_DIM]))
        y = jnp.concatenate(outs, axis=1) * _silu(xz_ref[...])
        y_ref[...] = y.astype(BF16)
        yt_ref[...] = y.T.astype(BF16)

    full = lambda shp: pl.BlockSpec(shp, lambda i: tuple(0 for _ in shp))
    sds = jax.ShapeDtypeStruct
    return pl.pallas_call(
        body, out_shape=(sds((s, BRANCH_W), BF16), sds((BRANCH_W, s), BF16)), grid=(s // t,),
        in_specs=[pl.BlockSpec((t, BRANCH_W), lambda i: (i, XQ_COL)), pl.BlockSpec((t, BRANCH_W), lambda i: (i, XQ_COL + 1)),
                  full((m, BRANCH_W)), full((m, BRANCH_W)), full((1, XATTN_HEAD_DIM))],
        out_specs=(pl.BlockSpec((t, BRANCH_W), lambda i: (i, 0)), pl.BlockSpec((BRANCH_W, t), lambda i: (0, i))),
        compiler_params=_cparams(("parallel",)), name="xattn_fwd")(p1, p1, kx, vx, q_g)


def _xattn_bwd(p1, dy4, kx, vx, q_g, *, t=256):
    s = p1.shape[0]
    m = kx.shape[0]
    t = _tile(s, t, 128)

    def body(xq_ref, xz_ref, dy_ref, k_ref, v_ref, g_ref, d_ref, dk_ref, dv_ref, gq_ref):
        i = pl.program_id(0)
        xq, xz, dy = xq_ref[...], xz_ref[...], dy_ref[...]
        do = dy * _silu(xz)
        gq = jnp.zeros((1, XATTN_HEAD_DIM), F32)
        outs, dks, dvs = [], [], []
        for h in range(XATTN_HEADS):
            sl = slice(h * XATTN_HEAD_DIM, (h + 1) * XATTN_HEAD_DIM)
            q, p = _xattn_probs(xq, k_ref, g_ref[...], h)
            pb = p.astype(BF16)
            outs.append(_dot(pb, v_ref[:, sl]))
            do_h = do[:, sl].astype(BF16)
            dvs.append(_dot_tn(pb, do_h))
            dp = _dot_nt(do_h, v_ref[:, sl])
            ds = (p * (dp - jnp.sum(p * dp, axis=-1, keepdims=True)) * XATT_SCALE).astype(BF16)
            dks.append(_dot_tn(ds, q))
            dx, dg = _rms_bwd(xq[:, sl], g_ref[...], XATTN_HEAD_DIM, _dot(ds, k_ref[:, sl]))
            d_ref[:, sl] = dx.astype(BF16)
            gq = gq + jnp.sum(dg, axis=0, keepdims=True)
        o = jnp.concatenate(outs, axis=1)
        d_ref[:, BRANCH_W:2 * BRANCH_W] = (dy * o * _dsilu(xz)).astype(BF16)
        dk = jnp.concatenate(dks, axis=1)
        dv = jnp.concatenate(dvs, axis=1)

        @pl.when(i == 0)
        def _():
            dk_ref[...] = dk
            dv_ref[...] = dv
            gq_ref[...] = gq

        @pl.when(i > 0)
        def _():
            dk_ref[...] += dk
            dv_ref[...] += dv
            gq_ref[...] += gq

    full = lambda shp: pl.BlockSpec(shp, lambda i: tuple(0 for _ in shp))
    sds = jax.ShapeDtypeStruct
    return pl.pallas_call(
        body, out_shape=(sds((s, 2 * BRANCH_W), BF16), sds((m, BRANCH_W), F32), sds((m, BRANCH_W), F32), sds((1, XATTN_HEAD_DIM), F32)),
        grid=(s // t,),
        in_specs=[pl.BlockSpec((t, BRANCH_W), lambda i: (i, XQ_COL)), pl.BlockSpec((t, BRANCH_W), lambda i: (i, XQ_COL + 1)),
                  pl.BlockSpec((None, t, BRANCH_W), lambda i: (3, i, 0)),
                  full((m, BRANCH_W)), full((m, BRANCH_W)), full((1, XATTN_HEAD_DIM))],
        out_specs=(pl.BlockSpec((t, 2 * BRANCH_W), lambda i: (i, 0)), full((m, BRANCH_W)), full((m, BRANCH_W)),
                   full((1, XATTN_HEAD_DIM))),
        compiler_params=_cparams(("arbitrary",)), name="xattn_bwd")(p1, p1, dy4, kx, vx, q_g)


def _gate_fwd(ystack, w_branch, gp, gate_b, *, tm=1024, tn=1024):
    _, s, _ = ystack.shape
    d = w_branch.shape[2]
    tm, tn = _tile(s, tm, 128), _tile(d, tn)
    nj = d // tn

    def body(y_ref, w_ref, gp_ref, gb_ref, o_ref, ot_ref, acc_scr):
        b = pl.program_id(2)
        part = jax.nn.sigmoid(gp_ref[...] + gb_ref[...]) * _dot(y_ref[...], w_ref[...])

        @pl.when(b == 0)
        def _():
            acc_scr[...] = part

        @pl.when(b > 0)
        def _():
            acc_scr[...] += part

        @pl.when(b == N_BRANCH - 1)
        def _():
            acc = acc_scr[...]
            o_ref[...] = acc.astype(BF16)
            ot_ref[...] = acc.T.astype(BF16)

    sds = jax.ShapeDtypeStruct
    return pl.pallas_call(
        body, out_shape=(sds((s, d), BF16), sds((d, s), BF16)), grid=(s // tm, nj, N_BRANCH),
        in_specs=[pl.BlockSpec((None, tm, BRANCH_W), lambda i, j, b: (b, i, 0)),
                  pl.BlockSpec((None, BRANCH_W, tn), lambda i, j, b: (b, 0, j)),
                  pl.BlockSpec((tm, tn), lambda i, j, b: (i, b * nj + j)),
                  pl.BlockSpec((1, tn), lambda i, j, b: (0, b * nj + j))],
        out_specs=(pl.BlockSpec((tm, tn), lambda i, j, b: (i, j)), pl.BlockSpec((tn, tm), lambda i, j, b: (j, i))),
        scratch_shapes=[pltpu.VMEM((tm, tn), F32)],
        compiler_params=_cparams(("parallel", "parallel", "arbitrary")), name="gate_fwd")(ystack, w_branch, gp, gate_b)


def _gate_bwd(ystack, w_branch, gp, gate_b, dm, *, tm=1024, tn=1024):
    _, s, _ = ystack.shape
    d = w_branch.shape[2]
    tm, tn = _tile(s, tm, 128), _tile(d, tn)
    nj = d // tn

    def body(y_ref, w_ref, gp_ref, gb_ref, dm_ref, dp_ref, dg_ref, gb_out_ref):
        i = pl.program_id(2)
        proj = _dot(y_ref[...], w_ref[...])
        gate = jax.nn.sigmoid(gp_ref[...] + gb_ref[...])
        dmv = dm_ref[...]
        dp_ref[...] = (dmv * gate).astype(BF16)
        dpre = dmv * proj * gate * (1.0 - gate)
        dg_ref[...] = dpre.astype(BF16)
        part = jnp.sum(dpre, axis=0, keepdims=True)

        @pl.when(i == 0)
        def _():
            gb_out_ref[...] = part

        @pl.when(i > 0)
        def _():
            gb_out_ref[...] += part

    sds = jax.ShapeDtypeStruct
    return pl.pallas_call(
        body, out_shape=(sds((N_BRANCH, s, d), BF16), sds((s, N_BRANCH * d), BF16), sds((1, N_BRANCH * d), F32)),
        grid=(N_BRANCH, nj, s // tm),
        in_specs=[pl.BlockSpec((None, tm, BRANCH_W), lambda b, j, i: (b, i, 0)),
                  pl.BlockSpec((None, BRANCH_W, tn), lambda b, j, i: (b, 0, j)),
                  pl.BlockSpec((tm, tn), lambda b, j, i: (i, b * nj + j)),
                  pl.BlockSpec((1, tn), lambda b, j, i: (0, b * nj + j)),
                  pl.BlockSpec((tm, tn), lambda b, j, i: (i, j))],
        out_specs=(pl.BlockSpec((None, tm, tn), lambda b, j, i: (b, i, j)),
                   pl.BlockSpec((tm, tn), lambda b, j, i: (i, b * nj + j)),
                   pl.BlockSpec((1, tn), lambda b, j, i: (0, b * nj + j))),
        compiler_params=_cparams(("parallel", "parallel", "arbitrary")), name="gate_bwd")(ystack, w_branch, gp, gate_b, dm)


def _adamw(w, g, m, v, *, name):
    shape = w.shape
    c = shape[-1]
    r = 1
    for n in shape[:-1]:
        r *= n
    w2, g2, m2, v2 = (a.reshape(r, c) for a in (w, g, m, v))
    tr = _tile(r, max(8, (1 << 19) // c // 8 * 8), 8)
    c1 = 1.0 / (1.0 - ADAM_B1 ** ADAM_STEP)
    c2 = 1.0 / (1.0 - ADAM_B2 ** ADAM_STEP)

    def body(w_ref, g_ref, m_ref, v_ref, d_ref, nm_ref, nv_ref):
        gv = g_ref[...]
        nm = ADAM_B1 * m_ref[...] + (1.0 - ADAM_B1) * gv
        nv = ADAM_B2 * v_ref[...] + (1.0 - ADAM_B2) * (gv * gv)
        nm_ref[...] = nm
        nv_ref[...] = nv
        d_ref[...] = -ADAM_LR * ((nm * c1) / (jnp.sqrt(nv * c2) + ADAM_EPS) + ADAM_WD * w_ref[...])

    blk = pl.BlockSpec((tr, c), lambda i: (i, 0))
    sd = jax.ShapeDtypeStruct((r, c), F32)
    d2, nm2, nv2 = pl.pallas_call(body, out_shape=(sd, sd, sd), grid=(r // tr,), in_specs=[blk] * 4, out_specs=(blk,) * 3,
                                  compiler_params=_cparams(("parallel",)), name=name)(w2, g2, m2, v2)
    return d2.reshape(shape), nm2.reshape(shape), nv2.reshape(shape)


def _place():
    x, y, c = lax.axis_index("x"), lax.axis_index("y"), lax.axis_index("c")
    chips = [(1 - x, y), (x, 1 - y), (1 - x, 1 - y)]
    return x, y, c, 2 * x + y, chips, [2 * cx + cy for cx, cy in chips]


ANY = pl.BlockSpec(memory_space=pl.ANY)


def _rdma(src, dst, send, recv, idx, dev):
    return pltpu.make_async_remote_copy(src_ref=src, dst_ref=dst, send_sem=send.at[idx], recv_sem=recv.at[idx],
                                        device_id=dev, device_id_type=MESH)


def _run_comm(comm, name):
    n_in, n_out = len(comm["ins"]), len(comm["outs"])

    def body(*refs):
        ins, outs = refs[:n_in], refs[n_in:n_in + n_out]
        send, recv = refs[n_in + n_out:]
        for phase in comm["phases"]:
            phase(ins, outs, send, recv)

    return pl.pallas_call(
        body, out_shape=tuple(comm["outs"]), in_specs=[ANY] * n_in, out_specs=(ANY,) * n_out,
        input_output_aliases=comm.get("aliases", {}),
        scratch_shapes=[pltpu.SemaphoreType.DMA((comm["nsem"],)), pltpu.SemaphoreType.DMA((comm["nsem"],))],
        name=name)(*comm["ins"])


def _gather_comm(shards, layer):
    n = len(shards)

    def start(ins, outs, send, recv):
        x, y, c, k, chips, ks = _place()

        @pl.when(c == layer)
        def _():
            for a in range(n):
                for j in range(3):
                    _rdma(ins[a], outs[a].at[k], send, recv, 6 * a + j, (*chips[j], c)).start()

    def forward(ins, outs, send, recv):
        x, y, c, k, chips, ks = _place()

        @pl.when(c == layer)
        def _():
            for a in range(n):
                for j in range(3):
                    slab = outs[a].at[ks[j]]
                    _rdma(slab, slab, send, recv, 6 * a + j, (*chips[j], c)).wait_recv()
                    _rdma(slab, slab, send, recv, 6 * a + 3 + j, (x, y, 1 - c)).start()

    def finish(ins, outs, send, recv):
        x, y, c, k, chips, ks = _place()

        @pl.when(c == layer)
        def _():
            for a in range(n):
                for j in range(3):
                    slab = outs[a].at[ks[j]]
                    _rdma(ins[a], outs[a].at[k], send, recv, 6 * a + j, (*chips[j], c)).wait_send()
                    _rdma(slab, slab, send, recv, 6 * a + 3 + j, (x, y, 1 - c)).wait_send()

        @pl.when(c != layer)
        def _():
            for a in range(n):
                for j in range(3):
                    slab = outs[a].at[ks[j]]
                    _rdma(slab, slab, send, recv, 6 * a + 3 + j, (x, y, 1 - c)).wait_recv()

    return dict(ins=list(shards), outs=[jax.ShapeDtypeStruct((4,) + s.shape, s.dtype) for s in shards], nsem=6 * n,
                phases=[start, forward, finish])


def _to_owner_comm(grads, layer):
    n = len(grads)

    def start(ins, outs, send, recv):
        x, y, c, _, _, _ = _place()

        @pl.when(c != layer)
        def _():
            for a in range(n):
                _rdma(ins[a], outs[a], send, recv, a, (x, y, 1 - c)).start()

    def finish(ins, outs, send, recv):
        x, y, c, _, _, _ = _place()

        @pl.when(c != layer)
        def _():
            for a in range(n):
                _rdma(ins[a], outs[a], send, recv, a, (x, y, 1 - c)).wait_send()

        @pl.when(c == layer)
        def _():
            for a in range(n):
                _rdma(ins[a], outs[a], send, recv, a, (x, y, 1 - c)).wait_recv()

    return dict(ins=list(grads), outs=[jax.ShapeDtypeStruct(g.shape, g.dtype) for g in grads], nsem=n, phases=[start, finish])


def _exchange_comm(parts, layer):
    n = len(parts)

    def start(ins, outs, send, recv):
        x, y, c, k, chips, ks = _place()

        @pl.when(c == layer)
        def _():
            for a in range(n):
                for j in range(3):
                    _rdma(ins[a].at[ks[j]], outs[a].at[j], send, recv, 3 * a + j, (*chips[j], c)).start()

    def finish(ins, outs, send, recv):
        x, y, c, k, chips, ks = _place()

        @pl.when(c == layer)
        def _():
            for a in range(n):
                for j in range(3):
                    _rdma(ins[a].at[ks[j]], outs[a].at[j], send, recv, 3 * a + j, (*chips[j], c)).wait()

    return dict(ins=list(parts), outs=[jax.ShapeDtypeStruct((3,) + p.shape[1:], p.dtype) for p in parts], nsem=3 * n,
                phases=[start, finish])


def _share_comm(bufs, layer):
    n = len(bufs)

    def go(ins, outs, send, recv):
        x, y, c, _, _, _ = _place()

        @pl.when(c == layer)
        def _():
            for a in range(n):
                _rdma(outs[a].at[layer], outs[a].at[layer], send, recv, a, (x, y, 1 - c)).start()
            for a in range(n):
                _rdma(outs[a].at[layer], outs[a].at[layer], send, recv, a, (x, y, 1 - c)).wait_send()

        @pl.when(c != layer)
        def _():
            for a in range(n):
                _rdma(outs[a].at[layer], outs[a].at[layer], send, recv, a, (x, y, 1 - c)).wait_recv()

    return dict(ins=list(bufs), outs=[jax.ShapeDtypeStruct(b.shape, b.dtype) for b in bufs], nsem=n, phases=[go],
                aliases={a: a for a in range(n)})


def _add_owner(g, ra, own, *, name):
    _, r, c = g.shape
    tr = _tile(r, max(16, (1 << 20) // c // 16 * 16), 16)

    def body(own_ref, g_ref, ra_ref, o_ref, ob_ref):
        tot = g_ref[...] + ra_ref[...].astype(F32)
        o_ref[...] = tot
        ob_ref[...] = tot.astype(BF16)

    blk = pl.BlockSpec((None, tr, c), lambda j, i, o: (j * o[0], i * o[0], 0))
    gs = pltpu.PrefetchScalarGridSpec(num_scalar_prefetch=1, grid=(4, r // tr), in_specs=[blk, blk], out_specs=(blk, blk))
    return pl.pallas_call(body, out_shape=(jax.ShapeDtypeStruct((4, r, c), F32), jax.ShapeDtypeStruct((4, r, c), BF16)),
                          grid_spec=gs, compiler_params=_cparams(("arbitrary", "arbitrary")), name=name)(own, g, ra)


def _add_chips_layer(p, r3, k_idx, own, layer, buf, *, name):
    _, r, c = p.shape
    tr = _tile(r, max(16, (1 << 20) // c // 16 * 16), 16)

    def body(k_ref, own_ref, p_ref, r_ref, *rest):
        o_ref = rest[-1]
        o_ref[...] = ((p_ref[...] + r_ref[0].astype(F32)) + r_ref[1].astype(F32)) + r_ref[2].astype(F32)

    in_specs = [pl.BlockSpec((None, tr, c), lambda i, kr, o: (kr[0] * o[0], i * o[0], 0)),
                pl.BlockSpec((3, tr, c), lambda i, kr, o: (0, i * o[0], 0))]
    args = [k_idx, own, p, r3]
    aliases = {}
    if buf is not None:
        in_specs.append(ANY)
        args.append(buf)
        aliases = {4: 0}
    gs = pltpu.PrefetchScalarGridSpec(num_scalar_prefetch=2, grid=(r // tr,), in_specs=in_specs,
                                      out_specs=pl.BlockSpec((None, tr, c), lambda i, kr, o: (layer, i * o[0], 0)))
    return pl.pallas_call(body, out_shape=jax.ShapeDtypeStruct((2, r, c), F32), grid_spec=gs, input_output_aliases=aliases,
                          compiler_params=_cparams(("arbitrary",)), name=name)(*args)


def _all_reduce_small(vec):
    r = vec.shape[0]

    def body(v_ref, gath_ref, sum_ref, send, recv):
        x, y, c = lax.axis_index("x"), lax.axis_index("y"), lax.axis_index("c")
        me = 4 * x + 2 * y + c
        gath_ref[me] = v_ref[...]
        cps = []
        for f in range(1, 8):
            fx, fy, fc = (f >> 2) & 1, (f >> 1) & 1, f & 1
            peer = (x ^ fx, y ^ fy, c ^ fc)
            cp = pltpu.make_async_remote_copy(src_ref=v_ref, dst_ref=gath_ref.at[me], send_sem=send.at[f - 1], recv_sem=recv.at[f - 1],
                                              device_id=peer, device_id_type=MESH)
            cp.start()
            cps.append(cp)
        for f in range(1, 8):
            fx, fy, fc = (f >> 2) & 1, (f >> 1) & 1, f & 1
            src = 4 * (x ^ fx) + 2 * (y ^ fy) + (c ^ fc)
            pltpu.make_async_remote_copy(src_ref=v_ref, dst_ref=gath_ref.at[src], send_sem=send.at[f - 1], recv_sem=recv.at[f - 1],
                                         device_id=(x ^ fx, y ^ fy, c ^ fc), device_id_type=MESH).wait_recv()
        for cp in cps:
            cp.wait_send()
        acc = gath_ref[0]
        for i in range(1, 8):
            acc = acc + gath_ref[i]
        sum_ref[...] = acc

    vm = pl.BlockSpec(memory_space=pltpu.VMEM)
    _, total = pl.pallas_call(
        body, out_shape=(jax.ShapeDtypeStruct((8, r, 128), F32), jax.ShapeDtypeStruct((r, 128), F32)),
        in_specs=[vm], out_specs=(vm, vm),
        scratch_shapes=[pltpu.SemaphoreType.DMA((7,)), pltpu.SemaphoreType.DMA((7,))],
        name="small_all_reduce")(vec)
    return total


def _full_weight(gw, name):
    gathered, own, chip = gw[name]
    return jnp.concatenate([jnp.where(chip == k, own, gathered[k]) for k in range(4)], axis=SHARD_AXIS[name])


def _to_shards(full, name):
    return jnp.stack(jnp.split(full, 4, axis=SHARD_AXIS[name]), axis=0)


def _rope_tables(positions):
    inv = ROPE_THETA ** (-jnp.arange(0, QK_ROPE, 2, dtype=F32) / QK_ROPE)
    ang = positions.astype(F32)[:, None] * inv
    cos, sin = jnp.cos(ang), jnp.sin(ang)
    s = positions.shape[0]
    pad = jnp.zeros((s, HEAD_PAD - QK_HEAD), F32)
    ctab = jnp.concatenate([jnp.ones((s, QK_NOPE), F32), cos, cos, pad], axis=1)
    stab = jnp.concatenate([jnp.zeros((s, QK_NOPE), F32), -sin, sin, pad], axis=1)
    return ctab, stab


def _pad_gain(g):
    return jnp.concatenate([g, jnp.zeros((HEAD_PAD - QK_HEAD,), F32)])[None, :]


def _layer_weights(gw, rep, l, ql, kvl):
    d = rep["norm_g"].shape[1]
    w_in = _full_weight(gw, "w_in")
    o_kr = 2 * BRANCH_W + ql + kvl
    o_g = o_kr + QK_ROPE + 7 * BRANCH_W
    w = {}
    w["w1"] = jnp.concatenate([w_in[:, :o_kr], w_in[:, o_kr + QK_ROPE:o_g]], axis=1)
    w["wg"] = w_in[:, o_g:]
    w["wkr"] = jnp.concatenate([w_in[:, o_kr:o_kr + QK_ROPE], jnp.zeros((d, 128 - QK_ROPE), BF16)], axis=1)
    for nme in ("norm_g", "gate_b", "pool_scale", "q_a_norm_g", "kv_a_norm_g", "mem_norm_g", "xattn_q_norm_g", "xattn_k_norm_g"):
        w[nme] = rep[nme][l][None, :]
    w["mla_q_norm_g"] = _pad_gain(rep["mla_q_norm_g"][l])
    w["mla_k_norm_g"] = _pad_gain(rep["mla_k_norm_g"][l])
    return w


def _other_weights(gw, ql):
    w = {}
    wuq = _full_weight(gw, "w_uq").reshape(ql, MLA_HEADS, QK_HEAD)
    w["w_uq"] = jnp.pad(wuq, ((0, 0), (0, 0), (0, HEAD_PAD - QK_HEAD))).reshape(ql, MLA_HEADS * HEAD_PAD)
    for nme in ("w_ukv", "pool_w", "conv_w", "w_mem_kv", "w_branch", "w_out"):
        w[nme] = _full_weight(gw, nme)
    return w


def _forward_layer(x, mem, ctab, stab, w, tq, l, comm=None, late=None):
    sfx = f"_l{l}"
    h, ht = _norm_fwd(x, w["norm_g"], name="norm_fwd" + sfx)
    if late is None:
        p1 = _mm(h, w["w1"], name="proj_main" + sfx)
    else:
        p1, got = _mm(h, w["w1"], name="proj_main_with_gather" + sfx, comm=late[0])
        w.update(late[1](got))
    gp = _mm(h, w["wg"], name="proj_gates" + sfx)
    kr = _mm(h, w["wkr"], name="proj_krope" + sfx)
    y_pool, yt_pool = _pool_fwd(p1, w["pool_w"], w["pool_scale"])
    qf, kf, vv, vt = _mla_prep_fwd(p1, kr, ctab, stab, w["q_a_norm_g"], w["kv_a_norm_g"], w["w_uq"], w["w_ukv"],
                               w["mla_q_norm_g"], w["mla_k_norm_g"])
    (y_mla, yt_mla, o_att, lse, lset), comm_out = _attn_fwd(qf, kf, vt, p1, tq=tq, comm=comm)
    y_conv, yt_conv = _conv_fwd(p1, w["conv_w"])
    memn, memnt = _norm_fwd(mem, w["mem_norm_g"], name="mem_norm" + sfx)
    mem_kv = _mm(memn, w["w_mem_kv"], name="mem_kv" + sfx)
    kx, vx = _memkv_prep(mem_kv, w["xattn_k_norm_g"])
    y_mem, yt_mem = _xattn_fwd(p1, kx, vx, w["xattn_q_norm_g"])
    ystack = jnp.stack([y_pool, y_mla, y_conv, y_mem])
    ytstack = jnp.stack([yt_pool, yt_mla, yt_conv, yt_mem])
    merged, mergedt = _gate_fwd(ystack, w["w_branch"], gp, w["gate_b"])
    x_out = _mm(merged, w["w_out"], add=x, name="out_proj" + sfx)
    saved = dict(x=x, ht=ht, p1=p1, gp=gp, kr=kr, qf=qf, kf=kf, vv=vv, o_att=o_att, lse=lse, lset=lset, memnt=memnt,
                 mem_kv=mem_kv, kx=kx, vx=vx, ystack=ystack, ytstack=ytstack, mergedt=mergedt)
    return x_out, saved, comm_out


def _backward_layer(dx_out, sv, mem, ctab, stab, w, tq, l, ql, kvl, comm=None, own=None):
    sfx = f"_l{l}"
    g = {}
    g["w_out"] = _mm(sv["mergedt"], dx_out, name="g_w_out" + sfx)
    dm = _mm(dx_out, w["w_out"], trans_b=True, name="d_merged" + sfx)
    dproj, dgp, g_gate_b = _gate_bwd(sv["ystack"], w["w_branch"], sv["gp"], w["gate_b"], dm)
    g["gate_b"] = g_gate_b[0]
    g["w_branch"] = _mm(sv["ytstack"], dproj, name="g_w_branch" + sfx)
    dy4 = _mm(dproj, w["w_branch"], trans_b=True, name="d_branches" + sfx)
    p1, kr = sv["p1"], sv["kr"]
    d_pool, g_pw, g_ps = _pool_bwd(p1, dy4, w["pool_w"], w["pool_scale"])
    g["pool_w"], g["pool_scale"] = g_pw, g_ps[0]
    do, d_mz, delta, deltat = _attn_bwd_pre(dy4, sv["o_att"], p1)
    dqf, got = _attn_bwd_dq(sv["qf"], sv["kf"], sv["vv"], do, sv["lse"], delta, tq=tq, comm=comm[0] if comm else None)
    parts, comm_dkv = comm[1](got) if comm else (None, None)
    dkf, dvv, got = _attn_bwd_dkv(sv["qf"], sv["kf"], sv["vv"], do, sv["lset"], deltat, tq=tq, comm=comm_dkv)
    comm_out = (parts, got)
    (d_c, d_kr, dq_raw, dkv_raw, cqnt, ckvnt, g_qa, g_kva, g_qg, g_kg) = _mla_prep_bwd(
        p1, kr, ctab, stab, w["q_a_norm_g"], w["kv_a_norm_g"], w["w_uq"], w["w_ukv"], w["mla_q_norm_g"], w["mla_k_norm_g"],
        dqf, dkf, dvv)
    g["q_a_norm_g"], g["kv_a_norm_g"] = g_qa[0], g_kva[0]
    g["mla_q_norm_g"], g["mla_k_norm_g"] = g_qg[0, :QK_HEAD], g_kg[0, :QK_HEAD]
    g_wuq = _mm(cqnt, dq_raw, name="g_w_uq" + sfx)
    g["w_uq"] = g_wuq.reshape(ql, MLA_HEADS, HEAD_PAD)[:, :, :QK_HEAD].reshape(ql, MLA_HEADS * QK_HEAD)
    g["w_ukv"] = _mm(ckvnt, dkv_raw, name="g_w_ukv" + sfx)
    d_conv, gc0, gc1, gc2 = _conv_bwd(p1, dy4, w["conv_w"])
    g["conv_w"] = jnp.concatenate([gc0, gc1, gc2], axis=0)
    d_x, dkx, dvx, g_xq = _xattn_bwd(p1, dy4, sv["kx"], sv["vx"], w["xattn_q_norm_g"])
    g["xattn_q_norm_g"] = g_xq[0]
    d_memkv, g_xk = _memkv_prep_bwd(sv["mem_kv"], w["xattn_k_norm_g"], dkx, dvx)
    g["xattn_k_norm_g"] = g_xk[0]
    g["w_mem_kv"] = _mm(sv["memnt"], d_memkv, name="g_w_mem_kv" + sfx)
    d_memn = _mm(d_memkv, w["w_mem_kv"], trans_b=True, name="d_memn" + sfx)
    _, g_mn = _norm_bwd(mem, w["mem_norm_g"], d_memn, d_memn, name="mem_norm_bwd" + sfx)
    g["mem_norm_g"] = g_mn[0]
    dp1 = jnp.concatenate([d_pool, d_c, d_mz, d_conv, d_x], axis=1)
    ht = sv["ht"]
    o_kr = 2 * BRANCH_W + ql + kvl
    if own is None:
        g_w1 = _mm(ht, dp1, name="g_w1" + sfx)
        g_wg = _mm(ht, dgp, name="g_wg" + sfx)
        g_wkr = _mm(ht, d_kr, name="g_wkr" + sfx)
        g["w_in"] = jnp.concatenate([g_w1[:, :o_kr], g_wkr[:, :QK_ROPE], g_w1[:, o_kr:], g_wg], axis=1)
        dh = _mm(dp1, w["w1"], trans_b=True, name="dh_main" + sfx)
        dh = _mm(dgp, w["wg"], trans_b=True, add=dh, name="dh_gates" + sfx)
        own_out = None
    else:
        rest_names = SHARDED[1:]
        gl_a = own["layout"](g, rest_names)
        g_w1, ra = _mm(ht, dp1, name="g_w1_with_to_owner" + sfx, comm=_to_owner_comm([t.astype(BF16) for t in gl_a], l))
        parts_a = own["add_owner"](gl_a, ra, rest_names)
        g_wg, r3_a = _mm(ht, dgp, name="g_wg_with_exchange" + sfx, comm=_exchange_comm([pb for _, pb in parts_a], l))
        g_wkr = _mm(ht, d_kr, name="g_wkr" + sfx)
        g["w_in"] = jnp.concatenate([g_w1[:, :o_kr], g_wkr[:, :QK_ROPE], g_w1[:, o_kr:], g_wg], axis=1)
        gl_b = own["layout"](g, SHARDED[:1])
        dh, rb = _mm(dp1, w["w1"], trans_b=True, name="dh_main_with_to_owner" + sfx,
                     comm=_to_owner_comm([t.astype(BF16) for t in gl_b], l))
        parts_b = own["add_owner"](gl_b, rb, SHARDED[:1])
        dh, r3_b = _mm(dgp, w["wg"], trans_b=True, add=dh, name="dh_gates_with_exchange" + sfx,
                       comm=_exchange_comm([pb for _, pb in parts_b], l))
        own_out = (parts_b + parts_a, r3_b + r3_a)
    dh = _mm(d_kr, w["wkr"], trans_b=True, add=dh, name="dh_krope" + sfx)
    dx, g_ng = _norm_bwd(sv["x"], w["norm_g"], dh, dx_out, name="norm_bwd" + sfx)
    g["norm_g"] = g_ng[0]
    return dx, g, comm_out, own_out


def kernel(x, mem, positions, norm_g, w_in, gate_b, pool_w, pool_scale, q_a_norm_g, kv_a_norm_g, w_uq, w_ukv, mla_q_norm_g, mla_k_norm_g, conv_w, mem_norm_g, w_mem_kv, xattn_q_norm_g, xattn_k_norm_g, w_branch, w_out, loss_target, m_norm_g, m_w_in, m_gate_b, m_pool_w, m_pool_scale, m_q_a_norm_g, m_kv_a_norm_g, m_w_uq, m_w_ukv, m_mla_q_norm_g, m_mla_k_norm_g, m_conv_w, m_mem_norm_g, m_w_mem_kv, m_xattn_q_norm_g, m_xattn_k_norm_g, m_w_branch, m_w_out, v_norm_g, v_w_in, v_gate_b, v_pool_w, v_pool_scale, v_q_a_norm_g, v_kv_a_norm_g, v_w_uq, v_w_ukv, v_mla_q_norm_g, v_mla_k_norm_g, v_conv_w, v_mem_norm_g, v_w_mem_kv, v_xattn_q_norm_g, v_xattn_k_norm_g, v_w_branch, v_w_out):
    wts = dict(norm_g=norm_g, w_in=w_in, gate_b=gate_b, pool_w=pool_w, pool_scale=pool_scale, q_a_norm_g=q_a_norm_g,
               kv_a_norm_g=kv_a_norm_g, w_uq=w_uq, w_ukv=w_ukv, mla_q_norm_g=mla_q_norm_g, mla_k_norm_g=mla_k_norm_g,
               conv_w=conv_w, mem_norm_g=mem_norm_g, w_mem_kv=w_mem_kv, xattn_q_norm_g=xattn_q_norm_g,
               xattn_k_norm_g=xattn_k_norm_g, w_branch=w_branch, w_out=w_out)
    mom = dict(norm_g=m_norm_g, w_in=m_w_in, gate_b=m_gate_b, pool_w=m_pool_w, pool_scale=m_pool_scale, q_a_norm_g=m_q_a_norm_g,
               kv_a_norm_g=m_kv_a_norm_g, w_uq=m_w_uq, w_ukv=m_w_ukv, mla_q_norm_g=m_mla_q_norm_g, mla_k_norm_g=m_mla_k_norm_g,
               conv_w=m_conv_w, mem_norm_g=m_mem_norm_g, w_mem_kv=m_w_mem_kv, xattn_q_norm_g=m_xattn_q_norm_g,
               xattn_k_norm_g=m_xattn_k_norm_g, w_branch=m_w_branch, w_out=m_w_out)
    vel = dict(norm_g=v_norm_g, w_in=v_w_in, gate_b=v_gate_b, pool_w=v_pool_w, pool_scale=v_pool_scale, q_a_norm_g=v_q_a_norm_g,
               kv_a_norm_g=v_kv_a_norm_g, w_uq=v_w_uq, w_ukv=v_w_ukv, mla_q_norm_g=v_mla_q_norm_g, mla_k_norm_g=v_mla_k_norm_g,
               conv_w=v_conv_w, mem_norm_g=v_mem_norm_g, w_mem_kv=v_w_mem_kv, xattn_q_norm_g=v_xattn_q_norm_g,
               xattn_k_norm_g=v_xattn_k_norm_g, w_branch=v_w_branch, w_out=v_w_out)
    depth = norm_g.shape[0]
    assert depth == 2 and x.shape[0] == 1
    xs, mems, tgt = x[0], mem[0], loss_target[0]
    s = xs.shape[0]
    ql, kvl = q_a_norm_g.shape[1], kv_a_norm_g.shape[1]
    tq = _tile(s, 1024, 128)
    ctab, stab = _rope_tables(positions[0])

    chip = 2 * lax.axis_index("x") + lax.axis_index("y")
    k_idx = chip.astype(jnp.int32).reshape(1)
    rep = {n: wts[n] for n in REPLICATED}
    send = [[wts[n][l].astype(F32 if n == "conv_w" else BF16) for n in SHARDED] for l in range(depth)]

    def gathered(got, l, names, first):
        return {n: (g, own, chip) for n, g, own in zip(names, got, send[l][first:first + len(names)])}

    got = _run_comm(_gather_comm(send[0][:1], 0), "weights_gather_w_in_l0")
    lw = [_layer_weights(gathered(got, 0, SHARDED[:1], 0), rep, 0, ql, kvl), None]
    late0 = (_gather_comm(send[0][1:], 0), lambda got: _other_weights(gathered(got, 0, SHARDED[1:], 1), ql))

    act, sv0, got1 = _forward_layer(xs, mems, ctab, stab, lw[0], tq, 0, comm=_gather_comm(send[1], 1), late=late0)
    lw[1] = _layer_weights(gathered(got1, 1, SHARDED, 0), rep, 1, ql, kvl)
    lw[1].update(_other_weights(gathered(got1, 1, SHARDED, 0), ql))
    act, sv1, _ = _forward_layer(act, mems, ctab, stab, lw[1], tq, 1)
    dy, loss_part = _loss_head(act, tgt)

    def shard_layout(g, names=SHARDED):
        shards = [jnp.swapaxes(_to_shards(g[n], n), -1, -2) if n == "w_in" else _to_shards(g[n], n) for n in names]
        return [t.reshape(4, -1, t.shape[-1]) for t in shards]

    def own_flag(l):
        return (lax.axis_index("c") == l).astype(jnp.int32).reshape(1)

    def add_owner(gl, ra, l, names=SHARDED):
        return [_add_owner(a, b, own_flag(l), name=f"add_owner_{n}_l{l}") for a, b, n in zip(gl, ra, names)]

    def finish_layer(parts, r3s, l, bufs):
        bufs = [_add_chips_layer(p, r3, k_idx, own_flag(l), l, None if bufs is None else bufs[i], name=f"add_chips_{n}_l{l}")
                for i, ((p, _), r3, n) in enumerate(zip(parts, r3s, SHARDED))]
        return _run_comm(_share_comm(bufs, l), f"grads_share_l{l}")

    grads = [None] * depth
    def own(l):
        return dict(layout=shard_layout, add_owner=lambda gl, ra, names: add_owner(gl, ra, l, names))

    dxl, grads[1], _, (parts1, r3_1) = _backward_layer(dy, sv1, mems, ctab, stab, lw[1], tq, 1, ql, kvl, own=own(1))
    dxl, grads[0], _, (parts0, r3_0) = _backward_layer(dxl, sv0, mems, ctab, stab, lw[0], tq, 0, ql, kvl, own=own(0))
    grad_x = dxl[None]
    bufs = finish_layer(parts1, r3_1, 1, None)
    reduced = finish_layer(parts0, r3_0, 0, bufs)
    gsum = {n: (jnp.swapaxes(r.reshape(2, wts[n].shape[2], wts[n].shape[1]), 1, 2) if n == "w_in" else r.reshape(wts[n].shape))
            for n, r in zip(SHARDED, reduced)}

    flat = [jnp.stack([grads[l][n] for l in range(depth)], axis=0).reshape(-1) for n in REPLICATED]
    sizes = [f.shape[0] for f in flat]
    total = sum(sizes) + 1
    rows = -(-total // 1024) * 8
    vec = jnp.concatenate(flat + [loss_part[0, :1], jnp.zeros((rows * 128 - total,), F32)]).reshape(rows, 128)
    red = _all_reduce_small(vec).reshape(-1)
    off = 0
    for n, sz in zip(REPLICATED, sizes):
        gsum[n] = red[off:off + sz].reshape(wts[n].shape)
        off += sz
    loss = red[off]

    delta, new_m, new_v = {}, {}, {}
    for n in WEIGHTS:
        if n == "w_in":
            tr_ = lambda a: jnp.swapaxes(a, 1, 2)
            delta[n], new_m[n], new_v[n] = (tr_(o) for o in _adamw(tr_(wts[n]), tr_(gsum[n]), tr_(mom[n]), tr_(vel[n]),
                                                                   name=f"adamw_{n}"))
        else:
            delta[n], new_m[n], new_v[n] = _adamw(wts[n], gsum[n], mom[n], vel[n], name=f"adamw_{n}")
    return (loss, grad_x, *[gsum[n] for n in WEIGHTS], *[delta[n] for n in WEIGHTS],
            *[new_m[n] for n in WEIGHTS], *[new_v[n] for n in WEIGHTS])
```
